```python
import math
import jax, jax.numpy as jnp
from jax import lax
import numpy as np

D_MODEL = 1024
BATCH = 8
SEQ = 2048
DEPTH = 1
DEC_BATCH = 128
DEC_SEQ = 8
PAST_LEN = 16384
PAGE_SIZE = 128

CHUNK = 128
D_A = D_MODEL
A_GROUPS = 8
A_GROUP_DIM = D_A // A_GROUPS
EXPAND = 2
D_INNER = EXPAND * D_MODEL
SSD_HEAD_DIM = 64
SSD_HEADS = D_INNER // SSD_HEAD_DIM
SSD_GROUPS = 8
SSD_STATE = 128
CONV_W = 4
CONV_DIM = D_INNER + 2 * SSD_GROUPS * SSD_STATE
SSD_CHUNK = 128
N_MEM = 256
X_HEADS = 4
X_HEAD_DIM = 128
D_X = X_HEADS * X_HEAD_DIM
N_BRANCH = 3
EPS = 1e-6
IN_SIZES = (D_A, D_A, D_A, D_INNER, CONV_DIM, SSD_HEADS, D_X, D_X, N_BRANCH * D_MODEL)
IN_DIM = sum(IN_SIZES)

kernel_name = "hybrid_gmlp_ssd_memattn_step"


def _split_points(sizes):
    pts, acc = [], 0
    for s in sizes[:-1]:
        acc += s
        pts.append(acc)
    return pts


def _rmsnorm(x, g):
    xf = x.astype(jnp.float32)
    xf = xf * lax.rsqrt(jnp.mean(xf * xf, axis=-1, keepdims=True) + EPS)
    return (xf * g.astype(jnp.float32)).astype(x.dtype)


def _layernorm(x, g, b):
    xf = x.astype(jnp.float32)
    mu = jnp.mean(xf, axis=-1, keepdims=True)
    xc = xf - mu
    xf = xc * lax.rsqrt(jnp.mean(xc * xc, axis=-1, keepdims=True) + EPS)
    return (xf * g.astype(jnp.float32) + b.astype(jnp.float32)).astype(x.dtype)


def _gated_group_rmsnorm(y, z, g):
    yf = (y * jax.nn.silu(z)).astype(jnp.float32)
    sh = yf.shape
    yg = yf.reshape(sh[:-1] + (SSD_GROUPS, D_INNER // SSD_GROUPS))
    yg = yg * lax.rsqrt(jnp.mean(yg * yg, axis=-1, keepdims=True) + EPS)
    return (yg.reshape(sh) * g.astype(jnp.float32)).astype(y.dtype)


def _gmlp_spatial(v, w_s, b_s):
    bsz, L, _ = v.shape
    n_c = -(-L // CHUNK)
    pad = n_c * CHUNK - L
    vp = jnp.pad(v, ((0, 0), (0, pad), (0, 0)))
    vr = vp.reshape(bsz, n_c, CHUNK, A_GROUPS, A_GROUP_DIM)
    mask = jnp.tril(jnp.ones((CHUNK, CHUNK), dtype=bool))
    w = jnp.where(mask[None], w_s, jnp.zeros_like(w_s))
    s = jnp.einsum('gts,bcsgd->bctgd', w, vr) + b_s.T[None, None, :, :, None]
    return s.reshape(bsz, n_c * CHUNK, D_A)[:, :L]


def _causal_conv(xbc, conv_state, w, b):
    L = xbc.shape[1]
    xpad = jnp.concatenate([conv_state.astype(xbc.dtype), xbc], axis=1)
    out = b + sum(xpad[:, k:k + L, :] * w[k] for k in range(CONV_W))
    return jax.nn.silu(out), xpad[:, -(CONV_W - 1):, :]


def _segsum_decay(a_cs):
    q = a_cs.shape[-1]
    diff = a_cs[..., :, None] - a_cs[..., None, :]
    mask = jnp.tril(jnp.ones((q, q), dtype=bool))
    return jnp.exp(jnp.where(mask, diff, -jnp.inf))


def _ssd(x, dt, a, b_in, c_in, h0):
    f32 = jnp.float32
    bsz, L = x.shape[:2]
    q = min(SSD_CHUNK, L)
    n_c = -(-L // q)
    pad = n_c * q - L
    R = SSD_HEADS // SSD_GROUPS
    x_dt = x.astype(f32) * dt[..., None]
    adt = dt * a
    pad_t = lambda t: jnp.pad(t, ((0, 0), (0, pad)) + ((0, 0),) * (t.ndim - 2))
    xc = pad_t(x_dt).reshape(bsz, n_c, q, SSD_GROUPS, R, SSD_HEAD_DIM)
    ac = pad_t(adt).reshape(bsz, n_c, q, SSD_GROUPS, R).transpose(0, 3, 4, 1, 2)
    bc = pad_t(b_in.astype(f32)).reshape(bsz, n_c, q, SSD_GROUPS, SSD_STATE)
    cc = pad_t(c_in.astype(f32)).reshape(bsz, n_c, q, SSD_GROUPS, SSD_STATE)
    a_cs = jnp.cumsum(ac, axis=-1)
    decay = _segsum_decay(a_cs)
    cb = jnp.einsum('bclgn,bcsgn->bgcls', cc, bc)
    y_diag = jnp.einsum('bgcls,bgrcls,bcsgrp->bclgrp', cb, decay, xc)
    decay_states = jnp.exp(a_cs[..., -1:] - a_cs)
    states = jnp.einsum('bcsgn,bgrcs,bcsgrp->cbgrpn', bc, decay_states, xc)
    chunk_decay = jnp.exp(a_cs[..., -1]).transpose(3, 0, 1, 2)
    h_init = h0.astype(f32).reshape(bsz, SSD_GROUPS, R, SSD_HEAD_DIM, SSD_STATE)

    def step(h, inp):
        s, d = inp
        return h * d[..., None, None] + s, h

    h_final, h_prev = lax.scan(step, h_init, (states, chunk_decay))
    y_off = jnp.einsum('bclgn,cbgrpn,bgrcl->bclgrp', cc, h_prev, jnp.exp(a_cs))
    y = (y_diag + y_off).reshape(bsz, n_c * q, SSD_HEADS, SSD_HEAD_DIM)[:, :L]
    return y, h_final.reshape(bsz, SSD_HEADS, SSD_HEAD_DIM, SSD_STATE)


def _mem_attention(q, mem_k, mem_v):
    bsz, L, _ = q.shape
    qh = q.reshape(bsz, L, X_HEADS, X_HEAD_DIM)
    s = jnp.einsum('blhd,bmhd->bhlm', qh, mem_k.astype(q.dtype)).astype(jnp.float32) * (X_HEAD_DIM ** -0.5)
    pr = jax.nn.softmax(s, axis=-1).astype(q.dtype)
    o = jnp.einsum('bhlm,bmhd->blhd', pr, mem_v.astype(q.dtype))
    return o.reshape(bsz, L, D_X)


def _mem_kv(mem, mem_norm_g, w_mem_kv):
    bsz = mem.shape[0]
    kv = _rmsnorm(mem, mem_norm_g) @ w_mem_kv
    k, v = jnp.split(kv, 2, axis=-1)
    return (k.reshape(bsz, N_MEM, X_HEADS, X_HEAD_DIM), v.reshape(bsz, N_MEM, X_HEADS, X_HEAD_DIM))


def _layer(x, mem_k, mem_v, h0, conv0, norm_g, w_in, conv_w, conv_b, dt_bias, a_log, d_skip, ssd_norm_g,
           ln_v_g, ln_v_b, w_spatial, b_spatial, w_proj_a, w_proj_b, w_proj_x, w_out):
    bsz, L, _ = x.shape
    hn = _rmsnorm(x, norm_g)
    proj = hn @ w_in
    u, v, gate_a, z, xbc, dt_raw, q, gate_x, merge = jnp.split(proj, _split_points(IN_SIZES), axis=-1)
    u = jax.nn.gelu(u, approximate=False)
    v = _layernorm(jax.nn.gelu(v, approximate=False), ln_v_g, ln_v_b)
    h_a = u * _gmlp_spatial(v, w_spatial, b_spatial) * jax.nn.silu(gate_a)
    xbc, conv_new = _causal_conv(xbc, conv0, conv_w, conv_b)
    xs, bm, cm = jnp.split(xbc, [D_INNER, D_INNER + SSD_GROUPS * SSD_STATE], axis=-1)
    dt = jax.nn.softplus(dt_raw.astype(jnp.float32) + dt_bias.astype(jnp.float32))
    a = -jnp.exp(a_log.astype(jnp.float32))
    xh = xs.reshape(bsz, L, SSD_HEADS, SSD_HEAD_DIM)
    y, h_new = _ssd(xh, dt, a,
                    bm.reshape(bsz, L, SSD_GROUPS, SSD_STATE),
                    cm.reshape(bsz, L, SSD_GROUPS, SSD_STATE), h0)
    y = y + xh.astype(jnp.float32) * d_skip.astype(jnp.float32)[:, None]
    h_b = _gated_group_rmsnorm(y.reshape(bsz, L, D_INNER).astype(x.dtype), z, ssd_norm_g)
    h_x = _mem_attention(q, mem_k, mem_v) * jax.nn.silu(gate_x)
    g_a, g_b, g_x = jnp.split(jax.nn.sigmoid(merge), N_BRANCH, axis=-1)
    m = g_a * (h_a @ w_proj_a) + g_b * (h_b @ w_proj_b) + g_x * (h_x @ w_proj_x)
    return x + m @ w_out, h_new.astype(h0.dtype), conv_new, v


def setup_inputs(seed: int = 0) -> dict:
    key = jax.random.key(seed)
    ks = jax.random.split(key, 32)
    f32 = jnp.float32
    nrm = lambda k, shape, s: jax.random.normal(k, shape, f32) * s
    Dp = DEPTH
    dt0 = jnp.exp(jax.random.uniform(ks[10], (Dp, SSD_HEADS), f32) * (math.log(0.1) - math.log(0.001)) + math.log(0.001))
    return {
        "x_prompt": nrm(ks[0], (BATCH, SEQ, D_MODEL), 1.0),
        "x_sample": nrm(ks[1], (DEC_BATCH, DEC_SEQ, D_MODEL), 1.0),
        "mem_prompt": nrm(ks[2], (BATCH, N_MEM, D_MODEL), 1.0),
        "cache_mem_k": nrm(ks[3], (Dp, DEC_BATCH, N_MEM, X_HEADS, X_HEAD_DIM), 1.0),
        "cache_mem_v": nrm(ks[4], (Dp, DEC_BATCH, N_MEM, X_HEADS, X_HEAD_DIM), 1.0),
        "state_ssm": nrm(ks[5], (Dp, DEC_BATCH, SSD_HEADS, SSD_HEAD_DIM, SSD_STATE), 0.1),
        "state_conv": nrm(ks[6], (Dp, DEC_BATCH, CONV_W - 1, CONV_DIM), 1.0),
        "norm_g": 1.0 + nrm(ks[7], (Dp, D_MODEL), 0.02),
        "w_in": nrm(ks[8], (Dp, D_MODEL, IN_DIM), D_MODEL ** -0.5),
        "conv_w": nrm(ks[9], (Dp, CONV_W, CONV_DIM), CONV_W ** -0.5),
        "conv_b": nrm(ks[11], (Dp, CONV_DIM), 0.02),
        "dt_bias": dt0 + jnp.log(-jnp.expm1(-dt0)),
        "a_log": jnp.log(jax.random.uniform(ks[12], (Dp, SSD_HEADS), f32, 1.0, 16.0)),
        "d_skip": 1.0 + nrm(ks[13], (Dp, SSD_HEADS), 0.1),
        "ssd_norm_g": 1.0 + nrm(ks[14], (Dp, D_INNER), 0.02),
        "ln_v_g": 1.0 + nrm(ks[15], (Dp, D_A), 0.02),
        "ln_v_b": nrm(ks[16], (Dp, D_A), 0.02),
        "w_spatial": nrm(ks[17], (Dp, A_GROUPS, CHUNK, CHUNK), CHUNK ** -0.5),
        "b_spatial": nrm(ks[18], (Dp, A_GROUPS, CHUNK), 0.02),
        "mem_norm_g": 1.0 + nrm(ks[19], (Dp, D_MODEL), 0.02),
        "w_mem_kv": nrm(ks[20], (Dp, D_MODEL, 2 * D_X), D_MODEL ** -0.5),
        "w_proj_a": nrm(ks[21], (Dp, D_A, D_MODEL), D_A ** -0.5),
        "w_proj_b": nrm(ks[22], (Dp, D_INNER, D_MODEL), D_INNER ** -0.5),
        "w_proj_x": nrm(ks[23], (Dp, D_X, D_MODEL), D_X ** -0.5),
        "w_out": nrm(ks[24], (Dp, D_MODEL, D_MODEL), D_MODEL ** -0.5),
        "final_norm_g": 1.0 + nrm(ks[25], (D_MODEL,), 0.02),
    }


def reference(x_prompt, x_sample, mem_prompt, cache_mem_k, cache_mem_v, state_ssm, state_conv, norm_g, w_in,
              conv_w, conv_b, dt_bias, a_log, d_skip, ssd_norm_g, ln_v_g, ln_v_b, w_spatial, b_spatial,
              mem_norm_g, w_mem_kv, w_proj_a, w_proj_b, w_proj_x, w_out, final_norm_g):
    yp, ys = x_prompt, x_sample
    bsz = x_prompt.shape[0]
    mk_l, mv_l, hp_l, cp_l, hs_l, cs_l, vs_l = [], [], [], [], [], [], []
    for l in range(DEPTH):
        lw = (norm_g[l], w_in[l], conv_w[l], conv_b[l], dt_bias[l], a_log[l], d_skip[l], ssd_norm_g[l],
              ln_v_g[l], ln_v_b[l], w_spatial[l], b_spatial[l], w_proj_a[l], w_proj_b[l], w_proj_x[l], w_out[l])
        mk, mv = _mem_kv(mem_prompt, mem_norm_g[l], w_mem_kv[l])
        h0 = jnp.zeros((bsz, SSD_HEADS, SSD_HEAD_DIM, SSD_STATE), state_ssm.dtype)
        c0 = jnp.zeros((bsz, CONV_W - 1, CONV_DIM), x_prompt.dtype)
        yp, hp, cp, _ = _layer(yp, mk, mv, h0, c0, *lw)
        ys, hs, cs, vs = _layer(ys, cache_mem_k[l], cache_mem_v[l], state_ssm[l], state_conv[l], *lw)
        mk_l.append(mk); mv_l.append(mv); hp_l.append(hp); cp_l.append(cp)
        hs_l.append(hs); cs_l.append(cs); vs_l.append(vs)
    y_prompt = _rmsnorm(yp, final_norm_g)
    y_sample = _rmsnorm(ys, final_norm_g)
    return (y_prompt, y_sample, jnp.stack(mk_l), jnp.stack(mv_l), jnp.stack(hp_l), jnp.stack(cp_l),
            jnp.stack(hs_l), jnp.stack(cs_l), jnp.stack(vs_l))
```

```python
import functools
import math

import jax
import jax.numpy as jnp
import numpy as np
from jax import lax
from jax.experimental import pallas as pl
from jax.experimental.pallas import tpu as pltpu

F32 = jnp.float32
BF16 = jnp.bfloat16
EPS = 1e-6
SQRT_HALF = math.sqrt(0.5)

LANES = 128
SUBLANES = 8
VMEM_LIMIT_BYTES = 56 * 1024 * 1024

CHUNK = 128
A_GROUPS = 8
SSD_HEAD_DIM = 64
SSD_GROUPS = 8
SSD_STATE = 128
X_HEAD_DIM = 128
N_BRANCH = 3
DT_REPLICAS = 3


def _dot(a, b):
    return jnp.dot(a, b, preferred_element_type=F32)


def _dot_nt(a, b):
    return lax.dot_general(a, b, (((1,), (1,)), ((), ())), preferred_element_type=F32)


def _dot_tn(a, b):
    return lax.dot_general(a, b, (((0,), (0,)), ((), ())), preferred_element_type=F32)


def _rmsnorm(x, g):
    return x * lax.rsqrt(jnp.mean(x * x, axis=-1, keepdims=True) + EPS) * g


def _gelu(x):
    return 0.5 * x * (1.0 + lax.erf(x * SQRT_HALF))


def _silu(x):
    return x * jax.nn.sigmoid(x)


def _bf16_parts(x):
    hi = x.astype(BF16)
    r1 = x - hi.astype(F32)
    mid = r1.astype(BF16)
    lo = (r1 - mid.astype(F32)).astype(BF16)
    return hi, mid, lo


def _split_by_replica(x, n_heads):
    hi, mid, lo = _bf16_parts(x)
    lane = lax.broadcasted_iota(jnp.int32, x.shape, 1)
    return jnp.where(lane < n_heads, hi, jnp.where(lane < 2 * n_heads, mid, lo))


def _cumsum_rows(tri, x):
    hi, mid, lo = _bf16_parts(x)
    return _dot(tri, hi) + _dot(tri, mid) + _dot(tri, lo)


def _memkv_kernel(x_ref, g_ref, w_ref, k_ref, v_ref):
    hn = _rmsnorm(x_ref[...], g_ref[...]).astype(BF16)
    kv = _dot(hn, w_ref[...])
    dx = k_ref.shape[-1]
    k_ref[...] = kv[:, :dx]
    v_ref[...] = kv[:, dx:]


def _const_spec():
    return pl.BlockSpec(memory_space=pltpu.VMEM)


def _params(n_grid):
    return pltpu.CompilerParams(dimension_semantics=("arbitrary",) * n_grid, vmem_limit_bytes=VMEM_LIMIT_BYTES)


def _mem_kv(mem2d, g, w, tm):
    rows, d = mem2d.shape
    dx = w.shape[1] // 2
    return pl.pallas_call(
        _memkv_kernel,
        out_shape=(jax.ShapeDtypeStruct((rows, dx), F32), jax.ShapeDtypeStruct((rows, dx), F32)),
        grid=(rows // tm,),
        in_specs=[pl.BlockSpec((tm, d), lambda i: (i, 0)), _const_spec(), _const_spec()],
        out_specs=(pl.BlockSpec((tm, dx), lambda i: (i, 0)), pl.BlockSpec((tm, dx), lambda i: (i, 0))),
        compiler_params=_params(1),
        name="mem_kv",
    )(mem2d, g, w)


def _x_kernel(x_ref, g_ref, wq_ref, wgx_ref, wmx_ref, k_ref, v_ref, wpx_ref, m_ref, *, nb, rows, head_dim):
    d = x_ref.shape[-1]
    x = x_ref[...].reshape(nb * rows, d)
    hn = _rmsnorm(x, g_ref[...]).astype(BF16)
    q = _dot(hn, wq_ref[...])
    gate = _silu(_dot(hn, wgx_ref[...]))
    gm = jax.nn.sigmoid(_dot(hn, wmx_ref[...]))
    n_heads = q.shape[1] // head_dim
    scale = head_dim ** -0.5
    outs = []
    for b in range(nb):
        kb = k_ref[b].astype(BF16)
        vb = v_ref[b].astype(BF16)
        qb = q[b * rows:(b + 1) * rows].astype(BF16)
        heads = []
        for h in range(n_heads):
            hs = slice(h * head_dim, (h + 1) * head_dim)
            s = _dot_nt(qb[:, hs], kb[:, hs]) * scale
            e = jnp.exp(s - jnp.max(s, axis=-1, keepdims=True))
            p = e / jnp.sum(e, axis=-1, keepdims=True)
            heads.append(_dot(p.astype(BF16), vb[:, hs]))
        outs.append(jnp.concatenate(heads, axis=-1))
    o = outs[0] if nb == 1 else jnp.concatenate(outs, axis=0)
    hx = (o * gate).astype(BF16)
    m = gm * _dot(hx, wpx_ref[...])
    m_ref[...] = m.reshape(nb, rows, d)


def _branch_x(x3, g, wq, wgx, wmx, k3, v3, wpx, *, nb, rows):
    bsz, seq, d = x3.shape
    n_mem, dx = k3.shape[1], k3.shape[2]
    kern = functools.partial(_x_kernel, nb=nb, rows=rows, head_dim=X_HEAD_DIM)
    return pl.pallas_call(
        kern,
        out_shape=jax.ShapeDtypeStruct((bsz, seq, d), F32),
        grid=(bsz // nb, seq // rows),
        in_specs=[
            pl.BlockSpec((nb, rows, d), lambda i, j: (i, j, 0)),
            _const_spec(), _const_spec(), _const_spec(), _const_spec(),
            pl.BlockSpec((nb, n_mem, dx), lambda i, j: (i, 0, 0)),
            pl.BlockSpec((nb, n_mem, dx), lambda i, j: (i, 0, 0)),
            _const_spec(),
        ],
        out_specs=pl.BlockSpec((nb, rows, d), lambda i, j: (i, j, 0)),
        compiler_params=_params(2),
        name="branch_x",
    )(x3, g, wq, wgx, wmx, k3, v3, wpx)


def _a_kernel(x_ref, g_ref, wu_ref, wv_ref, wga_ref, wma_ref, lng_ref, lnb_ref, ws_ref, bs_ref, wpa_ref, min_ref,
              mout_ref, *maybe_v_ref, chunk, groups):
    x = x_ref[...]
    tm, d = x.shape
    gd = d // groups
    hn = _rmsnorm(x, g_ref[...]).astype(BF16)
    u = _gelu(_dot(hn, wu_ref[...]))
    v = _gelu(_dot(hn, wv_ref[...]))
    vc = v - jnp.mean(v, axis=-1, keepdims=True)
    vn = vc * lax.rsqrt(jnp.mean(vc * vc, axis=-1, keepdims=True) + EPS) * lng_ref[...] + lnb_ref[...]
    if maybe_v_ref:
        maybe_v_ref[0][...] = vn
    ga = _silu(_dot(hn, wga_ref[...]))
    vb = vn.astype(BF16)
    bias = bs_ref[...]
    mixed = []
    for c in range(tm // chunk):
        rs = slice(c * chunk, (c + 1) * chunk)
        cols = [_dot(ws_ref[gi], vb[rs, gi * gd:(gi + 1) * gd]) for gi in range(groups)]
        mixed.append(jnp.concatenate(cols, axis=-1) + bias)
    s = mixed[0] if len(mixed) == 1 else jnp.concatenate(mixed, axis=0)
    ha = (u * s * ga).astype(BF16)
    gm = jax.nn.sigmoid(_dot(hn, wma_ref[...]))
    mout_ref[...] = min_ref[...] + gm * _dot(ha, wpa_ref[...])


def _branch_a(x2, g, wu, wv, wga, wma, lng, lnb, ws, bs, wpa, m_in, *, tm, want_v):
    rows, d = x2.shape
    row_spec = pl.BlockSpec((tm, d), lambda i: (i, 0))
    out_shape = [jax.ShapeDtypeStruct((rows, d), F32)]
    out_specs = [row_spec]
    if want_v:
        out_shape.append(jax.ShapeDtypeStruct((rows, d), F32))
        out_specs.append(row_spec)
    kern = functools.partial(_a_kernel, chunk=CHUNK, groups=A_GROUPS)
    res = pl.pallas_call(
        kern,
        out_shape=tuple(out_shape),
        grid=(rows // tm,),
        in_specs=[row_spec] + [_const_spec()] * 10 + [row_spec],
        out_specs=tuple(out_specs),
        compiler_params=_params(1),
        name="branch_a",
    )(x2, g, wu, wv, wga, wma, lng, lnb, ws, bs, wpa, m_in)
    return res if want_v else (res[0], None)


def _in_proj_b(x, g_ref, wz_ref, wxbc_ref, wdt_ref, wmb_ref, dtb_ref):
    hn = _rmsnorm(x, g_ref[...]).astype(BF16)
    z = _dot(hn, wz_ref[...])
    xbc = _dot(hn, wxbc_ref[...])
    dt = jax.nn.softplus(_dot(hn, wdt_ref[...]) + dtb_ref[...])
    gm = jax.nn.sigmoid(_dot(hn, wmb_ref[...]))
    return z, xbc, dt, gm


def _finish_b(x, y, z, gm, ng_ref, wpb_ref, m_in, wout_ref, fg_ref, *, n_groups, final_norm):
    yf = y * _silu(z)
    gw = yf.shape[1] // n_groups
    parts = []
    for gi in range(n_groups):
        yg = yf[:, gi * gw:(gi + 1) * gw]
        parts.append(yg * lax.rsqrt(jnp.mean(yg * yg, axis=-1, keepdims=True) + EPS))
    hb = (jnp.concatenate(parts, axis=-1) * ng_ref[...]).astype(BF16)
    m = m_in + gm * _dot(hb, wpb_ref[...])
    out = x + _dot(m.astype(BF16), wout_ref[...])
    return _rmsnorm(out, fg_ref[...]) if final_norm else out


def _bp_kernel(x_ref, g_ref, wz_ref, wxbc_ref, wdt_ref, wmb_ref, cw_ref, cb_ref, dtb_ref, alog_ref, dsk_ref,
               ng_ref, wpb_ref, min_ref, wout_ref, fg_ref, tri_ref, e3_ref,
               y_ref, ssm_ref, conv_ref, xs_scr, st_scr,
               *, tm, chunk, n_groups, n_heads, head_dim, n_state, conv_w, final_norm):
    j = pl.program_id(1)
    last = pl.num_programs(1) - 1
    hpg = n_heads // n_groups
    gw = hpg * head_dim
    d_inner = n_heads * head_dim
    pad = SUBLANES

    @pl.when(j == 0)
    def _init():
        st_scr[...] = jnp.zeros_like(st_scr)
        xs_scr[0:pad, :] = jnp.zeros((pad, xs_scr.shape[1]), F32)

    x = x_ref[0]
    z, xbc_raw, dt, gm = _in_proj_b(x, g_ref, wz_ref, wxbc_ref, wdt_ref, wmb_ref, dtb_ref)

    xs_scr[pad:pad + tm, :] = xbc_raw
    conv = cb_ref[...]
    for k in range(conv_w):
        conv = conv + xs_scr[pl.ds(pad - (conv_w - 1) + k, tm), :] * cw_ref[k:k + 1, :]
    tail = xs_scr[pl.ds(pad + tm - (conv_w - 1), conv_w - 1), :]
    xs_scr[pad - (conv_w - 1):pad, :] = tail

    @pl.when(j == last)
    def _conv_out():
        conv_ref[0] = tail

    xbc = _silu(conv)
    xs = xbc[:, :d_inner]
    bm = xbc[:, d_inner:d_inner + n_groups * n_state].astype(BF16)
    cm = xbc[:, d_inner + n_groups * n_state:].astype(BF16)

    a_row = -jnp.exp(alog_ref[...])
    adt = dt * a_row
    e3 = e3_ref[...]
    xdt = xs * _dot(_split_by_replica(dt, n_heads).astype(BF16), e3)

    li = lax.broadcasted_iota(jnp.int32, (chunk, chunk), 0)
    si = lax.broadcasted_iota(jnp.int32, (chunk, chunk), 1)
    causal = li >= si
    head_of_lane = lax.broadcasted_iota(jnp.int32, (chunk, gw), 1) // head_dim
    tri = tri_ref[...]

    ys = []
    for c in range(tm // chunk):
        rs = slice(c * chunk, (c + 1) * chunk)
        acs = _cumsum_rows(tri, adt[rs])
        acs_t = acs.T
        acs_e = _dot(_split_by_replica(acs, n_heads).astype(BF16), e3)
        a_last = acs_e[chunk - 1:chunk, :]
        xdt_c = xdt[rs]
        xd_state = (xdt_c * jnp.exp(a_last - acs_e)).astype(BF16)
        decay_in = jnp.exp(acs_e)
        st_prev = st_scr[...]
        st_prev_b = st_prev.astype(BF16)
        y_parts, st_parts = [], []
        for gi in range(n_groups):
            ns = slice(gi * n_state, (gi + 1) * n_state)
            gs = slice(gi * gw, (gi + 1) * gw)
            cg, bg = cm[rs, ns], bm[rs, ns]
            scores = _dot_nt(cg, bg)
            m_heads, x_blocks = [], []
            xg = xdt_c[:, gs]
            for r in range(hpg):
                h = gi * hpg + r
                diff = acs[:, h:h + 1] - acs_t[h:h + 1, :]
                decay = jnp.exp(jnp.where(causal, diff, -jnp.inf))
                m_heads.append((scores * decay).astype(BF16))
                x_blocks.append(jnp.where(head_of_lane == r, xg, 0.0).astype(BF16))
            y_diag = _dot(jnp.concatenate(m_heads, axis=1), jnp.concatenate(x_blocks, axis=0))
            y_off = _dot(cg, st_prev_b[:, gs])
            y_parts.append(y_diag + y_off * decay_in[:, gs])
            st_parts.append(_dot_tn(bg, xd_state[:, gs]))
        st_scr[...] = st_prev * jnp.exp(a_last) + jnp.concatenate(st_parts, axis=1)
        ys.append(jnp.concatenate(y_parts, axis=1))
    y = ys[0] if len(ys) == 1 else jnp.concatenate(ys, axis=0)
    y = y + xs * dsk_ref[...]

    y_ref[0] = _finish_b(x, y, z, gm, ng_ref, wpb_ref, min_ref[0], wout_ref, fg_ref,
                         n_groups=n_groups, final_norm=final_norm)

    @pl.when(j == last)
    def _state_out():
        ssm_ref[0] = st_scr[...].T


def _branch_b_prompt(x3, m3, w, *, tm, final_norm):
    bsz, seq, d = x3.shape
    conv_dim = w["wxbc"].shape[1]
    d_inner = w["wz"].shape[1]
    n_heads = d_inner // SSD_HEAD_DIM
    conv_w = w["cw"].shape[0]
    kern = functools.partial(_bp_kernel, tm=tm, chunk=CHUNK, n_groups=SSD_GROUPS, n_heads=n_heads,
                             head_dim=SSD_HEAD_DIM, n_state=SSD_STATE, conv_w=conv_w, final_norm=final_norm)
    row_spec = pl.BlockSpec((1, tm, d), lambda b, j: (b, j, 0))
    consts = [w[k] for k in ("g", "wz", "wxbc", "wdt", "wmb", "cw", "cb", "dtb", "alog", "dsk", "ng", "wpb")]
    tail_consts = [w[k] for k in ("wout", "fg", "tri", "e3")]
    return pl.pallas_call(
        kern,
        out_shape=(jax.ShapeDtypeStruct((bsz, seq, d), F32),
                   jax.ShapeDtypeStruct((bsz, d_inner, SSD_STATE), F32),
                   jax.ShapeDtypeStruct((bsz, conv_w - 1, conv_dim), F32)),
        grid=(bsz, seq // tm),
        in_specs=[row_spec] + [_const_spec()] * len(consts) + [row_spec] + [_const_spec()] * len(tail_consts),
        out_specs=(row_spec,
                   pl.BlockSpec((1, d_inner, SSD_STATE), lambda b, j: (b, 0, 0)),
                   pl.BlockSpec((1, conv_w - 1, conv_dim), lambda b, j: (b, 0, 0))),
        scratch_shapes=[pltpu.VMEM((SUBLANES + tm, conv_dim), F32), pltpu.VMEM((SSD_STATE, d_inner), F32)],
        compiler_params=_params(2),
        name="branch_b_prompt",
    )(x3, *consts, m3, *tail_consts)


def _bs_kernel(x_ref, g_ref, wz_ref, wxbc_ref, wdt_ref, wmb_ref, cw_ref, cb_ref, dtb_ref, alog_ref, dsk_ref,
               ng_ref, wpb_ref, min_ref, wout_ref, fg_ref, tri_ref, e3_ref, ecol_ref, conv0_ref, h0_ref,
               y_ref, hnew_ref, convnew_ref, xs_scr, tr_scr,
               *, nb, rows, n_groups, n_heads, head_dim, n_state, conv_w, final_norm):
    hpg = n_heads // n_groups
    gw = hpg * head_dim
    d_inner = n_heads * head_dim
    d = x_ref.shape[-1]
    r_all = nb * rows
    pad = SUBLANES
    hs_w = n_heads * rows

    @pl.when(pl.program_id(0) == 0)
    def _init():
        tr_scr[...] = jnp.zeros_like(tr_scr)

    x = x_ref[...].reshape(r_all, d)
    z, xbc_raw, dt, gm = _in_proj_b(x, g_ref, wz_ref, wxbc_ref, wdt_ref, wmb_ref, dtb_ref)

    convs = []
    for b in range(nb):
        xs_scr[b, pad - (conv_w - 1):pad, :] = conv0_ref[b]
        xs_scr[b, pad:pad + rows, :] = xbc_raw[b * rows:(b + 1) * rows]
        conv = cb_ref[...]
        for k in range(conv_w):
            conv = conv + xs_scr[b, pl.ds(pad - (conv_w - 1) + k, rows), :] * cw_ref[k:k + 1, :]
        convs.append(conv)
        convnew_ref[b] = xs_scr[b, pl.ds(pad + rows - (conv_w - 1), conv_w - 1), :]
    xbc = _silu(jnp.concatenate(convs, axis=0))
    xs = xbc[:, :d_inner]
    bm = xbc[:, d_inner:d_inner + n_groups * n_state]
    cm = xbc[:, d_inner + n_groups * n_state:]

    a_row = -jnp.exp(alog_ref[...])
    acs = _cumsum_rows(tri_ref[...], dt * a_row)
    e3 = e3_ref[...]
    acs_split = _split_by_replica(acs, n_heads).astype(BF16)
    acs_e = _dot(acs_split, e3)
    xdt = xs * _dot(_split_by_replica(dt, n_heads).astype(BF16), e3)
    decay_in = jnp.exp(acs_e)

    acs_col = _dot(acs_split, ecol_ref[...])
    l_idx = lax.broadcasted_iota(jnp.int32, (r_all, hs_w), 0) % rows
    s_idx = lax.broadcasted_iota(jnp.int32, (r_all, hs_w), 1) % rows
    on_diag = jnp.where(l_idx == s_idx, acs_col, 0.0).reshape(nb, rows, hs_w)
    acs_row = jnp.broadcast_to(jnp.sum(on_diag, axis=1, keepdims=True), (nb, rows, hs_w)).reshape(r_all, hs_w)
    decay = jnp.exp(jnp.where(l_idx >= s_idx, acs_col - acs_row, -jnp.inf))

    grp_of_row = lax.broadcasted_iota(jnp.int32, (hs_w, n_groups * n_state), 0) // (rows * hpg)
    grp_of_lane = lax.broadcasted_iota(jnp.int32, (hs_w, n_groups * n_state), 1) // n_state
    head_of_row = lax.broadcasted_iota(jnp.int32, (hs_w, d_inner), 0) // rows
    head_of_lane = lax.broadcasted_iota(jnp.int32, (hs_w, d_inner), 1) // head_dim
    ones_rows = lax.broadcasted_iota(jnp.int32, (LANES, n_state), 0)
    ones_blk = jnp.where((ones_rows >= rows) & (ones_rows < rows + 3), 1.0, 0.0)

    ys = []
    for b in range(nb):
        rs = slice(b * rows, (b + 1) * rows)
        bm_b, cm_b = bm[rs], cm[rs]
        cm_bb = cm_b.astype(BF16)
        b_rep = jnp.broadcast_to(bm_b[None], (n_heads, rows, bm_b.shape[1])).reshape(hs_w, bm_b.shape[1])
        b_exp = jnp.where(grp_of_row == grp_of_lane, b_rep, 0.0).astype(BF16)
        scores = _dot_nt(cm_bb, b_exp)
        m_all = (scores * decay[rs]).astype(BF16)
        xdt_b = xdt[rs]
        x_rep = jnp.broadcast_to(xdt_b[None], (n_heads, rows, d_inner)).reshape(hs_w, d_inner)
        x_blk = jnp.where(head_of_row == head_of_lane, x_rep, 0.0).astype(BF16)
        y_diag = _dot(m_all, x_blk)

        a_last = acs_e[(b + 1) * rows - 1:(b + 1) * rows, :]
        xd_state = xdt_b * jnp.exp(a_last - acs_e[rs])
        d_hi, d_mid, d_lo = _bf16_parts(jnp.exp(a_last))
        tr_scr[0:rows, :] = xd_state
        tr_scr[rows:rows + 1, :] = d_hi.astype(F32)
        tr_scr[rows + 1:rows + 2, :] = d_mid.astype(F32)
        tr_scr[rows + 2:rows + 3, :] = d_lo.astype(F32)
        tr_t = tr_scr[...].T.astype(BF16)

        y_parts = []
        for gi in range(n_groups):
            ns = slice(gi * n_state, (gi + 1) * n_state)
            gs = slice(gi * gw, (gi + 1) * gw)
            h0_g = h0_ref[b, gs, :]
            y_parts.append(_dot_nt(cm_bb[:, ns], h0_g.astype(BF16)))
            b_pad = jnp.concatenate([bm_b[:, ns], jnp.zeros((LANES - rows, n_state), F32)], axis=0)
            upd = _dot(tr_t[gs, :], jnp.concatenate([b_pad, ones_blk], axis=1).astype(BF16))
            hnew_ref[b, gs, :] = h0_g * upd[:, n_state:] + upd[:, :n_state]
        y_off = jnp.concatenate(y_parts, axis=1)
        ys.append(y_diag + y_off * decay_in[rs])
    y = jnp.concatenate(ys, axis=0) + xs * dsk_ref[...]

    m_in = min_ref[...].reshape(r_all, d)
    out = _finish_b(x, y, z, gm, ng_ref, wpb_ref, m_in, wout_ref, fg_ref, n_groups=n_groups, final_norm=final_norm)
    y_ref[...] = out.reshape(nb, rows, d)


def _branch_b_sample(x3, m3, conv0, h0, w, *, nb, final_norm):
    bsz, rows, d = x3.shape
    conv_dim = w["wxbc"].shape[1]
    d_inner = w["wz"].shape[1]
    n_heads = d_inner // SSD_HEAD_DIM
    conv_w = w["cw"].shape[0]
    kern = functools.partial(_bs_kernel, nb=nb, rows=rows, n_groups=SSD_GROUPS, n_heads=n_heads,
                             head_dim=SSD_HEAD_DIM, n_state=SSD_STATE, conv_w=conv_w, final_norm=final_norm)
    row_spec = pl.BlockSpec((nb, rows, d), lambda i: (i, 0, 0))
    conv_spec = pl.BlockSpec((nb, conv_w - 1, conv_dim), lambda i: (i, 0, 0))
    h_spec = pl.BlockSpec((nb, d_inner, SSD_STATE), lambda i: (i, 0, 0))
    consts = [w[k] for k in ("g", "wz", "wxbc", "wdt", "wmb", "cw", "cb", "dtb", "alog", "dsk", "ng", "wpb")]
    tail_consts = [w[k] for k in ("wout", "fg", "tri_s", "e3", "ecol")]
    return pl.pallas_call(
        kern,
        out_shape=(jax.ShapeDtypeStruct((bsz, rows, d), F32),
                   jax.ShapeDtypeStruct((bsz, d_inner, SSD_STATE), F32),
                   jax.ShapeDtypeStruct((bsz, conv_w - 1, conv_dim), F32)),
        grid=(bsz // nb,),
        in_specs=([row_spec] + [_const_spec()] * len(consts) + [row_spec] + [_const_spec()] * len(tail_consts)
                  + [conv_spec, h_spec]),
        out_specs=(row_spec, h_spec, conv_spec),
        scratch_shapes=[pltpu.VMEM((nb, 2 * SUBLANES, conv_dim), F32), pltpu.VMEM((LANES, d_inner), F32)],
        compiler_params=_params(1),
        name="branch_b_sample",
    )(x3, *consts, m3, *tail_consts, conv0, h0)


def _tile(n, pref):
    return pref if n % pref == 0 else n


def _head_expand(n_heads, width, n_rep):
    j = np.arange(LANES)[:, None]
    c = np.arange(n_heads * width)[None, :]
    return jnp.asarray(((j < n_rep * n_heads) & (j % n_heads == c // width)).astype(np.float32), dtype=BF16)


def _layer_weights(l, d, dec_rows, nb_s, norm_g, w_in, conv_w, conv_b, dt_bias, a_log, d_skip, ssd_norm_g, ln_v_g,
                   ln_v_b, w_spatial, b_spatial, w_proj_a, w_proj_b, w_proj_x, w_out, final_norm_g):
    d_a = d
    d_inner = w_proj_b.shape[1]
    conv_dim = conv_w.shape[2]
    n_heads = a_log.shape[1]
    d_x = w_proj_x.shape[1]
    sizes = (d_a, d_a, d_a, d_inner, conv_dim, n_heads, d_x, d_x, N_BRANCH * d)
    offs = np.concatenate([[0], np.cumsum(sizes)])
    wl = w_in[l]
    sec = lambda i: wl[:, offs[i]:offs[i + 1]]
    merge = sec(8)
    row = lambda v: v.reshape(1, -1).astype(F32)
    rep3 = lambda v, fill: jnp.concatenate(
        [v] * DT_REPLICAS + [jnp.full(v.shape[:-1] + (LANES - DT_REPLICAS * v.shape[-1],), fill, v.dtype)], axis=-1)

    tril = jnp.tril(jnp.ones((CHUNK, CHUNK), F32))
    ws_p = jnp.where(tril[None] > 0, w_spatial[l], 0.0)
    bs_p = jnp.repeat(b_spatial[l].T, d_a // A_GROUPS, axis=1)
    n_seq = CHUNK // dec_rows
    eye = jnp.eye(n_seq, dtype=F32)
    ws_s = jnp.stack([jnp.kron(eye, ws_p[gi, :dec_rows, :dec_rows]) for gi in range(A_GROUPS)])
    bs_s = jnp.tile(bs_p[:dec_rows], (n_seq, 1))

    w = {
        "g": row(norm_g[l]),
        "wu": sec(0).astype(BF16), "wv": sec(1).astype(BF16), "wga": sec(2).astype(BF16),
        "wma": merge[:, :d].astype(BF16),
        "lng": row(ln_v_g[l]), "lnb": row(ln_v_b[l]),
        "ws_p": ws_p.astype(BF16), "bs_p": bs_p, "ws_s": ws_s.astype(BF16), "bs_s": bs_s,
        "wpa": w_proj_a[l].astype(BF16),
        "wz": sec(3).astype(BF16), "wxbc": sec(4).astype(BF16),
        "wdt": rep3(sec(5), 0.0).astype(BF16),
        "wmb": merge[:, d:2 * d].astype(BF16),
        "cw": conv_w[l].astype(F32), "cb": row(conv_b[l]),
        "dtb": rep3(row(dt_bias[l]), 0.0), "alog": rep3(row(a_log[l]), 0.0),
        "dsk": jnp.repeat(row(d_skip[l]), SSD_HEAD_DIM, axis=1),
        "ng": row(ssd_norm_g[l]),
        "wpb": w_proj_b[l].astype(BF16),
        "wout": w_out[l].astype(BF16), "fg": row(final_norm_g),
        "tri": tril.astype(BF16),
        "tri_s": jnp.kron(jnp.eye(nb_s, dtype=F32), jnp.tril(jnp.ones((dec_rows, dec_rows), F32))).astype(BF16),
        "e3": _head_expand(n_heads, SSD_HEAD_DIM, DT_REPLICAS),
        "ecol": _head_expand(n_heads, dec_rows, DT_REPLICAS),
        "wq": sec(6).astype(BF16), "wgx": sec(7).astype(BF16), "wmx": merge[:, 2 * d:].astype(BF16),
        "wpx": w_proj_x[l].astype(BF16),
    }
    return w


def _layer(x3, k3, v3, w, *, nb_x, rows_x, tm_a, ws, bs, want_v, b_fn):
    bsz, seq, d = x3.shape
    m = _branch_x(x3, w["g"], w["wq"], w["wgx"], w["wmx"], k3, v3, w["wpx"], nb=nb_x, rows=rows_x)
    m, vn = _branch_a(x3.reshape(bsz * seq, d), w["g"], w["wu"], w["wv"], w["wga"], w["wma"], w["lng"], w["lnb"],
                      ws, bs, w["wpa"], m.reshape(bsz * seq, d), tm=tm_a, want_v=want_v)
    return b_fn(x3, m.reshape(bsz, seq, d)), vn


def kernel(x_prompt, x_sample, mem_prompt, cache_mem_k, cache_mem_v, state_ssm, state_conv, norm_g, w_in, conv_w,
           conv_b, dt_bias, a_log, d_skip, ssd_norm_g, ln_v_g, ln_v_b, w_spatial, b_spatial, mem_norm_g, w_mem_kv,
           w_proj_a, w_proj_b, w_proj_x, w_out, final_norm_g):
    depth = w_in.shape[0]
    bsz, seq, d = x_prompt.shape
    dec_b, dec_rows, _ = x_sample.shape
    n_mem = mem_prompt.shape[1]
    d_x = w_proj_x.shape[1]
    d_inner = w_proj_b.shape[1]
    n_heads = a_log.shape[1]
    assert seq % CHUNK == 0 and CHUNK % dec_rows == 0 and dec_rows == SUBLANES
    assert DT_REPLICAS * n_heads <= LANES and n_heads * dec_rows % LANES == 0

    nb_s = _tile(dec_b, 4)
    nb_xs = _tile(dec_b, 8)
    yp, ys = x_prompt, x_sample
    outs = {k: [] for k in ("mk", "mv", "hp", "cp", "hs", "cs", "vs")}
    for l in range(depth):
        w = _layer_weights(l, d, dec_rows, nb_s, norm_g, w_in, conv_w, conv_b, dt_bias, a_log, d_skip, ssd_norm_g,
                           ln_v_g, ln_v_b, w_spatial, b_spatial, w_proj_a, w_proj_b, w_proj_x, w_out, final_norm_g)
        final_norm = l == depth - 1
        mk, mv = _mem_kv(mem_prompt.reshape(bsz * n_mem, d), mem_norm_g[l].reshape(1, d).astype(F32),
                         w_mem_kv[l].astype(BF16), _tile(bsz * n_mem, 512))
        (yp, hp, cp), _ = _layer(
            yp, mk.reshape(bsz, n_mem, d_x), mv.reshape(bsz, n_mem, d_x), w,
            nb_x=1, rows_x=_tile(seq, 512), tm_a=_tile(bsz * seq, 512), ws=w["ws_p"], bs=w["bs_p"], want_v=False,
            b_fn=functools.partial(_branch_b_prompt, w=w, tm=_tile(seq, 256), final_norm=final_norm))
        (ys, hs, cs), vs = _layer(
            ys, cache_mem_k[l].reshape(dec_b, n_mem, d_x), cache_mem_v[l].reshape(dec_b, n_mem, d_x), w,
            nb_x=nb_xs, rows_x=dec_rows, tm_a=_tile(dec_b * dec_rows, 512), ws=w["ws_s"], bs=w["bs_s"], want_v=True,
            b_fn=functools.partial(_branch_b_sample, conv0=state_conv[l],
                                   h0=state_ssm[l].reshape(dec_b, d_inner, SSD_STATE), w=w, nb=nb_s,
                                   final_norm=final_norm))
        outs["mk"].append(mk.reshape(bsz, n_mem, d_x // X_HEAD_DIM, X_HEAD_DIM))
        outs["mv"].append(mv.reshape(bsz, n_mem, d_x // X_HEAD_DIM, X_HEAD_DIM))
        outs["hp"].append(hp.reshape(bsz, n_heads, SSD_HEAD_DIM, SSD_STATE))
        outs["cp"].append(cp)
        outs["hs"].append(hs.reshape(dec_b, n_heads, SSD_HEAD_DIM, SSD_STATE))
        outs["cs"].append(cs)
        outs["vs"].append(vs.reshape(dec_b, dec_rows, d))
    st = lambda k: jnp.stack(outs[k])
    return (yp, ys, st("mk"), st("mv"), st("hp"), st("cp"), st("hs"), st("cs"), st("vs"))
```

```python
import functools
import math

import jax
import jax.numpy as jnp
import numpy as np
from jax import lax
from jax.experimental import pallas as pl
from jax.experimental.pallas import tpu as pltpu

F32 = jnp.float32
BF16 = jnp.bfloat16
EPS = 1e-6
SQRT_HALF = math.sqrt(0.5)

LANES = 128
SUBLANES = 8
VMEM_LIMIT_BYTES = 56 * 1024 * 1024

CHUNK = 128
A_GROUPS = 8
SSD_HEAD_DIM = 64
SSD_GROUPS = 8
SSD_STATE = 128
X_HEAD_DIM = 128
N_BRANCH = 3
DT_REPLICAS = 3


def _dot(a, b):
    return jnp.dot(a, b, preferred_element_type=F32)


def _dot_nt(a, b):
    return lax.dot_general(a, b, (((1,), (1,)), ((), ())), preferred_element_type=F32)


def _dot_tn(a, b):
    return lax.dot_general(a, b, (((0,), (0,)), ((), ())), preferred_element_type=F32)


def _rmsnorm(x, g):
    return x * lax.rsqrt(jnp.mean(x * x, axis=-1, keepdims=True) + EPS) * g


def _gelu(x):
    return 0.5 * x * (1.0 + lax.erf(x * SQRT_HALF))


def _silu(x):
    return x * jax.nn.sigmoid(x)


def _bf16_parts(x):
    hi = x.astype(BF16)
    r1 = x - hi.astype(F32)
    mid = r1.astype(BF16)
    lo = (r1 - mid.astype(F32)).astype(BF16)
    return hi, mid, lo


def _split_by_replica(x, n_heads):
    hi, mid, lo = _bf16_parts(x)
    lane = lax.broadcasted_iota(jnp.int32, x.shape, 1)
    return jnp.where(lane < n_heads, hi, jnp.where(lane < 2 * n_heads, mid, lo))


def _cumsum_rows(tri, x):
    hi, mid, lo = _bf16_parts(x)
    return _dot(tri, hi) + _dot(tri, mid) + _dot(tri, lo)


def _memkv_kernel(x_ref, g_ref, w_ref, k_ref, v_ref):
    hn = _rmsnorm(x_ref[...], g_ref[...]).astype(BF16)
    kv = _dot(hn, w_ref[...])
    dx = k_ref.shape[-1]
    k_ref[...] = kv[:, :dx]
    v_ref[...] = kv[:, dx:]


def _const_spec():
    return pl.BlockSpec(memory_space=pltpu.VMEM)


def _params(n_grid):
    return pltpu.CompilerParams(dimension_semantics=("arbitrary",) * n_grid, vmem_limit_bytes=VMEM_LIMIT_BYTES)


def _mem_kv(mem2d, g, w, tm):
    rows, d = mem2d.shape
    dx = w.shape[1] // 2
    return pl.pallas_call(
        _memkv_kernel,
        out_shape=(jax.ShapeDtypeStruct((rows, dx), F32), jax.ShapeDtypeStruct((rows, dx), F32)),
        grid=(rows // tm,),
        in_specs=[pl.BlockSpec((tm, d), lambda i: (i, 0)), _const_spec(), _const_spec()],
        out_specs=(pl.BlockSpec((tm, dx), lambda i: (i, 0)), pl.BlockSpec((tm, dx), lambda i: (i, 0))),
        compiler_params=_params(1),
        name="mem_kv",
    )(mem2d, g, w)


def _x_kernel(x_ref, g_ref, wq_ref, wgx_ref, wmx_ref, k_ref, v_ref, wpx_ref, m_ref, *, nb, rows, head_dim):
    d = x_ref.shape[-1]
    x = x_ref[...].reshape(nb * rows, d)
    hn = _rmsnorm(x, g_ref[...]).astype(BF16)
    q = _dot(hn, wq_ref[...])
    gate = _silu(_dot(hn, wgx_ref[...]))
    gm = jax.nn.sigmoid(_dot(hn, wmx_ref[...]))
    n_heads = q.shape[1] // head_dim
    scale = head_dim ** -0.5
    outs = []
    for b in range(nb):
        kb = k_ref[b].astype(BF16)
        vb = v_ref[b].astype(BF16)
        qb = q[b * rows:(b + 1) * rows].astype(BF16)
        heads = []
        for h in range(n_heads):
            hs = slice(h * head_dim, (h + 1) * head_dim)
            s = _dot_nt(qb[:, hs], kb[:, hs]) * scale
            e = jnp.exp(s - jnp.max(s, axis=-1, keepdims=True))
            p = e / jnp.sum(e, axis=-1, keepdims=True)
            heads.append(_dot(p.astype(BF16), vb[:, hs]))
        outs.append(jnp.concatenate(heads, axis=-1))
    o = outs[0] if nb == 1 else jnp.concatenate(outs, axis=0)
    hx = (o * gate).astype(BF16)
    m = gm * _dot(hx, wpx_ref[...])
    m_ref[...] = m.reshape(nb, rows, d)


def _xs_kernel(x_ref, g_ref, wq_ref, wgx_ref, wmx_ref, k_ref, v_ref, wpx_ref, m_ref, *, nb, rows, head_dim):
    d = x_ref.shape[-1]
    x = x_ref[...].reshape(nb * rows, d)
    hn = _rmsnorm(x, g_ref[...]).astype(BF16)
    q = _dot(hn, wq_ref[...])
    gate = _silu(_dot(hn, wgx_ref[...]))
    gm = jax.nn.sigmoid(_dot(hn, wmx_ref[...]))
    n_heads = q.shape[1] // head_dim
    n_kv = k_ref.shape[1]
    scale = head_dim ** -0.5
    row_head = lax.broadcasted_iota(jnp.int32, (n_heads * rows, n_kv), 0) // rows
    col_head = lax.broadcasted_iota(jnp.int32, (n_heads * rows, n_kv), 1) % n_heads
    same_head = row_head == col_head
    outs = []
    for b in range(nb):
        kb = k_ref[b].astype(BF16)
        vb = v_ref[b].astype(BF16)
        qb = q[b * rows:(b + 1) * rows]
        q_heads = jnp.concatenate([qb[:, h * head_dim:(h + 1) * head_dim] for h in range(n_heads)], axis=0)
        s = jnp.where(same_head, _dot_nt(q_heads.astype(BF16), kb) * scale, -jnp.inf)
        e = jnp.exp(s - jnp.max(s, axis=-1, keepdims=True))
        p = e / jnp.sum(e, axis=-1, keepdims=True)
        o_heads = _dot(p.astype(BF16), vb)
        outs.append(jnp.concatenate([o_heads[h * rows:(h + 1) * rows] for h in range(n_heads)], axis=1))
    o = outs[0] if nb == 1 else jnp.concatenate(outs, axis=0)
    hx = (o * gate).astype(BF16)
    m = gm * _dot(hx, wpx_ref[...])
    m_ref[...] = m.reshape(nb, rows, d)


def _branch_x(x3, g, wq, wgx, wmx, k3, v3, wpx, *, nb, rows, heads_in_rows):
    bsz, seq, d = x3.shape
    n_mem, dx = k3.shape[1], k3.shape[2]
    kern = functools.partial(_xs_kernel if heads_in_rows else _x_kernel, nb=nb, rows=rows, head_dim=X_HEAD_DIM)
    return pl.pallas_call(
        kern,
        out_shape=jax.ShapeDtypeStruct((bsz, seq, d), F32),
        grid=(bsz // nb, seq // rows),
        in_specs=[
            pl.BlockSpec((nb, rows, d), lambda i, j: (i, j, 0)),
            _const_spec(), _const_spec(), _const_spec(), _const_spec(),
            pl.BlockSpec((nb, n_mem, dx), lambda i, j: (i, 0, 0)),
            pl.BlockSpec((nb, n_mem, dx), lambda i, j: (i, 0, 0)),
            _const_spec(),
        ],
        out_specs=pl.BlockSpec((nb, rows, d), lambda i, j: (i, j, 0)),
        compiler_params=_params(2),
        name="branch_x",
    )(x3, g, wq, wgx, wmx, k3, v3, wpx)


def _a_kernel(x_ref, g_ref, wu_ref, wv_ref, wga_ref, wma_ref, lng_ref, lnb_ref, ws_ref, bs_ref, wpa_ref, min_ref,
              mout_ref, *maybe_v_ref, chunk, groups):
    x = x_ref[...]
    tm, d = x.shape
    gd = d // groups
    hn = _rmsnorm(x, g_ref[...]).astype(BF16)
    u = _gelu(_dot(hn, wu_ref[...]))
    v = _gelu(_dot(hn, wv_ref[...]))
    vc = v - jnp.mean(v, axis=-1, keepdims=True)
    vn = vc * lax.rsqrt(jnp.mean(vc * vc, axis=-1, keepdims=True) + EPS) * lng_ref[...] + lnb_ref[...]
    if maybe_v_ref:
        maybe_v_ref[0][...] = vn
    ga = _silu(_dot(hn, wga_ref[...]))
    vb = vn.astype(BF16)
    bias = bs_ref[...]
    mixed = []
    for c in range(tm // chunk):
        rs = slice(c * chunk, (c + 1) * chunk)
        cols = [_dot(ws_ref[gi], vb[rs, gi * gd:(gi + 1) * gd]) for gi in range(groups)]
        mixed.append(jnp.concatenate(cols, axis=-1) + bias)
    s = mixed[0] if len(mixed) == 1 else jnp.concatenate(mixed, axis=0)
    ha = (u * s * ga).astype(BF16)
    gm = jax.nn.sigmoid(_dot(hn, wma_ref[...]))
    mout_ref[...] = min_ref[...] + gm * _dot(ha, wpa_ref[...])


def _branch_a(x2, g, wu, wv, wga, wma, lng, lnb, ws, bs, wpa, m_in, *, tm, want_v):
    rows, d = x2.shape
    row_spec = pl.BlockSpec((tm, d), lambda i: (i, 0))
    out_shape = [jax.ShapeDtypeStruct((rows, d), F32)]
    out_specs = [row_spec]
    if want_v:
        out_shape.append(jax.ShapeDtypeStruct((rows, d), F32))
        out_specs.append(row_spec)
    kern = functools.partial(_a_kernel, chunk=CHUNK, groups=A_GROUPS)
    res = pl.pallas_call(
        kern,
        out_shape=tuple(out_shape),
        grid=(rows // tm,),
        in_specs=[row_spec] + [_const_spec()] * 10 + [row_spec],
        out_specs=tuple(out_specs),
        compiler_params=_params(1),
        name="branch_a",
    )(x2, g, wu, wv, wga, wma, lng, lnb, ws, bs, wpa, m_in)
    return res if want_v else (res[0], None)


def _in_proj_b(x, g_ref, wz_ref, wxbc_ref, wdt_ref, wmb_ref, dtb_ref):
    hn = _rmsnorm(x, g_ref[...]).astype(BF16)
    z = _dot(hn, wz_ref[...])
    xbc = _dot(hn, wxbc_ref[...])
    dt = jax.nn.softplus(_dot(hn, wdt_ref[...]) + dtb_ref[...])
    gm = jax.nn.sigmoid(_dot(hn, wmb_ref[...]))
    return z, xbc, dt, gm


def _finish_b(x, y, z, gm, ng_ref, wpb_ref, m_in, wout_ref, fg_ref, *, n_groups, final_norm):
    yf = y * _silu(z)
    gw = yf.shape[1] // n_groups
    parts = []
    for gi in range(n_groups):
        yg = yf[:, gi * gw:(gi + 1) * gw]
        parts.append(yg * lax.rsqrt(jnp.mean(yg * yg, axis=-1, keepdims=True) + EPS))
    hb = (jnp.concatenate(parts, axis=-1) * ng_ref[...]).astype(BF16)
    m = m_in + gm * _dot(hb, wpb_ref[...])
    out = x + _dot(m.astype(BF16), wout_ref[...])
    return _rmsnorm(out, fg_ref[...]) if final_norm else out


def _bp_kernel(x_ref, g_ref, wz_ref, wxbc_ref, wdt_ref, wmb_ref, cw_ref, cb_ref, dtb_ref, alog_ref, dsk_ref,
               ng_ref, wpb_ref, min_ref, wout_ref, fg_ref, tri_ref, e3_ref,
               y_ref, ssm_ref, conv_ref, xs_scr, st_scr,
               *, tm, chunk, n_groups, n_heads, head_dim, n_state, conv_w, final_norm):
    j = pl.program_id(1)
    last = pl.num_programs(1) - 1
    hpg = n_heads // n_groups
    gw = hpg * head_dim
    d_inner = n_heads * head_dim
    pad = SUBLANES

    @pl.when(j == 0)
    def _init():
        st_scr[...] = jnp.zeros_like(st_scr)
        xs_scr[...] = jnp.zeros_like(xs_scr)

    x = x_ref[0]
    z, xbc_raw, dt, gm = _in_proj_b(x, g_ref, wz_ref, wxbc_ref, wdt_ref, wmb_ref, dtb_ref)

    ext = jnp.concatenate([xs_scr[...], xbc_raw], axis=0)
    xs_scr[...] = xbc_raw[tm - pad:, :]
    conv = cb_ref[...] + xbc_raw * cw_ref[conv_w - 1:conv_w, :]
    for back in range(1, conv_w):
        shifted = pltpu.roll(ext, back, axis=0)[pad:, :]
        conv = conv + shifted * cw_ref[conv_w - 1 - back:conv_w - back, :]

    @pl.when(j == last)
    def _conv_out():
        conv_ref[0] = xbc_raw[tm - (conv_w - 1):, :]

    xbc = _silu(conv)
    xs = xbc[:, :d_inner]
    bm = xbc[:, d_inner:d_inner + n_groups * n_state].astype(BF16)
    cm = xbc[:, d_inner + n_groups * n_state:].astype(BF16)

    a_row = -jnp.exp(alog_ref[...])
    adt = dt * a_row
    e3 = e3_ref[...]
    xdt = xs * _dot(_split_by_replica(dt, n_heads).astype(BF16), e3)

    li = lax.broadcasted_iota(jnp.int32, (chunk, chunk), 0)
    si = lax.broadcasted_iota(jnp.int32, (chunk, chunk), 1)
    causal = li >= si
    head_of_lane = lax.broadcasted_iota(jnp.int32, (chunk, gw), 1) // head_dim
    tri = tri_ref[...]

    ys = []
    for c in range(tm // chunk):
        rs = slice(c * chunk, (c + 1) * chunk)
        acs = _cumsum_rows(tri, adt[rs])
        acs_t = acs.T
        acs_e = _dot(_split_by_replica(acs, n_heads).astype(BF16), e3)
        a_last = acs_e[chunk - 1:chunk, :]
        xdt_c = xdt[rs]
        xdt_b = xdt_c.astype(BF16)
        xd_state = (xdt_c * jnp.exp(a_last - acs_e)).astype(BF16)
        decay_in = jnp.exp(acs_e)
        st_prev = st_scr[...]
        st_prev_b = st_prev.astype(BF16)
        y_parts, st_parts = [], []
        for gi in range(n_groups):
            ns = slice(gi * n_state, (gi + 1) * n_state)
            gs = slice(gi * gw, (gi + 1) * gw)
            cg, bg = cm[rs, ns], bm[rs, ns]
            scores = _dot_nt(cg, bg)
            m_heads, x_blocks = [], []
            xg = xdt_b[:, gs]
            for r in range(hpg):
                h = gi * hpg + r
                diff = acs[:, h:h + 1] - acs_t[h:h + 1, :]
                decay = jnp.exp(jnp.where(causal, diff, -jnp.inf))
                m_heads.append((scores * decay).astype(BF16))
                x_blocks.append(jnp.where(head_of_lane == r, xg, jnp.zeros_like(xg)))
            y_diag = _dot(jnp.concatenate(m_heads, axis=1), jnp.concatenate(x_blocks, axis=0))
            y_off = _dot(cg, st_prev_b[:, gs])
            y_parts.append(y_diag + y_off * decay_in[:, gs])
            st_parts.append(_dot_tn(bg, xd_state[:, gs]))
        st_scr[...] = st_prev * jnp.exp(a_last) + jnp.concatenate(st_parts, axis=1)
        ys.append(jnp.concatenate(y_parts, axis=1))
    y = ys[0] if len(ys) == 1 else jnp.concatenate(ys, axis=0)
    y = y + xs * dsk_ref[...]

    y_ref[0] = _finish_b(x, y, z, gm, ng_ref, wpb_ref, min_ref[0], wout_ref, fg_ref,
                         n_groups=n_groups, final_norm=final_norm)

    @pl.when(j == last)
    def _state_out():
        ssm_ref[0] = st_scr[...].T


def _branch_b_prompt(x3, m3, w, *, tm, final_norm):
    bsz, seq, d = x3.shape
    conv_dim = w["wxbc"].shape[1]
    d_inner = w["wz"].shape[1]
    n_heads = d_inner // SSD_HEAD_DIM
    conv_w = w["cw"].shape[0]
    kern = functools.partial(_bp_kernel, tm=tm, chunk=CHUNK, n_groups=SSD_GROUPS, n_heads=n_heads,
                             head_dim=SSD_HEAD_DIM, n_state=SSD_STATE, conv_w=conv_w, final_norm=final_norm)
    row_spec = pl.BlockSpec((1, tm, d), lambda b, j: (b, j, 0))
    consts = [w[k] for k in ("g", "wz", "wxbc", "wdt", "wmb", "cw", "cb", "dtb", "alog", "dsk", "ng", "wpb")]
    tail_consts = [w[k] for k in ("wout", "fg", "tri", "e3")]
    return pl.pallas_call(
        kern,
        out_shape=(jax.ShapeDtypeStruct((bsz, seq, d), F32),
                   jax.ShapeDtypeStruct((bsz, d_inner, SSD_STATE), F32),
                   jax.ShapeDtypeStruct((bsz, conv_w - 1, conv_dim), F32)),
        grid=(bsz, seq // tm),
        in_specs=[row_spec] + [_const_spec()] * len(consts) + [row_spec] + [_const_spec()] * len(tail_consts),
        out_specs=(row_spec,
                   pl.BlockSpec((1, d_inner, SSD_STATE), lambda b, j: (b, 0, 0)),
                   pl.BlockSpec((1, conv_w - 1, conv_dim), lambda b, j: (b, 0, 0))),
        scratch_shapes=[pltpu.VMEM((SUBLANES, conv_dim), F32), pltpu.VMEM((SSD_STATE, d_inner), F32)],
        compiler_params=_params(2),
        name="branch_b_prompt",
    )(x3, *consts, m3, *tail_consts)


def _bs_kernel(x_ref, g_ref, wz_ref, wxbc_ref, wdt_ref, wmb_ref, cw_ref, cb_ref, dtb_ref, alog_ref, dsk_ref,
               ng_ref, wpb_ref, min_ref, wout_ref, fg_ref, tri_ref, e3_ref, ecol_ref, conv0_ref, h0_ref,
               y_ref, hnew_ref, convnew_ref, xs_scr, tr_scr,
               *, nb, rows, n_groups, n_heads, head_dim, n_state, conv_w, final_norm):
    hpg = n_heads // n_groups
    gw = hpg * head_dim
    d_inner = n_heads * head_dim
    d = x_ref.shape[-1]
    r_all = nb * rows
    pad = SUBLANES
    hs_w = n_heads * rows

    @pl.when(pl.program_id(0) == 0)
    def _init():
        tr_scr[...] = jnp.zeros_like(tr_scr)

    x = x_ref[...].reshape(r_all, d)
    z, xbc_raw, dt, gm = _in_proj_b(x, g_ref, wz_ref, wxbc_ref, wdt_ref, wmb_ref, dtb_ref)

    convs = []
    for b in range(nb):
        xs_scr[b, pad - (conv_w - 1):pad, :] = conv0_ref[b]
        xs_scr[b, pad:pad + rows, :] = xbc_raw[b * rows:(b + 1) * rows]
        conv = cb_ref[...]
        for k in range(conv_w):
            conv = conv + xs_scr[b, pl.ds(pad - (conv_w - 1) + k, rows), :] * cw_ref[k:k + 1, :]
        convs.append(conv)
        convnew_ref[b] = xs_scr[b, pl.ds(pad + rows - (conv_w - 1), conv_w - 1), :]
    xbc = _silu(jnp.concatenate(convs, axis=0))
    xs = xbc[:, :d_inner]
    bm = xbc[:, d_inner:d_inner + n_groups * n_state]
    cm = xbc[:, d_inner + n_groups * n_state:]

    a_row = -jnp.exp(alog_ref[...])
    acs = _cumsum_rows(tri_ref[...], dt * a_row)
    e3 = e3_ref[...]
    acs_split = _split_by_replica(acs, n_heads).astype(BF16)
    acs_e = _dot(acs_split, e3)
    xdt = xs * _dot(_split_by_replica(dt, n_heads).astype(BF16), e3)
    decay_in = jnp.exp(acs_e)

    acs_col = _dot(acs_split, ecol_ref[...])
    l_idx = lax.broadcasted_iota(jnp.int32, (r_all, hs_w), 0) % rows
    s_idx = lax.broadcasted_iota(jnp.int32, (r_all, hs_w), 1) % rows
    on_diag = jnp.where(l_idx == s_idx, acs_col, 0.0).reshape(nb, rows, hs_w)
    acs_row = jnp.broadcast_to(jnp.sum(on_diag, axis=1, keepdims=True), (nb, rows, hs_w)).reshape(r_all, hs_w)
    decay = jnp.exp(jnp.where(l_idx >= s_idx, acs_col - acs_row, -jnp.inf))

    grp_of_row = lax.broadcasted_iota(jnp.int32, (hs_w, n_groups * n_state), 0) // (rows * hpg)
    grp_of_lane = lax.broadcasted_iota(jnp.int32, (hs_w, n_groups * n_state), 1) // n_state
    head_of_row = lax.broadcasted_iota(jnp.int32, (hs_w, d_inner), 0) // rows
    head_of_lane = lax.broadcasted_iota(jnp.int32, (hs_w, d_inner), 1) // head_dim
    ones_rows = lax.broadcasted_iota(jnp.int32, (LANES, n_state), 0)
    ones_blk = jnp.where((ones_rows >= rows) & (ones_rows < rows + 3), 1.0, 0.0)

    ys = []
    for b in range(nb):
        rs = slice(b * rows, (b + 1) * rows)
        bm_b, cm_b = bm[rs], cm[rs]
        cm_bb = cm_b.astype(BF16)
        b_rep = jnp.broadcast_to(bm_b[None], (n_heads, rows, bm_b.shape[1])).reshape(hs_w, bm_b.shape[1])
        b_exp = jnp.where(grp_of_row == grp_of_lane, b_rep, 0.0).astype(BF16)
        scores = _dot_nt(cm_bb, b_exp)
        m_all = (scores * decay[rs]).astype(BF16)
        xdt_b = xdt[rs]
        x_rep = jnp.broadcast_to(xdt_b[None], (n_heads, rows, d_inner)).reshape(hs_w, d_inner)
        x_blk = jnp.where(head_of_row == head_of_lane, x_rep, 0.0).astype(BF16)
        y_diag = _dot(m_all, x_blk)

        a_last = acs_e[(b + 1) * rows - 1:(b + 1) * rows, :]
        xd_state = xdt_b * jnp.exp(a_last - acs_e[rs])
        d_hi, d_mid, d_lo = _bf16_parts(jnp.exp(a_last))
        tr_scr[0:rows, :] = xd_state
        tr_scr[rows:rows + 1, :] = d_hi.astype(F32)
        tr_scr[rows + 1:rows + 2, :] = d_mid.astype(F32)
        tr_scr[rows + 2:rows + 3, :] = d_lo.astype(F32)
        tr_t = tr_scr[...].T.astype(BF16)

        y_parts = []
        for gi in range(n_groups):
            ns = slice(gi * n_state, (gi + 1) * n_state)
            gs = slice(gi * gw, (gi + 1) * gw)
            h0_g = h0_ref[b, gs, :]
            y_parts.append(_dot_nt(cm_bb[:, ns], h0_g.astype(BF16)))
            b_pad = jnp.concatenate([bm_b[:, ns], jnp.zeros((LANES - rows, n_state), F32)], axis=0)
            upd = _dot(tr_t[gs, :], jnp.concatenate([b_pad, ones_blk], axis=1).astype(BF16))
            hnew_ref[b, gs, :] = h0_g * upd[:, n_state:] + upd[:, :n_state]
        y_off = jnp.concatenate(y_parts, axis=1)
        ys.append(y_diag + y_off * decay_in[rs])
    y = jnp.concatenate(ys, axis=0) + xs * dsk_ref[...]

    m_in = min_ref[...].reshape(r_all, d)
    out = _finish_b(x, y, z, gm, ng_ref, wpb_ref, m_in, wout_ref, fg_ref, n_groups=n_groups, final_norm=final_norm)
    y_ref[...] = out.reshape(nb, rows, d)


def _branch_b_sample(x3, m3, conv0, h0, w, *, nb, final_norm):
    bsz, rows, d = x3.shape
    conv_dim = w["wxbc"].shape[1]
    d_inner = w["wz"].shape[1]
    n_heads = d_inner // SSD_HEAD_DIM
    conv_w = w["cw"].shape[0]
    kern = functools.partial(_bs_kernel, nb=nb, rows=rows, n_groups=SSD_GROUPS, n_heads=n_heads,
                             head_dim=SSD_HEAD_DIM, n_state=SSD_STATE, conv_w=conv_w, final_norm=final_norm)
    row_spec = pl.BlockSpec((nb, rows, d), lambda i: (i, 0, 0))
    conv_spec = pl.BlockSpec((nb, conv_w - 1, conv_dim), lambda i: (i, 0, 0))
    h_spec = pl.BlockSpec((nb, d_inner, SSD_STATE), lambda i: (i, 0, 0))
    consts = [w[k] for k in ("g", "wz", "wxbc", "wdt", "wmb", "cw", "cb", "dtb", "alog", "dsk", "ng", "wpb")]
    tail_consts = [w[k] for k in ("wout", "fg", "tri_s", "e3", "ecol")]
    return pl.pallas_call(
        kern,
        out_shape=(jax.ShapeDtypeStruct((bsz, rows, d), F32),
                   jax.ShapeDtypeStruct((bsz, d_inner, SSD_STATE), F32),
                   jax.ShapeDtypeStruct((bsz, conv_w - 1, conv_dim), F32)),
        grid=(bsz // nb,),
        in_specs=([row_spec] + [_const_spec()] * len(consts) + [row_spec] + [_const_spec()] * len(tail_consts)
                  + [conv_spec, h_spec]),
        out_specs=(row_spec, h_spec, conv_spec),
        scratch_shapes=[pltpu.VMEM((nb, 2 * SUBLANES, conv_dim), F32), pltpu.VMEM((LANES, d_inner), F32)],
        compiler_params=_params(1),
        name="branch_b_sample",
    )(x3, *consts, m3, *tail_consts, conv0, h0)


def _tile(n, pref):
    return pref if n % pref == 0 else n


def _head_expand(n_heads, width, n_rep):
    j = np.arange(LANES)[:, None]
    c = np.arange(n_heads * width)[None, :]
    return jnp.asarray(((j < n_rep * n_heads) & (j % n_heads == c // width)).astype(np.float32), dtype=BF16)


def _layer_weights(l, d, dec_rows, nb_s, norm_g, w_in, conv_w, conv_b, dt_bias, a_log, d_skip, ssd_norm_g, ln_v_g,
                   ln_v_b, w_spatial, b_spatial, w_proj_a, w_proj_b, w_proj_x, w_out, final_norm_g):
    d_a = d
    d_inner = w_proj_b.shape[1]
    conv_dim = conv_w.shape[2]
    n_heads = a_log.shape[1]
    d_x = w_proj_x.shape[1]
    sizes = (d_a, d_a, d_a, d_inner, conv_dim, n_heads, d_x, d_x, N_BRANCH * d)
    offs = np.concatenate([[0], np.cumsum(sizes)])
    wl = w_in[l]
    sec = lambda i: wl[:, offs[i]:offs[i + 1]]
    merge = sec(8)
    row = lambda v: v.reshape(1, -1).astype(F32)
    rep3 = lambda v, fill: jnp.concatenate(
        [v] * DT_REPLICAS + [jnp.full(v.shape[:-1] + (LANES - DT_REPLICAS * v.shape[-1],), fill, v.dtype)], axis=-1)

    tril = jnp.tril(jnp.ones((CHUNK, CHUNK), F32))
    ws_p = jnp.where(tril[None] > 0, w_spatial[l], 0.0)
    bs_p = jnp.repeat(b_spatial[l].T, d_a // A_GROUPS, axis=1)
    n_seq = CHUNK // dec_rows
    eye = jnp.eye(n_seq, dtype=F32)
    ws_s = jnp.stack([jnp.kron(eye, ws_p[gi, :dec_rows, :dec_rows]) for gi in range(A_GROUPS)])
    bs_s = jnp.tile(bs_p[:dec_rows], (n_seq, 1))

    w = {
        "g": row(norm_g[l]),
        "wu": sec(0).astype(BF16), "wv": sec(1).astype(BF16), "wga": sec(2).astype(BF16),
        "wma": merge[:, :d].astype(BF16),
        "lng": row(ln_v_g[l]), "lnb": row(ln_v_b[l]),
        "ws_p": ws_p.astype(BF16), "bs_p": bs_p, "ws_s": ws_s.astype(BF16), "bs_s": bs_s,
        "wpa": w_proj_a[l].astype(BF16),
        "wz": sec(3).astype(BF16), "wxbc": sec(4).astype(BF16),
        "wdt": rep3(sec(5), 0.0).astype(BF16),
        "wmb": merge[:, d:2 * d].astype(BF16),
        "cw": conv_w[l].astype(F32), "cb": row(conv_b[l]),
        "dtb": rep3(row(dt_bias[l]), 0.0), "alog": rep3(row(a_log[l]), 0.0),
        "dsk": jnp.repeat(row(d_skip[l]), SSD_HEAD_DIM, axis=1),
        "ng": row(ssd_norm_g[l]),
        "wpb": w_proj_b[l].astype(BF16),
        "wout": w_out[l].astype(BF16), "fg": row(final_norm_g),
        "tri": tril.astype(BF16),
        "tri_s": jnp.kron(jnp.eye(nb_s, dtype=F32), jnp.tril(jnp.ones((dec_rows, dec_rows), F32))).astype(BF16),
        "e3": _head_expand(n_heads, SSD_HEAD_DIM, DT_REPLICAS),
        "ecol": _head_expand(n_heads, dec_rows, DT_REPLICAS),
        "wq": sec(6).astype(BF16), "wgx": sec(7).astype(BF16), "wmx": merge[:, 2 * d:].astype(BF16),
        "wpx": w_proj_x[l].astype(BF16),
    }
    return w


def _layer(x3, k3, v3, w, *, nb_x, rows_x, heads_in_rows, tm_a, ws, bs, want_v, b_fn):
    bsz, seq, d = x3.shape
    m = _branch_x(x3, w["g"], w["wq"], w["wgx"], w["wmx"], k3, v3, w["wpx"], nb=nb_x, rows=rows_x,
                  heads_in_rows=heads_in_rows)
    m, vn = _branch_a(x3.reshape(bsz * seq, d), w["g"], w["wu"], w["wv"], w["wga"], w["wma"], w["lng"], w["lnb"],
                      ws, bs, w["wpa"], m.reshape(bsz * seq, d), tm=tm_a, want_v=want_v)
    return b_fn(x3, m.reshape(bsz, seq, d)), vn


def kernel(x_prompt, x_sample, mem_prompt, cache_mem_k, cache_mem_v, state_ssm, state_conv, norm_g, w_in, conv_w,
           conv_b, dt_bias, a_log, d_skip, ssd_norm_g, ln_v_g, ln_v_b, w_spatial, b_spatial, mem_norm_g, w_mem_kv,
           w_proj_a, w_proj_b, w_proj_x, w_out, final_norm_g):
    depth = w_in.shape[0]
    bsz, seq, d = x_prompt.shape
    dec_b, dec_rows, _ = x_sample.shape
    n_mem = mem_prompt.shape[1]
    d_x = w_proj_x.shape[1]
    d_inner = w_proj_b.shape[1]
    n_heads = a_log.shape[1]
    assert seq % CHUNK == 0 and CHUNK % dec_rows == 0 and dec_rows == SUBLANES
    assert DT_REPLICAS * n_heads <= LANES and n_heads * dec_rows % LANES == 0

    nb_s = _tile(dec_b, 4)
    nb_xs = _tile(dec_b, 8)
    yp, ys = x_prompt, x_sample
    outs = {k: [] for k in ("mk", "mv", "hp", "cp", "hs", "cs", "vs")}
    for l in range(depth):
        w = _layer_weights(l, d, dec_rows, nb_s, norm_g, w_in, conv_w, conv_b, dt_bias, a_log, d_skip, ssd_norm_g,
                           ln_v_g, ln_v_b, w_spatial, b_spatial, w_proj_a, w_proj_b, w_proj_x, w_out, final_norm_g)
        final_norm = l == depth - 1
        mk, mv = _mem_kv(mem_prompt.reshape(bsz * n_mem, d), mem_norm_g[l].reshape(1, d).astype(F32),
                         w_mem_kv[l].astype(BF16), _tile(bsz * n_mem, 512))
        (yp, hp, cp), _ = _layer(
            yp, mk.reshape(bsz, n_mem, d_x), mv.reshape(bsz, n_mem, d_x), w,
            nb_x=1, rows_x=_tile(seq, 512), heads_in_rows=False, tm_a=_tile(bsz * seq, 512), ws=w["ws_p"], bs=w["bs_p"], want_v=False,
            b_fn=functools.partial(_branch_b_prompt, w=w, tm=_tile(seq, 256), final_norm=final_norm))
        (ys, hs, cs), vs = _layer(
            ys, cache_mem_k[l].reshape(dec_b, n_mem * (d_x // X_HEAD_DIM), X_HEAD_DIM),
            cache_mem_v[l].reshape(dec_b, n_mem * (d_x // X_HEAD_DIM), X_HEAD_DIM), w,
            nb_x=nb_xs, rows_x=dec_rows, heads_in_rows=True, tm_a=_tile(dec_b * dec_rows, 512), ws=w["ws_s"], bs=w["bs_s"], want_v=True,
            b_fn=functools.partial(_branch_b_sample, conv0=state_conv[l],
                                   h0=state_ssm[l].reshape(dec_b, d_inner, SSD_STATE), w=w, nb=nb_s,
                                   final_norm=final_norm))
        outs["mk"].append(mk.reshape(bsz, n_mem, d_x // X_HEAD_DIM, X_HEAD_DIM))
        outs["mv"].append(mv.reshape(bsz, n_mem, d_x // X_HEAD_DIM, X_HEAD_DIM))
        outs["hp"].append(hp.reshape(bsz, n_heads, SSD_HEAD_DIM, SSD_STATE))
        outs["cp"].append(cp)
        outs["hs"].append(hs.reshape(dec_b, n_heads, SSD_HEAD_DIM, SSD_STATE))
        outs["cs"].append(cs)
        outs["vs"].append(vs.reshape(dec_b, dec_rows, d))
    st = lambda k: jnp.stack(outs[k])
    return (yp, ys, st("mk"), st("mv"), st("hp"), st("cp"), st("hs"), st("cs"), st("vs"))
```

```python
import functools
import math

import jax
import jax.numpy as jnp
import numpy as np
from jax import lax
from jax.experimental import pallas as pl
from jax.experimental.pallas import tpu as pltpu

F32 = jnp.float32
BF16 = jnp.bfloat16
EPS = 1e-6
SQRT_HALF = math.sqrt(0.5)

LANES = 128
SUBLANES = 8
VMEM_LIMIT_BYTES = 56 * 1024 * 1024

CHUNK = 128
A_GROUPS = 8
SSD_HEAD_DIM = 64
SSD_GROUPS = 8
SSD_STATE = 128
X_HEAD_DIM = 128
N_BRANCH = 3
DT_REPLICAS = 3


def _dot(a, b):
    return jnp.dot(a, b, preferred_element_type=F32)


def _dot_nt(a, b):
    return lax.dot_general(a, b, (((1,), (1,)), ((), ())), preferred_element_type=F32)


def _dot_tn(a, b):
    return lax.dot_general(a, b, (((0,), (0,)), ((), ())), preferred_element_type=F32)


MXU_COLS = 256


def _pack_w(w):
    return w.astype(BF16)


def _dotw(a, w_ref, cols=slice(None)):
    return _dot(a, w_ref[:, cols])


def _dot_cols(a, w_ref, fn, width=MXU_COLS):
    n = w_ref.shape[1]
    width = min(width, n)
    blocks = [fn(_dotw(a, w_ref, slice(c, c + width))) for c in range(0, n, width)]
    return blocks[0] if len(blocks) == 1 else jnp.concatenate(blocks, axis=1)


def _rmsnorm(x, g):
    return x * lax.rsqrt(jnp.mean(x * x, axis=-1, keepdims=True) + EPS) * g


def _gelu(x):
    return 0.5 * x * (1.0 + lax.erf(x * SQRT_HALF))


def _sigmoid(x):
    return 0.5 * jnp.tanh(0.5 * x) + 0.5


def _silu(x):
    return x * _sigmoid(x)


def _bf16_parts(x):
    hi = x.astype(BF16)
    r1 = x - hi.astype(F32)
    mid = r1.astype(BF16)
    lo = (r1 - mid.astype(F32)).astype(BF16)
    return hi, mid, lo


def _split_by_replica(x, n_heads):
    hi, mid, lo = _bf16_parts(x)
    lane = lax.broadcasted_iota(jnp.int32, x.shape, 1)
    return jnp.where(lane < n_heads, hi, jnp.where(lane < 2 * n_heads, mid, lo))


def _cumsum_rows(tri, x):
    hi, mid, lo = _bf16_parts(x)
    return _dot(tri, hi) + _dot(tri, mid) + _dot(tri, lo)


def _memkv_kernel(x_ref, g_ref, w_ref, k_ref, v_ref):
    hn = _rmsnorm(x_ref[...], g_ref[...]).astype(BF16)
    kv = _dotw(hn, w_ref)
    dx = k_ref.shape[-1]
    k_ref[...] = kv[:, :dx]
    v_ref[...] = kv[:, dx:]


def _const_spec():
    return pl.BlockSpec(memory_space=pltpu.VMEM)


def _params(n_grid):
    return pltpu.CompilerParams(dimension_semantics=("arbitrary",) * n_grid, vmem_limit_bytes=VMEM_LIMIT_BYTES)


def _mem_kv(mem2d, g, w, tm):
    rows, d = mem2d.shape
    dx = w.shape[1] // 2
    return pl.pallas_call(
        _memkv_kernel,
        out_shape=(jax.ShapeDtypeStruct((rows, dx), F32), jax.ShapeDtypeStruct((rows, dx), F32)),
        grid=(rows // tm,),
        in_specs=[pl.BlockSpec((tm, d), lambda i: (i, 0)), _const_spec(), _const_spec()],
        out_specs=(pl.BlockSpec((tm, dx), lambda i: (i, 0)), pl.BlockSpec((tm, dx), lambda i: (i, 0))),
        compiler_params=_params(1),
        name="mem_kv",
    )(mem2d, g, w)


def _x_kernel(x_ref, g_ref, wq_ref, wgx_ref, wmx_ref, k_ref, v_ref, wpx_ref, m_ref, *, nb, rows, head_dim):
    d = x_ref.shape[-1]
    x = x_ref[...].reshape(nb * rows, d)
    hn = _rmsnorm(x, g_ref[...]).astype(BF16)
    q = _dotw(hn, wq_ref)
    gate = _silu(_dotw(hn, wgx_ref))
    gm = _sigmoid(_dotw(hn, wmx_ref))
    n_heads = q.shape[1] // head_dim
    scale = head_dim ** -0.5
    outs = []
    for b in range(nb):
        kb = k_ref[b].astype(BF16)
        vb = v_ref[b].astype(BF16)
        qb = q[b * rows:(b + 1) * rows].astype(BF16)
        heads = []
        for h in range(n_heads):
            hs = slice(h * head_dim, (h + 1) * head_dim)
            s = _dot_nt(qb[:, hs], kb[:, hs]) * scale
            e = jnp.exp(s - jnp.max(s, axis=-1, keepdims=True))
            p = e / jnp.sum(e, axis=-1, keepdims=True)
            heads.append(_dot(p.astype(BF16), vb[:, hs]))
        outs.append(jnp.concatenate(heads, axis=-1))
    o = outs[0] if nb == 1 else jnp.concatenate(outs, axis=0)
    hx = (o * gate).astype(BF16)
    m = gm * _dotw(hx, wpx_ref)
    m_ref[...] = m.reshape(nb, rows, d)


def _xs_kernel(x_ref, g_ref, wq_ref, wgx_ref, wmx_ref, k_ref, v_ref, wpx_ref, m_ref, *, nb, rows, head_dim):
    d = x_ref.shape[-1]
    x = x_ref[...].reshape(nb * rows, d)
    hn = _rmsnorm(x, g_ref[...]).astype(BF16)
    q = _dotw(hn, wq_ref)
    gate = _silu(_dotw(hn, wgx_ref))
    gm = _sigmoid(_dotw(hn, wmx_ref))
    n_heads = q.shape[1] // head_dim
    n_kv = k_ref.shape[1]
    scale = head_dim ** -0.5
    row_head = lax.broadcasted_iota(jnp.int32, (n_heads * rows, n_kv), 0) // rows
    col_head = lax.broadcasted_iota(jnp.int32, (n_heads * rows, n_kv), 1) % n_heads
    same_head = row_head == col_head
    outs = []
    for b in range(nb):
        kb = k_ref[b].astype(BF16)
        vb = v_ref[b].astype(BF16)
        qb = q[b * rows:(b + 1) * rows]
        q_heads = jnp.concatenate([qb[:, h * head_dim:(h + 1) * head_dim] for h in range(n_heads)], axis=0)
        s = jnp.where(same_head, _dot_nt(q_heads.astype(BF16), kb) * scale, -jnp.inf)
        e = jnp.exp(s - jnp.max(s, axis=-1, keepdims=True))
        p = e / jnp.sum(e, axis=-1, keepdims=True)
        o_heads = _dot(p.astype(BF16), vb)
        outs.append(jnp.concatenate([o_heads[h * rows:(h + 1) * rows] for h in range(n_heads)], axis=1))
    o = outs[0] if nb == 1 else jnp.concatenate(outs, axis=0)
    hx = (o * gate).astype(BF16)
    m = gm * _dotw(hx, wpx_ref)
    m_ref[...] = m.reshape(nb, rows, d)


def _branch_x(x3, g, wq, wgx, wmx, k3, v3, wpx, *, nb, rows, heads_in_rows):
    bsz, seq, d = x3.shape
    n_mem, dx = k3.shape[1], k3.shape[2]
    kern = functools.partial(_xs_kernel if heads_in_rows else _x_kernel, nb=nb, rows=rows, head_dim=X_HEAD_DIM)
    return pl.pallas_call(
        kern,
        out_shape=jax.ShapeDtypeStruct((bsz, seq, d), F32),
        grid=(bsz // nb, seq // rows),
        in_specs=[
            pl.BlockSpec((nb, rows, d), lambda i, j: (i, j, 0)),
            _const_spec(), _const_spec(), _const_spec(), _const_spec(),
            pl.BlockSpec((nb, n_mem, dx), lambda i, j: (i, 0, 0)),
            pl.BlockSpec((nb, n_mem, dx), lambda i, j: (i, 0, 0)),
            _const_spec(),
        ],
        out_specs=pl.BlockSpec((nb, rows, d), lambda i, j: (i, j, 0)),
        compiler_params=_params(2),
        name="branch_x",
    )(x3, g, wq, wgx, wmx, k3, v3, wpx)


def _a_kernel(x_ref, g_ref, wu_ref, wv_ref, wga_ref, wma_ref, lng_ref, lnb_ref, ws_ref, bs_ref, wpa_ref, min_ref,
              mout_ref, *maybe_v_ref, chunk, groups, sub):
    tm, d = x_ref.shape
    gd = d // groups
    bias = bs_ref[...]
    for t in range(tm // sub):
        ts = slice(t * sub, (t + 1) * sub)
        hn = _rmsnorm(x_ref[ts, :], g_ref[...]).astype(BF16)
        v = _dot_cols(hn, wv_ref, _gelu)
        u = _dot_cols(hn, wu_ref, _gelu)
        vc = v - jnp.mean(v, axis=-1, keepdims=True)
        vn = vc * lax.rsqrt(jnp.mean(vc * vc, axis=-1, keepdims=True) + EPS) * lng_ref[...] + lnb_ref[...]
        if maybe_v_ref:
            maybe_v_ref[0][ts, :] = vn
        ga = _dot_cols(hn, wga_ref, _silu)
        vb = vn.astype(BF16)
        mixed = []
        for c in range(sub // chunk):
            rs = slice(c * chunk, (c + 1) * chunk)
            cols = [_dot(ws_ref[gi], vb[rs, gi * gd:(gi + 1) * gd]) for gi in range(groups)]
            mixed.append(jnp.concatenate(cols, axis=-1) + bias)
        s = mixed[0] if len(mixed) == 1 else jnp.concatenate(mixed, axis=0)
        ha = (u * s * ga).astype(BF16)
        gm = _sigmoid(_dotw(hn, wma_ref))
        mout_ref[ts, :] = min_ref[ts, :] + gm * _dotw(ha, wpa_ref)


def _branch_a(x2, g, wu, wv, wga, wma, lng, lnb, ws, bs, wpa, m_in, *, tm, want_v):
    rows, d = x2.shape
    row_spec = pl.BlockSpec((tm, d), lambda i: (i, 0))
    out_shape = [jax.ShapeDtypeStruct((rows, d), F32)]
    out_specs = [row_spec]
    if want_v:
        out_shape.append(jax.ShapeDtypeStruct((rows, d), F32))
        out_specs.append(row_spec)
    kern = functools.partial(_a_kernel, chunk=CHUNK, groups=A_GROUPS, sub=min(tm, 256))
    res = pl.pallas_call(
        kern,
        out_shape=tuple(out_shape),
        grid=(rows // tm,),
        in_specs=[row_spec] + [_const_spec()] * 10 + [row_spec],
        out_specs=tuple(out_specs),
        compiler_params=_params(1),
        name="branch_a",
    )(x2, g, wu, wv, wga, wma, lng, lnb, ws, bs, wpa, m_in)
    return res if want_v else (res[0], None)


def _in_proj_b(x, g_ref, wz_ref, wxbc_ref, wdt_ref, wmb_ref, dtb_ref, perm=None):
    hn = _rmsnorm(x, g_ref[...]).astype(BF16)
    gm = _sigmoid(_dotw(hn, wmb_ref))
    if perm is not None:
        hn = _dot(perm, hn).astype(BF16)
    z = _dotw(hn, wz_ref)
    xbc = _dotw(hn, wxbc_ref)
    dt = jax.nn.softplus(_dotw(hn, wdt_ref) + dtb_ref[...])
    return z, xbc, dt, gm


def _finish_b(x, y, z, gm, ng_ref, wpb_ref, m_in, wout_ref, fg_ref, *, n_groups, final_norm, unperm=None):
    yf = y * _silu(z)
    gw = yf.shape[1] // n_groups
    parts = []
    for gi in range(n_groups):
        yg = yf[:, gi * gw:(gi + 1) * gw]
        parts.append(yg * lax.rsqrt(jnp.mean(yg * yg, axis=-1, keepdims=True) + EPS))
    hb = (jnp.concatenate(parts, axis=-1) * ng_ref[...]).astype(BF16)
    if unperm is not None:
        hb = _dot(unperm, hb).astype(BF16)
    m = m_in + gm * _dotw(hb, wpb_ref)
    out = x + _dotw(m.astype(BF16), wout_ref)
    return _rmsnorm(out, fg_ref[...]) if final_norm else out


def _bp_kernel(x_ref, g_ref, wz_ref, wxbc_ref, wdt_ref, wmb_ref, cw_ref, cb_ref, dtb_ref, alog_ref, dsk_ref,
               ng_ref, wpb_ref, min_ref, wout_ref, fg_ref, tri_ref, e3_ref, perm_ref,
               y_ref, ssm_ref, conv_ref, xs_scr, st_scr,
               *, tm, chunk, n_groups, n_heads, head_dim, n_state, conv_w, final_norm):
    j = pl.program_id(1)
    last = pl.num_programs(1) - 1
    hpg = n_heads // n_groups
    gw = hpg * head_dim
    d_inner = n_heads * head_dim
    pad = SUBLANES

    @pl.when(j == 0)
    def _init():
        st_scr[...] = jnp.zeros_like(st_scr)
        xs_scr[...] = jnp.zeros_like(xs_scr)

    vpc = chunk // pad
    n_chunks = tm // chunk

    x = x_ref[0]
    z, xbc_raw, dt, gm = _in_proj_b(x, g_ref, wz_ref, wxbc_ref, wdt_ref, wmb_ref, dtb_ref, perm=perm_ref[0])

    first_sublane = lax.broadcasted_iota(jnp.int32, (pad, xbc_raw.shape[1]), 0) == 0
    convs = []
    prev_tail = xs_scr[...]
    for c in range(n_chunks):
        raw = xbc_raw[c * chunk:(c + 1) * chunk]
        conv = cb_ref[...] + raw * cw_ref[conv_w - 1:conv_w, :]
        for back in range(1, conv_w):
            head = []
            for v in range(back):
                w = vpc - back + v
                cur = pltpu.roll(raw[w * pad:(w + 1) * pad], 1, axis=0)
                prv = pltpu.roll(prev_tail[(w - vpc + conv_w - 1) * pad:(w - vpc + conv_w) * pad], 1, axis=0)
                head.append(jnp.where(first_sublane, prv, cur))
            shifted = jnp.concatenate(head + [raw[:chunk - back * pad]], axis=0)
            conv = conv + shifted * cw_ref[conv_w - 1 - back:conv_w - back, :]
        convs.append(conv)
        prev_tail = raw[chunk - (conv_w - 1) * pad:]
    xs_scr[...] = prev_tail

    @pl.when(j == last)
    def _conv_out():
        conv_ref[0] = jnp.concatenate([prev_tail[(v + 1) * pad - 1:(v + 1) * pad] for v in range(conv_w - 1)], axis=0)

    xbc = _silu(convs[0] if n_chunks == 1 else jnp.concatenate(convs, axis=0))
    xs = xbc[:, :d_inner]
    bm = xbc[:, d_inner:d_inner + n_groups * n_state].astype(BF16)
    cm = xbc[:, d_inner + n_groups * n_state:].astype(BF16)

    a_row = -jnp.exp(alog_ref[...])
    adt = dt * a_row
    e3 = e3_ref[...]
    xdt = xs * _dot(_split_by_replica(dt, n_heads).astype(BF16), e3)

    def token_of(q):
        return (q % pad) * vpc + q // pad

    li = token_of(lax.broadcasted_iota(jnp.int32, (chunk, chunk), 0))
    si = token_of(lax.broadcasted_iota(jnp.int32, (chunk, chunk), 1))
    causal = li >= si
    head_of_lane = lax.broadcasted_iota(jnp.int32, (chunk, gw), 1) // head_dim
    tri = tri_ref[...]

    ys = []
    for c in range(tm // chunk):
        rs = slice(c * chunk, (c + 1) * chunk)
        acs = _cumsum_rows(tri, adt[rs])
        acs_t = acs.T
        acs_e = _dot(_split_by_replica(acs, n_heads).astype(BF16), e3)
        a_last = acs_e[chunk - 1:chunk, :]
        xdt_c = xdt[rs]
        xdt_b = xdt_c.astype(BF16)
        xd_state = (xdt_c * jnp.exp(a_last - acs_e)).astype(BF16)
        decay_in = jnp.exp(acs_e)
        st_prev = st_scr[...]
        st_prev_b = st_prev.astype(BF16)
        y_parts, st_parts = [], []
        for gi in range(n_groups):
            ns = slice(gi * n_state, (gi + 1) * n_state)
            gs = slice(gi * gw, (gi + 1) * gw)
            cg, bg = cm[rs, ns], bm[rs, ns]
            scores = _dot_nt(cg, bg)
            m_heads, x_blocks = [], []
            xg = xdt_b[:, gs]
            for r in range(hpg):
                h = gi * hpg + r
                diff = acs[:, h:h + 1] - acs_t[h:h + 1, :]
                decay = jnp.exp(jnp.where(causal, diff, -jnp.inf))
                m_heads.append((scores * decay).astype(BF16))
                x_blocks.append(jnp.where(head_of_lane == r, xg, jnp.zeros_like(xg)))
            y_diag = _dot(jnp.concatenate(m_heads, axis=1), jnp.concatenate(x_blocks, axis=0))
            y_off = _dot(cg, st_prev_b[:, gs])
            y_parts.append(y_diag + y_off * decay_in[:, gs])
            st_parts.append(_dot_tn(bg, xd_state[:, gs]))
        st_scr[...] = st_prev * jnp.exp(a_last) + jnp.concatenate(st_parts, axis=1)
        ys.append(jnp.concatenate(y_parts, axis=1))
    y = ys[0] if len(ys) == 1 else jnp.concatenate(ys, axis=0)
    y = y + xs * dsk_ref[...]

    y_ref[0] = _finish_b(x, y, z, gm, ng_ref, wpb_ref, min_ref[0], wout_ref, fg_ref,
                         n_groups=n_groups, final_norm=final_norm, unperm=perm_ref[1])

    @pl.when(j == last)
    def _state_out():
        ssm_ref[0] = st_scr[...].T


def _branch_b_prompt(x3, m3, w, *, tm, final_norm):
    bsz, seq, d = x3.shape
    conv_dim = w["wxbc"].shape[1]
    d_inner = w["wz"].shape[1]
    n_heads = d_inner // SSD_HEAD_DIM
    conv_w = w["cw"].shape[0]
    kern = functools.partial(_bp_kernel, tm=tm, chunk=CHUNK, n_groups=SSD_GROUPS, n_heads=n_heads,
                             head_dim=SSD_HEAD_DIM, n_state=SSD_STATE, conv_w=conv_w, final_norm=final_norm)
    row_spec = pl.BlockSpec((1, tm, d), lambda b, j: (b, j, 0))
    consts = [w[k] for k in ("g", "wz", "wxbc", "wdt", "wmb", "cw", "cb", "dtb", "alog", "dsk", "ng", "wpb")]
    tok = (np.arange(tm) // CHUNK) * CHUNK + np.tile(_chunk_tokens(), tm // CHUNK)
    p = (tok[:, None] == np.arange(tm)[None, :]).astype(np.float32)
    perm = jnp.asarray(np.stack([p, p.T]), dtype=BF16)
    tail_consts = [w[k] for k in ("wout", "fg", "tri", "e3")] + [perm]
    return pl.pallas_call(
        kern,
        out_shape=(jax.ShapeDtypeStruct((bsz, seq, d), F32),
                   jax.ShapeDtypeStruct((bsz, d_inner, SSD_STATE), F32),
                   jax.ShapeDtypeStruct((bsz, conv_w - 1, conv_dim), F32)),
        grid=(bsz, seq // tm),
        in_specs=[row_spec] + [_const_spec()] * len(consts) + [row_spec] + [_const_spec()] * len(tail_consts),
        out_specs=(row_spec,
                   pl.BlockSpec((1, d_inner, SSD_STATE), lambda b, j: (b, 0, 0)),
                   pl.BlockSpec((1, conv_w - 1, conv_dim), lambda b, j: (b, 0, 0))),
        scratch_shapes=[pltpu.VMEM(((conv_w - 1) * SUBLANES, conv_dim), F32), pltpu.VMEM((SSD_STATE, d_inner), F32)],
        compiler_params=_params(2),
        name="branch_b_prompt",
    )(x3, *consts, m3, *tail_consts)


def _bs_kernel(x_ref, g_ref, wz_ref, wxbc_ref, wdt_ref, wmb_ref, cw_ref, cb_ref, dtb_ref, alog_ref, dsk_ref,
               ng_ref, wpb_ref, min_ref, wout_ref, fg_ref, tri_ref, e3_ref, ecol_ref, conv0_ref, h0_ref,
               y_ref, hnew_ref, convnew_ref, xs_scr, tr_scr,
               *, nb, rows, n_groups, n_heads, head_dim, n_state, conv_w, final_norm):
    hpg = n_heads // n_groups
    gw = hpg * head_dim
    d_inner = n_heads * head_dim
    d = x_ref.shape[-1]
    r_all = nb * rows
    pad = SUBLANES
    hs_w = n_heads * rows

    @pl.when(pl.program_id(0) == 0)
    def _init():
        tr_scr[...] = jnp.zeros_like(tr_scr)

    x = x_ref[...].reshape(r_all, d)
    z, xbc_raw, dt, gm = _in_proj_b(x, g_ref, wz_ref, wxbc_ref, wdt_ref, wmb_ref, dtb_ref)

    convs = []
    for b in range(nb):
        xs_scr[b, pad - (conv_w - 1):pad, :] = conv0_ref[b]
        xs_scr[b, pad:pad + rows, :] = xbc_raw[b * rows:(b + 1) * rows]
        conv = cb_ref[...]
        for k in range(conv_w):
            conv = conv + xs_scr[b, pl.ds(pad - (conv_w - 1) + k, rows), :] * cw_ref[k:k + 1, :]
        convs.append(conv)
        convnew_ref[b] = xs_scr[b, pl.ds(pad + rows - (conv_w - 1), conv_w - 1), :]
    xbc = _silu(jnp.concatenate(convs, axis=0))
    xs = xbc[:, :d_inner]
    bm = xbc[:, d_inner:d_inner + n_groups * n_state]
    cm = xbc[:, d_inner + n_groups * n_state:]

    a_row = -jnp.exp(alog_ref[...])
    acs = _cumsum_rows(tri_ref[...], dt * a_row)
    e3 = e3_ref[...]
    acs_split = _split_by_replica(acs, n_heads).astype(BF16)
    acs_e = _dot(acs_split, e3)
    xdt = xs * _dot(_split_by_replica(dt, n_heads).astype(BF16), e3)
    decay_in = jnp.exp(acs_e)

    acs_col = _dot(acs_split, ecol_ref[...])
    l_idx = lax.broadcasted_iota(jnp.int32, (r_all, hs_w), 0) % rows
    s_idx = lax.broadcasted_iota(jnp.int32, (r_all, hs_w), 1) % rows
    on_diag = jnp.where(l_idx == s_idx, acs_col, 0.0).reshape(nb, rows, hs_w)
    acs_row = jnp.broadcast_to(jnp.sum(on_diag, axis=1, keepdims=True), (nb, rows, hs_w)).reshape(r_all, hs_w)
    decay = jnp.exp(jnp.where(l_idx >= s_idx, acs_col - acs_row, -jnp.inf))

    grp_of_row = lax.broadcasted_iota(jnp.int32, (hs_w, n_groups * n_state), 0) // (rows * hpg)
    grp_of_lane = lax.broadcasted_iota(jnp.int32, (hs_w, n_groups * n_state), 1) // n_state
    head_of_row = lax.broadcasted_iota(jnp.int32, (hs_w, d_inner), 0) // rows
    head_of_lane = lax.broadcasted_iota(jnp.int32, (hs_w, d_inner), 1) // head_dim
    ones_rows = lax.broadcasted_iota(jnp.int32, (LANES, n_state), 0)
    ones_blk = jnp.where((ones_rows >= rows) & (ones_rows < rows + 3), 1.0, 0.0)

    ys = []
    for b in range(nb):
        rs = slice(b * rows, (b + 1) * rows)
        bm_b, cm_b = bm[rs], cm[rs]
        cm_bb = cm_b.astype(BF16)
        b_rep = jnp.broadcast_to(bm_b[None], (n_heads, rows, bm_b.shape[1])).reshape(hs_w, bm_b.shape[1])
        b_exp = jnp.where(grp_of_row == grp_of_lane, b_rep, 0.0).astype(BF16)
        scores = _dot_nt(cm_bb, b_exp)
        m_all = (scores * decay[rs]).astype(BF16)
        xdt_b = xdt[rs]
        x_rep = jnp.broadcast_to(xdt_b[None], (n_heads, rows, d_inner)).reshape(hs_w, d_inner)
        x_blk = jnp.where(head_of_row == head_of_lane, x_rep, 0.0).astype(BF16)
        y_diag = _dot(m_all, x_blk)

        a_last = acs_e[(b + 1) * rows - 1:(b + 1) * rows, :]
        xd_state = xdt_b * jnp.exp(a_last - acs_e[rs])
        d_hi, d_mid, d_lo = _bf16_parts(jnp.exp(a_last))
        tr_scr[0:rows, :] = xd_state
        tr_scr[rows:rows + 1, :] = d_hi.astype(F32)
        tr_scr[rows + 1:rows + 2, :] = d_mid.astype(F32)
        tr_scr[rows + 2:rows + 3, :] = d_lo.astype(F32)
        tr_t = tr_scr[...].T.astype(BF16)

        y_parts = []
        for gi in range(n_groups):
            ns = slice(gi * n_state, (gi + 1) * n_state)
            gs = slice(gi * gw, (gi + 1) * gw)
            h0_g = h0_ref[b, gs, :]
            y_parts.append(_dot_nt(cm_bb[:, ns], h0_g.astype(BF16)))
            b_pad = jnp.concatenate([bm_b[:, ns], jnp.zeros((LANES - rows, n_state), F32)], axis=0)
            upd = _dot(tr_t[gs, :], jnp.concatenate([b_pad, ones_blk], axis=1).astype(BF16))
            hnew_ref[b, gs, :] = h0_g * upd[:, n_state:] + upd[:, :n_state]
        y_off = jnp.concatenate(y_parts, axis=1)
        ys.append(y_diag + y_off * decay_in[rs])
    y = jnp.concatenate(ys, axis=0) + xs * dsk_ref[...]

    m_in = min_ref[...].reshape(r_all, d)
    out = _finish_b(x, y, z, gm, ng_ref, wpb_ref, m_in, wout_ref, fg_ref, n_groups=n_groups, final_norm=final_norm)
    y_ref[...] = out.reshape(nb, rows, d)


def _branch_b_sample(x3, m3, conv0, h0, w, *, nb, final_norm):
    bsz, rows, d = x3.shape
    conv_dim = w["wxbc"].shape[1]
    d_inner = w["wz"].shape[1]
    n_heads = d_inner // SSD_HEAD_DIM
    conv_w = w["cw"].shape[0]
    kern = functools.partial(_bs_kernel, nb=nb, rows=rows, n_groups=SSD_GROUPS, n_heads=n_heads,
                             head_dim=SSD_HEAD_DIM, n_state=SSD_STATE, conv_w=conv_w, final_norm=final_norm)
    row_spec = pl.BlockSpec((nb, rows, d), lambda i: (i, 0, 0))
    conv_spec = pl.BlockSpec((nb, conv_w - 1, conv_dim), lambda i: (i, 0, 0))
    h_spec = pl.BlockSpec((nb, d_inner, SSD_STATE), lambda i: (i, 0, 0))
    consts = [w[k] for k in ("g", "wz", "wxbc", "wdt", "wmb", "cw", "cb", "dtb", "alog", "dsk", "ng", "wpb")]
    tail_consts = [w[k] for k in ("wout", "fg", "tri_s", "e3", "ecol")]
    return pl.pallas_call(
        kern,
        out_shape=(jax.ShapeDtypeStruct((bsz, rows, d), F32),
                   jax.ShapeDtypeStruct((bsz, d_inner, SSD_STATE), F32),
                   jax.ShapeDtypeStruct((bsz, conv_w - 1, conv_dim), F32)),
        grid=(bsz // nb,),
        in_specs=([row_spec] + [_const_spec()] * len(consts) + [row_spec] + [_const_spec()] * len(tail_consts)
                  + [conv_spec, h_spec]),
        out_specs=(row_spec, h_spec, conv_spec),
        scratch_shapes=[pltpu.VMEM((nb, 2 * SUBLANES, conv_dim), F32), pltpu.VMEM((LANES, d_inner), F32)],
        compiler_params=_params(1),
        name="branch_b_sample",
    )(x3, *consts, m3, *tail_consts, conv0, h0)


def _tile(n, pref):
    return pref if n % pref == 0 else n


def _chunk_tokens():
    q = np.arange(CHUNK)
    return (q % SUBLANES) * (CHUNK // SUBLANES) + q // SUBLANES


def _head_expand(n_heads, width, n_rep):
    j = np.arange(LANES)[:, None]
    c = np.arange(n_heads * width)[None, :]
    return jnp.asarray(((j < n_rep * n_heads) & (j % n_heads == c // width)).astype(np.float32), dtype=BF16)


def _layer_weights(l, d, dec_rows, nb_s, norm_g, w_in, conv_w, conv_b, dt_bias, a_log, d_skip, ssd_norm_g, ln_v_g,
                   ln_v_b, w_spatial, b_spatial, w_proj_a, w_proj_b, w_proj_x, w_out, final_norm_g):
    d_a = d
    d_inner = w_proj_b.shape[1]
    conv_dim = conv_w.shape[2]
    n_heads = a_log.shape[1]
    d_x = w_proj_x.shape[1]
    sizes = (d_a, d_a, d_a, d_inner, conv_dim, n_heads, d_x, d_x, N_BRANCH * d)
    offs = np.concatenate([[0], np.cumsum(sizes)])
    wl = w_in[l]
    sec = lambda i: wl[:, offs[i]:offs[i + 1]]
    merge = sec(8)
    row = lambda v: v.reshape(1, -1).astype(F32)
    rep3 = lambda v, fill: jnp.concatenate(
        [v] * DT_REPLICAS + [jnp.full(v.shape[:-1] + (LANES - DT_REPLICAS * v.shape[-1],), fill, v.dtype)], axis=-1)

    tril = jnp.tril(jnp.ones((CHUNK, CHUNK), F32))
    ws_p = jnp.where(tril[None] > 0, w_spatial[l], 0.0)
    bs_p = jnp.repeat(b_spatial[l].T, d_a // A_GROUPS, axis=1)
    n_seq = CHUNK // dec_rows
    eye = jnp.eye(n_seq, dtype=F32)
    ws_s = jnp.stack([jnp.kron(eye, ws_p[gi, :dec_rows, :dec_rows]) for gi in range(A_GROUPS)])
    bs_s = jnp.tile(bs_p[:dec_rows], (n_seq, 1))

    w = {
        "g": row(norm_g[l]),
        "wu": _pack_w(sec(0)), "wv": _pack_w(sec(1)), "wga": _pack_w(sec(2)),
        "wma": _pack_w(merge[:, :d]),
        "lng": row(ln_v_g[l]), "lnb": row(ln_v_b[l]),
        "ws_p": ws_p.astype(BF16), "bs_p": bs_p, "ws_s": ws_s.astype(BF16), "bs_s": bs_s,
        "wpa": _pack_w(w_proj_a[l]),
        "wz": _pack_w(sec(3)), "wxbc": _pack_w(sec(4)),
        "wdt": _pack_w(rep3(sec(5), 0.0)),
        "wmb": _pack_w(merge[:, d:2 * d]),
        "cw": conv_w[l].astype(F32), "cb": row(conv_b[l]),
        "dtb": rep3(row(dt_bias[l]), 0.0), "alog": rep3(row(a_log[l]), 0.0),
        "dsk": jnp.repeat(row(d_skip[l]), SSD_HEAD_DIM, axis=1),
        "ng": row(ssd_norm_g[l]),
        "wpb": _pack_w(w_proj_b[l]),
        "wout": _pack_w(w_out[l]), "fg": row(final_norm_g),
        "tri": jnp.asarray(_chunk_tokens()[:, None] >= _chunk_tokens()[None, :], dtype=BF16),
        "tri_s": jnp.kron(jnp.eye(nb_s, dtype=F32), jnp.tril(jnp.ones((dec_rows, dec_rows), F32))).astype(BF16),
        "e3": _head_expand(n_heads, SSD_HEAD_DIM, DT_REPLICAS),
        "ecol": _head_expand(n_heads, dec_rows, DT_REPLICAS),
        "wq": _pack_w(sec(6)), "wgx": _pack_w(sec(7)), "wmx": _pack_w(merge[:, 2 * d:]),
        "wpx": _pack_w(w_proj_x[l]),
    }
    return w


def _layer(x3, k3, v3, w, *, nb_x, rows_x, heads_in_rows, tm_a, ws, bs, want_v, b_fn):
    bsz, seq, d = x3.shape
    m = _branch_x(x3, w["g"], w["wq"], w["wgx"], w["wmx"], k3, v3, w["wpx"], nb=nb_x, rows=rows_x,
                  heads_in_rows=heads_in_rows)
    m, vn = _branch_a(x3.reshape(bsz * seq, d), w["g"], w["wu"], w["wv"], w["wga"], w["wma"], w["lng"], w["lnb"],
                      ws, bs, w["wpa"], m.reshape(bsz * seq, d), tm=tm_a, want_v=want_v)
    return b_fn(x3, m.reshape(bsz, seq, d)), vn


def kernel(x_prompt, x_sample, mem_prompt, cache_mem_k, cache_mem_v, state_ssm, state_conv, norm_g, w_in, conv_w,
           conv_b, dt_bias, a_log, d_skip, ssd_norm_g, ln_v_g, ln_v_b, w_spatial, b_spatial, mem_norm_g, w_mem_kv,
           w_proj_a, w_proj_b, w_proj_x, w_out, final_norm_g):
    depth = w_in.shape[0]
    bsz, seq, d = x_prompt.shape
    dec_b, dec_rows, _ = x_sample.shape
    n_mem = mem_prompt.shape[1]
    d_x = w_proj_x.shape[1]
    d_inner = w_proj_b.shape[1]
    n_heads = a_log.shape[1]
    assert seq % CHUNK == 0 and CHUNK % dec_rows == 0 and dec_rows == SUBLANES
    assert DT_REPLICAS * n_heads <= LANES and n_heads * dec_rows % LANES == 0

    nb_s = _tile(dec_b, 4)
    nb_xs = _tile(dec_b, 8)
    yp, ys = x_prompt, x_sample
    outs = {k: [] for k in ("mk", "mv", "hp", "cp", "hs", "cs", "vs")}
    for l in range(depth):
        w = _layer_weights(l, d, dec_rows, nb_s, norm_g, w_in, conv_w, conv_b, dt_bias, a_log, d_skip, ssd_norm_g,
                           ln_v_g, ln_v_b, w_spatial, b_spatial, w_proj_a, w_proj_b, w_proj_x, w_out, final_norm_g)
        final_norm = l == depth - 1
        mk, mv = _mem_kv(mem_prompt.reshape(bsz * n_mem, d), mem_norm_g[l].reshape(1, d).astype(F32),
                         _pack_w(w_mem_kv[l]), _tile(bsz * n_mem, 512))
        (yp, hp, cp), _ = _layer(
            yp, mk.reshape(bsz, n_mem, d_x), mv.reshape(bsz, n_mem, d_x), w,
            nb_x=1, rows_x=_tile(seq, 512), heads_in_rows=False, tm_a=_tile(bsz * seq, 512), ws=w["ws_p"], bs=w["bs_p"], want_v=False,
            b_fn=functools.partial(_branch_b_prompt, w=w, tm=_tile(seq, 256), final_norm=final_norm))
        (ys, hs, cs), vs = _layer(
            ys, cache_mem_k[l].reshape(dec_b, n_mem * (d_x // X_HEAD_DIM), X_HEAD_DIM),
            cache_mem_v[l].reshape(dec_b, n_mem * (d_x // X_HEAD_DIM), X_HEAD_DIM), w,
            nb_x=nb_xs, rows_x=dec_rows, heads_in_rows=True, tm_a=_tile(dec_b * dec_rows, 512), ws=w["ws_s"], bs=w["bs_s"], want_v=True,
            b_fn=functools.partial(_branch_b_sample, conv0=state_conv[l],
                                   h0=state_ssm[l].reshape(dec_b, d_inner, SSD_STATE), w=w, nb=nb_s,
                                   final_norm=final_norm))
        outs["mk"].append(mk.reshape(bsz, n_mem, d_x // X_HEAD_DIM, X_HEAD_DIM))
        outs["mv"].append(mv.reshape(bsz, n_mem, d_x // X_HEAD_DIM, X_HEAD_DIM))
        outs["hp"].append(hp.reshape(bsz, n_heads, SSD_HEAD_DIM, SSD_STATE))
        outs["cp"].append(cp)
        outs["hs"].append(hs.reshape(dec_b, n_heads, SSD_HEAD_DIM, SSD_STATE))
        outs["cs"].append(cs)
        outs["vs"].append(vs.reshape(dec_b, dec_rows, d))
    st = lambda k: jnp.stack(outs[k])
    return (yp, ys, st("mk"), st("mv"), st("hp"), st("cp"), st("hs"), st("cs"), st("vs"))
```

```python
import functools
import math

import jax
import jax.numpy as jnp
import numpy as np
from jax import lax
from jax.experimental import pallas as pl
from jax.experimental.pallas import tpu as pltpu

F32 = jnp.float32
BF16 = jnp.bfloat16
EPS = 1e-6
SQRT_HALF = math.sqrt(0.5)

LANES = 128
SUBLANES = 8
VMEM_LIMIT_BYTES = 56 * 1024 * 1024

CHUNK = 128
A_GROUPS = 8
SSD_HEAD_DIM = 64
SSD_GROUPS = 8
SSD_STATE = 128
X_HEAD_DIM = 128
N_BRANCH = 3
DT_REPLICAS = 3


def _dot(a, b):
    return jnp.dot(a, b, preferred_element_type=F32)


def _dot_nt(a, b):
    return lax.dot_general(a, b, (((1,), (1,)), ((), ())), preferred_element_type=F32)


def _dot_tn(a, b):
    return lax.dot_general(a, b, (((0,), (0,)), ((), ())), preferred_element_type=F32)


MXU_COLS = 256


def _pack_w(w):
    return w.astype(BF16)


def _dotw(a, w_ref, cols=slice(None)):
    return _dot(a, w_ref[:, cols])


def _dot_cols(a, w_ref, fn, width=MXU_COLS):
    n = w_ref.shape[1]
    width = min(width, n)
    blocks = [fn(_dotw(a, w_ref, slice(c, c + width))) for c in range(0, n, width)]
    return blocks[0] if len(blocks) == 1 else jnp.concatenate(blocks, axis=1)


def _rmsnorm(x, g):
    return x * lax.rsqrt(jnp.mean(x * x, axis=-1, keepdims=True) + EPS) * g


def _gelu(x):
    return 0.5 * x * (1.0 + lax.erf(x * SQRT_HALF))


def _sigmoid(x):
    return 0.5 * jnp.tanh(0.5 * x) + 0.5


def _silu(x):
    return x * _sigmoid(x)


def _bf16_parts(x):
    hi = x.astype(BF16)
    r1 = x - hi.astype(F32)
    mid = r1.astype(BF16)
    lo = (r1 - mid.astype(F32)).astype(BF16)
    return hi, mid, lo


def _split_by_replica(x, n_heads):
    hi, mid, lo = _bf16_parts(x)
    lane = lax.broadcasted_iota(jnp.int32, x.shape, 1)
    return jnp.where(lane < n_heads, hi, jnp.where(lane < 2 * n_heads, mid, lo))


def _cumsum_rows(tri, x):
    hi, mid, lo = _bf16_parts(x)
    return _dot(tri, hi) + _dot(tri, mid) + _dot(tri, lo)


def _memkv_kernel(x_ref, g_ref, w_ref, k_ref, v_ref):
    hn = _rmsnorm(x_ref[...], g_ref[...]).astype(BF16)
    kv = _dotw(hn, w_ref)
    dx = k_ref.shape[-1]
    k_ref[...] = kv[:, :dx]
    v_ref[...] = kv[:, dx:]


def _const_spec():
    return pl.BlockSpec(memory_space=pltpu.VMEM)


def _params(n_grid):
    return pltpu.CompilerParams(dimension_semantics=("arbitrary",) * n_grid, vmem_limit_bytes=VMEM_LIMIT_BYTES)


def _mem_kv(mem2d, g, w, tm):
    rows, d = mem2d.shape
    dx = w.shape[1] // 2
    return pl.pallas_call(
        _memkv_kernel,
        out_shape=(jax.ShapeDtypeStruct((rows, dx), F32), jax.ShapeDtypeStruct((rows, dx), F32)),
        grid=(rows // tm,),
        in_specs=[pl.BlockSpec((tm, d), lambda i: (i, 0)), _const_spec(), _const_spec()],
        out_specs=(pl.BlockSpec((tm, dx), lambda i: (i, 0)), pl.BlockSpec((tm, dx), lambda i: (i, 0))),
        compiler_params=_params(1),
        name="mem_kv",
    )(mem2d, g, w)


def _x_kernel(x_ref, g_ref, wq_ref, wgx_ref, wmx_ref, k_ref, v_ref, wpx_ref, m_ref, *, nb, rows, head_dim):
    d = x_ref.shape[-1]
    x = x_ref[...].reshape(nb * rows, d)
    hn = _rmsnorm(x, g_ref[...]).astype(BF16)
    q = _dotw(hn, wq_ref)
    gate = _silu(_dotw(hn, wgx_ref))
    gm = _sigmoid(_dotw(hn, wmx_ref))
    n_heads = q.shape[1] // head_dim
    scale = head_dim ** -0.5
    outs = []
    for b in range(nb):
        kb = k_ref[b].astype(BF16)
        vb = v_ref[b].astype(BF16)
        qb = q[b * rows:(b + 1) * rows].astype(BF16)
        heads = []
        for h in range(n_heads):
            hs = slice(h * head_dim, (h + 1) * head_dim)
            s = _dot_nt(qb[:, hs], kb[:, hs]) * scale
            e = jnp.exp(s - jnp.max(s, axis=-1, keepdims=True))
            p = e / jnp.sum(e, axis=-1, keepdims=True)
            heads.append(_dot(p.astype(BF16), vb[:, hs]))
        outs.append(jnp.concatenate(heads, axis=-1))
    o = outs[0] if nb == 1 else jnp.concatenate(outs, axis=0)
    hx = (o * gate).astype(BF16)
    m = gm * _dotw(hx, wpx_ref)
    m_ref[...] = m.reshape(nb, rows, d)


def _xs_kernel(x_ref, g_ref, wq_ref, wgx_ref, wmx_ref, k_ref, v_ref, wpx_ref, m_ref, *, nb, rows, head_dim):
    d = x_ref.shape[-1]
    x = x_ref[...].reshape(nb * rows, d)
    hn = _rmsnorm(x, g_ref[...]).astype(BF16)
    q = _dotw(hn, wq_ref)
    gate = _silu(_dotw(hn, wgx_ref))
    gm = _sigmoid(_dotw(hn, wmx_ref))
    n_heads = q.shape[1] // head_dim
    n_kv = k_ref.shape[1]
    scale = head_dim ** -0.5
    row_head = lax.broadcasted_iota(jnp.int32, (n_heads * rows, n_kv), 0) // rows
    col_head = lax.broadcasted_iota(jnp.int32, (n_heads * rows, n_kv), 1) % n_heads
    same_head = row_head == col_head
    outs = []
    for b in range(nb):
        kb = k_ref[b].astype(BF16)
        vb = v_ref[b].astype(BF16)
        qb = q[b * rows:(b + 1) * rows]
        q_heads = jnp.concatenate([qb[:, h * head_dim:(h + 1) * head_dim] for h in range(n_heads)], axis=0)
        s = jnp.where(same_head, _dot_nt(q_heads.astype(BF16), kb) * scale, -jnp.inf)
        e = jnp.exp(s - jnp.max(s, axis=-1, keepdims=True))
        p = e / jnp.sum(e, axis=-1, keepdims=True)
        o_heads = _dot(p.astype(BF16), vb)
        outs.append(jnp.concatenate([o_heads[h * rows:(h + 1) * rows] for h in range(n_heads)], axis=1))
    o = outs[0] if nb == 1 else jnp.concatenate(outs, axis=0)
    hx = (o * gate).astype(BF16)
    m = gm * _dotw(hx, wpx_ref)
    m_ref[...] = m.reshape(nb, rows, d)


def _branch_x(x3, g, wq, wgx, wmx, k3, v3, wpx, *, nb, rows, heads_in_rows):
    bsz, seq, d = x3.shape
    n_mem, dx = k3.shape[1], k3.shape[2]
    kern = functools.partial(_xs_kernel if heads_in_rows else _x_kernel, nb=nb, rows=rows, head_dim=X_HEAD_DIM)
    return pl.pallas_call(
        kern,
        out_shape=jax.ShapeDtypeStruct((bsz, seq, d), F32),
        grid=(bsz // nb, seq // rows),
        in_specs=[
            pl.BlockSpec((nb, rows, d), lambda i, j: (i, j, 0)),
            _const_spec(), _const_spec(), _const_spec(), _const_spec(),
            pl.BlockSpec((nb, n_mem, dx), lambda i, j: (i, 0, 0)),
            pl.BlockSpec((nb, n_mem, dx), lambda i, j: (i, 0, 0)),
            _const_spec(),
        ],
        out_specs=pl.BlockSpec((nb, rows, d), lambda i, j: (i, j, 0)),
        compiler_params=_params(2),
        name="branch_x",
    )(x3, g, wq, wgx, wmx, k3, v3, wpx)


def _a_kernel(x_ref, g_ref, wu_ref, wv_ref, wga_ref, wma_ref, lng_ref, lnb_ref, ws_ref, bs_ref, wpa_ref, min_ref,
              mout_ref, *maybe_v_ref, chunk, groups, sub):
    tm, d = x_ref.shape
    gd = d // groups
    bias = bs_ref[...]
    for t in range(tm // sub):
        ts = slice(t * sub, (t + 1) * sub)
        hn = _rmsnorm(x_ref[ts, :], g_ref[...]).astype(BF16)
        v = _dot_cols(hn, wv_ref, _gelu)
        u = _dot_cols(hn, wu_ref, _gelu)
        vc = v - jnp.mean(v, axis=-1, keepdims=True)
        vn = vc * lax.rsqrt(jnp.mean(vc * vc, axis=-1, keepdims=True) + EPS) * lng_ref[...] + lnb_ref[...]
        if maybe_v_ref:
            maybe_v_ref[0][ts, :] = vn
        ga = _dot_cols(hn, wga_ref, _silu)
        vb = vn.astype(BF16)
        mixed = []
        for c in range(sub // chunk):
            rs = slice(c * chunk, (c + 1) * chunk)
            cols = [_dot(ws_ref[gi], vb[rs, gi * gd:(gi + 1) * gd]) for gi in range(groups)]
            mixed.append(jnp.concatenate(cols, axis=-1) + bias)
        s = mixed[0] if len(mixed) == 1 else jnp.concatenate(mixed, axis=0)
        ha = (u * s * ga).astype(BF16)
        gm = _sigmoid(_dotw(hn, wma_ref))
        mout_ref[ts, :] = min_ref[ts, :] + gm * _dotw(ha, wpa_ref)


def _branch_a(x2, g, wu, wv, wga, wma, lng, lnb, ws, bs, wpa, m_in, *, tm, want_v):
    rows, d = x2.shape
    row_spec = pl.BlockSpec((tm, d), lambda i: (i, 0))
    out_shape = [jax.ShapeDtypeStruct((rows, d), F32)]
    out_specs = [row_spec]
    if want_v:
        out_shape.append(jax.ShapeDtypeStruct((rows, d), F32))
        out_specs.append(row_spec)
    kern = functools.partial(_a_kernel, chunk=CHUNK, groups=A_GROUPS, sub=min(tm, 256))
    res = pl.pallas_call(
        kern,
        out_shape=tuple(out_shape),
        grid=(rows // tm,),
        in_specs=[row_spec] + [_const_spec()] * 10 + [row_spec],
        out_specs=tuple(out_specs),
        compiler_params=_params(1),
        name="branch_a",
    )(x2, g, wu, wv, wga, wma, lng, lnb, ws, bs, wpa, m_in)
    return res if want_v else (res[0], None)


def _in_proj_b(x, g_ref, wz_ref, wxbc_ref, wdt_ref, wmb_ref, dtb_ref):
    hn = _rmsnorm(x, g_ref[...]).astype(BF16)
    gm = _sigmoid(_dotw(hn, wmb_ref))
    z = _dotw(hn, wz_ref)
    xbc = _dotw(hn, wxbc_ref)
    dt = jax.nn.softplus(_dotw(hn, wdt_ref) + dtb_ref[...])
    return z, xbc, dt, gm


def _finish_b(x, y, z, gm, ng_ref, wpb_ref, m_in, wout_ref, fg_ref, *, n_groups, final_norm):
    yf = y * _silu(z)
    gw = yf.shape[1] // n_groups
    parts = []
    for gi in range(n_groups):
        yg = yf[:, gi * gw:(gi + 1) * gw]
        parts.append(yg * lax.rsqrt(jnp.mean(yg * yg, axis=-1, keepdims=True) + EPS))
    hb = (jnp.concatenate(parts, axis=-1) * ng_ref[...]).astype(BF16)
    m = m_in + gm * _dotw(hb, wpb_ref)
    out = x + _dotw(m.astype(BF16), wout_ref)
    return _rmsnorm(out, fg_ref[...]) if final_norm else out


HANDOFF = ("xs", "bm", "cm", "zs", "dt", "gm", "tail")


def _bp_stage1(x_ref, g_ref, wz_ref, wxbc_ref, wdt_ref, wmb_ref, cw_ref, cb_ref, dtb_ref, perm_ref, xs_scr, out,
               first_tile, *, tm, chunk, d_inner, bc_width, conv_w):
    pad = SUBLANES
    vpc = chunk // pad
    n_chunks = tm // chunk
    width = MXU_COLS
    hn = _rmsnorm(x_ref[0], g_ref[...]).astype(BF16)
    for c0 in range(0, out["gm"].shape[1], width):
        cols = slice(c0, c0 + width)
        out["gm"][:, cols] = _sigmoid(_dotw(hn, wmb_ref, cols))
        yield
    hn = _dot(perm_ref[0], hn).astype(BF16)
    out["dt"][...] = jax.nn.softplus(_dotw(hn, wdt_ref) + dtb_ref[...])
    yield
    for c0 in range(0, d_inner, width):
        cols = slice(c0, c0 + width)
        out["zs"][:, cols] = _silu(_dotw(hn, wz_ref, cols))
        yield
    first_sublane = lax.broadcasted_iota(jnp.int32, (pad, width), 0) == 0
    for c0 in range(0, wxbc_ref.shape[1], width):
        cols = slice(c0, c0 + width)
        raw_all = _dotw(hn, wxbc_ref, cols)
        prev_tail = jnp.where(first_tile, 0.0, xs_scr[:, cols])
        acts = []
        for c in range(n_chunks):
            raw = raw_all[c * chunk:(c + 1) * chunk]
            conv = cb_ref[:, cols] + raw * cw_ref[conv_w - 1:conv_w, cols]
            for back in range(1, conv_w):
                head = []
                for v in range(back):
                    w = vpc - back + v
                    cur = pltpu.roll(raw[w * pad:(w + 1) * pad], 1, axis=0)
                    prv = pltpu.roll(prev_tail[(w - vpc + conv_w - 1) * pad:(w - vpc + conv_w) * pad], 1, axis=0)
                    head.append(jnp.where(first_sublane, prv, cur))
                shifted = jnp.concatenate(head + [raw[:chunk - back * pad]], axis=0)
                conv = conv + shifted * cw_ref[conv_w - 1 - back:conv_w - back, cols]
            acts.append(_silu(conv))
            prev_tail = raw[chunk - (conv_w - 1) * pad:]
        xs_scr[:, cols] = prev_tail
        out["tail"][:, cols] = prev_tail
        act = acts[0] if n_chunks == 1 else jnp.concatenate(acts, axis=0)
        if c0 < d_inner:
            out["xs"][:, cols] = act
        elif c0 < d_inner + bc_width:
            out["bm"][:, c0 - d_inner:c0 - d_inner + width] = act.astype(BF16)
        else:
            out["cm"][:, c0 - d_inner - bc_width:c0 - d_inner - bc_width + width] = act.astype(BF16)
        yield


def _bp_stage2(x_ref, min_ref, alog_ref, dsk_ref, ng_ref, wpb_ref, wout_ref, fg_ref, tri_ref, e3_ref, perm_ref,
               y_ref, st_scr, inp, first_tile, *, tm, chunk, n_groups, n_heads, head_dim, n_state, final_norm):
    pad = SUBLANES
    vpc = chunk // pad
    hpg = n_heads // n_groups
    gw = hpg * head_dim
    width = MXU_COLS

    def token_of(q):
        return (q % pad) * vpc + q // pad

    li = token_of(lax.broadcasted_iota(jnp.int32, (chunk, chunk), 0))
    si = token_of(lax.broadcasted_iota(jnp.int32, (chunk, chunk), 1))
    causal = li >= si
    head_of_lane = lax.broadcasted_iota(jnp.int32, (chunk, gw), 1) // head_dim
    tri = tri_ref[...]
    e3 = e3_ref[...]
    a_row = -jnp.exp(alog_ref[...])

    hb_rows = []
    for c in range(tm // chunk):
        rs = slice(c * chunk, (c + 1) * chunk)
        dt = inp["dt"][rs, :]
        acs = _cumsum_rows(tri, dt * a_row)
        acs_t = acs.T
        acs_e = _dot(_split_by_replica(acs, n_heads).astype(BF16), e3)
        a_last = acs_e[chunk - 1:chunk, :]
        xs_c = inp["xs"][rs, :]
        xdt_c = xs_c * _dot(_split_by_replica(dt, n_heads).astype(BF16), e3)
        xdt_b = xdt_c.astype(BF16)
        xd_state = (xdt_c * jnp.exp(a_last - acs_e)).astype(BF16)
        decay_in = jnp.exp(acs_e)
        st_prev = st_scr[...]
        if c == 0:
            st_prev = jnp.where(first_tile, 0.0, st_prev)
        st_prev_b = st_prev.astype(BF16)
        yield
        hb_parts, st_parts = [], []
        for gi in range(n_groups):
            ns = slice(gi * n_state, (gi + 1) * n_state)
            gs = slice(gi * gw, (gi + 1) * gw)
            cg, bg = inp["cm"][rs, ns], inp["bm"][rs, ns]
            scores = _dot_nt(cg, bg)
            m_heads, x_blocks = [], []
            xg = xdt_b[:, gs]
            for r in range(hpg):
                h = gi * hpg + r
                diff = acs[:, h:h + 1] - acs_t[h:h + 1, :]
                decay = jnp.exp(jnp.where(causal, diff, -jnp.inf))
                m_heads.append((scores * decay).astype(BF16))
                x_blocks.append(jnp.where(head_of_lane == r, xg, jnp.zeros_like(xg)))
            y_diag = _dot(jnp.concatenate(m_heads, axis=1), jnp.concatenate(x_blocks, axis=0))
            y_off = _dot(cg, st_prev_b[:, gs])
            y = y_diag + y_off * decay_in[:, gs] + xs_c[:, gs] * dsk_ref[:, gs]
            yf = y * inp["zs"][rs, gs]
            yn = yf * lax.rsqrt(jnp.mean(yf * yf, axis=-1, keepdims=True) + EPS)
            hb_parts.append((yn * ng_ref[:, gs]).astype(BF16))
            st_parts.append(_dot_tn(bg, xd_state[:, gs]))
            yield
        st_scr[...] = st_prev * jnp.exp(a_last) + jnp.concatenate(st_parts, axis=1)
        hb_rows.append(jnp.concatenate(hb_parts, axis=1))
    hb = hb_rows[0] if len(hb_rows) == 1 else jnp.concatenate(hb_rows, axis=0)
    hb = _dot(perm_ref[1], hb).astype(BF16)
    yield
    m_parts = []
    for c0 in range(0, wpb_ref.shape[1], width):
        cols = slice(c0, c0 + width)
        m_parts.append((min_ref[0, :, cols] + inp["gm"][:, cols] * _dotw(hb, wpb_ref, cols)).astype(BF16))
        yield
    m = jnp.concatenate(m_parts, axis=1)
    out = x_ref[0] + _dotw(m, wout_ref)
    y_ref[0] = _rmsnorm(out, fg_ref[...]) if final_norm else out
    yield


def _bp_kernel(x1_ref, x2_ref, min_ref, g_ref, wz_ref, wxbc_ref, wdt_ref, wmb_ref, cw_ref, cb_ref, dtb_ref, alog_ref,
               dsk_ref, ng_ref, wpb_ref, wout_ref, fg_ref, tri_ref, e3_ref, perm_ref,
               y_ref, ssm_ref, conv_ref, xs_scr, st_scr, *slot_refs,
               nj, n_tiles, tm, chunk, n_groups, n_heads, head_dim, n_state, conv_w, final_norm):
    t = pl.program_id(0)
    d_inner = n_heads * head_dim
    pad = SUBLANES
    tile1 = jnp.minimum(t, n_tiles - 1)
    tile2 = jnp.maximum(t - 1, 0)
    n_h = len(HANDOFF)
    slots = [dict(zip(HANDOFF, slot_refs[i * n_h:(i + 1) * n_h])) for i in range(2)]

    @pl.when(t == 0)
    def _init():
        for ref in slots[1].values():
            ref[...] = jnp.zeros_like(ref)
        st_scr[...] = jnp.zeros_like(st_scr)
        xs_scr[...] = jnp.zeros_like(xs_scr)

    def step(slot_w, slot_r):
        s1 = _bp_stage1(x1_ref, g_ref, wz_ref, wxbc_ref, wdt_ref, wmb_ref, cw_ref, cb_ref, dtb_ref, perm_ref, xs_scr,
                        slot_w, tile1 % nj == 0, tm=tm, chunk=chunk, d_inner=d_inner,
                        bc_width=n_groups * n_state, conv_w=conv_w)
        s2 = _bp_stage2(x2_ref, min_ref, alog_ref, dsk_ref, ng_ref, wpb_ref, wout_ref, fg_ref, tri_ref, e3_ref,
                        perm_ref, y_ref, st_scr, slot_r, tile2 % nj == 0, tm=tm, chunk=chunk, n_groups=n_groups,
                        n_heads=n_heads, head_dim=head_dim, n_state=n_state, final_norm=final_norm)
        live = [s2, s1]
        while live:
            for gen in list(live):
                if next(gen, StopIteration) is StopIteration:
                    live.remove(gen)

        @pl.when((t > 0) & (tile2 % nj == nj - 1))
        def _seq_end():
            tail = slot_r["tail"][...]
            conv_ref[0] = jnp.concatenate([tail[(v + 1) * pad - 1:(v + 1) * pad] for v in range(conv_w - 1)], axis=0)
            ssm_ref[0] = st_scr[...].T

    for parity in range(2):
        pl.when(t % 2 == parity)(functools.partial(step, slots[parity], slots[1 - parity]))


def _branch_b_prompt(x3, m3, w, *, tm, final_norm):
    bsz, seq, d = x3.shape
    conv_dim = w["wxbc"].shape[1]
    d_inner = w["wz"].shape[1]
    n_heads = d_inner // SSD_HEAD_DIM
    conv_w = w["cw"].shape[0]
    nj = seq // tm
    n_tiles = bsz * nj
    bc_width = SSD_GROUPS * SSD_STATE
    kern = functools.partial(_bp_kernel, nj=nj, n_tiles=n_tiles, tm=tm, chunk=CHUNK, n_groups=SSD_GROUPS,
                             n_heads=n_heads, head_dim=SSD_HEAD_DIM, n_state=SSD_STATE, conv_w=conv_w,
                             final_norm=final_norm)

    def tile1(t):
        return jnp.minimum(t, n_tiles - 1)

    def tile2(t):
        return jnp.maximum(t - 1, 0)

    spec1 = pl.BlockSpec((1, tm, d), lambda t: (tile1(t) // nj, tile1(t) % nj, 0))
    spec2 = pl.BlockSpec((1, tm, d), lambda t: (tile2(t) // nj, tile2(t) % nj, 0))
    consts = [w[k] for k in ("g", "wz", "wxbc", "wdt", "wmb", "cw", "cb", "dtb", "alog", "dsk", "ng", "wpb", "wout",
                             "fg", "tri", "e3")]
    tok = (np.arange(tm) // CHUNK) * CHUNK + np.tile(_chunk_tokens(), tm // CHUNK)
    p = (tok[:, None] == np.arange(tm)[None, :]).astype(np.float32)
    perm = jnp.asarray(np.stack([p, p.T]), dtype=BF16)
    handoff = {"xs": ((tm, d_inner), F32), "bm": ((tm, bc_width), BF16), "cm": ((tm, bc_width), BF16),
               "zs": ((tm, d_inner), F32), "dt": ((tm, LANES), F32), "gm": ((tm, d), F32),
               "tail": (((conv_w - 1) * SUBLANES, conv_dim), F32)}
    return pl.pallas_call(
        kern,
        out_shape=(jax.ShapeDtypeStruct((bsz, seq, d), F32),
                   jax.ShapeDtypeStruct((bsz, d_inner, SSD_STATE), F32),
                   jax.ShapeDtypeStruct((bsz, conv_w - 1, conv_dim), F32)),
        grid=(n_tiles + 1,),
        in_specs=[spec1, spec2, spec2] + [_const_spec()] * (len(consts) + 1),
        out_specs=(spec2,
                   pl.BlockSpec((1, d_inner, SSD_STATE), lambda t: (tile2(t) // nj, 0, 0)),
                   pl.BlockSpec((1, conv_w - 1, conv_dim), lambda t: (tile2(t) // nj, 0, 0))),
        scratch_shapes=([pltpu.VMEM(((conv_w - 1) * SUBLANES, conv_dim), F32), pltpu.VMEM((SSD_STATE, d_inner), F32)]
                        + [pltpu.VMEM(*handoff[k]) for _ in range(2) for k in HANDOFF]),
        compiler_params=_params(1),
        name="branch_b_prompt",
    )(x3, x3, m3, *consts, perm)


def _bs_kernel(x_ref, g_ref, wz_ref, wxbc_ref, wdt_ref, wmb_ref, cw_ref, cb_ref, dtb_ref, alog_ref, dsk_ref,
               ng_ref, wpb_ref, min_ref, wout_ref, fg_ref, tri_ref, e3_ref, ecol_ref, conv0_ref, h0_ref,
               y_ref, hnew_ref, convnew_ref, xs_scr, tr_scr,
               *, nb, rows, n_groups, n_heads, head_dim, n_state, conv_w, final_norm):
    hpg = n_heads // n_groups
    gw = hpg * head_dim
    d_inner = n_heads * head_dim
    d = x_ref.shape[-1]
    r_all = nb * rows
    pad = SUBLANES
    hs_w = n_heads * rows

    @pl.when(pl.program_id(0) == 0)
    def _init():
        tr_scr[...] = jnp.zeros_like(tr_scr)

    x = x_ref[...].reshape(r_all, d)
    z, xbc_raw, dt, gm = _in_proj_b(x, g_ref, wz_ref, wxbc_ref, wdt_ref, wmb_ref, dtb_ref)

    convs = []
    for b in range(nb):
        xs_scr[b, pad - (conv_w - 1):pad, :] = conv0_ref[b]
        xs_scr[b, pad:pad + rows, :] = xbc_raw[b * rows:(b + 1) * rows]
        conv = cb_ref[...]
        for k in range(conv_w):
            conv = conv + xs_scr[b, pl.ds(pad - (conv_w - 1) + k, rows), :] * cw_ref[k:k + 1, :]
        convs.append(conv)
        convnew_ref[b] = xs_scr[b, pl.ds(pad + rows - (conv_w - 1), conv_w - 1), :]
    xbc = _silu(jnp.concatenate(convs, axis=0))
    xs = xbc[:, :d_inner]
    bm = xbc[:, d_inner:d_inner + n_groups * n_state]
    cm = xbc[:, d_inner + n_groups * n_state:]

    a_row = -jnp.exp(alog_ref[...])
    acs = _cumsum_rows(tri_ref[...], dt * a_row)
    e3 = e3_ref[...]
    acs_split = _split_by_replica(acs, n_heads).astype(BF16)
    acs_e = _dot(acs_split, e3)
    xdt = xs * _dot(_split_by_replica(dt, n_heads).astype(BF16), e3)
    decay_in = jnp.exp(acs_e)

    acs_col = _dot(acs_split, ecol_ref[...])
    l_idx = lax.broadcasted_iota(jnp.int32, (r_all, hs_w), 0) % rows
    s_idx = lax.broadcasted_iota(jnp.int32, (r_all, hs_w), 1) % rows
    on_diag = jnp.where(l_idx == s_idx, acs_col, 0.0).reshape(nb, rows, hs_w)
    acs_row = jnp.broadcast_to(jnp.sum(on_diag, axis=1, keepdims=True), (nb, rows, hs_w)).reshape(r_all, hs_w)
    decay = jnp.exp(jnp.where(l_idx >= s_idx, acs_col - acs_row, -jnp.inf))

    grp_of_row = lax.broadcasted_iota(jnp.int32, (hs_w, n_groups * n_state), 0) // (rows * hpg)
    grp_of_lane = lax.broadcasted_iota(jnp.int32, (hs_w, n_groups * n_state), 1) // n_state
    head_of_row = lax.broadcasted_iota(jnp.int32, (hs_w, d_inner), 0) // rows
    head_of_lane = lax.broadcasted_iota(jnp.int32, (hs_w, d_inner), 1) // head_dim
    ones_rows = lax.broadcasted_iota(jnp.int32, (LANES, n_state), 0)
    ones_blk = jnp.where((ones_rows >= rows) & (ones_rows < rows + 3), 1.0, 0.0)

    ys = []
    for b in range(nb):
        rs = slice(b * rows, (b + 1) * rows)
        bm_b, cm_b = bm[rs], cm[rs]
        cm_bb = cm_b.astype(BF16)
        b_rep = jnp.broadcast_to(bm_b[None], (n_heads, rows, bm_b.shape[1])).reshape(hs_w, bm_b.shape[1])
        b_exp = jnp.where(grp_of_row == grp_of_lane, b_rep, 0.0).astype(BF16)
        scores = _dot_nt(cm_bb, b_exp)
        m_all = (scores * decay[rs]).astype(BF16)
        xdt_b = xdt[rs]
        x_rep = jnp.broadcast_to(xdt_b[None], (n_heads, rows, d_inner)).reshape(hs_w, d_inner)
        x_blk = jnp.where(head_of_row == head_of_lane, x_rep, 0.0).astype(BF16)
        y_diag = _dot(m_all, x_blk)

        a_last = acs_e[(b + 1) * rows - 1:(b + 1) * rows, :]
        xd_state = xdt_b * jnp.exp(a_last - acs_e[rs])
        d_hi, d_mid, d_lo = _bf16_parts(jnp.exp(a_last))
        tr_scr[0:rows, :] = xd_state
        tr_scr[rows:rows + 1, :] = d_hi.astype(F32)
        tr_scr[rows + 1:rows + 2, :] = d_mid.astype(F32)
        tr_scr[rows + 2:rows + 3, :] = d_lo.astype(F32)
        tr_t = tr_scr[...].T.astype(BF16)

        y_parts = []
        for gi in range(n_groups):
            ns = slice(gi * n_state, (gi + 1) * n_state)
            gs = slice(gi * gw, (gi + 1) * gw)
            h0_g = h0_ref[b, gs, :]
            y_parts.append(_dot_nt(cm_bb[:, ns], h0_g.astype(BF16)))
            b_pad = jnp.concatenate([bm_b[:, ns], jnp.zeros((LANES - rows, n_state), F32)], axis=0)
            upd = _dot(tr_t[gs, :], jnp.concatenate([b_pad, ones_blk], axis=1).astype(BF16))
            hnew_ref[b, gs, :] = h0_g * upd[:, n_state:] + upd[:, :n_state]
        y_off = jnp.concatenate(y_parts, axis=1)
        ys.append(y_diag + y_off * decay_in[rs])
    y = jnp.concatenate(ys, axis=0) + xs * dsk_ref[...]

    m_in = min_ref[...].reshape(r_all, d)
    out = _finish_b(x, y, z, gm, ng_ref, wpb_ref, m_in, wout_ref, fg_ref, n_groups=n_groups, final_norm=final_norm)
    y_ref[...] = out.reshape(nb, rows, d)


def _branch_b_sample(x3, m3, conv0, h0, w, *, nb, final_norm):
    bsz, rows, d = x3.shape
    conv_dim = w["wxbc"].shape[1]
    d_inner = w["wz"].shape[1]
    n_heads = d_inner // SSD_HEAD_DIM
    conv_w = w["cw"].shape[0]
    kern = functools.partial(_bs_kernel, nb=nb, rows=rows, n_groups=SSD_GROUPS, n_heads=n_heads,
                             head_dim=SSD_HEAD_DIM, n_state=SSD_STATE, conv_w=conv_w, final_norm=final_norm)
    row_spec = pl.BlockSpec((nb, rows, d), lambda i: (i, 0, 0))
    conv_spec = pl.BlockSpec((nb, conv_w - 1, conv_dim), lambda i: (i, 0, 0))
    h_spec = pl.BlockSpec((nb, d_inner, SSD_STATE), lambda i: (i, 0, 0))
    consts = [w[k] for k in ("g", "wz", "wxbc", "wdt", "wmb", "cw", "cb", "dtb", "alog", "dsk", "ng", "wpb")]
    tail_consts = [w[k] for k in ("wout", "fg", "tri_s", "e3", "ecol")]
    return pl.pallas_call(
        kern,
        out_shape=(jax.ShapeDtypeStruct((bsz, rows, d), F32),
                   jax.ShapeDtypeStruct((bsz, d_inner, SSD_STATE), F32),
                   jax.ShapeDtypeStruct((bsz, conv_w - 1, conv_dim), F32)),
        grid=(bsz // nb,),
        in_specs=([row_spec] + [_const_spec()] * len(consts) + [row_spec] + [_const_spec()] * len(tail_consts)
                  + [conv_spec, h_spec]),
        out_specs=(row_spec, h_spec, conv_spec),
        scratch_shapes=[pltpu.VMEM((nb, 2 * SUBLANES, conv_dim), F32), pltpu.VMEM((LANES, d_inner), F32)],
        compiler_params=_params(1),
        name="branch_b_sample",
    )(x3, *consts, m3, *tail_consts, conv0, h0)


def _tile(n, pref):
    return pref if n % pref == 0 else n


def _chunk_tokens():
    q = np.arange(CHUNK)
    return (q % SUBLANES) * (CHUNK // SUBLANES) + q // SUBLANES


def _head_expand(n_heads, width, n_rep):
    j = np.arange(LANES)[:, None]
    c = np.arange(n_heads * width)[None, :]
    return jnp.asarray(((j < n_rep * n_heads) & (j % n_heads == c // width)).astype(np.float32), dtype=BF16)


def _layer_weights(l, d, dec_rows, nb_s, norm_g, w_in, conv_w, conv_b, dt_bias, a_log, d_skip, ssd_norm_g, ln_v_g,
                   ln_v_b, w_spatial, b_spatial, w_proj_a, w_proj_b, w_proj_x, w_out, final_norm_g):
    d_a = d
    d_inner = w_proj_b.shape[1]
    conv_dim = conv_w.shape[2]
    n_heads = a_log.shape[1]
    d_x = w_proj_x.shape[1]
    sizes = (d_a, d_a, d_a, d_inner, conv_dim, n_heads, d_x, d_x, N_BRANCH * d)
    offs = np.concatenate([[0], np.cumsum(sizes)])
    wl = w_in[l]
    sec = lambda i: wl[:, offs[i]:offs[i + 1]]
    merge = sec(8)
    row = lambda v: v.reshape(1, -1).astype(F32)
    rep3 = lambda v, fill: jnp.concatenate(
        [v] * DT_REPLICAS + [jnp.full(v.shape[:-1] + (LANES - DT_REPLICAS * v.shape[-1],), fill, v.dtype)], axis=-1)

    tril = jnp.tril(jnp.ones((CHUNK, CHUNK), F32))
    ws_p = jnp.where(tril[None] > 0, w_spatial[l], 0.0)
    bs_p = jnp.repeat(b_spatial[l].T, d_a // A_GROUPS, axis=1)
    n_seq = CHUNK // dec_rows
    eye = jnp.eye(n_seq, dtype=F32)
    ws_s = jnp.stack([jnp.kron(eye, ws_p[gi, :dec_rows, :dec_rows]) for gi in range(A_GROUPS)])
    bs_s = jnp.tile(bs_p[:dec_rows], (n_seq, 1))

    w = {
        "g": row(norm_g[l]),
        "wu": _pack_w(sec(0)), "wv": _pack_w(sec(1)), "wga": _pack_w(sec(2)),
        "wma": _pack_w(merge[:, :d]),
        "lng": row(ln_v_g[l]), "lnb": row(ln_v_b[l]),
        "ws_p": ws_p.astype(BF16), "bs_p": bs_p, "ws_s": ws_s.astype(BF16), "bs_s": bs_s,
        "wpa": _pack_w(w_proj_a[l]),
        "wz": _pack_w(sec(3)), "wxbc": _pack_w(sec(4)),
        "wdt": _pack_w(rep3(sec(5), 0.0)),
        "wmb": _pack_w(merge[:, d:2 * d]),
        "cw": conv_w[l].astype(F32), "cb": row(conv_b[l]),
        "dtb": rep3(row(dt_bias[l]), 0.0), "alog": rep3(row(a_log[l]), 0.0),
        "dsk": jnp.repeat(row(d_skip[l]), SSD_HEAD_DIM, axis=1),
        "ng": row(ssd_norm_g[l]),
        "wpb": _pack_w(w_proj_b[l]),
        "wout": _pack_w(w_out[l]), "fg": row(final_norm_g),
        "tri": jnp.asarray(_chunk_tokens()[:, None] >= _chunk_tokens()[None, :], dtype=BF16),
        "tri_s": jnp.kron(jnp.eye(nb_s, dtype=F32), jnp.tril(jnp.ones((dec_rows, dec_rows), F32))).astype(BF16),
        "e3": _head_expand(n_heads, SSD_HEAD_DIM, DT_REPLICAS),
        "ecol": _head_expand(n_heads, dec_rows, DT_REPLICAS),
        "wq": _pack_w(sec(6)), "wgx": _pack_w(sec(7)), "wmx": _pack_w(merge[:, 2 * d:]),
        "wpx": _pack_w(w_proj_x[l]),
    }
    return w


def _layer(x3, k3, v3, w, *, nb_x, rows_x, heads_in_rows, tm_a, ws, bs, want_v, b_fn):
    bsz, seq, d = x3.shape
    m = _branch_x(x3, w["g"], w["wq"], w["wgx"], w["wmx"], k3, v3, w["wpx"], nb=nb_x, rows=rows_x,
                  heads_in_rows=heads_in_rows)
    m, vn = _branch_a(x3.reshape(bsz * seq, d), w["g"], w["wu"], w["wv"], w["wga"], w["wma"], w["lng"], w["lnb"],
                      ws, bs, w["wpa"], m.reshape(bsz * seq, d), tm=tm_a, want_v=want_v)
    return b_fn(x3, m.reshape(bsz, seq, d)), vn


def kernel(x_prompt, x_sample, mem_prompt, cache_mem_k, cache_mem_v, state_ssm, state_conv, norm_g, w_in, conv_w,
           conv_b, dt_bias, a_log, d_skip, ssd_norm_g, ln_v_g, ln_v_b, w_spatial, b_spatial, mem_norm_g, w_mem_kv,
           w_proj_a, w_proj_b, w_proj_x, w_out, final_norm_g):
    depth = w_in.shape[0]
    bsz, seq, d = x_prompt.shape
    dec_b, dec_rows, _ = x_sample.shape
    n_mem = mem_prompt.shape[1]
    d_x = w_proj_x.shape[1]
    d_inner = w_proj_b.shape[1]
    n_heads = a_log.shape[1]
    assert seq % CHUNK == 0 and CHUNK % dec_rows == 0 and dec_rows == SUBLANES
    assert DT_REPLICAS * n_heads <= LANES and n_heads * dec_rows % LANES == 0

    nb_s = _tile(dec_b, 4)
    nb_xs = _tile(dec_b, 8)
    yp, ys = x_prompt, x_sample
    outs = {k: [] for k in ("mk", "mv", "hp", "cp", "hs", "cs", "vs")}
    for l in range(depth):
        w = _layer_weights(l, d, dec_rows, nb_s, norm_g, w_in, conv_w, conv_b, dt_bias, a_log, d_skip, ssd_norm_g,
                           ln_v_g, ln_v_b, w_spatial, b_spatial, w_proj_a, w_proj_b, w_proj_x, w_out, final_norm_g)
        final_norm = l == depth - 1
        mk, mv = _mem_kv(mem_prompt.reshape(bsz * n_mem, d), mem_norm_g[l].reshape(1, d).astype(F32),
                         _pack_w(w_mem_kv[l]), _tile(bsz * n_mem, 512))
        (yp, hp, cp), _ = _layer(
            yp, mk.reshape(bsz, n_mem, d_x), mv.reshape(bsz, n_mem, d_x), w,
            nb_x=1, rows_x=_tile(seq, 512), heads_in_rows=False, tm_a=_tile(bsz * seq, 512), ws=w["ws_p"], bs=w["bs_p"], want_v=False,
            b_fn=functools.partial(_branch_b_prompt, w=w, tm=_tile(seq, 256), final_norm=final_norm))
        (ys, hs, cs), vs = _layer(
            ys, cache_mem_k[l].reshape(dec_b, n_mem * (d_x // X_HEAD_DIM), X_HEAD_DIM),
            cache_mem_v[l].reshape(dec_b, n_mem * (d_x // X_HEAD_DIM), X_HEAD_DIM), w,
            nb_x=nb_xs, rows_x=dec_rows, heads_in_rows=True, tm_a=_tile(dec_b * dec_rows, 512), ws=w["ws_s"], bs=w["bs_s"], want_v=True,
            b_fn=functools.partial(_branch_b_sample, conv0=state_conv[l],
                                   h0=state_ssm[l].reshape(dec_b, d_inner, SSD_STATE), w=w, nb=nb_s,
                                   final_norm=final_norm))
        outs["mk"].append(mk.reshape(bsz, n_mem, d_x // X_HEAD_DIM, X_HEAD_DIM))
        outs["mv"].append(mv.reshape(bsz, n_mem, d_x // X_HEAD_DIM, X_HEAD_DIM))
        outs["hp"].append(hp.reshape(bsz, n_heads, SSD_HEAD_DIM, SSD_STATE))
        outs["cp"].append(cp)
        outs["hs"].append(hs.reshape(dec_b, n_heads, SSD_HEAD_DIM, SSD_STATE))
        outs["cs"].append(cs)
        outs["vs"].append(vs.reshape(dec_b, dec_rows, d))
    st = lambda k: jnp.stack(outs[k])
    return (yp, ys, st("mk"), st("mv"), st("hp"), st("cp"), st("hs"), st("cs"), st("vs"))
```

```python
import functools
import math

import jax
import jax.numpy as jnp
import numpy as np
from jax import lax
from jax.experimental import pallas as pl
from jax.experimental.pallas import tpu as pltpu

F32 = jnp.float32
BF16 = jnp.bfloat16
EPS = 1e-6
SQRT_HALF = math.sqrt(0.5)

LANES = 128
SUBLANES = 8
VMEM_LIMIT_BYTES = 56 * 1024 * 1024

CHUNK = 128
A_GROUPS = 8
SSD_HEAD_DIM = 64
SSD_GROUPS = 8
SSD_STATE = 128
X_HEAD_DIM = 128
N_BRANCH = 3
DT_REPLICAS = 3


def _dot(a, b):
    return jnp.dot(a, b, preferred_element_type=F32)


def _dot_nt(a, b):
    return lax.dot_general(a, b, (((1,), (1,)), ((), ())), preferred_element_type=F32)


def _dot_tn(a, b):
    return lax.dot_general(a, b, (((0,), (0,)), ((), ())), preferred_element_type=F32)


MXU_COLS = 256


def _pack_w(w):
    return w.astype(BF16)


def _dotw(a, w_ref, cols=slice(None)):
    return _dot(a, w_ref[:, cols])


def _dot_cols(a, w_ref, fn, width=MXU_COLS):
    n = w_ref.shape[1]
    width = min(width, n)
    blocks = [fn(_dotw(a, w_ref, slice(c, c + width))) for c in range(0, n, width)]
    return blocks[0] if len(blocks) == 1 else jnp.concatenate(blocks, axis=1)


def _rmsnorm(x, g):
    return x * lax.rsqrt(jnp.mean(x * x, axis=-1, keepdims=True) + EPS) * g


def _gelu(x):
    return 0.5 * x * (1.0 + lax.erf(x * SQRT_HALF))


def _sigmoid(x):
    return 0.5 * jnp.tanh(0.5 * x) + 0.5


def _silu(x):
    return x * _sigmoid(x)


def _bf16_parts(x):
    hi = x.astype(BF16)
    r1 = x - hi.astype(F32)
    mid = r1.astype(BF16)
    lo = (r1 - mid.astype(F32)).astype(BF16)
    return hi, mid, lo


def _split_by_replica(x, n_heads):
    hi, mid, lo = _bf16_parts(x)
    lane = lax.broadcasted_iota(jnp.int32, x.shape, 1)
    return jnp.where(lane < n_heads, hi, jnp.where(lane < 2 * n_heads, mid, lo))


def _cumsum_rows(tri, x):
    hi, mid, lo = _bf16_parts(x)
    return _dot(tri, hi) + _dot(tri, mid) + _dot(tri, lo)


def _memkv_kernel(x_ref, g_ref, w_ref, k_ref, v_ref):
    hn = _rmsnorm(x_ref[...], g_ref[...]).astype(BF16)
    kv = _dotw(hn, w_ref)
    dx = k_ref.shape[-1]
    k_ref[...] = kv[:, :dx]
    v_ref[...] = kv[:, dx:]


def _const_spec():
    return pl.BlockSpec(memory_space=pltpu.VMEM)


def _params(n_grid):
    return pltpu.CompilerParams(dimension_semantics=("arbitrary",) * n_grid, vmem_limit_bytes=VMEM_LIMIT_BYTES)


def _mem_kv(mem2d, g, w, tm):
    rows, d = mem2d.shape
    dx = w.shape[1] // 2
    return pl.pallas_call(
        _memkv_kernel,
        out_shape=(jax.ShapeDtypeStruct((rows, dx), F32), jax.ShapeDtypeStruct((rows, dx), F32)),
        grid=(rows // tm,),
        in_specs=[pl.BlockSpec((tm, d), lambda i: (i, 0)), _const_spec(), _const_spec()],
        out_specs=(pl.BlockSpec((tm, dx), lambda i: (i, 0)), pl.BlockSpec((tm, dx), lambda i: (i, 0))),
        compiler_params=_params(1),
        name="mem_kv",
    )(mem2d, g, w)


def _x_kernel(x_ref, g_ref, wq_ref, wgx_ref, wmx_ref, k_ref, v_ref, wpx_ref, m_ref, *, nb, rows, head_dim):
    d = x_ref.shape[-1]
    x = x_ref[...].reshape(nb * rows, d)
    hn = _rmsnorm(x, g_ref[...]).astype(BF16)
    q = _dotw(hn, wq_ref)
    gate = _silu(_dotw(hn, wgx_ref))
    gm = _sigmoid(_dotw(hn, wmx_ref))
    n_heads = q.shape[1] // head_dim
    scale = head_dim ** -0.5
    outs = []
    for b in range(nb):
        kb = k_ref[b].astype(BF16)
        vb = v_ref[b].astype(BF16)
        qb = q[b * rows:(b + 1) * rows].astype(BF16)
        heads = []
        for h in range(n_heads):
            hs = slice(h * head_dim, (h + 1) * head_dim)
            s = _dot_nt(qb[:, hs], kb[:, hs]) * scale
            e = jnp.exp(s - jnp.max(s, axis=-1, keepdims=True))
            p = e / jnp.sum(e, axis=-1, keepdims=True)
            heads.append(_dot(p.astype(BF16), vb[:, hs]))
        outs.append(jnp.concatenate(heads, axis=-1))
    o = outs[0] if nb == 1 else jnp.concatenate(outs, axis=0)
    hx = (o * gate).astype(BF16)
    m = gm * _dotw(hx, wpx_ref)
    m_ref[...] = m.reshape(nb, rows, d)


def _xs_kernel(x_ref, g_ref, wq_ref, wgx_ref, wmx_ref, k_ref, v_ref, wpx_ref, m_ref, *, nb, rows, head_dim):
    d = x_ref.shape[-1]
    x = x_ref[...].reshape(nb * rows, d)
    hn = _rmsnorm(x, g_ref[...]).astype(BF16)
    q = _dotw(hn, wq_ref)
    gate = _silu(_dotw(hn, wgx_ref))
    gm = _sigmoid(_dotw(hn, wmx_ref))
    n_heads = q.shape[1] // head_dim
    n_kv = k_ref.shape[1]
    scale = head_dim ** -0.5
    row_head = lax.broadcasted_iota(jnp.int32, (n_heads * rows, n_kv), 0) // rows
    col_head = lax.broadcasted_iota(jnp.int32, (n_heads * rows, n_kv), 1) % n_heads
    same_head = row_head == col_head
    outs = []
    for b in range(nb):
        kb = k_ref[b].astype(BF16)
        vb = v_ref[b].astype(BF16)
        qb = q[b * rows:(b + 1) * rows]
        q_heads = jnp.concatenate([qb[:, h * head_dim:(h + 1) * head_dim] for h in range(n_heads)], axis=0)
        s = jnp.where(same_head, _dot_nt(q_heads.astype(BF16), kb) * scale, -jnp.inf)
        e = jnp.exp(s - jnp.max(s, axis=-1, keepdims=True))
        p = e / jnp.sum(e, axis=-1, keepdims=True)
        o_heads = _dot(p.astype(BF16), vb)
        outs.append(jnp.concatenate([o_heads[h * rows:(h + 1) * rows] for h in range(n_heads)], axis=1))
    o = outs[0] if nb == 1 else jnp.concatenate(outs, axis=0)
    hx = (o * gate).astype(BF16)
    m = gm * _dotw(hx, wpx_ref)
    m_ref[...] = m.reshape(nb, rows, d)


def _branch_x(x3, g, wq, wgx, wmx, k3, v3, wpx, *, nb, rows, heads_in_rows):
    bsz, seq, d = x3.shape
    n_mem, dx = k3.shape[1], k3.shape[2]
    kern = functools.partial(_xs_kernel if heads_in_rows else _x_kernel, nb=nb, rows=rows, head_dim=X_HEAD_DIM)
    return pl.pallas_call(
        kern,
        out_shape=jax.ShapeDtypeStruct((bsz, seq, d), F32),
        grid=(bsz // nb, seq // rows),
        in_specs=[
            pl.BlockSpec((nb, rows, d), lambda i, j: (i, j, 0)),
            _const_spec(), _const_spec(), _const_spec(), _const_spec(),
            pl.BlockSpec((nb, n_mem, dx), lambda i, j: (i, 0, 0)),
            pl.BlockSpec((nb, n_mem, dx), lambda i, j: (i, 0, 0)),
            _const_spec(),
        ],
        out_specs=pl.BlockSpec((nb, rows, d), lambda i, j: (i, j, 0)),
        compiler_params=_params(2),
        name="branch_x",
    )(x3, g, wq, wgx, wmx, k3, v3, wpx)


def _a_kernel(x_ref, g_ref, wu_ref, wv_ref, wga_ref, wma_ref, lng_ref, lnb_ref, ws_ref, bs_ref, wpa_ref, min_ref,
              mout_ref, *maybe_v_ref, chunk, groups, sub):
    tm, d = x_ref.shape
    gd = d // groups
    bias = bs_ref[...]
    for t in range(tm // sub):
        ts = slice(t * sub, (t + 1) * sub)
        hn = _rmsnorm(x_ref[ts, :], g_ref[...]).astype(BF16)
        v = _dot_cols(hn, wv_ref, _gelu)
        u = _dot_cols(hn, wu_ref, _gelu)
        vc = v - jnp.mean(v, axis=-1, keepdims=True)
        vn = vc * lax.rsqrt(jnp.mean(vc * vc, axis=-1, keepdims=True) + EPS) * lng_ref[...] + lnb_ref[...]
        if maybe_v_ref:
            maybe_v_ref[0][ts, :] = vn
        ga = _dot_cols(hn, wga_ref, _silu)
        vb = vn.astype(BF16)
        mixed = []
        for c in range(sub // chunk):
            rs = slice(c * chunk, (c + 1) * chunk)
            cols = [_dot(ws_ref[gi], vb[rs, gi * gd:(gi + 1) * gd]) for gi in range(groups)]
            mixed.append(jnp.concatenate(cols, axis=-1) + bias)
        s = mixed[0] if len(mixed) == 1 else jnp.concatenate(mixed, axis=0)
        ha = (u * s * ga).astype(BF16)
        gm = _sigmoid(_dotw(hn, wma_ref))
        mout_ref[ts, :] = min_ref[ts, :] + gm * _dotw(ha, wpa_ref)


def _branch_a(x2, g, wu, wv, wga, wma, lng, lnb, ws, bs, wpa, m_in, *, tm, want_v):
    rows, d = x2.shape
    row_spec = pl.BlockSpec((tm, d), lambda i: (i, 0))
    out_shape = [jax.ShapeDtypeStruct((rows, d), F32)]
    out_specs = [row_spec]
    if want_v:
        out_shape.append(jax.ShapeDtypeStruct((rows, d), F32))
        out_specs.append(row_spec)
    kern = functools.partial(_a_kernel, chunk=CHUNK, groups=A_GROUPS, sub=min(tm, 256))
    res = pl.pallas_call(
        kern,
        out_shape=tuple(out_shape),
        grid=(rows // tm,),
        in_specs=[row_spec] + [_const_spec()] * 10 + [row_spec],
        out_specs=tuple(out_specs),
        compiler_params=_params(1),
        name="branch_a",
    )(x2, g, wu, wv, wga, wma, lng, lnb, ws, bs, wpa, m_in)
    return res if want_v else (res[0], None)


def _finish_b(x, y, g_ref, wz_ref, wmb_ref, ng_ref, wpb_ref, m_in, wout_ref, fg_ref, *, n_groups, final_norm):
    hn = _rmsnorm(x, g_ref[...]).astype(BF16)
    gm = _sigmoid(_dotw(hn, wmb_ref))
    yf = y * _silu(_dotw(hn, wz_ref))
    gw = yf.shape[1] // n_groups
    parts = []
    for gi in range(n_groups):
        yg = yf[:, gi * gw:(gi + 1) * gw]
        parts.append(yg * lax.rsqrt(jnp.mean(yg * yg, axis=-1, keepdims=True) + EPS))
    hb = (jnp.concatenate(parts, axis=-1) * ng_ref[...]).astype(BF16)
    m = m_in + gm * _dotw(hb, wpb_ref)
    out = x + _dotw(m.astype(BF16), wout_ref)
    return _rmsnorm(out, fg_ref[...]) if final_norm else out


HANDOFF = ("xs", "bm", "cm", "zs", "dt", "gm", "tail")


def _bp_stage1(x_ref, g_ref, wz_ref, wxbc_ref, wdt_ref, wmb_ref, cw_ref, cb_ref, dtb_ref, perm_ref, xs_scr, out,
               first_tile, *, tm, chunk, d_inner, bc_width, conv_w):
    pad = SUBLANES
    vpc = chunk // pad
    n_chunks = tm // chunk
    width = MXU_COLS
    hn = _rmsnorm(x_ref[0], g_ref[...]).astype(BF16)
    for c0 in range(0, out["gm"].shape[1], width):
        cols = slice(c0, c0 + width)
        out["gm"][:, cols] = _sigmoid(_dotw(hn, wmb_ref, cols))
        yield
    hn = _dot(perm_ref[0], hn).astype(BF16)
    out["dt"][...] = jax.nn.softplus(_dotw(hn, wdt_ref) + dtb_ref[...])
    yield
    for c0 in range(0, d_inner, width):
        cols = slice(c0, c0 + width)
        out["zs"][:, cols] = _silu(_dotw(hn, wz_ref, cols))
        yield
    first_sublane = lax.broadcasted_iota(jnp.int32, (pad, width), 0) == 0
    for c0 in range(0, wxbc_ref.shape[1], width):
        cols = slice(c0, c0 + width)
        raw_all = _dotw(hn, wxbc_ref, cols)
        prev_tail = jnp.where(first_tile, 0.0, xs_scr[:, cols])
        acts = []
        for c in range(n_chunks):
            raw = raw_all[c * chunk:(c + 1) * chunk]
            conv = cb_ref[:, cols] + raw * cw_ref[conv_w - 1:conv_w, cols]
            for back in range(1, conv_w):
                head = []
                for v in range(back):
                    w = vpc - back + v
                    cur = pltpu.roll(raw[w * pad:(w + 1) * pad], 1, axis=0)
                    prv = pltpu.roll(prev_tail[(w - vpc + conv_w - 1) * pad:(w - vpc + conv_w) * pad], 1, axis=0)
                    head.append(jnp.where(first_sublane, prv, cur))
                shifted = jnp.concatenate(head + [raw[:chunk - back * pad]], axis=0)
                conv = conv + shifted * cw_ref[conv_w - 1 - back:conv_w - back, cols]
            acts.append(_silu(conv))
            prev_tail = raw[chunk - (conv_w - 1) * pad:]
        xs_scr[:, cols] = prev_tail
        out["tail"][:, cols] = prev_tail
        act = acts[0] if n_chunks == 1 else jnp.concatenate(acts, axis=0)
        if c0 < d_inner:
            out["xs"][:, cols] = act
        elif c0 < d_inner + bc_width:
            out["bm"][:, c0 - d_inner:c0 - d_inner + width] = act.astype(BF16)
        else:
            out["cm"][:, c0 - d_inner - bc_width:c0 - d_inner - bc_width + width] = act.astype(BF16)
        yield


def _bp_stage2(x_ref, min_ref, alog_ref, dsk_ref, ng_ref, wpb_ref, wout_ref, fg_ref, tri_ref, e3_ref, perm_ref,
               y_ref, st_scr, inp, first_tile, *, tm, chunk, n_groups, n_heads, head_dim, n_state, final_norm):
    pad = SUBLANES
    vpc = chunk // pad
    hpg = n_heads // n_groups
    gw = hpg * head_dim
    width = MXU_COLS

    def token_of(q):
        return (q % pad) * vpc + q // pad

    li = token_of(lax.broadcasted_iota(jnp.int32, (chunk, chunk), 0))
    si = token_of(lax.broadcasted_iota(jnp.int32, (chunk, chunk), 1))
    causal = li >= si
    head_of_lane = lax.broadcasted_iota(jnp.int32, (chunk, gw), 1) // head_dim
    tri = tri_ref[...]
    e3 = e3_ref[...]
    a_row = -jnp.exp(alog_ref[...])

    hb_rows = []
    for c in range(tm // chunk):
        rs = slice(c * chunk, (c + 1) * chunk)
        dt = inp["dt"][rs, :]
        acs = _cumsum_rows(tri, dt * a_row)
        acs_t = acs.T
        acs_e = _dot(_split_by_replica(acs, n_heads).astype(BF16), e3)
        a_last = acs_e[chunk - 1:chunk, :]
        xs_c = inp["xs"][rs, :]
        xdt_c = xs_c * _dot(_split_by_replica(dt, n_heads).astype(BF16), e3)
        xdt_b = xdt_c.astype(BF16)
        xd_state = (xdt_c * jnp.exp(a_last - acs_e)).astype(BF16)
        decay_in = jnp.exp(acs_e)
        st_prev = st_scr[...]
        if c == 0:
            st_prev = jnp.where(first_tile, 0.0, st_prev)
        st_prev_b = st_prev.astype(BF16)
        yield
        hb_parts, st_parts = [], []
        for gi in range(n_groups):
            ns = slice(gi * n_state, (gi + 1) * n_state)
            gs = slice(gi * gw, (gi + 1) * gw)
            cg, bg = inp["cm"][rs, ns], inp["bm"][rs, ns]
            scores = _dot_nt(cg, bg)
            m_heads, x_blocks = [], []
            xg = xdt_b[:, gs]
            for r in range(hpg):
                h = gi * hpg + r
                diff = acs[:, h:h + 1] - acs_t[h:h + 1, :]
                decay = jnp.exp(jnp.where(causal, diff, -jnp.inf))
                m_heads.append((scores * decay).astype(BF16))
                x_blocks.append(jnp.where(head_of_lane == r, xg, jnp.zeros_like(xg)))
            y_diag = _dot(jnp.concatenate(m_heads, axis=1), jnp.concatenate(x_blocks, axis=0))
            y_off = _dot(cg, st_prev_b[:, gs])
            y = y_diag + y_off * decay_in[:, gs] + xs_c[:, gs] * dsk_ref[:, gs]
            yf = y * inp["zs"][rs, gs]
            yn = yf * lax.rsqrt(jnp.mean(yf * yf, axis=-1, keepdims=True) + EPS)
            hb_parts.append((yn * ng_ref[:, gs]).astype(BF16))
            st_parts.append(_dot_tn(bg, xd_state[:, gs]))
            yield
        st_scr[...] = st_prev * jnp.exp(a_last) + jnp.concatenate(st_parts, axis=1)
        hb_rows.append(jnp.concatenate(hb_parts, axis=1))
    hb = hb_rows[0] if len(hb_rows) == 1 else jnp.concatenate(hb_rows, axis=0)
    hb = _dot(perm_ref[1], hb).astype(BF16)
    yield
    m_parts = []
    for c0 in range(0, wpb_ref.shape[1], width):
        cols = slice(c0, c0 + width)
        m_parts.append((min_ref[0, :, cols] + inp["gm"][:, cols] * _dotw(hb, wpb_ref, cols)).astype(BF16))
        yield
    m = jnp.concatenate(m_parts, axis=1)
    out = x_ref[0] + _dotw(m, wout_ref)
    y_ref[0] = _rmsnorm(out, fg_ref[...]) if final_norm else out
    yield


def _bp_kernel(x1_ref, x2_ref, min_ref, g_ref, wz_ref, wxbc_ref, wdt_ref, wmb_ref, cw_ref, cb_ref, dtb_ref, alog_ref,
               dsk_ref, ng_ref, wpb_ref, wout_ref, fg_ref, tri_ref, e3_ref, perm_ref,
               y_ref, ssm_ref, conv_ref, xs_scr, st_scr, *slot_refs,
               nj, n_tiles, tm, chunk, n_groups, n_heads, head_dim, n_state, conv_w, final_norm):
    t = pl.program_id(0)
    d_inner = n_heads * head_dim
    pad = SUBLANES
    tile1 = jnp.minimum(t, n_tiles - 1)
    tile2 = jnp.maximum(t - 1, 0)
    n_h = len(HANDOFF)
    slots = [dict(zip(HANDOFF, slot_refs[i * n_h:(i + 1) * n_h])) for i in range(2)]

    @pl.when(t == 0)
    def _init():
        for ref in slots[1].values():
            ref[...] = jnp.zeros_like(ref)
        st_scr[...] = jnp.zeros_like(st_scr)
        xs_scr[...] = jnp.zeros_like(xs_scr)

    def step(slot_w, slot_r):
        s1 = _bp_stage1(x1_ref, g_ref, wz_ref, wxbc_ref, wdt_ref, wmb_ref, cw_ref, cb_ref, dtb_ref, perm_ref, xs_scr,
                        slot_w, tile1 % nj == 0, tm=tm, chunk=chunk, d_inner=d_inner,
                        bc_width=n_groups * n_state, conv_w=conv_w)
        s2 = _bp_stage2(x2_ref, min_ref, alog_ref, dsk_ref, ng_ref, wpb_ref, wout_ref, fg_ref, tri_ref, e3_ref,
                        perm_ref, y_ref, st_scr, slot_r, tile2 % nj == 0, tm=tm, chunk=chunk, n_groups=n_groups,
                        n_heads=n_heads, head_dim=head_dim, n_state=n_state, final_norm=final_norm)
        live = [s2, s1]
        while live:
            for gen in list(live):
                if next(gen, StopIteration) is StopIteration:
                    live.remove(gen)

        @pl.when((t > 0) & (tile2 % nj == nj - 1))
        def _seq_end():
            tail = slot_r["tail"][...]
            conv_ref[0] = jnp.concatenate([tail[(v + 1) * pad - 1:(v + 1) * pad] for v in range(conv_w - 1)], axis=0)
            ssm_ref[0] = st_scr[...].T

    for parity in range(2):
        pl.when(t % 2 == parity)(functools.partial(step, slots[parity], slots[1 - parity]))


def _branch_b_prompt(x3, m3, w, *, tm, final_norm):
    bsz, seq, d = x3.shape
    conv_dim = w["wxbc"].shape[1]
    d_inner = w["wz"].shape[1]
    n_heads = d_inner // SSD_HEAD_DIM
    conv_w = w["cw"].shape[0]
    nj = seq // tm
    n_tiles = bsz * nj
    bc_width = SSD_GROUPS * SSD_STATE
    kern = functools.partial(_bp_kernel, nj=nj, n_tiles=n_tiles, tm=tm, chunk=CHUNK, n_groups=SSD_GROUPS,
                             n_heads=n_heads, head_dim=SSD_HEAD_DIM, n_state=SSD_STATE, conv_w=conv_w,
                             final_norm=final_norm)

    def tile1(t):
        return jnp.minimum(t, n_tiles - 1)

    def tile2(t):
        return jnp.maximum(t - 1, 0)

    spec1 = pl.BlockSpec((1, tm, d), lambda t: (tile1(t) // nj, tile1(t) % nj, 0))
    spec2 = pl.BlockSpec((1, tm, d), lambda t: (tile2(t) // nj, tile2(t) % nj, 0))
    consts = [w[k] for k in ("g", "wz", "wxbc", "wdt", "wmb", "cw", "cb", "dtb", "alog", "dsk", "ng", "wpb", "wout",
                             "fg", "tri", "e3")]
    tok = (np.arange(tm) // CHUNK) * CHUNK + np.tile(_chunk_tokens(), tm // CHUNK)
    p = (tok[:, None] == np.arange(tm)[None, :]).astype(np.float32)
    perm = jnp.asarray(np.stack([p, p.T]), dtype=BF16)
    handoff = {"xs": ((tm, d_inner), F32), "bm": ((tm, bc_width), BF16), "cm": ((tm, bc_width), BF16),
               "zs": ((tm, d_inner), F32), "dt": ((tm, LANES), F32), "gm": ((tm, d), F32),
               "tail": (((conv_w - 1) * SUBLANES, conv_dim), F32)}
    return pl.pallas_call(
        kern,
        out_shape=(jax.ShapeDtypeStruct((bsz, seq, d), F32),
                   jax.ShapeDtypeStruct((bsz, d_inner, SSD_STATE), F32),
                   jax.ShapeDtypeStruct((bsz, conv_w - 1, conv_dim), F32)),
        grid=(n_tiles + 1,),
        in_specs=[spec1, spec2, spec2] + [_const_spec()] * (len(consts) + 1),
        out_specs=(spec2,
                   pl.BlockSpec((1, d_inner, SSD_STATE), lambda t: (tile2(t) // nj, 0, 0)),
                   pl.BlockSpec((1, conv_w - 1, conv_dim), lambda t: (tile2(t) // nj, 0, 0))),
        scratch_shapes=([pltpu.VMEM(((conv_w - 1) * SUBLANES, conv_dim), F32), pltpu.VMEM((SSD_STATE, d_inner), F32)]
                        + [pltpu.VMEM(*handoff[k]) for _ in range(2) for k in HANDOFF]),
        compiler_params=_params(1),
        name="branch_b_prompt",
    )(x3, x3, m3, *consts, perm)


def _bs_kernel(x_ref, g_ref, wxbc_ref, wdt_ref, cw_ref, cb_ref, dtb_ref, alog_ref, dsk_ref,
               tri_ref, e3_ref, ecol_ref, conv0_ref, h0_ref,
               y_ref, hnew_ref, convnew_ref, xs_scr, tr_scr,
               *, nb, rows, n_groups, n_heads, head_dim, n_state, conv_w):
    hpg = n_heads // n_groups
    gw = hpg * head_dim
    d_inner = n_heads * head_dim
    d = x_ref.shape[-1]
    r_all = nb * rows
    pad = SUBLANES
    hs_w = n_heads * rows

    @pl.when(pl.program_id(0) == 0)
    def _init():
        tr_scr[...] = jnp.zeros_like(tr_scr)
        xs_scr[...] = jnp.zeros_like(xs_scr)

    x = x_ref[...].reshape(r_all, d)
    hn = _rmsnorm(x, g_ref[...]).astype(BF16)
    xbc_raw = _dotw(hn, wxbc_ref)
    dt = jax.nn.softplus(_dotw(hn, wdt_ref) + dtb_ref[...])

    raw = xbc_raw.reshape(nb, rows, xbc_raw.shape[1])
    xs_scr[:, rows - (conv_w - 1):, :] = conv0_ref[...]
    state = xs_scr[...]
    row_in_seq = lax.broadcasted_iota(jnp.int32, raw.shape, 1)
    conv = cb_ref[...] + raw * cw_ref[conv_w - 1:conv_w, :]
    for back in range(1, conv_w):
        shifted = jnp.where(row_in_seq < back, pltpu.roll(state, back, axis=1), pltpu.roll(raw, back, axis=1))
        conv = conv + shifted * cw_ref[conv_w - 1 - back:conv_w - back, :]
    convnew_ref[...] = raw[:, rows - (conv_w - 1):, :]
    xbc = _silu(conv.reshape(r_all, xbc_raw.shape[1]))
    xs = xbc[:, :d_inner]
    bm = xbc[:, d_inner:d_inner + n_groups * n_state]
    cm = xbc[:, d_inner + n_groups * n_state:]

    a_row = -jnp.exp(alog_ref[...])
    acs = _cumsum_rows(tri_ref[...], dt * a_row)
    e3 = e3_ref[...]
    acs_split = _split_by_replica(acs, n_heads).astype(BF16)
    acs_e = _dot(acs_split, e3)
    xdt = xs * _dot(_split_by_replica(dt, n_heads).astype(BF16), e3)
    decay_in = jnp.exp(acs_e)

    acs_col = _dot(acs_split, ecol_ref[...])
    l_idx = lax.broadcasted_iota(jnp.int32, (r_all, hs_w), 0) % rows
    s_idx = lax.broadcasted_iota(jnp.int32, (r_all, hs_w), 1) % rows
    on_diag = jnp.where(l_idx == s_idx, acs_col, 0.0).reshape(nb, rows, hs_w)
    acs_row = jnp.broadcast_to(jnp.sum(on_diag, axis=1, keepdims=True), (nb, rows, hs_w)).reshape(r_all, hs_w)
    decay = jnp.exp(jnp.where(l_idx >= s_idx, acs_col - acs_row, -jnp.inf))

    grp_of_row = lax.broadcasted_iota(jnp.int32, (hs_w, n_groups * n_state), 0) // (rows * hpg)
    grp_of_lane = lax.broadcasted_iota(jnp.int32, (hs_w, n_groups * n_state), 1) // n_state
    head_of_row = lax.broadcasted_iota(jnp.int32, (hs_w, d_inner), 0) // rows
    head_of_lane = lax.broadcasted_iota(jnp.int32, (hs_w, d_inner), 1) // head_dim
    slot = LANES // nb
    ones_rows = lax.broadcasted_iota(jnp.int32, (slot, n_state), 0)
    ones_blk = jnp.where((ones_rows >= rows) & (ones_rows < rows + 3), 1.0, 0.0)

    y_diags = []
    for b in range(nb):
        rs = slice(b * rows, (b + 1) * rows)
        bm_b, cm_b = bm[rs], cm[rs]
        cm_bb = cm_b.astype(BF16)
        b_rep = jnp.broadcast_to(bm_b[None], (n_heads, rows, bm_b.shape[1])).reshape(hs_w, bm_b.shape[1])
        b_exp = jnp.where(grp_of_row == grp_of_lane, b_rep, 0.0).astype(BF16)
        scores = _dot_nt(cm_bb, b_exp)
        m_all = (scores * decay[rs]).astype(BF16)
        xdt_b = xdt[rs]
        x_rep = jnp.broadcast_to(xdt_b[None], (n_heads, rows, d_inner)).reshape(hs_w, d_inner)
        x_blk = jnp.where(head_of_row == head_of_lane, x_rep, 0.0).astype(BF16)
        y_diags.append(_dot(m_all, x_blk))

        a_last = acs_e[(b + 1) * rows - 1:(b + 1) * rows, :]
        d_hi, d_mid, d_lo = _bf16_parts(jnp.exp(a_last))
        tr_scr[b * slot:b * slot + rows, :] = xdt_b * jnp.exp(a_last - acs_e[rs])
        tr_scr[b * slot + rows:b * slot + rows + 1, :] = d_hi.astype(F32)
        tr_scr[b * slot + rows + 1:b * slot + rows + 2, :] = d_mid.astype(F32)
        tr_scr[b * slot + rows + 2:b * slot + rows + 3, :] = d_lo.astype(F32)
    tr_t = tr_scr[...].T.astype(BF16)

    ys = []
    for b in range(nb):
        rs = slice(b * rows, (b + 1) * rows)
        bm_b, cm_bb = bm[rs], cm[rs].astype(BF16)
        y_parts = []
        for gi in range(n_groups):
            ns = slice(gi * n_state, (gi + 1) * n_state)
            gs = slice(gi * gw, (gi + 1) * gw)
            h0_g = h0_ref[b, gs, :]
            y_parts.append(_dot_nt(cm_bb[:, ns], h0_g.astype(BF16)))
            w_seq = jnp.concatenate(
                [jnp.concatenate([bm_b[:, ns], jnp.zeros((slot - rows, n_state), F32)], axis=0), ones_blk], axis=1)
            pieces = ([jnp.zeros((b * slot, 2 * n_state), F32)] if b else []) + [w_seq]
            if b < nb - 1:
                pieces.append(jnp.zeros((LANES - (b + 1) * slot, 2 * n_state), F32))
            upd = _dot(tr_t[gs, :], jnp.concatenate(pieces, axis=0).astype(BF16))
            hnew_ref[b, gs, :] = h0_g * upd[:, n_state:] + upd[:, :n_state]
        y_off = jnp.concatenate(y_parts, axis=1)
        ys.append(y_diags[b] + y_off * decay_in[rs])
    y = jnp.concatenate(ys, axis=0) + xs * dsk_ref[...]
    y_ref[...] = y.reshape(nb, rows, d_inner)


def _bfin_kernel(x_ref, y_ref, min_ref, g_ref, wz_ref, wmb_ref, ng_ref, wpb_ref, wout_ref, fg_ref, out_ref,
                 *, n_groups, final_norm):
    out_ref[...] = _finish_b(x_ref[...], y_ref[...], g_ref, wz_ref, wmb_ref, ng_ref, wpb_ref, min_ref[...],
                             wout_ref, fg_ref, n_groups=n_groups, final_norm=final_norm)


def _branch_b_sample(x3, m3, conv0, h0, w, *, nb, tm, final_norm):
    bsz, rows, d = x3.shape
    conv_dim = w["wxbc"].shape[1]
    d_inner = w["wz"].shape[1]
    n_heads = d_inner // SSD_HEAD_DIM
    conv_w = w["cw"].shape[0]
    kern = functools.partial(_bs_kernel, nb=nb, rows=rows, n_groups=SSD_GROUPS, n_heads=n_heads,
                             head_dim=SSD_HEAD_DIM, n_state=SSD_STATE, conv_w=conv_w)

    def seq_spec(width):
        return pl.BlockSpec((nb, rows, width), lambda i: (i, 0, 0))

    conv_spec = pl.BlockSpec((nb, conv_w - 1, conv_dim), lambda i: (i, 0, 0))
    h_spec = pl.BlockSpec((nb, d_inner, SSD_STATE), lambda i: (i, 0, 0))
    consts = [w[k] for k in ("g", "wxbc", "wdt", "cw", "cb", "dtb", "alog", "dsk", "tri_s", "e3", "ecol")]
    y, h_new, conv_new = pl.pallas_call(
        kern,
        out_shape=(jax.ShapeDtypeStruct((bsz, rows, d_inner), F32),
                   jax.ShapeDtypeStruct((bsz, d_inner, SSD_STATE), F32),
                   jax.ShapeDtypeStruct((bsz, conv_w - 1, conv_dim), F32)),
        grid=(bsz // nb,),
        in_specs=[seq_spec(d)] + [_const_spec()] * len(consts) + [conv_spec, h_spec],
        out_specs=(seq_spec(d_inner), h_spec, conv_spec),
        scratch_shapes=[pltpu.VMEM((nb, SUBLANES, conv_dim), F32), pltpu.VMEM((LANES, d_inner), F32)],
        compiler_params=_params(1),
        name="branch_b_sample",
    )(x3, *consts, conv0, h0)

    n_rows = bsz * rows

    def rows_spec(width):
        return pl.BlockSpec((tm, width), lambda i: (i, 0))

    out = pl.pallas_call(
        functools.partial(_bfin_kernel, n_groups=SSD_GROUPS, final_norm=final_norm),
        out_shape=jax.ShapeDtypeStruct((n_rows, d), F32),
        grid=(n_rows // tm,),
        in_specs=[rows_spec(d), rows_spec(d_inner), rows_spec(d)] + [_const_spec()] * 7,
        out_specs=rows_spec(d),
        compiler_params=_params(1),
        name="branch_b_sample_out",
    )(x3.reshape(n_rows, d), y.reshape(n_rows, d_inner), m3.reshape(n_rows, d),
      w["g"], w["wz"], w["wmb"], w["ng"], w["wpb"], w["wout"], w["fg"])
    return out.reshape(bsz, rows, d), h_new, conv_new


def _tile(n, pref):
    return pref if n % pref == 0 else n


def _chunk_tokens():
    q = np.arange(CHUNK)
    return (q % SUBLANES) * (CHUNK // SUBLANES) + q // SUBLANES


def _head_expand(n_heads, width, n_rep):
    j = np.arange(LANES)[:, None]
    c = np.arange(n_heads * width)[None, :]
    return jnp.asarray(((j < n_rep * n_heads) & (j % n_heads == c // width)).astype(np.float32), dtype=BF16)


def _layer_weights(l, d, dec_rows, nb_s, norm_g, w_in, conv_w, conv_b, dt_bias, a_log, d_skip, ssd_norm_g, ln_v_g,
                   ln_v_b, w_spatial, b_spatial, w_proj_a, w_proj_b, w_proj_x, w_out, final_norm_g):
    d_a = d
    d_inner = w_proj_b.shape[1]
    conv_dim = conv_w.shape[2]
    n_heads = a_log.shape[1]
    d_x = w_proj_x.shape[1]
    sizes = (d_a, d_a, d_a, d_inner, conv_dim, n_heads, d_x, d_x, N_BRANCH * d)
    offs = np.concatenate([[0], np.cumsum(sizes)])
    wl = w_in[l]
    sec = lambda i: wl[:, offs[i]:offs[i + 1]]
    merge = sec(8)
    row = lambda v: v.reshape(1, -1).astype(F32)
    rep3 = lambda v, fill: jnp.concatenate(
        [v] * DT_REPLICAS + [jnp.full(v.shape[:-1] + (LANES - DT_REPLICAS * v.shape[-1],), fill, v.dtype)], axis=-1)

    tril = jnp.tril(jnp.ones((CHUNK, CHUNK), F32))
    ws_p = jnp.where(tril[None] > 0, w_spatial[l], 0.0)
    bs_p = jnp.repeat(b_spatial[l].T, d_a // A_GROUPS, axis=1)
    n_seq = CHUNK // dec_rows
    eye = jnp.eye(n_seq, dtype=F32)
    ws_s = jnp.stack([jnp.kron(eye, ws_p[gi, :dec_rows, :dec_rows]) for gi in range(A_GROUPS)])
    bs_s = jnp.tile(bs_p[:dec_rows], (n_seq, 1))

    w = {
        "g": row(norm_g[l]),
        "wu": _pack_w(sec(0)), "wv": _pack_w(sec(1)), "wga": _pack_w(sec(2)),
        "wma": _pack_w(merge[:, :d]),
        "lng": row(ln_v_g[l]), "lnb": row(ln_v_b[l]),
        "ws_p": ws_p.astype(BF16), "bs_p": bs_p, "ws_s": ws_s.astype(BF16), "bs_s": bs_s,
        "wpa": _pack_w(w_proj_a[l]),
        "wz": _pack_w(sec(3)), "wxbc": _pack_w(sec(4)),
        "wdt": _pack_w(rep3(sec(5), 0.0)),
        "wmb": _pack_w(merge[:, d:2 * d]),
        "cw": conv_w[l].astype(F32), "cb": row(conv_b[l]),
        "dtb": rep3(row(dt_bias[l]), 0.0), "alog": rep3(row(a_log[l]), 0.0),
        "dsk": jnp.repeat(row(d_skip[l]), SSD_HEAD_DIM, axis=1),
        "ng": row(ssd_norm_g[l]),
        "wpb": _pack_w(w_proj_b[l]),
        "wout": _pack_w(w_out[l]), "fg": row(final_norm_g),
        "tri": jnp.asarray(_chunk_tokens()[:, None] >= _chunk_tokens()[None, :], dtype=BF16),
        "tri_s": jnp.kron(jnp.eye(nb_s, dtype=F32), jnp.tril(jnp.ones((dec_rows, dec_rows), F32))).astype(BF16),
        "e3": _head_expand(n_heads, SSD_HEAD_DIM, DT_REPLICAS),
        "ecol": _head_expand(n_heads, dec_rows, DT_REPLICAS),
        "wq": _pack_w(sec(6)), "wgx": _pack_w(sec(7)), "wmx": _pack_w(merge[:, 2 * d:]),
        "wpx": _pack_w(w_proj_x[l]),
    }
    return w


def _layer(x3, k3, v3, w, *, nb_x, rows_x, heads_in_rows, tm_a, ws, bs, want_v, b_fn):
    bsz, seq, d = x3.shape
    m = _branch_x(x3, w["g"], w["wq"], w["wgx"], w["wmx"], k3, v3, w["wpx"], nb=nb_x, rows=rows_x,
                  heads_in_rows=heads_in_rows)
    m, vn = _branch_a(x3.reshape(bsz * seq, d), w["g"], w["wu"], w["wv"], w["wga"], w["wma"], w["lng"], w["lnb"],
                      ws, bs, w["wpa"], m.reshape(bsz * seq, d), tm=tm_a, want_v=want_v)
    return b_fn(x3, m.reshape(bsz, seq, d)), vn


def kernel(x_prompt, x_sample, mem_prompt, cache_mem_k, cache_mem_v, state_ssm, state_conv, norm_g, w_in, conv_w,
           conv_b, dt_bias, a_log, d_skip, ssd_norm_g, ln_v_g, ln_v_b, w_spatial, b_spatial, mem_norm_g, w_mem_kv,
           w_proj_a, w_proj_b, w_proj_x, w_out, final_norm_g):
    depth = w_in.shape[0]
    bsz, seq, d = x_prompt.shape
    dec_b, dec_rows, _ = x_sample.shape
    n_mem = mem_prompt.shape[1]
    d_x = w_proj_x.shape[1]
    d_inner = w_proj_b.shape[1]
    n_heads = a_log.shape[1]
    assert seq % CHUNK == 0 and CHUNK % dec_rows == 0 and dec_rows == SUBLANES
    assert DT_REPLICAS * n_heads <= LANES and n_heads * dec_rows % LANES == 0

    nb_s = _tile(dec_b, 8)
    nb_xs = _tile(dec_b, 8)
    yp, ys = x_prompt, x_sample
    outs = {k: [] for k in ("mk", "mv", "hp", "cp", "hs", "cs", "vs")}
    for l in range(depth):
        w = _layer_weights(l, d, dec_rows, nb_s, norm_g, w_in, conv_w, conv_b, dt_bias, a_log, d_skip, ssd_norm_g,
                           ln_v_g, ln_v_b, w_spatial, b_spatial, w_proj_a, w_proj_b, w_proj_x, w_out, final_norm_g)
        final_norm = l == depth - 1
        mk, mv = _mem_kv(mem_prompt.reshape(bsz * n_mem, d), mem_norm_g[l].reshape(1, d).astype(F32),
                         _pack_w(w_mem_kv[l]), _tile(bsz * n_mem, 512))
        (yp, hp, cp), _ = _layer(
            yp, mk.reshape(bsz, n_mem, d_x), mv.reshape(bsz, n_mem, d_x), w,
            nb_x=1, rows_x=_tile(seq, 512), heads_in_rows=False, tm_a=_tile(bsz * seq, 512), ws=w["ws_p"], bs=w["bs_p"], want_v=False,
            b_fn=functools.partial(_branch_b_prompt, w=w, tm=_tile(seq, 256), final_norm=final_norm))
        (ys, hs, cs), vs = _layer(
            ys, cache_mem_k[l].reshape(dec_b, n_mem * (d_x // X_HEAD_DIM), X_HEAD_DIM),
            cache_mem_v[l].reshape(dec_b, n_mem * (d_x // X_HEAD_DIM), X_HEAD_DIM), w,
            nb_x=nb_xs, rows_x=dec_rows, heads_in_rows=True, tm_a=_tile(dec_b * dec_rows, 512), ws=w["ws_s"], bs=w["bs_s"], want_v=True,
            b_fn=functools.partial(_branch_b_sample, conv0=state_conv[l],
                                   h0=state_ssm[l].reshape(dec_b, d_inner, SSD_STATE), w=w, nb=nb_s,
                                   tm=_tile(dec_b * dec_rows, 512), final_norm=final_norm))
        outs["mk"].append(mk.reshape(bsz, n_mem, d_x // X_HEAD_DIM, X_HEAD_DIM))
        outs["mv"].append(mv.reshape(bsz, n_mem, d_x // X_HEAD_DIM, X_HEAD_DIM))
        outs["hp"].append(hp.reshape(bsz, n_heads, SSD_HEAD_DIM, SSD_STATE))
        outs["cp"].append(cp)
        outs["hs"].append(hs.reshape(dec_b, n_heads, SSD_HEAD_DIM, SSD_STATE))
        outs["cs"].append(cs)
        outs["vs"].append(vs.reshape(dec_b, dec_rows, d))
    st = lambda k: jnp.stack(outs[k])
    return (yp, ys, st("mk"), st("mv"), st("hp"), st("cp"), st("hs"), st("cs"), st("vs"))
```

```python
import functools
import math

import jax
import jax.numpy as jnp
import numpy as np
from jax import lax
from jax.experimental import pallas as pl
from jax.experimental.pallas import tpu as pltpu

F32 = jnp.float32
BF16 = jnp.bfloat16
EPS = 1e-6
SQRT_HALF = math.sqrt(0.5)

LANES = 128
SUBLANES = 8
VMEM_LIMIT_BYTES = 56 * 1024 * 1024

CHUNK = 128
A_GROUPS = 8
SSD_HEAD_DIM = 64
SSD_GROUPS = 8
SSD_STATE = 128
X_HEAD_DIM = 128
N_BRANCH = 3
DT_REPLICAS = 3
SEQ_BATCH = 4


def _dot(a, b):
    return jnp.dot(a, b, preferred_element_type=F32)


def _dot_nt(a, b):
    return lax.dot_general(a, b, (((1,), (1,)), ((), ())), preferred_element_type=F32)


def _dot_tn(a, b):
    return lax.dot_general(a, b, (((0,), (0,)), ((), ())), preferred_element_type=F32)


MXU_COLS = 256


def _pack_w(w):
    return w.astype(BF16)


def _dotw(a, w_ref, cols=slice(None)):
    return _dot(a, w_ref[:, cols])


def _dot_cols(a, w_ref, fn, width=MXU_COLS):
    n = w_ref.shape[1]
    width = min(width, n)
    blocks = [fn(_dotw(a, w_ref, slice(c, c + width))) for c in range(0, n, width)]
    return blocks[0] if len(blocks) == 1 else jnp.concatenate(blocks, axis=1)


def _rmsnorm(x, g):
    return x * lax.rsqrt(jnp.mean(x * x, axis=-1, keepdims=True) + EPS) * g


def _gelu(x):
    return 0.5 * x * (1.0 + lax.erf(x * SQRT_HALF))


def _sigmoid(x):
    return 0.5 * jnp.tanh(0.5 * x) + 0.5


def _silu(x):
    return x * _sigmoid(x)


def _bf16_parts(x):
    hi = x.astype(BF16)
    r1 = x - hi.astype(F32)
    mid = r1.astype(BF16)
    lo = (r1 - mid.astype(F32)).astype(BF16)
    return hi, mid, lo


def _split_by_replica(x, n_heads):
    hi, mid, lo = _bf16_parts(x)
    lane = lax.broadcasted_iota(jnp.int32, x.shape, 1)
    return jnp.where(lane < n_heads, hi, jnp.where(lane < 2 * n_heads, mid, lo))


def _cumsum_rows(tri, x):
    hi, mid, lo = _bf16_parts(x)
    return _dot(tri, hi) + _dot(tri, mid) + _dot(tri, lo)


def _memkv_kernel(x_ref, g_ref, w_ref, k_ref, v_ref):
    hn = _rmsnorm(x_ref[...], g_ref[...]).astype(BF16)
    kv = _dotw(hn, w_ref)
    dx = k_ref.shape[-1]
    k_ref[...] = kv[:, :dx]
    v_ref[...] = kv[:, dx:]


def _const_spec():
    return pl.BlockSpec(memory_space=pltpu.VMEM)


def _params(n_grid):
    return pltpu.CompilerParams(dimension_semantics=("arbitrary",) * n_grid, vmem_limit_bytes=VMEM_LIMIT_BYTES)


def _mem_kv(mem2d, g, w, tm):
    rows, d = mem2d.shape
    dx = w.shape[1] // 2
    return pl.pallas_call(
        _memkv_kernel,
        out_shape=(jax.ShapeDtypeStruct((rows, dx), F32), jax.ShapeDtypeStruct((rows, dx), F32)),
        grid=(rows // tm,),
        in_specs=[pl.BlockSpec((tm, d), lambda i: (i, 0)), _const_spec(), _const_spec()],
        out_specs=(pl.BlockSpec((tm, dx), lambda i: (i, 0)), pl.BlockSpec((tm, dx), lambda i: (i, 0))),
        compiler_params=_params(1),
        name="mem_kv",
    )(mem2d, g, w)


def _x_kernel(x_ref, g_ref, wq_ref, wgx_ref, wmx_ref, k_ref, v_ref, wpx_ref, m_ref, *, nb, rows, head_dim):
    d = x_ref.shape[-1]
    x = x_ref[...].reshape(nb * rows, d)
    hn = _rmsnorm(x, g_ref[...]).astype(BF16)
    q = _dotw(hn, wq_ref)
    gate = _silu(_dotw(hn, wgx_ref))
    gm = _sigmoid(_dotw(hn, wmx_ref))
    n_heads = q.shape[1] // head_dim
    scale = head_dim ** -0.5
    outs = []
    for b in range(nb):
        kb = k_ref[b].astype(BF16)
        vb = v_ref[b].astype(BF16)
        qb = q[b * rows:(b + 1) * rows].astype(BF16)
        h_sl = [slice(h * head_dim, (h + 1) * head_dim) for h in range(n_heads)]
        scores = [_dot_nt(qb[:, hs], kb[:, hs]) * scale for hs in h_sl]
        exps = [jnp.exp(s - jnp.max(s, axis=-1, keepdims=True)) for s in scores]
        probs = [(e / jnp.sum(e, axis=-1, keepdims=True)).astype(BF16) for e in exps]
        outs.append(jnp.concatenate([_dot(p, vb[:, hs]) for p, hs in zip(probs, h_sl)], axis=-1))
    o = outs[0] if nb == 1 else jnp.concatenate(outs, axis=0)
    hx = (o * gate).astype(BF16)
    m = gm * _dotw(hx, wpx_ref)
    m_ref[...] = m.reshape(nb, rows, d)


def _xs_kernel(x_ref, g_ref, wq_ref, wgx_ref, wmx_ref, k_ref, v_ref, wpx_ref, m_ref, *, nb, rows, head_dim):
    d = x_ref.shape[-1]
    x = x_ref[...].reshape(nb * rows, d)
    hn = _rmsnorm(x, g_ref[...]).astype(BF16)
    q = _dotw(hn, wq_ref)
    gate = _silu(_dotw(hn, wgx_ref))
    gm = _sigmoid(_dotw(hn, wmx_ref))
    n_heads = q.shape[1] // head_dim
    n_kv = k_ref.shape[1]
    scale = head_dim ** -0.5
    row_head = lax.broadcasted_iota(jnp.int32, (n_heads * rows, n_kv), 0) // rows
    col_head = lax.broadcasted_iota(jnp.int32, (n_heads * rows, n_kv), 1) % n_heads
    same_head = row_head == col_head
    seqs = range(nb)
    scores = []
    for b in seqs:
        qb = q[b * rows:(b + 1) * rows]
        q_heads = jnp.concatenate([qb[:, h * head_dim:(h + 1) * head_dim] for h in range(n_heads)], axis=0)
        scores.append(_dot_nt(q_heads.astype(BF16), k_ref[b].astype(BF16)))
    masked = [jnp.where(same_head, s * scale, -jnp.inf) for s in scores]
    exps = [jnp.exp(s - jnp.max(s, axis=-1, keepdims=True)) for s in masked]
    probs = [(e / jnp.sum(e, axis=-1, keepdims=True)).astype(BF16) for e in exps]
    o_heads = [_dot(probs[b], v_ref[b].astype(BF16)) for b in seqs]
    outs = [jnp.concatenate([oh[h * rows:(h + 1) * rows] for h in range(n_heads)], axis=1) for oh in o_heads]
    o = outs[0] if nb == 1 else jnp.concatenate(outs, axis=0)
    hx = (o * gate).astype(BF16)
    m = gm * _dotw(hx, wpx_ref)
    m_ref[...] = m.reshape(nb, rows, d)


def _branch_x(x3, g, wq, wgx, wmx, k3, v3, wpx, *, nb, rows, heads_in_rows):
    bsz, seq, d = x3.shape
    n_mem, dx = k3.shape[1], k3.shape[2]
    kern = functools.partial(_xs_kernel if heads_in_rows else _x_kernel, nb=nb, rows=rows, head_dim=X_HEAD_DIM)
    return pl.pallas_call(
        kern,
        out_shape=jax.ShapeDtypeStruct((bsz, seq, d), F32),
        grid=(bsz // nb, seq // rows),
        in_specs=[
            pl.BlockSpec((nb, rows, d), lambda i, j: (i, j, 0)),
            _const_spec(), _const_spec(), _const_spec(), _const_spec(),
            pl.BlockSpec((nb, n_mem, dx), lambda i, j: (i, 0, 0)),
            pl.BlockSpec((nb, n_mem, dx), lambda i, j: (i, 0, 0)),
            _const_spec(),
        ],
        out_specs=pl.BlockSpec((nb, rows, d), lambda i, j: (i, j, 0)),
        compiler_params=_params(2),
        name="branch_x",
    )(x3, g, wq, wgx, wmx, k3, v3, wpx)


def _a_kernel(x_ref, g_ref, wu_ref, wv_ref, wga_ref, wma_ref, lng_ref, lnb_ref, ws_ref, bs_ref, wpa_ref, min_ref,
              mout_ref, *maybe_v_ref, chunk, groups, sub):
    tm, d = x_ref.shape
    gd = d // groups
    bias = bs_ref[...]
    for t in range(tm // sub):
        ts = slice(t * sub, (t + 1) * sub)
        hn = _rmsnorm(x_ref[ts, :], g_ref[...]).astype(BF16)
        v = _dot_cols(hn, wv_ref, _gelu)
        u = _dot_cols(hn, wu_ref, _gelu)
        vc = v - jnp.mean(v, axis=-1, keepdims=True)
        vn = vc * lax.rsqrt(jnp.mean(vc * vc, axis=-1, keepdims=True) + EPS) * lng_ref[...] + lnb_ref[...]
        if maybe_v_ref:
            maybe_v_ref[0][ts, :] = vn
        ga = _dot_cols(hn, wga_ref, _silu)
        vb = vn.astype(BF16)
        mixed = []
        for c in range(sub // chunk):
            rs = slice(c * chunk, (c + 1) * chunk)
            cols = [_dot(ws_ref[gi], vb[rs, gi * gd:(gi + 1) * gd]) for gi in range(groups)]
            mixed.append(jnp.concatenate(cols, axis=-1) + bias)
        s = mixed[0] if len(mixed) == 1 else jnp.concatenate(mixed, axis=0)
        ha = (u * s * ga).astype(BF16)
        gm = _sigmoid(_dotw(hn, wma_ref))
        mout_ref[ts, :] = min_ref[ts, :] + gm * _dotw(ha, wpa_ref)


def _branch_a(x2, g, wu, wv, wga, wma, lng, lnb, ws, bs, wpa, m_in, *, tm, want_v):
    rows, d = x2.shape
    row_spec = pl.BlockSpec((tm, d), lambda i: (i, 0))
    out_shape = [jax.ShapeDtypeStruct((rows, d), F32)]
    out_specs = [row_spec]
    if want_v:
        out_shape.append(jax.ShapeDtypeStruct((rows, d), F32))
        out_specs.append(row_spec)
    kern = functools.partial(_a_kernel, chunk=CHUNK, groups=A_GROUPS, sub=min(tm, 256))
    res = pl.pallas_call(
        kern,
        out_shape=tuple(out_shape),
        grid=(rows // tm,),
        in_specs=[row_spec] + [_const_spec()] * 10 + [row_spec],
        out_specs=tuple(out_specs),
        compiler_params=_params(1),
        name="branch_a",
    )(x2, g, wu, wv, wga, wma, lng, lnb, ws, bs, wpa, m_in)
    return res if want_v else (res[0], None)


def _finish_b(x, y, g_ref, wz_ref, wmb_ref, ng_ref, wpb_ref, m_in, wout_ref, fg_ref, *, n_groups, final_norm):
    hn = _rmsnorm(x, g_ref[...]).astype(BF16)
    gm = _sigmoid(_dotw(hn, wmb_ref))
    yf = y * _silu(_dotw(hn, wz_ref))
    gw = yf.shape[1] // n_groups
    parts = []
    for gi in range(n_groups):
        yg = yf[:, gi * gw:(gi + 1) * gw]
        parts.append(yg * lax.rsqrt(jnp.mean(yg * yg, axis=-1, keepdims=True) + EPS))
    hb = (jnp.concatenate(parts, axis=-1) * ng_ref[...]).astype(BF16)
    m = m_in + gm * _dotw(hb, wpb_ref)
    out = x + _dotw(m.astype(BF16), wout_ref)
    return _rmsnorm(out, fg_ref[...]) if final_norm else out


HANDOFF = ("xs", "bm", "cm", "zs", "dt", "gm", "tail")


def _bp_stage1(x_ref, g_ref, wz_ref, wxbc_ref, wdt_ref, wmb_ref, cw_ref, cb_ref, dtb_ref, perm_ref, xs_scr, out,
               first_tile, *, tm, chunk, d_inner, bc_width, conv_w):
    pad = SUBLANES
    vpc = chunk // pad
    n_chunks = tm // chunk
    width = MXU_COLS
    hn = _rmsnorm(x_ref[0], g_ref[...]).astype(BF16)
    for c0 in range(0, out["gm"].shape[1], width):
        cols = slice(c0, c0 + width)
        out["gm"][:, cols] = _sigmoid(_dotw(hn, wmb_ref, cols))
        yield
    hn = _dot(perm_ref[0], hn).astype(BF16)
    out["dt"][...] = jax.nn.softplus(_dotw(hn, wdt_ref) + dtb_ref[...])
    yield
    for c0 in range(0, d_inner, width):
        cols = slice(c0, c0 + width)
        out["zs"][:, cols] = _silu(_dotw(hn, wz_ref, cols))
        yield
    first_sublane = lax.broadcasted_iota(jnp.int32, (pad, width), 0) == 0
    for c0 in range(0, wxbc_ref.shape[1], width):
        cols = slice(c0, c0 + width)
        raw_all = _dotw(hn, wxbc_ref, cols)
        prev_tail = jnp.where(first_tile, 0.0, xs_scr[:, cols])
        acts = []
        for c in range(n_chunks):
            raw = raw_all[c * chunk:(c + 1) * chunk]
            conv = cb_ref[:, cols] + raw * cw_ref[conv_w - 1:conv_w, cols]
            for back in range(1, conv_w):
                head = []
                for v in range(back):
                    w = vpc - back + v
                    cur = pltpu.roll(raw[w * pad:(w + 1) * pad], 1, axis=0)
                    prv = pltpu.roll(prev_tail[(w - vpc + conv_w - 1) * pad:(w - vpc + conv_w) * pad], 1, axis=0)
                    head.append(jnp.where(first_sublane, prv, cur))
                shifted = jnp.concatenate(head + [raw[:chunk - back * pad]], axis=0)
                conv = conv + shifted * cw_ref[conv_w - 1 - back:conv_w - back, cols]
            acts.append(_silu(conv))
            prev_tail = raw[chunk - (conv_w - 1) * pad:]
        xs_scr[:, cols] = prev_tail
        out["tail"][:, cols] = prev_tail
        act = acts[0] if n_chunks == 1 else jnp.concatenate(acts, axis=0)
        if c0 < d_inner:
            out["xs"][:, cols] = act
        elif c0 < d_inner + bc_width:
            out["bm"][:, c0 - d_inner:c0 - d_inner + width] = act.astype(BF16)
        else:
            out["cm"][:, c0 - d_inner - bc_width:c0 - d_inner - bc_width + width] = act.astype(BF16)
        yield


def _bp_stage2(x_ref, min_ref, alog_ref, dsk_ref, ng_ref, wpb_ref, wout_ref, fg_ref, tri_ref, e3_ref, perm_ref,
               y_ref, st_scr, inp, first_tile, *, tm, chunk, n_groups, n_heads, head_dim, n_state, final_norm):
    pad = SUBLANES
    vpc = chunk // pad
    hpg = n_heads // n_groups
    gw = hpg * head_dim
    width = MXU_COLS

    def token_of(q):
        return (q % pad) * vpc + q // pad

    li = token_of(lax.broadcasted_iota(jnp.int32, (chunk, chunk), 0))
    si = token_of(lax.broadcasted_iota(jnp.int32, (chunk, chunk), 1))
    causal = li >= si
    head_of_lane = lax.broadcasted_iota(jnp.int32, (chunk, gw), 1) // head_dim
    tri = tri_ref[...]
    e3 = e3_ref[...]
    a_row = -jnp.exp(alog_ref[...])

    hb_rows = []
    for c in range(tm // chunk):
        rs = slice(c * chunk, (c + 1) * chunk)
        dt = inp["dt"][rs, :]
        acs = _cumsum_rows(tri, dt * a_row)
        acs_t = acs.T
        acs_e = _dot(_split_by_replica(acs, n_heads).astype(BF16), e3)
        a_last = acs_e[chunk - 1:chunk, :]
        xs_c = inp["xs"][rs, :]
        xdt_c = xs_c * _dot(_split_by_replica(dt, n_heads).astype(BF16), e3)
        xdt_b = xdt_c.astype(BF16)
        xd_state = (xdt_c * jnp.exp(a_last - acs_e)).astype(BF16)
        decay_in = jnp.exp(acs_e)
        st_prev = st_scr[...]
        if c == 0:
            st_prev = jnp.where(first_tile, 0.0, st_prev)
        st_prev_b = st_prev.astype(BF16)
        yield
        groups = range(n_groups)
        n_sl = [slice(gi * n_state, (gi + 1) * n_state) for gi in groups]
        g_sl = [slice(gi * gw, (gi + 1) * gw) for gi in groups]
        scores = [_dot_nt(inp["cm"][rs, n_sl[gi]], inp["bm"][rs, n_sl[gi]]) for gi in groups]
        yield
        y_offs = [_dot(inp["cm"][rs, n_sl[gi]], st_prev_b[:, g_sl[gi]]) for gi in groups]
        st_parts = [_dot_tn(inp["bm"][rs, n_sl[gi]], xd_state[:, g_sl[gi]]) for gi in groups]
        st_scr[...] = st_prev * jnp.exp(a_last) + jnp.concatenate(st_parts, axis=1)
        yield
        y_diags = []
        for gi in groups:
            m_heads, x_blocks = [], []
            xg = xdt_b[:, g_sl[gi]]
            for r in range(hpg):
                h = gi * hpg + r
                diff = acs[:, h:h + 1] - acs_t[h:h + 1, :]
                decay = jnp.exp(jnp.where(causal, diff, -jnp.inf))
                m_heads.append((scores[gi] * decay).astype(BF16))
                x_blocks.append(jnp.where(head_of_lane == r, xg, jnp.zeros_like(xg)))
            y_diags.append(_dot(jnp.concatenate(m_heads, axis=1), jnp.concatenate(x_blocks, axis=0)))
            yield
        hb_parts = []
        for gi in groups:
            gs = g_sl[gi]
            y = y_diags[gi] + y_offs[gi] * decay_in[:, gs] + xs_c[:, gs] * dsk_ref[:, gs]
            yf = y * inp["zs"][rs, gs]
            yn = yf * lax.rsqrt(jnp.mean(yf * yf, axis=-1, keepdims=True) + EPS)
            hb_parts.append((yn * ng_ref[:, gs]).astype(BF16))
            if gi % 2:
                yield
        hb_rows.append(jnp.concatenate(hb_parts, axis=1))
    hb = hb_rows[0] if len(hb_rows) == 1 else jnp.concatenate(hb_rows, axis=0)
    hb = _dot(perm_ref[1], hb).astype(BF16)
    yield
    m_parts = []
    for c0 in range(0, wpb_ref.shape[1], width):
        cols = slice(c0, c0 + width)
        m_parts.append((min_ref[0, :, cols] + inp["gm"][:, cols] * _dotw(hb, wpb_ref, cols)).astype(BF16))
        yield
    m = jnp.concatenate(m_parts, axis=1)
    out = x_ref[0] + _dotw(m, wout_ref)
    y_ref[0] = _rmsnorm(out, fg_ref[...]) if final_norm else out
    yield


def _bp_kernel(x1_ref, x2_ref, min_ref, g_ref, wz_ref, wxbc_ref, wdt_ref, wmb_ref, cw_ref, cb_ref, dtb_ref, alog_ref,
               dsk_ref, ng_ref, wpb_ref, wout_ref, fg_ref, tri_ref, e3_ref, perm_ref,
               y_ref, ssm_ref, conv_ref, xs_scr, st_scr, *slot_refs,
               nj, n_tiles, tm, chunk, n_groups, n_heads, head_dim, n_state, conv_w, final_norm):
    t = pl.program_id(0)
    d_inner = n_heads * head_dim
    pad = SUBLANES
    tile1 = jnp.minimum(t, n_tiles - 1)
    tile2 = jnp.maximum(t - 1, 0)
    n_h = len(HANDOFF)
    slots = [dict(zip(HANDOFF, slot_refs[i * n_h:(i + 1) * n_h])) for i in range(2)]

    @pl.when(t == 0)
    def _init():
        for ref in slots[1].values():
            ref[...] = jnp.zeros_like(ref)
        st_scr[...] = jnp.zeros_like(st_scr)
        xs_scr[...] = jnp.zeros_like(xs_scr)

    def step(slot_w, slot_r):
        s1 = _bp_stage1(x1_ref, g_ref, wz_ref, wxbc_ref, wdt_ref, wmb_ref, cw_ref, cb_ref, dtb_ref, perm_ref, xs_scr,
                        slot_w, tile1 % nj == 0, tm=tm, chunk=chunk, d_inner=d_inner,
                        bc_width=n_groups * n_state, conv_w=conv_w)
        s2 = _bp_stage2(x2_ref, min_ref, alog_ref, dsk_ref, ng_ref, wpb_ref, wout_ref, fg_ref, tri_ref, e3_ref,
                        perm_ref, y_ref, st_scr, slot_r, tile2 % nj == 0, tm=tm, chunk=chunk, n_groups=n_groups,
                        n_heads=n_heads, head_dim=head_dim, n_state=n_state, final_norm=final_norm)
        live = [s2, s1]
        while live:
            for gen in list(live):
                if next(gen, StopIteration) is StopIteration:
                    live.remove(gen)

        @pl.when((t > 0) & (tile2 % nj == nj - 1))
        def _seq_end():
            tail = slot_r["tail"][...]
            conv_ref[0] = jnp.concatenate([tail[(v + 1) * pad - 1:(v + 1) * pad] for v in range(conv_w - 1)], axis=0)
            ssm_ref[0] = st_scr[...].T

    for parity in range(2):
        pl.when(t % 2 == parity)(functools.partial(step, slots[parity], slots[1 - parity]))


def _branch_b_prompt(x3, m3, w, *, tm, final_norm):
    bsz, seq, d = x3.shape
    conv_dim = w["wxbc"].shape[1]
    d_inner = w["wz"].shape[1]
    n_heads = d_inner // SSD_HEAD_DIM
    conv_w = w["cw"].shape[0]
    nj = seq // tm
    n_tiles = bsz * nj
    bc_width = SSD_GROUPS * SSD_STATE
    kern = functools.partial(_bp_kernel, nj=nj, n_tiles=n_tiles, tm=tm, chunk=CHUNK, n_groups=SSD_GROUPS,
                             n_heads=n_heads, head_dim=SSD_HEAD_DIM, n_state=SSD_STATE, conv_w=conv_w,
                             final_norm=final_norm)

    def tile1(t):
        return jnp.minimum(t, n_tiles - 1)

    def tile2(t):
        return jnp.maximum(t - 1, 0)

    spec1 = pl.BlockSpec((1, tm, d), lambda t: (tile1(t) // nj, tile1(t) % nj, 0))
    spec2 = pl.BlockSpec((1, tm, d), lambda t: (tile2(t) // nj, tile2(t) % nj, 0))
    consts = [w[k] for k in ("g", "wz", "wxbc", "wdt", "wmb", "cw", "cb", "dtb", "alog", "dsk", "ng", "wpb", "wout",
                             "fg", "tri", "e3")]
    tok = (np.arange(tm) // CHUNK) * CHUNK + np.tile(_chunk_tokens(), tm // CHUNK)
    p = (tok[:, None] == np.arange(tm)[None, :]).astype(np.float32)
    perm = jnp.asarray(np.stack([p, p.T]), dtype=BF16)
    handoff = {"xs": ((tm, d_inner), F32), "bm": ((tm, bc_width), BF16), "cm": ((tm, bc_width), BF16),
               "zs": ((tm, d_inner), F32), "dt": ((tm, LANES), F32), "gm": ((tm, d), F32),
               "tail": (((conv_w - 1) * SUBLANES, conv_dim), F32)}
    return pl.pallas_call(
        kern,
        out_shape=(jax.ShapeDtypeStruct((bsz, seq, d), F32),
                   jax.ShapeDtypeStruct((bsz, d_inner, SSD_STATE), F32),
                   jax.ShapeDtypeStruct((bsz, conv_w - 1, conv_dim), F32)),
        grid=(n_tiles + 1,),
        in_specs=[spec1, spec2, spec2] + [_const_spec()] * (len(consts) + 1),
        out_specs=(spec2,
                   pl.BlockSpec((1, d_inner, SSD_STATE), lambda t: (tile2(t) // nj, 0, 0)),
                   pl.BlockSpec((1, conv_w - 1, conv_dim), lambda t: (tile2(t) // nj, 0, 0))),
        scratch_shapes=([pltpu.VMEM(((conv_w - 1) * SUBLANES, conv_dim), F32), pltpu.VMEM((SSD_STATE, d_inner), F32)]
                        + [pltpu.VMEM(*handoff[k]) for _ in range(2) for k in HANDOFF]),
        compiler_params=_params(1),
        name="branch_b_prompt",
    )(x3, x3, m3, *consts, perm)


def _bs_kernel(x_ref, g_ref, wxbc_ref, wdt_ref, cw_ref, cb_ref, dtb_ref, alog_ref, dsk_ref,
               tri_ref, e3_ref, ecol_ref, conv0_ref, h0_ref,
               y_ref, hnew_ref, convnew_ref, xs_scr, tr_scr,
               *, nb, rows, n_groups, n_heads, head_dim, n_state, conv_w):
    hpg = n_heads // n_groups
    gw = hpg * head_dim
    d_inner = n_heads * head_dim
    d = x_ref.shape[-1]
    r_all = nb * rows
    pad = SUBLANES
    hs_w = n_heads * rows

    @pl.when(pl.program_id(0) == 0)
    def _init():
        tr_scr[...] = jnp.zeros_like(tr_scr)
        xs_scr[...] = jnp.zeros_like(xs_scr)

    x = x_ref[...].reshape(r_all, d)
    hn = _rmsnorm(x, g_ref[...]).astype(BF16)
    xbc_raw = _dotw(hn, wxbc_ref)
    dt = jax.nn.softplus(_dotw(hn, wdt_ref) + dtb_ref[...])

    raw = xbc_raw.reshape(nb, rows, xbc_raw.shape[1])
    xs_scr[:, rows - (conv_w - 1):, :] = conv0_ref[...]
    state = xs_scr[...]
    row_in_seq = lax.broadcasted_iota(jnp.int32, raw.shape, 1)
    conv = cb_ref[...] + raw * cw_ref[conv_w - 1:conv_w, :]
    for back in range(1, conv_w):
        shifted = jnp.where(row_in_seq < back, pltpu.roll(state, back, axis=1), pltpu.roll(raw, back, axis=1))
        conv = conv + shifted * cw_ref[conv_w - 1 - back:conv_w - back, :]
    convnew_ref[...] = raw[:, rows - (conv_w - 1):, :]
    xbc = _silu(conv.reshape(r_all, xbc_raw.shape[1]))
    xs = xbc[:, :d_inner]
    bm = xbc[:, d_inner:d_inner + n_groups * n_state]
    cm = xbc[:, d_inner + n_groups * n_state:]

    a_row = -jnp.exp(alog_ref[...])
    acs = _cumsum_rows(tri_ref[...], dt * a_row)
    e3 = e3_ref[...]
    acs_split = _split_by_replica(acs, n_heads).astype(BF16)
    acs_e = _dot(acs_split, e3)
    xdt = xs * _dot(_split_by_replica(dt, n_heads).astype(BF16), e3)
    decay_in = jnp.exp(acs_e)

    acs_col = _dot(acs_split, ecol_ref[...])
    l_idx = lax.broadcasted_iota(jnp.int32, (r_all, hs_w), 0) % rows
    s_idx = lax.broadcasted_iota(jnp.int32, (r_all, hs_w), 1) % rows
    on_diag = jnp.where(l_idx == s_idx, acs_col, 0.0).reshape(nb, rows, hs_w)
    acs_row = jnp.broadcast_to(jnp.sum(on_diag, axis=1, keepdims=True), (nb, rows, hs_w)).reshape(r_all, hs_w)
    decay = jnp.exp(jnp.where(l_idx >= s_idx, acs_col - acs_row, -jnp.inf))

    grp_of_row = lax.broadcasted_iota(jnp.int32, (hs_w, n_groups * n_state), 0) // (rows * hpg)
    grp_of_lane = lax.broadcasted_iota(jnp.int32, (hs_w, n_groups * n_state), 1) // n_state
    head_of_row = lax.broadcasted_iota(jnp.int32, (hs_w, d_inner), 0) // rows
    head_of_lane = lax.broadcasted_iota(jnp.int32, (hs_w, d_inner), 1) // head_dim
    slot = LANES // nb
    ones_rows = lax.broadcasted_iota(jnp.int32, (slot, n_state), 0)
    ones_blk = jnp.where((ones_rows >= rows) & (ones_rows < rows + 3), 1.0, 0.0)

    def seq_rows(b):
        return slice(b * rows, (b + 1) * rows)

    def expand_rows(v, same):
        rep = jnp.broadcast_to(v[None], (n_heads, rows, v.shape[1])).reshape(hs_w, v.shape[1])
        return jnp.where(same, rep, 0.0).astype(BF16)

    y_diags = []
    for b0 in range(0, nb, SEQ_BATCH):
        seqs = range(b0, min(b0 + SEQ_BATCH, nb))
        b_exps = [expand_rows(bm[seq_rows(b)], grp_of_row == grp_of_lane) for b in seqs]
        scores = [_dot_nt(cm[seq_rows(b)].astype(BF16), be) for b, be in zip(seqs, b_exps)]
        x_blks = [expand_rows(xdt[seq_rows(b)], head_of_row == head_of_lane) for b in seqs]
        m_alls = [(s * decay[seq_rows(b)]).astype(BF16) for b, s in zip(seqs, scores)]
        y_diags += [_dot(m, xb) for m, xb in zip(m_alls, x_blks)]
    for b in range(nb):
        a_last = acs_e[(b + 1) * rows - 1:(b + 1) * rows, :]
        d_hi, d_mid, d_lo = _bf16_parts(jnp.exp(a_last))
        tr_scr[b * slot:b * slot + rows, :] = xdt[seq_rows(b)] * jnp.exp(a_last - acs_e[seq_rows(b)])
        tr_scr[b * slot + rows:b * slot + rows + 1, :] = d_hi.astype(F32)
        tr_scr[b * slot + rows + 1:b * slot + rows + 2, :] = d_mid.astype(F32)
        tr_scr[b * slot + rows + 2:b * slot + rows + 3, :] = d_lo.astype(F32)
    tr_t = tr_scr[...].T.astype(BF16)

    ys = []
    for b in range(nb):
        rs = slice(b * rows, (b + 1) * rows)
        bm_b, cm_bb = bm[rs], cm[rs].astype(BF16)
        y_parts = []
        for gi in range(n_groups):
            ns = slice(gi * n_state, (gi + 1) * n_state)
            gs = slice(gi * gw, (gi + 1) * gw)
            h0_g = h0_ref[b, gs, :]
            y_parts.append(_dot_nt(cm_bb[:, ns], h0_g.astype(BF16)))
            w_seq = jnp.concatenate(
                [jnp.concatenate([bm_b[:, ns], jnp.zeros((slot - rows, n_state), F32)], axis=0), ones_blk], axis=1)
            pieces = ([jnp.zeros((b * slot, 2 * n_state), F32)] if b else []) + [w_seq]
            if b < nb - 1:
                pieces.append(jnp.zeros((LANES - (b + 1) * slot, 2 * n_state), F32))
            upd = _dot(tr_t[gs, :], jnp.concatenate(pieces, axis=0).astype(BF16))
            hnew_ref[b, gs, :] = h0_g * upd[:, n_state:] + upd[:, :n_state]
        ys.append(y_diags[b] + jnp.concatenate(y_parts, axis=1) * decay_in[rs])
    y = jnp.concatenate(ys, axis=0) + xs * dsk_ref[...]
    y_ref[...] = y.reshape(nb, rows, d_inner)


def _bfin_kernel(x_ref, y_ref, min_ref, g_ref, wz_ref, wmb_ref, ng_ref, wpb_ref, wout_ref, fg_ref, out_ref,
                 *, n_groups, final_norm):
    out_ref[...] = _finish_b(x_ref[...], y_ref[...], g_ref, wz_ref, wmb_ref, ng_ref, wpb_ref, min_ref[...],
                             wout_ref, fg_ref, n_groups=n_groups, final_norm=final_norm)


def _branch_b_sample(x3, m3, conv0, h0, w, *, nb, tm, final_norm):
    bsz, rows, d = x3.shape
    conv_dim = w["wxbc"].shape[1]
    d_inner = w["wz"].shape[1]
    n_heads = d_inner // SSD_HEAD_DIM
    conv_w = w["cw"].shape[0]
    kern = functools.partial(_bs_kernel, nb=nb, rows=rows, n_groups=SSD_GROUPS, n_heads=n_heads,
                             head_dim=SSD_HEAD_DIM, n_state=SSD_STATE, conv_w=conv_w)

    def seq_spec(width):
        return pl.BlockSpec((nb, rows, width), lambda i: (i, 0, 0))

    conv_spec = pl.BlockSpec((nb, conv_w - 1, conv_dim), lambda i: (i, 0, 0))
    h_spec = pl.BlockSpec((nb, d_inner, SSD_STATE), lambda i: (i, 0, 0))
    consts = [w[k] for k in ("g", "wxbc", "wdt", "cw", "cb", "dtb", "alog", "dsk", "tri_s", "e3", "ecol")]
    y, h_new, conv_new = pl.pallas_call(
        kern,
        out_shape=(jax.ShapeDtypeStruct((bsz, rows, d_inner), F32),
                   jax.ShapeDtypeStruct((bsz, d_inner, SSD_STATE), F32),
                   jax.ShapeDtypeStruct((bsz, conv_w - 1, conv_dim), F32)),
        grid=(bsz // nb,),
        in_specs=[seq_spec(d)] + [_const_spec()] * len(consts) + [conv_spec, h_spec],
        out_specs=(seq_spec(d_inner), h_spec, conv_spec),
        scratch_shapes=[pltpu.VMEM((nb, SUBLANES, conv_dim), F32), pltpu.VMEM((LANES, d_inner), F32)],
        compiler_params=_params(1),
        name="branch_b_sample",
    )(x3, *consts, conv0, h0)

    n_rows = bsz * rows

    def rows_spec(width):
        return pl.BlockSpec((tm, width), lambda i: (i, 0))

    out = pl.pallas_call(
        functools.partial(_bfin_kernel, n_groups=SSD_GROUPS, final_norm=final_norm),
        out_shape=jax.ShapeDtypeStruct((n_rows, d), F32),
        grid=(n_rows // tm,),
        in_specs=[rows_spec(d), rows_spec(d_inner), rows_spec(d)] + [_const_spec()] * 7,
        out_specs=rows_spec(d),
        compiler_params=_params(1),
        name="branch_b_sample_out",
    )(x3.reshape(n_rows, d), y.reshape(n_rows, d_inner), m3.reshape(n_rows, d),
      w["g"], w["wz"], w["wmb"], w["ng"], w["wpb"], w["wout"], w["fg"])
    return out.reshape(bsz, rows, d), h_new, conv_new


def _tile(n, pref):
    return pref if n % pref == 0 else n


def _chunk_tokens():
    q = np.arange(CHUNK)
    return (q % SUBLANES) * (CHUNK // SUBLANES) + q // SUBLANES


def _head_expand(n_heads, width, n_rep):
    j = np.arange(LANES)[:, None]
    c = np.arange(n_heads * width)[None, :]
    return jnp.asarray(((j < n_rep * n_heads) & (j % n_heads == c // width)).astype(np.float32), dtype=BF16)


def _layer_weights(l, d, dec_rows, nb_s, norm_g, w_in, conv_w, conv_b, dt_bias, a_log, d_skip, ssd_norm_g, ln_v_g,
                   ln_v_b, w_spatial, b_spatial, w_proj_a, w_proj_b, w_proj_x, w_out, final_norm_g):
    d_a = d
    d_inner = w_proj_b.shape[1]
    conv_dim = conv_w.shape[2]
    n_heads = a_log.shape[1]
    d_x = w_proj_x.shape[1]
    sizes = (d_a, d_a, d_a, d_inner, conv_dim, n_heads, d_x, d_x, N_BRANCH * d)
    offs = np.concatenate([[0], np.cumsum(sizes)])
    wl = w_in[l]
    sec = lambda i: wl[:, offs[i]:offs[i + 1]]
    merge = sec(8)
    row = lambda v: v.reshape(1, -1).astype(F32)
    rep3 = lambda v, fill: jnp.concatenate(
        [v] * DT_REPLICAS + [jnp.full(v.shape[:-1] + (LANES - DT_REPLICAS * v.shape[-1],), fill, v.dtype)], axis=-1)

    tril = jnp.tril(jnp.ones((CHUNK, CHUNK), F32))
    ws_p = jnp.where(tril[None] > 0, w_spatial[l], 0.0)
    bs_p = jnp.repeat(b_spatial[l].T, d_a // A_GROUPS, axis=1)
    n_seq = CHUNK // dec_rows
    eye = jnp.eye(n_seq, dtype=F32)
    ws_s = jnp.stack([jnp.kron(eye, ws_p[gi, :dec_rows, :dec_rows]) for gi in range(A_GROUPS)])
    bs_s = jnp.tile(bs_p[:dec_rows], (n_seq, 1))

    w = {
        "g": row(norm_g[l]),
        "wu": _pack_w(sec(0)), "wv": _pack_w(sec(1)), "wga": _pack_w(sec(2)),
        "wma": _pack_w(merge[:, :d]),
        "lng": row(ln_v_g[l]), "lnb": row(ln_v_b[l]),
        "ws_p": ws_p.astype(BF16), "bs_p": bs_p, "ws_s": ws_s.astype(BF16), "bs_s": bs_s,
        "wpa": _pack_w(w_proj_a[l]),
        "wz": _pack_w(sec(3)), "wxbc": _pack_w(sec(4)),
        "wdt": _pack_w(rep3(sec(5), 0.0)),
        "wmb": _pack_w(merge[:, d:2 * d]),
        "cw": conv_w[l].astype(F32), "cb": row(conv_b[l]),
        "dtb": rep3(row(dt_bias[l]), 0.0), "alog": rep3(row(a_log[l]), 0.0),
        "dsk": jnp.repeat(row(d_skip[l]), SSD_HEAD_DIM, axis=1),
        "ng": row(ssd_norm_g[l]),
        "wpb": _pack_w(w_proj_b[l]),
        "wout": _pack_w(w_out[l]), "fg": row(final_norm_g),
        "tri": jnp.asarray(_chunk_tokens()[:, None] >= _chunk_tokens()[None, :], dtype=BF16),
        "tri_s": jnp.kron(jnp.eye(nb_s, dtype=F32), jnp.tril(jnp.ones((dec_rows, dec_rows), F32))).astype(BF16),
        "e3": _head_expand(n_heads, SSD_HEAD_DIM, DT_REPLICAS),
        "ecol": _head_expand(n_heads, dec_rows, DT_REPLICAS),
        "wq": _pack_w(sec(6)), "wgx": _pack_w(sec(7)), "wmx": _pack_w(merge[:, 2 * d:]),
        "wpx": _pack_w(w_proj_x[l]),
    }
    return w


def _layer(x3, k3, v3, w, *, nb_x, rows_x, heads_in_rows, tm_a, ws, bs, want_v, b_fn):
    bsz, seq, d = x3.shape
    m = _branch_x(x3, w["g"], w["wq"], w["wgx"], w["wmx"], k3, v3, w["wpx"], nb=nb_x, rows=rows_x,
                  heads_in_rows=heads_in_rows)
    m, vn = _branch_a(x3.reshape(bsz * seq, d), w["g"], w["wu"], w["wv"], w["wga"], w["wma"], w["lng"], w["lnb"],
                      ws, bs, w["wpa"], m.reshape(bsz * seq, d), tm=tm_a, want_v=want_v)
    return b_fn(x3, m.reshape(bsz, seq, d)), vn


def kernel(x_prompt, x_sample, mem_prompt, cache_mem_k, cache_mem_v, state_ssm, state_conv, norm_g, w_in, conv_w,
           conv_b, dt_bias, a_log, d_skip, ssd_norm_g, ln_v_g, ln_v_b, w_spatial, b_spatial, mem_norm_g, w_mem_kv,
           w_proj_a, w_proj_b, w_proj_x, w_out, final_norm_g):
    depth = w_in.shape[0]
    bsz, seq, d = x_prompt.shape
    dec_b, dec_rows, _ = x_sample.shape
    n_mem = mem_prompt.shape[1]
    d_x = w_proj_x.shape[1]
    d_inner = w_proj_b.shape[1]
    n_heads = a_log.shape[1]
    assert seq % CHUNK == 0 and CHUNK % dec_rows == 0 and dec_rows == SUBLANES
    assert DT_REPLICAS * n_heads <= LANES and n_heads * dec_rows % LANES == 0

    nb_s = _tile(dec_b, 8)
    nb_xs = _tile(dec_b, 8)
    yp, ys = x_prompt, x_sample
    outs = {k: [] for k in ("mk", "mv", "hp", "cp", "hs", "cs", "vs")}
    for l in range(depth):
        w = _layer_weights(l, d, dec_rows, nb_s, norm_g, w_in, conv_w, conv_b, dt_bias, a_log, d_skip, ssd_norm_g,
                           ln_v_g, ln_v_b, w_spatial, b_spatial, w_proj_a, w_proj_b, w_proj_x, w_out, final_norm_g)
        final_norm = l == depth - 1
        mk, mv = _mem_kv(mem_prompt.reshape(bsz * n_mem, d), mem_norm_g[l].reshape(1, d).astype(F32),
                         _pack_w(w_mem_kv[l]), _tile(bsz * n_mem, 512))
        (yp, hp, cp), _ = _layer(
            yp, mk.reshape(bsz, n_mem, d_x), mv.reshape(bsz, n_mem, d_x), w,
            nb_x=1, rows_x=_tile(seq, 512), heads_in_rows=False, tm_a=_tile(bsz * seq, 512), ws=w["ws_p"], bs=w["bs_p"], want_v=False,
            b_fn=functools.partial(_branch_b_prompt, w=w, tm=_tile(seq, 256), final_norm=final_norm))
        (ys, hs, cs), vs = _layer(
            ys, cache_mem_k[l].reshape(dec_b, n_mem * (d_x // X_HEAD_DIM), X_HEAD_DIM),
            cache_mem_v[l].reshape(dec_b, n_mem * (d_x // X_HEAD_DIM), X_HEAD_DIM), w,
            nb_x=nb_xs, rows_x=dec_rows, heads_in_rows=True, tm_a=_tile(dec_b * dec_rows, 512), ws=w["ws_s"], bs=w["bs_s"], want_v=True,
            b_fn=functools.partial(_branch_b_sample, conv0=state_conv[l],
                                   h0=state_ssm[l].reshape(dec_b, d_inner, SSD_STATE), w=w, nb=nb_s,
                                   tm=_tile(dec_b * dec_rows, 512), final_norm=final_norm))
        outs["mk"].append(mk.reshape(bsz, n_mem, d_x // X_HEAD_DIM, X_HEAD_DIM))
        outs["mv"].append(mv.reshape(bsz, n_mem, d_x // X_HEAD_DIM, X_HEAD_DIM))
        outs["hp"].append(hp.reshape(bsz, n_heads, SSD_HEAD_DIM, SSD_STATE))
        outs["cp"].append(cp)
        outs["hs"].append(hs.reshape(dec_b, n_heads, SSD_HEAD_DIM, SSD_STATE))
        outs["cs"].append(cs)
        outs["vs"].append(vs.reshape(dec_b, dec_rows, d))
    st = lambda k: jnp.stack(outs[k])
    return (yp, ys, st("mk"), st("mv"), st("hp"), st("cp"), st("hs"), st("cs"), st("vs"))
```

```python
import functools
import math
from typing import NamedTuple

import jax
import jax.numpy as jnp
import numpy as np
from jax import lax
from jax.experimental import pallas as pl
from jax.experimental.pallas import tpu as pltpu

F32 = jnp.float32
BF16 = jnp.bfloat16
EPS = 1e-6
SQRT_HALF = math.sqrt(0.5)

LANES = 128
SUBLANES = 8
VMEM_LIMIT_BYTES = 56 * 1024 * 1024

CHUNK = 128
A_GROUPS = 8
SSD_HEAD_DIM = 64
SSD_GROUPS = 8
SSD_STATE = 128
X_HEAD_DIM = 128
N_BRANCH = 3
DT_REPLICAS = 3
SEQ_BATCH = 4


def _dot(a, b):
    return jnp.dot(a, b, preferred_element_type=F32)


def _dot_nt(a, b):
    return lax.dot_general(a, b, (((1,), (1,)), ((), ())), preferred_element_type=F32)


def _dot_tn(a, b):
    return lax.dot_general(a, b, (((0,), (0,)), ((), ())), preferred_element_type=F32)


MXU_COLS = 256


def _pack_w(w):
    return w.astype(BF16)


def _ncols(w):
    return sum(r.shape[1] for r in w) if isinstance(w, tuple) else w.shape[1]


def _dotw(a, w, cols=None):
    if not isinstance(w, tuple):
        return _dot(a, w[...] if cols is None else w[:, cols])
    width = w[0].shape[1]
    lo, hi = (0, width * len(w)) if cols is None else (cols.start, cols.stop)
    parts = []
    while lo < hi:
        k, off = divmod(lo, width)
        n = min(width - off, hi - lo)
        parts.append(_dot(a, w[k][:, off:off + n]))
        lo += n
    return parts[0] if len(parts) == 1 else jnp.concatenate(parts, axis=1)


def _dot_cols(a, w_ref, fn, width=MXU_COLS):
    n = w_ref.shape[1]
    width = min(width, n)
    blocks = [fn(_dotw(a, w_ref, slice(c, c + width))) for c in range(0, n, width)]
    return blocks[0] if len(blocks) == 1 else jnp.concatenate(blocks, axis=1)


def _rmsnorm(x, g):
    return x * lax.rsqrt(jnp.mean(x * x, axis=-1, keepdims=True) + EPS) * g


def _gelu(x):
    return 0.5 * x * (1.0 + lax.erf(x * SQRT_HALF))


def _sigmoid(x):
    return 0.5 * jnp.tanh(0.5 * x) + 0.5


def _silu(x):
    return x * _sigmoid(x)


def _bf16_parts(x):
    hi = x.astype(BF16)
    r1 = x - hi.astype(F32)
    mid = r1.astype(BF16)
    lo = (r1 - mid.astype(F32)).astype(BF16)
    return hi, mid, lo


def _split_by_replica(x, n_heads):
    hi, mid, lo = _bf16_parts(x)
    lane = lax.broadcasted_iota(jnp.int32, x.shape, 1)
    return jnp.where(lane < n_heads, hi, jnp.where(lane < 2 * n_heads, mid, lo))


def _cumsum_rows(tri, x):
    hi, mid, lo = _bf16_parts(x)
    return _dot(tri, hi) + _dot(tri, mid) + _dot(tri, lo)


def _memkv_kernel(x_ref, g_ref, w_ref, k_ref, v_ref):
    hn = _rmsnorm(x_ref[...], g_ref[...]).astype(BF16)
    kv = _dotw(hn, w_ref)
    dx = k_ref.shape[-1]
    k_ref[...] = kv[:, :dx]
    v_ref[...] = kv[:, dx:]


class _Cols(NamedTuple):
    mat: jax.Array
    start: int
    width: int

    @property
    def shape(self):
        return (self.mat.shape[0], self.width)


def _const_spec(op=None):
    if isinstance(op, _Cols):
        idx = op.start // op.width
        return pl.BlockSpec(op.shape, lambda *_: (0, idx), pipeline_mode=pl.Buffered(1))
    return pl.BlockSpec(memory_space=pltpu.VMEM)


def _flatten(ops):
    return [o for op in ops for o in (op if isinstance(op, list) else [op])]


def _const_specs(ops):
    return [_const_spec(op) for op in _flatten(ops)]


def _arrays(ops):
    return [op.mat if isinstance(op, _Cols) else op for op in _flatten(ops)]


def _regroup(kernel_fn, n_lead, ops):
    sizes = [len(op) if isinstance(op, list) else 0 for op in ops]

    def wrapped(*refs):
        refs = list(refs)
        args, pos = refs[:n_lead], n_lead
        for n in sizes:
            args.append(tuple(refs[pos:pos + n]) if n else refs[pos])
            pos += max(n, 1)
        return kernel_fn(*args, *refs[pos:])

    return wrapped


def _params(n_grid):
    return pltpu.CompilerParams(dimension_semantics=("arbitrary",) * n_grid, vmem_limit_bytes=VMEM_LIMIT_BYTES)


def _mem_kv(mem2d, g, w, tm):
    rows, d = mem2d.shape
    dx = w.shape[1] // 2
    return pl.pallas_call(
        _memkv_kernel,
        out_shape=(jax.ShapeDtypeStruct((rows, dx), F32), jax.ShapeDtypeStruct((rows, dx), F32)),
        grid=(rows // tm,),
        in_specs=[pl.BlockSpec((tm, d), lambda i: (i, 0)), _const_spec(), _const_spec()],
        out_specs=(pl.BlockSpec((tm, dx), lambda i: (i, 0)), pl.BlockSpec((tm, dx), lambda i: (i, 0))),
        compiler_params=_params(1),
        name="mem_kv",
    )(mem2d, g, w)


def _x_kernel(x_ref, g_ref, wq_ref, wgx_ref, wmx_ref, k_ref, v_ref, wpx_ref, m_ref, *, nb, rows, head_dim):
    d = x_ref.shape[-1]
    x = x_ref[...].reshape(nb * rows, d)
    hn = _rmsnorm(x, g_ref[...]).astype(BF16)
    q = _dotw(hn, wq_ref)
    gate = _silu(_dotw(hn, wgx_ref))
    gm = _sigmoid(_dotw(hn, wmx_ref))
    n_heads = q.shape[1] // head_dim
    scale = head_dim ** -0.5
    outs = []
    for b in range(nb):
        kb = k_ref[b].astype(BF16)
        vb = v_ref[b].astype(BF16)
        qb = q[b * rows:(b + 1) * rows].astype(BF16)
        h_sl = [slice(h * head_dim, (h + 1) * head_dim) for h in range(n_heads)]
        scores = [_dot_nt(qb[:, hs], kb[:, hs]) * scale for hs in h_sl]
        exps = [jnp.exp(s - jnp.max(s, axis=-1, keepdims=True)) for s in scores]
        probs = [(e / jnp.sum(e, axis=-1, keepdims=True)).astype(BF16) for e in exps]
        outs.append(jnp.concatenate([_dot(p, vb[:, hs]) for p, hs in zip(probs, h_sl)], axis=-1))
    o = outs[0] if nb == 1 else jnp.concatenate(outs, axis=0)
    hx = (o * gate).astype(BF16)
    m = gm * _dotw(hx, wpx_ref)
    m_ref[...] = m.reshape(nb, rows, d)


def _xs_kernel(x_ref, g_ref, wq_ref, wgx_ref, wmx_ref, k_ref, v_ref, wpx_ref, m_ref, *, nb, rows, head_dim):
    d = x_ref.shape[-1]
    x = x_ref[...].reshape(nb * rows, d)
    hn = _rmsnorm(x, g_ref[...]).astype(BF16)
    q = _dotw(hn, wq_ref)
    gate = _silu(_dotw(hn, wgx_ref))
    gm = _sigmoid(_dotw(hn, wmx_ref))
    n_heads = q.shape[1] // head_dim
    n_kv = k_ref.shape[1]
    scale = head_dim ** -0.5
    row_head = lax.broadcasted_iota(jnp.int32, (n_heads * rows, n_kv), 0) // rows
    col_head = lax.broadcasted_iota(jnp.int32, (n_heads * rows, n_kv), 1) % n_heads
    same_head = row_head == col_head
    seqs = range(nb)
    scores = []
    for b in seqs:
        qb = q[b * rows:(b + 1) * rows]
        q_heads = jnp.concatenate([qb[:, h * head_dim:(h + 1) * head_dim] for h in range(n_heads)], axis=0)
        scores.append(_dot_nt(q_heads.astype(BF16), k_ref[b].astype(BF16)))
    masked = [jnp.where(same_head, s * scale, -jnp.inf) for s in scores]
    exps = [jnp.exp(s - jnp.max(s, axis=-1, keepdims=True)) for s in masked]
    probs = [(e / jnp.sum(e, axis=-1, keepdims=True)).astype(BF16) for e in exps]
    o_heads = [_dot(probs[b], v_ref[b].astype(BF16)) for b in seqs]
    outs = [jnp.concatenate([oh[h * rows:(h + 1) * rows] for h in range(n_heads)], axis=1) for oh in o_heads]
    o = outs[0] if nb == 1 else jnp.concatenate(outs, axis=0)
    hx = (o * gate).astype(BF16)
    m = gm * _dotw(hx, wpx_ref)
    m_ref[...] = m.reshape(nb, rows, d)


def _branch_x(x3, g, wq, wgx, wmx, k3, v3, wpx, *, nb, rows, heads_in_rows):
    bsz, seq, d = x3.shape
    n_mem, dx = k3.shape[1], k3.shape[2]
    kern = functools.partial(_xs_kernel if heads_in_rows else _x_kernel, nb=nb, rows=rows, head_dim=X_HEAD_DIM)
    return pl.pallas_call(
        kern,
        out_shape=jax.ShapeDtypeStruct((bsz, seq, d), F32),
        grid=(bsz // nb, seq // rows),
        in_specs=[pl.BlockSpec((nb, rows, d), lambda i, j: (i, j, 0))] + _const_specs([g, wq, wgx, wmx]) + [
            pl.BlockSpec((nb, n_mem, dx), lambda i, j: (i, 0, 0)),
            pl.BlockSpec((nb, n_mem, dx), lambda i, j: (i, 0, 0)),
            _const_spec(wpx),
        ],
        out_specs=pl.BlockSpec((nb, rows, d), lambda i, j: (i, j, 0)),
        compiler_params=_params(2),
        name="branch_x",
    )(x3, *_arrays([g, wq, wgx, wmx]), k3, v3, wpx)


def _a_kernel(x_ref, g_ref, wu_ref, wv_ref, wga_ref, wma_ref, lng_ref, lnb_ref, ws_ref, bs_ref, wpa_ref, min_ref,
              mout_ref, *maybe_v_ref, chunk, groups, sub):
    tm, d = x_ref.shape
    gd = d // groups
    bias = bs_ref[...]
    for t in range(tm // sub):
        ts = slice(t * sub, (t + 1) * sub)
        hn = _rmsnorm(x_ref[ts, :], g_ref[...]).astype(BF16)
        v = _dot_cols(hn, wv_ref, _gelu)
        u = _dot_cols(hn, wu_ref, _gelu)
        vc = v - jnp.mean(v, axis=-1, keepdims=True)
        vn = vc * lax.rsqrt(jnp.mean(vc * vc, axis=-1, keepdims=True) + EPS) * lng_ref[...] + lnb_ref[...]
        if maybe_v_ref:
            maybe_v_ref[0][ts, :] = vn
        ga = _dot_cols(hn, wga_ref, _silu)
        vb = vn.astype(BF16)
        n_c = sub // chunk
        wide = [_dot(ws_ref[gi], jnp.concatenate([vb[c * chunk:(c + 1) * chunk, gi * gd:(gi + 1) * gd]
                                                  for c in range(n_c)], axis=1)) for gi in range(groups)]
        mixed = [jnp.concatenate([wg[:, c * gd:(c + 1) * gd] for wg in wide], axis=1) + bias for c in range(n_c)]
        s = mixed[0] if n_c == 1 else jnp.concatenate(mixed, axis=0)
        ha = (u * s * ga).astype(BF16)
        gm = _sigmoid(_dotw(hn, wma_ref))
        mout_ref[ts, :] = min_ref[ts, :] + gm * _dotw(ha, wpa_ref)


def _branch_a(x2, g, wu, wv, wga, wma, lng, lnb, ws, bs, wpa, m_in, *, tm, want_v):
    rows, d = x2.shape
    row_spec = pl.BlockSpec((tm, d), lambda i: (i, 0))
    out_shape = [jax.ShapeDtypeStruct((rows, d), F32)]
    out_specs = [row_spec]
    if want_v:
        out_shape.append(jax.ShapeDtypeStruct((rows, d), F32))
        out_specs.append(row_spec)
    kern = functools.partial(_a_kernel, chunk=CHUNK, groups=A_GROUPS, sub=min(tm, 256))
    consts = [g, wu, wv, wga, wma, lng, lnb, ws, bs, wpa]
    res = pl.pallas_call(
        kern,
        out_shape=tuple(out_shape),
        grid=(rows // tm,),
        in_specs=[row_spec] + _const_specs(consts) + [row_spec],
        out_specs=tuple(out_specs),
        compiler_params=_params(1),
        name="branch_a",
    )(x2, *_arrays(consts), m_in)
    return res if want_v else (res[0], None)


def _finish_b(x, y, g_ref, wz_ref, wmb_ref, ng_ref, wpb_ref, m_in, wout_ref, fg_ref, *, n_groups, final_norm):
    hn = _rmsnorm(x, g_ref[...]).astype(BF16)
    gm = _sigmoid(_dotw(hn, wmb_ref))
    yf = y * _silu(_dotw(hn, wz_ref))
    gw = yf.shape[1] // n_groups
    parts = []
    for gi in range(n_groups):
        yg = yf[:, gi * gw:(gi + 1) * gw]
        parts.append(yg * lax.rsqrt(jnp.mean(yg * yg, axis=-1, keepdims=True) + EPS))
    hb = (jnp.concatenate(parts, axis=-1) * ng_ref[...]).astype(BF16)
    m = m_in + gm * _dotw(hb, wpb_ref)
    out = x + _dotw(m.astype(BF16), wout_ref)
    return _rmsnorm(out, fg_ref[...]) if final_norm else out


HANDOFF = ("xs", "bm", "cm", "zs", "dt", "gm", "tail")


def _bp_stage1(x_ref, g_ref, wz_ref, wxbc_ref, wdt_ref, wmb_ref, cw_ref, cb_ref, dtb_ref, perm_ref, xs_scr, out,
               first_tile, *, tm, chunk, d_inner, bc_width, conv_w):
    pad = SUBLANES
    vpc = chunk // pad
    n_chunks = tm // chunk
    width = MXU_COLS
    hn = _rmsnorm(x_ref[0], g_ref[...]).astype(BF16)
    for c0 in range(0, out["gm"].shape[1], width):
        cols = slice(c0, c0 + width)
        out["gm"][:, cols] = _sigmoid(_dotw(hn, wmb_ref, cols))
        yield
    hn = _dot(perm_ref[0], hn).astype(BF16)
    out["dt"][...] = jax.nn.softplus(_dotw(hn, wdt_ref) + dtb_ref[...])
    yield
    for c0 in range(0, d_inner, width):
        cols = slice(c0, c0 + width)
        out["zs"][:, cols] = _silu(_dotw(hn, wz_ref, cols))
        yield
    first_sublane = lax.broadcasted_iota(jnp.int32, (pad, width), 0) == 0
    for c0 in range(0, _ncols(wxbc_ref), width):
        cols = slice(c0, c0 + width)
        raw_all = _dotw(hn, wxbc_ref, cols)
        prev_tail = jnp.where(first_tile, 0.0, xs_scr[:, cols])
        acts = []
        for c in range(n_chunks):
            raw = raw_all[c * chunk:(c + 1) * chunk]
            conv = cb_ref[:, cols] + raw * cw_ref[conv_w - 1:conv_w, cols]
            for back in range(1, conv_w):
                head = []
                for v in range(back):
                    w = vpc - back + v
                    cur = pltpu.roll(raw[w * pad:(w + 1) * pad], 1, axis=0)
                    prv = pltpu.roll(prev_tail[(w - vpc + conv_w - 1) * pad:(w - vpc + conv_w) * pad], 1, axis=0)
                    head.append(jnp.where(first_sublane, prv, cur))
                shifted = jnp.concatenate(head + [raw[:chunk - back * pad]], axis=0)
                conv = conv + shifted * cw_ref[conv_w - 1 - back:conv_w - back, cols]
            acts.append(_silu(conv))
            prev_tail = raw[chunk - (conv_w - 1) * pad:]
        xs_scr[:, cols] = prev_tail
        out["tail"][:, cols] = prev_tail
        act = acts[0] if n_chunks == 1 else jnp.concatenate(acts, axis=0)
        if c0 < d_inner:
            out["xs"][:, cols] = act
        elif c0 < d_inner + bc_width:
            out["bm"][:, c0 - d_inner:c0 - d_inner + width] = act.astype(BF16)
        else:
            out["cm"][:, c0 - d_inner - bc_width:c0 - d_inner - bc_width + width] = act.astype(BF16)
        yield


def _bp_stage2(x_ref, min_ref, alog_ref, dsk_ref, ng_ref, wpb_ref, wout_ref, fg_ref, tri_ref, e3_ref, perm_ref,
               y_ref, st_scr, inp, first_tile, *, tm, chunk, n_groups, n_heads, head_dim, n_state, final_norm):
    pad = SUBLANES
    vpc = chunk // pad
    hpg = n_heads // n_groups
    gw = hpg * head_dim
    width = MXU_COLS

    def token_of(q):
        return (q % pad) * vpc + q // pad

    li = token_of(lax.broadcasted_iota(jnp.int32, (chunk, chunk), 0))
    si = token_of(lax.broadcasted_iota(jnp.int32, (chunk, chunk), 1))
    causal = li >= si
    head_of_lane = lax.broadcasted_iota(jnp.int32, (chunk, gw), 1) // head_dim
    tri = tri_ref[...]
    e3 = e3_ref[...]
    a_row = -jnp.exp(alog_ref[...])

    hb_rows = []
    for c in range(tm // chunk):
        rs = slice(c * chunk, (c + 1) * chunk)
        dt = inp["dt"][rs, :]
        acs = _cumsum_rows(tri, dt * a_row)
        acs_t = acs.T
        acs_e = _dot(_split_by_replica(acs, n_heads).astype(BF16), e3)
        a_last = acs_e[chunk - 1:chunk, :]
        xs_c = inp["xs"][rs, :]
        xdt_c = xs_c * _dot(_split_by_replica(dt, n_heads).astype(BF16), e3)
        xdt_b = xdt_c.astype(BF16)
        xd_state = (xdt_c * jnp.exp(a_last - acs_e)).astype(BF16)
        decay_in = jnp.exp(acs_e)
        st_prev = st_scr[...]
        if c == 0:
            st_prev = jnp.where(first_tile, 0.0, st_prev)
        st_prev_b = st_prev.astype(BF16)
        yield
        groups = range(n_groups)
        n_sl = [slice(gi * n_state, (gi + 1) * n_state) for gi in groups]
        g_sl = [slice(gi * gw, (gi + 1) * gw) for gi in groups]
        scores = [_dot_nt(inp["cm"][rs, n_sl[gi]], inp["bm"][rs, n_sl[gi]]) for gi in groups]
        yield
        y_offs = [_dot(inp["cm"][rs, n_sl[gi]], st_prev_b[:, g_sl[gi]]) for gi in groups]
        st_parts = [_dot_tn(inp["bm"][rs, n_sl[gi]], xd_state[:, g_sl[gi]]) for gi in groups]
        st_scr[...] = st_prev * jnp.exp(a_last) + jnp.concatenate(st_parts, axis=1)
        yield
        y_diags = []
        for gi in groups:
            m_heads, x_blocks = [], []
            xg = xdt_b[:, g_sl[gi]]
            for r in range(hpg):
                h = gi * hpg + r
                diff = acs[:, h:h + 1] - acs_t[h:h + 1, :]
                decay = jnp.exp(jnp.where(causal, diff, -jnp.inf))
                m_heads.append((scores[gi] * decay).astype(BF16))
                x_blocks.append(jnp.where(head_of_lane == r, xg, jnp.zeros_like(xg)))
            y_diags.append(_dot(jnp.concatenate(m_heads, axis=1), jnp.concatenate(x_blocks, axis=0)))
            yield
        hb_parts = []
        for gi in groups:
            gs = g_sl[gi]
            y = y_diags[gi] + y_offs[gi] * decay_in[:, gs] + xs_c[:, gs] * dsk_ref[:, gs]
            yf = y * inp["zs"][rs, gs]
            yn = yf * lax.rsqrt(jnp.mean(yf * yf, axis=-1, keepdims=True) + EPS)
            hb_parts.append((yn * ng_ref[:, gs]).astype(BF16))
            if gi % 2:
                yield
        hb_rows.append(jnp.concatenate(hb_parts, axis=1))
    hb = hb_rows[0] if len(hb_rows) == 1 else jnp.concatenate(hb_rows, axis=0)
    hb = _dot(perm_ref[1], hb).astype(BF16)
    yield
    m_parts = []
    for c0 in range(0, wpb_ref.shape[1], width):
        cols = slice(c0, c0 + width)
        m_parts.append((min_ref[0, :, cols] + inp["gm"][:, cols] * _dotw(hb, wpb_ref, cols)).astype(BF16))
        yield
    m = jnp.concatenate(m_parts, axis=1)
    out = x_ref[0] + _dotw(m, wout_ref)
    y_ref[0] = _rmsnorm(out, fg_ref[...]) if final_norm else out
    yield


def _bp_kernel(x1_ref, x2_ref, min_ref, g_ref, wz_ref, wxbc_ref, wdt_ref, wmb_ref, cw_ref, cb_ref, dtb_ref, alog_ref,
               dsk_ref, ng_ref, wpb_ref, wout_ref, fg_ref, tri_ref, e3_ref, perm_ref,
               y_ref, ssm_ref, conv_ref, xs_scr, st_scr, *slot_refs,
               nj, n_tiles, tm, chunk, n_groups, n_heads, head_dim, n_state, conv_w, final_norm):
    t = pl.program_id(0)
    d_inner = n_heads * head_dim
    pad = SUBLANES
    tile1 = jnp.minimum(t, n_tiles - 1)
    tile2 = jnp.maximum(t - 1, 0)
    n_h = len(HANDOFF)
    slots = [dict(zip(HANDOFF, slot_refs[i * n_h:(i + 1) * n_h])) for i in range(2)]

    @pl.when(t == 0)
    def _init():
        for ref in slots[1].values():
            ref[...] = jnp.zeros_like(ref)
        st_scr[...] = jnp.zeros_like(st_scr)
        xs_scr[...] = jnp.zeros_like(xs_scr)

    def step(slot_w, slot_r):
        s1 = _bp_stage1(x1_ref, g_ref, wz_ref, wxbc_ref, wdt_ref, wmb_ref, cw_ref, cb_ref, dtb_ref, perm_ref, xs_scr,
                        slot_w, tile1 % nj == 0, tm=tm, chunk=chunk, d_inner=d_inner,
                        bc_width=n_groups * n_state, conv_w=conv_w)
        s2 = _bp_stage2(x2_ref, min_ref, alog_ref, dsk_ref, ng_ref, wpb_ref, wout_ref, fg_ref, tri_ref, e3_ref,
                        perm_ref, y_ref, st_scr, slot_r, tile2 % nj == 0, tm=tm, chunk=chunk, n_groups=n_groups,
                        n_heads=n_heads, head_dim=head_dim, n_state=n_state, final_norm=final_norm)
        live = [s2, s1]
        while live:
            for gen in list(live):
                if next(gen, StopIteration) is StopIteration:
                    live.remove(gen)

        @pl.when((t > 0) & (tile2 % nj == nj - 1))
        def _seq_end():
            tail = slot_r["tail"][...]
            conv_ref[0] = jnp.concatenate([tail[(v + 1) * pad - 1:(v + 1) * pad] for v in range(conv_w - 1)], axis=0)
            ssm_ref[0] = st_scr[...].T

    for parity in range(2):
        pl.when(t % 2 == parity)(functools.partial(step, slots[parity], slots[1 - parity]))


def _branch_b_prompt(x3, m3, w, *, tm, final_norm):
    bsz, seq, d = x3.shape
    conv_dim = w["cw"].shape[1]
    d_inner = w["ng"].shape[1]
    n_heads = d_inner // SSD_HEAD_DIM
    conv_w = w["cw"].shape[0]
    nj = seq // tm
    n_tiles = bsz * nj
    bc_width = SSD_GROUPS * SSD_STATE
    kern = functools.partial(_bp_kernel, nj=nj, n_tiles=n_tiles, tm=tm, chunk=CHUNK, n_groups=SSD_GROUPS,
                             n_heads=n_heads, head_dim=SSD_HEAD_DIM, n_state=SSD_STATE, conv_w=conv_w,
                             final_norm=final_norm)

    def tile1(t):
        return jnp.minimum(t, n_tiles - 1)

    def tile2(t):
        return jnp.maximum(t - 1, 0)

    spec1 = pl.BlockSpec((1, tm, d), lambda t: (tile1(t) // nj, tile1(t) % nj, 0))
    spec2 = pl.BlockSpec((1, tm, d), lambda t: (tile2(t) // nj, tile2(t) % nj, 0))
    consts = [w[k] for k in ("g", "wz", "wxbc", "wdt", "wmb", "cw", "cb", "dtb", "alog", "dsk", "ng", "wpb", "wout",
                             "fg", "tri", "e3")]
    tok = (np.arange(tm) // CHUNK) * CHUNK + np.tile(_chunk_tokens(), tm // CHUNK)
    p = (tok[:, None] == np.arange(tm)[None, :]).astype(np.float32)
    perm = jnp.asarray(np.stack([p, p.T]), dtype=BF16)
    handoff = {"xs": ((tm, d_inner), F32), "bm": ((tm, bc_width), BF16), "cm": ((tm, bc_width), BF16),
               "zs": ((tm, d_inner), F32), "dt": ((tm, LANES), F32), "gm": ((tm, d), F32),
               "tail": (((conv_w - 1) * SUBLANES, conv_dim), F32)}
    return pl.pallas_call(
        _regroup(kern, 3, consts),
        out_shape=(jax.ShapeDtypeStruct((bsz, seq, d), F32),
                   jax.ShapeDtypeStruct((bsz, d_inner, SSD_STATE), F32),
                   jax.ShapeDtypeStruct((bsz, conv_w - 1, conv_dim), F32)),
        grid=(n_tiles + 1,),
        in_specs=[spec1, spec2, spec2] + _const_specs(consts + [perm]),
        out_specs=(spec2,
                   pl.BlockSpec((1, d_inner, SSD_STATE), lambda t: (tile2(t) // nj, 0, 0)),
                   pl.BlockSpec((1, conv_w - 1, conv_dim), lambda t: (tile2(t) // nj, 0, 0))),
        scratch_shapes=([pltpu.VMEM(((conv_w - 1) * SUBLANES, conv_dim), F32), pltpu.VMEM((SSD_STATE, d_inner), F32)]
                        + [pltpu.VMEM(*handoff[k]) for _ in range(2) for k in HANDOFF]),
        compiler_params=_params(1),
        name="branch_b_prompt",
    )(x3, x3, m3, *_arrays(consts), perm)


def _bs_kernel(x_ref, g_ref, wxbc_ref, wdt_ref, cw_ref, cb_ref, dtb_ref, alog_ref, dsk_ref,
               tri_ref, e3_ref, ecol_ref, conv0_ref, h0_ref,
               y_ref, hnew_ref, convnew_ref, xs_scr, tr_scr,
               *, nb, rows, n_groups, n_heads, head_dim, n_state, conv_w):
    hpg = n_heads // n_groups
    gw = hpg * head_dim
    d_inner = n_heads * head_dim
    d = x_ref.shape[-1]
    r_all = nb * rows
    pad = SUBLANES
    hs_w = n_heads * rows

    @pl.when(pl.program_id(0) == 0)
    def _init():
        tr_scr[...] = jnp.zeros_like(tr_scr)
        xs_scr[...] = jnp.zeros_like(xs_scr)

    x = x_ref[...].reshape(r_all, d)
    hn = _rmsnorm(x, g_ref[...]).astype(BF16)
    xbc_raw = _dotw(hn, wxbc_ref)
    dt = jax.nn.softplus(_dotw(hn, wdt_ref) + dtb_ref[...])

    raw = xbc_raw.reshape(nb, rows, xbc_raw.shape[1])
    xs_scr[:, rows - (conv_w - 1):, :] = conv0_ref[...]
    state = xs_scr[...]
    row_in_seq = lax.broadcasted_iota(jnp.int32, raw.shape, 1)
    conv = cb_ref[...] + raw * cw_ref[conv_w - 1:conv_w, :]
    for back in range(1, conv_w):
        shifted = jnp.where(row_in_seq < back, pltpu.roll(state, back, axis=1), pltpu.roll(raw, back, axis=1))
        conv = conv + shifted * cw_ref[conv_w - 1 - back:conv_w - back, :]
    convnew_ref[...] = raw[:, rows - (conv_w - 1):, :]
    xbc = _silu(conv.reshape(r_all, xbc_raw.shape[1]))
    xs = xbc[:, :d_inner]
    bm = xbc[:, d_inner:d_inner + n_groups * n_state]
    cm = xbc[:, d_inner + n_groups * n_state:]

    a_row = -jnp.exp(alog_ref[...])
    acs = _cumsum_rows(tri_ref[...], dt * a_row)
    e3 = e3_ref[...]
    acs_split = _split_by_replica(acs, n_heads).astype(BF16)
    acs_e = _dot(acs_split, e3)
    xdt = xs * _dot(_split_by_replica(dt, n_heads).astype(BF16), e3)
    decay_in = jnp.exp(acs_e)

    acs_col = _dot(acs_split, ecol_ref[...])
    l_idx = lax.broadcasted_iota(jnp.int32, (r_all, hs_w), 0) % rows
    s_idx = lax.broadcasted_iota(jnp.int32, (r_all, hs_w), 1) % rows
    on_diag = jnp.where(l_idx == s_idx, acs_col, 0.0).reshape(nb, rows, hs_w)
    acs_row = jnp.broadcast_to(jnp.sum(on_diag, axis=1, keepdims=True), (nb, rows, hs_w)).reshape(r_all, hs_w)
    decay = jnp.exp(jnp.where(l_idx >= s_idx, acs_col - acs_row, -jnp.inf))

    grp_of_row = lax.broadcasted_iota(jnp.int32, (hs_w, n_groups * n_state), 0) // (rows * hpg)
    grp_of_lane = lax.broadcasted_iota(jnp.int32, (hs_w, n_groups * n_state), 1) // n_state
    head_of_row = lax.broadcasted_iota(jnp.int32, (hs_w, d_inner), 0) // rows
    head_of_lane = lax.broadcasted_iota(jnp.int32, (hs_w, d_inner), 1) // head_dim
    slot = LANES // nb
    ones_rows = lax.broadcasted_iota(jnp.int32, (slot, n_state), 0)
    ones_blk = jnp.where((ones_rows >= rows) & (ones_rows < rows + 3), 1.0, 0.0)

    def seq_rows(b):
        return slice(b * rows, (b + 1) * rows)

    def expand_rows(v, same):
        rep = jnp.broadcast_to(v[None], (n_heads, rows, v.shape[1])).reshape(hs_w, v.shape[1])
        return jnp.where(same, rep, 0.0).astype(BF16)

    y_diags = []
    for b0 in range(0, nb, SEQ_BATCH):
        seqs = range(b0, min(b0 + SEQ_BATCH, nb))
        b_exps = [expand_rows(bm[seq_rows(b)], grp_of_row == grp_of_lane) for b in seqs]
        scores = [_dot_nt(cm[seq_rows(b)].astype(BF16), be) for b, be in zip(seqs, b_exps)]
        x_blks = [expand_rows(xdt[seq_rows(b)], head_of_row == head_of_lane) for b in seqs]
        m_alls = [(s * decay[seq_rows(b)]).astype(BF16) for b, s in zip(seqs, scores)]
        y_diags += [_dot(m, xb) for m, xb in zip(m_alls, x_blks)]
    for b in range(nb):
        a_last = acs_e[(b + 1) * rows - 1:(b + 1) * rows, :]
        d_hi, d_mid, d_lo = _bf16_parts(jnp.exp(a_last))
        tr_scr[b * slot:b * slot + rows, :] = xdt[seq_rows(b)] * jnp.exp(a_last - acs_e[seq_rows(b)])
        tr_scr[b * slot + rows:b * slot + rows + 1, :] = d_hi.astype(F32)
        tr_scr[b * slot + rows + 1:b * slot + rows + 2, :] = d_mid.astype(F32)
        tr_scr[b * slot + rows + 2:b * slot + rows + 3, :] = d_lo.astype(F32)
    tr_t = tr_scr[...].T.astype(BF16)

    ys = []
    for b in range(nb):
        rs = slice(b * rows, (b + 1) * rows)
        bm_b, cm_bb = bm[rs], cm[rs].astype(BF16)
        y_parts = []
        for gi in range(n_groups):
            ns = slice(gi * n_state, (gi + 1) * n_state)
            gs = slice(gi * gw, (gi + 1) * gw)
            h0_g = h0_ref[b, gs, :]
            y_parts.append(_dot_nt(cm_bb[:, ns], h0_g.astype(BF16)))
            w_seq = jnp.concatenate(
                [jnp.concatenate([bm_b[:, ns], jnp.zeros((slot - rows, n_state), F32)], axis=0), ones_blk], axis=1)
            pieces = ([jnp.zeros((b * slot, 2 * n_state), F32)] if b else []) + [w_seq]
            if b < nb - 1:
                pieces.append(jnp.zeros((LANES - (b + 1) * slot, 2 * n_state), F32))
            upd = _dot(tr_t[gs, :], jnp.concatenate(pieces, axis=0).astype(BF16))
            hnew_ref[b, gs, :] = h0_g * upd[:, n_state:] + upd[:, :n_state]
        ys.append(y_diags[b] + jnp.concatenate(y_parts, axis=1) * decay_in[rs])
    y = jnp.concatenate(ys, axis=0) + xs * dsk_ref[...]
    y_ref[...] = y.reshape(nb, rows, d_inner)


def _bfin_kernel(x_ref, y_ref, min_ref, g_ref, wz_ref, wmb_ref, ng_ref, wpb_ref, wout_ref, fg_ref, out_ref,
                 *, n_groups, final_norm):
    out_ref[...] = _finish_b(x_ref[...], y_ref[...], g_ref, wz_ref, wmb_ref, ng_ref, wpb_ref, min_ref[...],
                             wout_ref, fg_ref, n_groups=n_groups, final_norm=final_norm)


def _branch_b_sample(x3, m3, conv0, h0, w, *, nb, tm, final_norm):
    bsz, rows, d = x3.shape
    conv_dim = w["cw"].shape[1]
    d_inner = w["ng"].shape[1]
    n_heads = d_inner // SSD_HEAD_DIM
    conv_w = w["cw"].shape[0]
    kern = functools.partial(_bs_kernel, nb=nb, rows=rows, n_groups=SSD_GROUPS, n_heads=n_heads,
                             head_dim=SSD_HEAD_DIM, n_state=SSD_STATE, conv_w=conv_w)

    def seq_spec(width):
        return pl.BlockSpec((nb, rows, width), lambda i: (i, 0, 0))

    conv_spec = pl.BlockSpec((nb, conv_w - 1, conv_dim), lambda i: (i, 0, 0))
    h_spec = pl.BlockSpec((nb, d_inner, SSD_STATE), lambda i: (i, 0, 0))
    consts = [w[k] for k in ("g", "wxbc", "wdt", "cw", "cb", "dtb", "alog", "dsk", "tri_s", "e3", "ecol")]
    y, h_new, conv_new = pl.pallas_call(
        _regroup(kern, 1, consts),
        out_shape=(jax.ShapeDtypeStruct((bsz, rows, d_inner), F32),
                   jax.ShapeDtypeStruct((bsz, d_inner, SSD_STATE), F32),
                   jax.ShapeDtypeStruct((bsz, conv_w - 1, conv_dim), F32)),
        grid=(bsz // nb,),
        in_specs=[seq_spec(d)] + _const_specs(consts) + [conv_spec, h_spec],
        out_specs=(seq_spec(d_inner), h_spec, conv_spec),
        scratch_shapes=[pltpu.VMEM((nb, SUBLANES, conv_dim), F32), pltpu.VMEM((LANES, d_inner), F32)],
        compiler_params=_params(1),
        name="branch_b_sample",
    )(x3, *_arrays(consts), conv0, h0)

    n_rows = bsz * rows
    fin_consts = [w[k] for k in ("g", "wz", "wmb", "ng", "wpb", "wout", "fg")]

    def rows_spec(width):
        return pl.BlockSpec((tm, width), lambda i: (i, 0))

    out = pl.pallas_call(
        _regroup(functools.partial(_bfin_kernel, n_groups=SSD_GROUPS, final_norm=final_norm), 3, fin_consts),
        out_shape=jax.ShapeDtypeStruct((n_rows, d), F32),
        grid=(n_rows // tm,),
        in_specs=[rows_spec(d), rows_spec(d_inner), rows_spec(d)] + _const_specs(fin_consts),
        out_specs=rows_spec(d),
        compiler_params=_params(1),
        name="branch_b_sample_out",
    )(x3.reshape(n_rows, d), y.reshape(n_rows, d_inner), m3.reshape(n_rows, d), *_arrays(fin_consts))
    return out.reshape(bsz, rows, d), h_new, conv_new


def _tile(n, pref):
    return pref if n % pref == 0 else n


def _chunk_tokens():
    q = np.arange(CHUNK)
    return (q % SUBLANES) * (CHUNK // SUBLANES) + q // SUBLANES


def _head_expand(n_heads, width, n_rep):
    j = np.arange(LANES)[:, None]
    c = np.arange(n_heads * width)[None, :]
    return jnp.asarray(((j < n_rep * n_heads) & (j % n_heads == c // width)).astype(np.float32), dtype=BF16)


def _layer_weights(l, d, dec_rows, nb_s, norm_g, w_in, conv_w, conv_b, dt_bias, a_log, d_skip, ssd_norm_g, ln_v_g,
                   ln_v_b, w_spatial, b_spatial, w_proj_a, w_proj_b, w_proj_x, w_out, final_norm_g):
    d_a = d
    d_inner = w_proj_b.shape[1]
    conv_dim = conv_w.shape[2]
    n_heads = a_log.shape[1]
    d_x = w_proj_x.shape[1]
    sizes = (d_a, d_a, d_a, d_inner, conv_dim, n_heads, d_x, d_x, N_BRANCH * d)
    offs = np.concatenate([[0], np.cumsum(sizes)])
    wl = w_in[l]
    sec = lambda i: wl[:, offs[i]:offs[i + 1]]
    merge = sec(8)
    row = lambda v: v.reshape(1, -1).astype(F32)
    rep3 = lambda v, fill: jnp.concatenate(
        [v] * DT_REPLICAS + [jnp.full(v.shape[:-1] + (LANES - DT_REPLICAS * v.shape[-1],), fill, v.dtype)], axis=-1)

    tril = jnp.tril(jnp.ones((CHUNK, CHUNK), F32))
    ws_p = jnp.where(tril[None] > 0, w_spatial[l], 0.0)
    bs_p = jnp.repeat(b_spatial[l].T, d_a // A_GROUPS, axis=1)
    n_seq = CHUNK // dec_rows
    eye = jnp.eye(n_seq, dtype=F32)
    ws_s = jnp.stack([jnp.kron(eye, ws_p[gi, :dec_rows, :dec_rows]) for gi in range(A_GROUPS)])
    bs_s = jnp.tile(bs_p[:dec_rows], (n_seq, 1))

    w = {
        "g": row(norm_g[l]),
        "lng": row(ln_v_g[l]), "lnb": row(ln_v_b[l]),
        "ws_p": ws_p.astype(BF16), "bs_p": bs_p, "ws_s": ws_s.astype(BF16), "bs_s": bs_s,
        "wpa": _pack_w(w_proj_a[l]),
        "cw": conv_w[l].astype(F32), "cb": row(conv_b[l]),
        "dtb": rep3(row(dt_bias[l]), 0.0), "alog": rep3(row(a_log[l]), 0.0),
        "dsk": jnp.repeat(row(d_skip[l]), SSD_HEAD_DIM, axis=1),
        "ng": row(ssd_norm_g[l]),
        "wpb": _pack_w(w_proj_b[l]),
        "wout": _pack_w(w_out[l]), "fg": row(final_norm_g),
        "tri": jnp.asarray(_chunk_tokens()[:, None] >= _chunk_tokens()[None, :], dtype=BF16),
        "tri_s": jnp.kron(jnp.eye(nb_s, dtype=F32), jnp.tril(jnp.ones((dec_rows, dec_rows), F32))).astype(BF16),
        "e3": _head_expand(n_heads, SSD_HEAD_DIM, DT_REPLICAS),
        "ecol": _head_expand(n_heads, dec_rows, DT_REPLICAS),
        "wpx": _pack_w(w_proj_x[l]),
    }
    w_bf = _pack_w(wl)
    tail = w_bf[:, offs[6]:]
    assert all(o % d == 0 for o in offs[:5]) and d_x % LANES == 0 and d % d_x == 0
    for name, i in (("wu", 0), ("wv", 1), ("wga", 2), ("wz", 3), ("wxbc", 4)):
        windows = [_Cols(w_bf, c, d) for c in range(offs[i], offs[i + 1], d)]
        w[name] = windows if len(windows) > 1 else windows[0]
    w["wdt"] = _pack_w(rep3(sec(5), 0.0))
    w["wq"], w["wgx"] = _Cols(tail, 0, d_x), _Cols(tail, d_x, d_x)
    for k, name in enumerate(("wma", "wmb", "wmx")):
        w[name] = _Cols(tail, 2 * d_x + k * d, d)
    return w


def _layer(x3, k3, v3, w, *, nb_x, rows_x, heads_in_rows, tm_a, ws, bs, want_v, b_fn):
    bsz, seq, d = x3.shape
    m = _branch_x(x3, w["g"], w["wq"], w["wgx"], w["wmx"], k3, v3, w["wpx"], nb=nb_x, rows=rows_x,
                  heads_in_rows=heads_in_rows)
    m, vn = _branch_a(x3.reshape(bsz * seq, d), w["g"], w["wu"], w["wv"], w["wga"], w["wma"], w["lng"], w["lnb"],
                      ws, bs, w["wpa"], m.reshape(bsz * seq, d), tm=tm_a, want_v=want_v)
    return b_fn(x3, m.reshape(bsz, seq, d)), vn


def kernel(x_prompt, x_sample, mem_prompt, cache_mem_k, cache_mem_v, state_ssm, state_conv, norm_g, w_in, conv_w,
           conv_b, dt_bias, a_log, d_skip, ssd_norm_g, ln_v_g, ln_v_b, w_spatial, b_spatial, mem_norm_g, w_mem_kv,
           w_proj_a, w_proj_b, w_proj_x, w_out, final_norm_g):
    depth = w_in.shape[0]
    bsz, seq, d = x_prompt.shape
    dec_b, dec_rows, _ = x_sample.shape
    n_mem = mem_prompt.shape[1]
    d_x = w_proj_x.shape[1]
    d_inner = w_proj_b.shape[1]
    n_heads = a_log.shape[1]
    assert seq % CHUNK == 0 and CHUNK % dec_rows == 0 and dec_rows == SUBLANES
    assert DT_REPLICAS * n_heads <= LANES and n_heads * dec_rows % LANES == 0

    nb_s = _tile(dec_b, 8)
    nb_xs = _tile(dec_b, 8)
    yp, ys = x_prompt, x_sample
    outs = {k: [] for k in ("mk", "mv", "hp", "cp", "hs", "cs", "vs")}
    for l in range(depth):
        w = _layer_weights(l, d, dec_rows, nb_s, norm_g, w_in, conv_w, conv_b, dt_bias, a_log, d_skip, ssd_norm_g,
                           ln_v_g, ln_v_b, w_spatial, b_spatial, w_proj_a, w_proj_b, w_proj_x, w_out, final_norm_g)
        final_norm = l == depth - 1
        mk, mv = _mem_kv(mem_prompt.reshape(bsz * n_mem, d), mem_norm_g[l].reshape(1, d).astype(F32),
                         _pack_w(w_mem_kv[l]), _tile(bsz * n_mem, 512))
        (yp, hp, cp), _ = _layer(
            yp, mk.reshape(bsz, n_mem, d_x), mv.reshape(bsz, n_mem, d_x), w,
            nb_x=1, rows_x=_tile(seq, 512), heads_in_rows=False, tm_a=_tile(bsz * seq, 512), ws=w["ws_p"], bs=w["bs_p"], want_v=False,
            b_fn=functools.partial(_branch_b_prompt, w=w, tm=_tile(seq, 256), final_norm=final_norm))
        (ys, hs, cs), vs = _layer(
            ys, cache_mem_k[l].reshape(dec_b, n_mem * (d_x // X_HEAD_DIM), X_HEAD_DIM),
            cache_mem_v[l].reshape(dec_b, n_mem * (d_x // X_HEAD_DIM), X_HEAD_DIM), w,
            nb_x=nb_xs, rows_x=dec_rows, heads_in_rows=True, tm_a=_tile(dec_b * dec_rows, 512), ws=w["ws_s"], bs=w["bs_s"], want_v=True,
            b_fn=functools.partial(_branch_b_sample, conv0=state_conv[l],
                                   h0=state_ssm[l].reshape(dec_b, d_inner, SSD_STATE), w=w, nb=nb_s,
                                   tm=_tile(dec_b * dec_rows, 512), final_norm=final_norm))
        outs["mk"].append(mk.reshape(bsz, n_mem, d_x // X_HEAD_DIM, X_HEAD_DIM))
        outs["mv"].append(mv.reshape(bsz, n_mem, d_x // X_HEAD_DIM, X_HEAD_DIM))
        outs["hp"].append(hp.reshape(bsz, n_heads, SSD_HEAD_DIM, SSD_STATE))
        outs["cp"].append(cp)
        outs["hs"].append(hs.reshape(dec_b, n_heads, SSD_HEAD_DIM, SSD_STATE))
        outs["cs"].append(cs)
        outs["vs"].append(vs.reshape(dec_b, dec_rows, d))
    st = lambda k: jnp.stack(outs[k])
    return (yp, ys, st("mk"), st("mv"), st("hp"), st("cp"), st("hs"), st("cs"), st("vs"))
```

```python
import functools
import math
from typing import NamedTuple

import jax
import jax.numpy as jnp
import numpy as np
from jax import lax
from jax.experimental import pallas as pl
from jax.experimental.pallas import tpu as pltpu

F32 = jnp.float32
BF16 = jnp.bfloat16
EPS = 1e-6
SQRT_HALF = math.sqrt(0.5)

LANES = 128
SUBLANES = 8
VMEM_LIMIT_BYTES = 56 * 1024 * 1024

CHUNK = 128
A_GROUPS = 8
SSD_HEAD_DIM = 64
SSD_GROUPS = 8
SSD_STATE = 128
X_HEAD_DIM = 128
N_BRANCH = 3
DT_REPLICAS = 3
SEQ_BATCH = 4

def _dot(a, b):
    return jnp.dot(a, b, preferred_element_type=F32)


def _dot_nt(a, b):
    return lax.dot_general(a, b, (((1,), (1,)), ((), ())), preferred_element_type=F32)


def _dot_tn(a, b):
    return lax.dot_general(a, b, (((0,), (0,)), ((), ())), preferred_element_type=F32)


MXU_COLS = 256


def _pack_w(w):
    return w.astype(BF16)


def _ncols(w):
    return sum(r.shape[1] for r in w) if isinstance(w, tuple) else w.shape[1]


def _dotw(a, w, cols=None):
    if not isinstance(w, tuple):
        return _dot(a, w[...] if cols is None else w[:, cols])
    width = w[0].shape[1]
    lo, hi = (0, width * len(w)) if cols is None else (cols.start, cols.stop)
    parts = []
    while lo < hi:
        k, off = divmod(lo, width)
        n = min(width - off, hi - lo)
        parts.append(_dot(a, w[k][:, off:off + n]))
        lo += n
    return parts[0] if len(parts) == 1 else jnp.concatenate(parts, axis=1)


def _dot_cols(a, w_ref, fn, width=MXU_COLS):
    n = w_ref.shape[1]
    width = min(width, n)
    blocks = [fn(_dotw(a, w_ref, slice(c, c + width))) for c in range(0, n, width)]
    return blocks[0] if len(blocks) == 1 else jnp.concatenate(blocks, axis=1)


def _rmsnorm(x, g):
    return x * lax.rsqrt(jnp.mean(x * x, axis=-1, keepdims=True) + EPS) * g


def _gelu(x):
    return 0.5 * x * (1.0 + lax.erf(x * SQRT_HALF))


def _sigmoid(x):
    return 0.5 * jnp.tanh(0.5 * x) + 0.5


def _silu(x):
    return x * _sigmoid(x)


def _bf16_parts(x):
    hi = x.astype(BF16)
    r1 = x - hi.astype(F32)
    mid = r1.astype(BF16)
    lo = (r1 - mid.astype(F32)).astype(BF16)
    return hi, mid, lo


def _split_by_replica(x, n_heads):
    hi, mid, lo = _bf16_parts(x)
    lane = lax.broadcasted_iota(jnp.int32, x.shape, 1)
    return jnp.where(lane < n_heads, hi, jnp.where(lane < 2 * n_heads, mid, lo))


def _cumsum_rows(tri, x):
    hi, mid, lo = _bf16_parts(x)
    return _dot(tri, hi) + _dot(tri, mid) + _dot(tri, lo)


def _memkv_kernel(x_ref, g_ref, w_ref, k_ref, v_ref, *, n_heads):
    tm = x_ref.shape[0]
    hd = k_ref.shape[-1]
    hn = _rmsnorm(x_ref[...], g_ref[...]).astype(BF16)
    kv = _dotw(hn, w_ref)
    for h in range(n_heads):
        k_ref[pl.ds(h, tm, stride=n_heads), :] = kv[:, h * hd:(h + 1) * hd]
        v_ref[pl.ds(h, tm, stride=n_heads), :] = kv[:, (n_heads + h) * hd:(n_heads + h + 1) * hd]


class _Cols(NamedTuple):
    mat: jax.Array
    start: int
    width: int

    @property
    def shape(self):
        return (self.mat.shape[0], self.width)


def _const_spec(op=None):
    if isinstance(op, _Cols):
        idx = op.start // op.width
        return pl.BlockSpec(op.shape, lambda *_: (0, idx), pipeline_mode=pl.Buffered(1))
    return pl.BlockSpec(memory_space=pltpu.VMEM)


def _flatten(ops):
    return [o for op in ops for o in (op if isinstance(op, list) else [op])]


def _const_specs(ops):
    return [_const_spec(op) for op in _flatten(ops)]


def _arrays(ops):
    return [op.mat if isinstance(op, _Cols) else op for op in _flatten(ops)]


def _regroup(kernel_fn, n_lead, ops):
    sizes = [len(op) if isinstance(op, list) else 0 for op in ops]

    def wrapped(*refs):
        refs = list(refs)
        args, pos = refs[:n_lead], n_lead
        for n in sizes:
            args.append(tuple(refs[pos:pos + n]) if n else refs[pos])
            pos += max(n, 1)
        return kernel_fn(*args, *refs[pos:])

    return wrapped


def _params(n_grid):
    return pltpu.CompilerParams(dimension_semantics=("arbitrary",) * n_grid, vmem_limit_bytes=VMEM_LIMIT_BYTES)


def _mem_kv(mem2d, g, w, tm):
    rows, d = mem2d.shape
    n_heads = w.shape[1] // 2 // X_HEAD_DIM
    out = jax.ShapeDtypeStruct((rows * n_heads, X_HEAD_DIM), F32)
    out_spec = pl.BlockSpec((tm * n_heads, X_HEAD_DIM), lambda i: (i, 0))
    return pl.pallas_call(
        functools.partial(_memkv_kernel, n_heads=n_heads),
        out_shape=(out, out),
        grid=(rows // tm,),
        in_specs=[pl.BlockSpec((tm, d), lambda i: (i, 0)), _const_spec(), _const_spec()],
        out_specs=(out_spec, out_spec),
        compiler_params=_params(1),
        name="mem_kv",
    )(mem2d, g, w)


def _x_kernel(x_ref, g_ref, wq_ref, wgx_ref, wmx_ref, k_ref, v_ref, wpx_ref, m_ref, *, nb, rows, head_dim):
    d = x_ref.shape[-1]
    x = x_ref[...].reshape(nb * rows, d)
    hn = _rmsnorm(x, g_ref[...]).astype(BF16)
    q = _dotw(hn, wq_ref)
    gate = _silu(_dotw(hn, wgx_ref))
    gm = _sigmoid(_dotw(hn, wmx_ref))
    n_heads = q.shape[1] // head_dim
    scale = head_dim ** -0.5
    outs = []
    n_mem = k_ref.shape[1] // n_heads
    for b in range(nb):
        qb = q[b * rows:(b + 1) * rows].astype(BF16)
        heads = range(n_heads)
        kb = [k_ref[b, pl.ds(h, n_mem, stride=n_heads), :].astype(BF16) for h in heads]
        vb = [v_ref[b, pl.ds(h, n_mem, stride=n_heads), :].astype(BF16) for h in heads]
        scores = [_dot_nt(qb[:, h * head_dim:(h + 1) * head_dim], kb[h]) * scale for h in heads]
        exps = [jnp.exp(s - jnp.max(s, axis=-1, keepdims=True)) for s in scores]
        probs = [(e / jnp.sum(e, axis=-1, keepdims=True)).astype(BF16) for e in exps]
        outs.append(jnp.concatenate([_dot(probs[h], vb[h]) for h in heads], axis=-1))
    o = outs[0] if nb == 1 else jnp.concatenate(outs, axis=0)
    hx = (o * gate).astype(BF16)
    m = gm * _dotw(hx, wpx_ref)
    m_ref[...] = m.reshape(nb, rows, d)


def _xs_kernel(x_ref, g_ref, wq_ref, wgx_ref, wmx_ref, k_ref, v_ref, wpx_ref, m_ref, *, nb, rows, head_dim):
    d = x_ref.shape[-1]
    x = x_ref[...].reshape(nb * rows, d)
    hn = _rmsnorm(x, g_ref[...]).astype(BF16)
    q = _dotw(hn, wq_ref)
    gate = _silu(_dotw(hn, wgx_ref))
    gm = _sigmoid(_dotw(hn, wmx_ref))
    n_heads = q.shape[1] // head_dim
    n_kv = k_ref.shape[1]
    scale = head_dim ** -0.5
    row_head = lax.broadcasted_iota(jnp.int32, (n_heads * rows, n_kv), 0) // rows
    col_head = lax.broadcasted_iota(jnp.int32, (n_heads * rows, n_kv), 1) % n_heads
    same_head = row_head == col_head
    seqs = range(nb)
    scores = []
    for b in seqs:
        qb = q[b * rows:(b + 1) * rows]
        q_heads = jnp.concatenate([qb[:, h * head_dim:(h + 1) * head_dim] for h in range(n_heads)], axis=0)
        scores.append(_dot_nt(q_heads.astype(BF16), k_ref[b].astype(BF16)))
    masked = [jnp.where(same_head, s * scale, -jnp.inf) for s in scores]
    exps = [jnp.exp(s - jnp.max(s, axis=-1, keepdims=True)) for s in masked]
    probs = [(e / jnp.sum(e, axis=-1, keepdims=True)).astype(BF16) for e in exps]
    o_heads = [_dot(probs[b], v_ref[b].astype(BF16)) for b in seqs]
    outs = [jnp.concatenate([oh[h * rows:(h + 1) * rows] for h in range(n_heads)], axis=1) for oh in o_heads]
    o = outs[0] if nb == 1 else jnp.concatenate(outs, axis=0)
    hx = (o * gate).astype(BF16)
    m = gm * _dotw(hx, wpx_ref)
    m_ref[...] = m.reshape(nb, rows, d)


def _branch_x(x3, g, wq, wgx, wmx, k3, v3, wpx, *, nb, rows, heads_in_rows):
    bsz, seq, d = x3.shape
    n_mem, dx = k3.shape[1], k3.shape[2]
    kern = functools.partial(_xs_kernel if heads_in_rows else _x_kernel, nb=nb, rows=rows, head_dim=X_HEAD_DIM)
    return pl.pallas_call(
        kern,
        out_shape=jax.ShapeDtypeStruct((bsz, seq, d), F32),
        grid=(bsz // nb, seq // rows),
        in_specs=[pl.BlockSpec((nb, rows, d), lambda i, j: (i, j, 0))] + _const_specs([g, wq, wgx, wmx]) + [
            pl.BlockSpec((nb, n_mem, dx), lambda i, j: (i, 0, 0)),
            pl.BlockSpec((nb, n_mem, dx), lambda i, j: (i, 0, 0)),
            _const_spec(wpx),
        ],
        out_specs=pl.BlockSpec((nb, rows, d), lambda i, j: (i, j, 0)),
        compiler_params=_params(2),
        name="branch_x",
    )(x3, *_arrays([g, wq, wgx, wmx]), k3, v3, wpx)


def _a_kernel(x_ref, g_ref, wu_ref, wv_ref, wga_ref, wma_ref, lng_ref, lnb_ref, ws_ref, bs_ref, wpa_ref, min_ref,
              mout_ref, *maybe_v_ref, chunk, groups, sub):
    tm, d = x_ref.shape
    gd = d // groups
    bias = bs_ref[...]
    for t in range(tm // sub):
        ts = slice(t * sub, (t + 1) * sub)
        hn = _rmsnorm(x_ref[ts, :], g_ref[...]).astype(BF16)
        v = _dot_cols(hn, wv_ref, _gelu)
        u = _dot_cols(hn, wu_ref, _gelu)
        vc = v - jnp.mean(v, axis=-1, keepdims=True)
        vn = vc * lax.rsqrt(jnp.mean(vc * vc, axis=-1, keepdims=True) + EPS) * lng_ref[...] + lnb_ref[...]
        if maybe_v_ref:
            maybe_v_ref[0][ts, :] = vn
        ga = _dot_cols(hn, wga_ref, _silu)
        vb = vn.astype(BF16)
        n_c = sub // chunk
        wide = [_dot(ws_ref[gi], jnp.concatenate([vb[c * chunk:(c + 1) * chunk, gi * gd:(gi + 1) * gd]
                                                  for c in range(n_c)], axis=1)) for gi in range(groups)]
        mixed = [jnp.concatenate([wg[:, c * gd:(c + 1) * gd] for wg in wide], axis=1) + bias for c in range(n_c)]
        s = mixed[0] if n_c == 1 else jnp.concatenate(mixed, axis=0)
        ha = (u * s * ga).astype(BF16)
        gm = _sigmoid(_dotw(hn, wma_ref))
        mout_ref[ts, :] = min_ref[ts, :] + gm * _dotw(ha, wpa_ref)


def _branch_a(x2, g, wu, wv, wga, wma, lng, lnb, ws, bs, wpa, m_in, *, tm, want_v):
    rows, d = x2.shape
    row_spec = pl.BlockSpec((tm, d), lambda i: (i, 0))
    out_shape = [jax.ShapeDtypeStruct((rows, d), F32)]
    out_specs = [row_spec]
    if want_v:
        out_shape.append(jax.ShapeDtypeStruct((rows, d), F32))
        out_specs.append(row_spec)
    kern = functools.partial(_a_kernel, chunk=CHUNK, groups=A_GROUPS, sub=min(tm, 256))
    consts = [g, wu, wv, wga, wma, lng, lnb, ws, bs, wpa]
    res = pl.pallas_call(
        kern,
        out_shape=tuple(out_shape),
        grid=(rows // tm,),
        in_specs=[row_spec] + _const_specs(consts) + [row_spec],
        out_specs=tuple(out_specs),
        compiler_params=_params(1),
        name="branch_a",
    )(x2, *_arrays(consts), m_in)
    return res if want_v else (res[0], None)


def _finish_b(x, y, g_ref, wz_ref, wmb_ref, ng_ref, wpb_ref, m_in, wout_ref, fg_ref, *, n_groups, final_norm):
    hn = _rmsnorm(x, g_ref[...]).astype(BF16)
    gm = _sigmoid(_dotw(hn, wmb_ref))
    yf = y * _silu(_dotw(hn, wz_ref))
    gw = yf.shape[1] // n_groups
    parts = []
    for gi in range(n_groups):
        yg = yf[:, gi * gw:(gi + 1) * gw]
        parts.append(yg * lax.rsqrt(jnp.mean(yg * yg, axis=-1, keepdims=True) + EPS))
    hb = (jnp.concatenate(parts, axis=-1) * ng_ref[...]).astype(BF16)
    m = m_in + gm * _dotw(hb, wpb_ref)
    out = x + _dotw(m.astype(BF16), wout_ref)
    return _rmsnorm(out, fg_ref[...]) if final_norm else out


HANDOFF = ("xs", "bm", "cm", "zs", "dt", "gm", "tail")


def _bp_stage1(x_ref, g_ref, wz_ref, wxbc_ref, wdt_ref, wmb_ref, cw_ref, cb_ref, dtb_ref, perm_ref, xs_scr, out,
               first_tile, *, tm, chunk, d_inner, bc_width, conv_w):
    pad = SUBLANES
    vpc = chunk // pad
    n_chunks = tm // chunk
    width = MXU_COLS
    hn = _rmsnorm(x_ref[0], g_ref[...]).astype(BF16)
    for c0 in range(0, out["gm"].shape[1], width):
        cols = slice(c0, c0 + width)
        out["gm"][:, cols] = _sigmoid(_dotw(hn, wmb_ref, cols))
        yield
    hn = _dot(perm_ref[0], hn).astype(BF16)
    out["dt"][...] = jax.nn.softplus(_dotw(hn, wdt_ref) + dtb_ref[...])
    yield
    for c0 in range(0, d_inner, width):
        cols = slice(c0, c0 + width)
        out["zs"][:, cols] = _silu(_dotw(hn, wz_ref, cols))
        yield
    first_sublane = lax.broadcasted_iota(jnp.int32, (pad, width), 0) == 0
    for c0 in range(0, _ncols(wxbc_ref), width):
        cols = slice(c0, c0 + width)
        raw_all = _dotw(hn, wxbc_ref, cols)
        prev_tail = jnp.where(first_tile, 0.0, xs_scr[:, cols])
        acts = []
        for c in range(n_chunks):
            raw = raw_all[c * chunk:(c + 1) * chunk]
            conv = cb_ref[:, cols] + raw * cw_ref[conv_w - 1:conv_w, cols]
            for back in range(1, conv_w):
                head = []
                for v in range(back):
                    w = vpc - back + v
                    cur = pltpu.roll(raw[w * pad:(w + 1) * pad], 1, axis=0)
                    prv = pltpu.roll(prev_tail[(w - vpc + conv_w - 1) * pad:(w - vpc + conv_w) * pad], 1, axis=0)
                    head.append(jnp.where(first_sublane, prv, cur))
                shifted = jnp.concatenate(head + [raw[:chunk - back * pad]], axis=0)
                conv = conv + shifted * cw_ref[conv_w - 1 - back:conv_w - back, cols]
            acts.append(_silu(conv))
            prev_tail = raw[chunk - (conv_w - 1) * pad:]
        xs_scr[:, cols] = prev_tail
        out["tail"][:, cols] = prev_tail
        act = acts[0] if n_chunks == 1 else jnp.concatenate(acts, axis=0)
        if c0 < d_inner:
            out["xs"][:, cols] = act
        elif c0 < d_inner + bc_width:
            out["bm"][:, c0 - d_inner:c0 - d_inner + width] = act.astype(BF16)
        else:
            out["cm"][:, c0 - d_inner - bc_width:c0 - d_inner - bc_width + width] = act.astype(BF16)
        yield


def _bp_stage2(x_ref, min_ref, alog_ref, dsk_ref, ng_ref, wpb_ref, wout_ref, fg_ref, tri_ref, e3_ref, perm_ref,
               y_ref, st_scr, inp, first_tile, *, tm, chunk, n_groups, n_heads, head_dim, n_state, final_norm):
    pad = SUBLANES
    vpc = chunk // pad
    hpg = n_heads // n_groups
    gw = hpg * head_dim
    width = MXU_COLS

    def token_of(q):
        return (q % pad) * vpc + q // pad

    li = token_of(lax.broadcasted_iota(jnp.int32, (chunk, chunk), 0))
    si = token_of(lax.broadcasted_iota(jnp.int32, (chunk, chunk), 1))
    causal = li >= si
    head_of_lane = lax.broadcasted_iota(jnp.int32, (chunk, gw), 1) // head_dim
    tri = tri_ref[...]
    e3 = e3_ref[...]
    a_row = -jnp.exp(alog_ref[...])

    hb_rows = []
    for c in range(tm // chunk):
        rs = slice(c * chunk, (c + 1) * chunk)
        dt = inp["dt"][rs, :]
        acs = _cumsum_rows(tri, dt * a_row)
        acs_t = acs.T
        acs_e = _dot(_split_by_replica(acs, n_heads).astype(BF16), e3)
        a_last = acs_e[chunk - 1:chunk, :]
        xs_c = inp["xs"][rs, :]
        xdt_c = xs_c * _dot(_split_by_replica(dt, n_heads).astype(BF16), e3)
        xdt_b = xdt_c.astype(BF16)
        xd_state = (xdt_c * jnp.exp(a_last - acs_e)).astype(BF16)
        decay_in = jnp.exp(acs_e)
        st_prev = st_scr[...]
        if c == 0:
            st_prev = jnp.where(first_tile, 0.0, st_prev)
        st_prev_b = st_prev.astype(BF16)
        yield
        groups = range(n_groups)
        n_sl = [slice(gi * n_state, (gi + 1) * n_state) for gi in groups]
        g_sl = [slice(gi * gw, (gi + 1) * gw) for gi in groups]
        scores = [_dot_nt(inp["cm"][rs, n_sl[gi]], inp["bm"][rs, n_sl[gi]]) for gi in groups]
        yield
        y_offs = [_dot(inp["cm"][rs, n_sl[gi]], st_prev_b[:, g_sl[gi]]) for gi in groups]
        st_parts = [_dot_tn(inp["bm"][rs, n_sl[gi]], xd_state[:, g_sl[gi]]) for gi in groups]
        st_scr[...] = st_prev * jnp.exp(a_last) + jnp.concatenate(st_parts, axis=1)
        yield
        y_diags = []
        for gi in groups:
            m_heads, x_blocks = [], []
            xg = xdt_b[:, g_sl[gi]]
            for r in range(hpg):
                h = gi * hpg + r
                diff = acs[:, h:h + 1] - acs_t[h:h + 1, :]
                decay = jnp.exp(jnp.where(causal, diff, -jnp.inf))
                m_heads.append((scores[gi] * decay).astype(BF16))
                x_blocks.append(jnp.where(head_of_lane == r, xg, jnp.zeros_like(xg)))
            y_diags.append(_dot(jnp.concatenate(m_heads, axis=1), jnp.concatenate(x_blocks, axis=0)))
            yield
        hb_parts = []
        for gi in groups:
            gs = g_sl[gi]
            y = y_diags[gi] + y_offs[gi] * decay_in[:, gs] + xs_c[:, gs] * dsk_ref[:, gs]
            yf = y * inp["zs"][rs, gs]
            yn = yf * lax.rsqrt(jnp.mean(yf * yf, axis=-1, keepdims=True) + EPS)
            hb_parts.append((yn * ng_ref[:, gs]).astype(BF16))
            if gi % 2:
                yield
        hb_rows.append(jnp.concatenate(hb_parts, axis=1))
    hb = hb_rows[0] if len(hb_rows) == 1 else jnp.concatenate(hb_rows, axis=0)
    hb = _dot(perm_ref[1], hb).astype(BF16)
    yield
    m_parts = []
    for c0 in range(0, wpb_ref.shape[1], width):
        cols = slice(c0, c0 + width)
        m_parts.append((min_ref[0, :, cols] + inp["gm"][:, cols] * _dotw(hb, wpb_ref, cols)).astype(BF16))
        yield
    m = jnp.concatenate(m_parts, axis=1)
    out = x_ref[0] + _dotw(m, wout_ref)
    y_ref[0] = _rmsnorm(out, fg_ref[...]) if final_norm else out


def _bp_kernel(x1_ref, x2_ref, min_ref, g_ref, wz_ref, wxbc_ref, wdt_ref, wmb_ref, cw_ref, cb_ref, dtb_ref, alog_ref,
               dsk_ref, ng_ref, wpb_ref, wout_ref, fg_ref, tri_ref, e3_ref, perm_ref,
               y_ref, ssm_ref, conv_ref, xs_scr, st_scr, *slot_refs,
               nj, n_tiles, tm, chunk, n_groups, n_heads, head_dim, n_state, conv_w, final_norm):
    t = pl.program_id(0)
    d_inner = n_heads * head_dim
    pad = SUBLANES
    tile1 = jnp.minimum(t, n_tiles - 1)
    tile2 = jnp.maximum(t - 1, 0)
    n_h = len(HANDOFF)
    slots = [dict(zip(HANDOFF, slot_refs[i * n_h:(i + 1) * n_h])) for i in range(2)]

    @pl.when(t == 0)
    def _init():
        for ref in slots[1].values():
            ref[...] = jnp.zeros_like(ref)
        st_scr[...] = jnp.zeros_like(st_scr)
        xs_scr[...] = jnp.zeros_like(xs_scr)

    def step(slot_w, slot_r):
        s1 = _bp_stage1(x1_ref, g_ref, wz_ref, wxbc_ref, wdt_ref, wmb_ref, cw_ref, cb_ref, dtb_ref, perm_ref, xs_scr,
                        slot_w, tile1 % nj == 0, tm=tm, chunk=chunk, d_inner=d_inner,
                        bc_width=n_groups * n_state, conv_w=conv_w)
        s2 = _bp_stage2(x2_ref, min_ref, alog_ref, dsk_ref, ng_ref, wpb_ref, wout_ref, fg_ref, tri_ref, e3_ref,
                        perm_ref, y_ref, st_scr, slot_r, tile2 % nj == 0, tm=tm, chunk=chunk, n_groups=n_groups,
                        n_heads=n_heads, head_dim=head_dim, n_state=n_state, final_norm=final_norm)
        live = [s2, s1]
        while live:
            for gen in list(live):
                if next(gen, StopIteration) is StopIteration:
                    live.remove(gen)

        @pl.when((t > 0) & (tile2 % nj == nj - 1))
        def _seq_end():
            tail = slot_r["tail"][...]
            conv_ref[0] = jnp.concatenate([tail[(v + 1) * pad - 1:(v + 1) * pad] for v in range(conv_w - 1)], axis=0)
            ssm_ref[0] = st_scr[...].T

    for parity in range(2):
        pl.when(t % 2 == parity)(functools.partial(step, slots[parity], slots[1 - parity]))


def _branch_b_prompt(x3, m3, w, *, tm, final_norm):
    bsz, seq, d = x3.shape
    conv_dim = w["cw"].shape[1]
    d_inner = w["ng"].shape[1]
    n_heads = d_inner // SSD_HEAD_DIM
    conv_w = w["cw"].shape[0]
    nj = seq // tm
    n_tiles = bsz * nj
    bc_width = SSD_GROUPS * SSD_STATE
    kern = functools.partial(_bp_kernel, nj=nj, n_tiles=n_tiles, tm=tm, chunk=CHUNK, n_groups=SSD_GROUPS,
                             n_heads=n_heads, head_dim=SSD_HEAD_DIM, n_state=SSD_STATE, conv_w=conv_w,
                             final_norm=final_norm)

    def tile1(t):
        return jnp.minimum(t, n_tiles - 1)

    def tile2(t):
        return jnp.maximum(t - 1, 0)

    spec1 = pl.BlockSpec((1, tm, d), lambda t: (tile1(t) // nj, tile1(t) % nj, 0))
    spec2 = pl.BlockSpec((1, tm, d), lambda t: (tile2(t) // nj, tile2(t) % nj, 0))
    consts = [w[k] for k in ("g", "wz", "wxbc", "wdt", "wmb", "cw", "cb", "dtb", "alog", "dsk", "ng", "wpb", "wout",
                             "fg", "tri", "e3")]
    tok = (np.arange(tm) // CHUNK) * CHUNK + np.tile(_chunk_tokens(), tm // CHUNK)
    p = (tok[:, None] == np.arange(tm)[None, :]).astype(np.float32)
    perm = jnp.asarray(np.stack([p, p.T]), dtype=BF16)
    handoff = {"xs": ((tm, d_inner), F32), "bm": ((tm, bc_width), BF16), "cm": ((tm, bc_width), BF16),
               "zs": ((tm, d_inner), F32), "dt": ((tm, LANES), F32), "gm": ((tm, d), F32),
               "tail": (((conv_w - 1) * SUBLANES, conv_dim), F32)}
    return pl.pallas_call(
        _regroup(kern, 3, consts),
        out_shape=(jax.ShapeDtypeStruct((bsz, seq, d), F32),
                   jax.ShapeDtypeStruct((bsz, d_inner, SSD_STATE), F32),
                   jax.ShapeDtypeStruct((bsz, conv_w - 1, conv_dim), F32)),
        grid=(n_tiles + 1,),
        in_specs=[spec1, spec2, spec2] + _const_specs(consts + [perm]),
        out_specs=(spec2,
                   pl.BlockSpec((1, d_inner, SSD_STATE), lambda t: (tile2(t) // nj, 0, 0)),
                   pl.BlockSpec((1, conv_w - 1, conv_dim), lambda t: (tile2(t) // nj, 0, 0))),
        scratch_shapes=([pltpu.VMEM(((conv_w - 1) * SUBLANES, conv_dim), F32), pltpu.VMEM((SSD_STATE, d_inner), F32)]
                        + [pltpu.VMEM(*handoff[k]) for _ in range(2) for k in HANDOFF]),
        compiler_params=_params(1),
        name="branch_b_prompt",
    )(x3, x3, m3, *_arrays(consts), perm)


def _bs_kernel(x_ref, g_ref, wxbc_ref, wdt_ref, cw_ref, cb_ref, dtb_ref, alog_ref, dsk_ref,
               tri_ref, e3_ref, ecol_ref, conv0_ref, h0_ref,
               y_ref, hnew_ref, convnew_ref, xs_scr, tr_scr,
               *, nb, rows, n_groups, n_heads, head_dim, n_state, conv_w):
    hpg = n_heads // n_groups
    gw = hpg * head_dim
    d_inner = n_heads * head_dim
    d = x_ref.shape[-1]
    r_all = nb * rows
    pad = SUBLANES
    hs_w = n_heads * rows

    @pl.when(pl.program_id(0) == 0)
    def _init():
        tr_scr[...] = jnp.zeros_like(tr_scr)
        xs_scr[...] = jnp.zeros_like(xs_scr)

    x = x_ref[...].reshape(r_all, d)
    hn = _rmsnorm(x, g_ref[...]).astype(BF16)
    xbc_raw = _dotw(hn, wxbc_ref)
    dt = jax.nn.softplus(_dotw(hn, wdt_ref) + dtb_ref[...])

    raw = xbc_raw.reshape(nb, rows, xbc_raw.shape[1])
    xs_scr[:, rows - (conv_w - 1):, :] = conv0_ref[...]
    state = xs_scr[...]
    row_in_seq = lax.broadcasted_iota(jnp.int32, raw.shape, 1)
    conv = cb_ref[...] + raw * cw_ref[conv_w - 1:conv_w, :]
    for back in range(1, conv_w):
        shifted = jnp.where(row_in_seq < back, pltpu.roll(state, back, axis=1), pltpu.roll(raw, back, axis=1))
        conv = conv + shifted * cw_ref[conv_w - 1 - back:conv_w - back, :]
    convnew_ref[...] = raw[:, rows - (conv_w - 1):, :]
    xbc = _silu(conv.reshape(r_all, xbc_raw.shape[1]))
    xs = xbc[:, :d_inner]
    bm = xbc[:, d_inner:d_inner + n_groups * n_state]
    cm = xbc[:, d_inner + n_groups * n_state:]

    a_row = -jnp.exp(alog_ref[...])
    acs = _cumsum_rows(tri_ref[...], dt * a_row)
    e3 = e3_ref[...]
    acs_split = _split_by_replica(acs, n_heads).astype(BF16)
    acs_e = _dot(acs_split, e3)
    xdt = xs * _dot(_split_by_replica(dt, n_heads).astype(BF16), e3)
    decay_in = jnp.exp(acs_e)

    acs_col = _dot(acs_split, ecol_ref[...])
    l_idx = lax.broadcasted_iota(jnp.int32, (r_all, hs_w), 0) % rows
    s_idx = lax.broadcasted_iota(jnp.int32, (r_all, hs_w), 1) % rows
    on_diag = jnp.where(l_idx == s_idx, acs_col, 0.0).reshape(nb, rows, hs_w)
    acs_row = jnp.broadcast_to(jnp.sum(on_diag, axis=1, keepdims=True), (nb, rows, hs_w)).reshape(r_all, hs_w)
    decay = jnp.exp(jnp.where(l_idx >= s_idx, acs_col - acs_row, -jnp.inf))

    grp_of_row = lax.broadcasted_iota(jnp.int32, (hs_w, n_groups * n_state), 0) // (rows * hpg)
    grp_of_lane = lax.broadcasted_iota(jnp.int32, (hs_w, n_groups * n_state), 1) // n_state
    head_of_row = lax.broadcasted_iota(jnp.int32, (hs_w, d_inner), 0) // rows
    head_of_lane = lax.broadcasted_iota(jnp.int32, (hs_w, d_inner), 1) // head_dim
    slot = LANES // nb
    ones_rows = lax.broadcasted_iota(jnp.int32, (slot, n_state), 0)
    ones_blk = jnp.where((ones_rows >= rows) & (ones_rows < rows + 3), 1.0, 0.0)

    def seq_rows(b):
        return slice(b * rows, (b + 1) * rows)

    def expand_rows(v, same):
        rep = jnp.broadcast_to(v[None], (n_heads, rows, v.shape[1])).reshape(hs_w, v.shape[1])
        return jnp.where(same, rep, 0.0).astype(BF16)

    y_diags = []
    for b0 in range(0, nb, SEQ_BATCH):
        seqs = range(b0, min(b0 + SEQ_BATCH, nb))
        b_exps = [expand_rows(bm[seq_rows(b)], grp_of_row == grp_of_lane) for b in seqs]
        scores = [_dot_nt(cm[seq_rows(b)].astype(BF16), be) for b, be in zip(seqs, b_exps)]
        x_blks = [expand_rows(xdt[seq_rows(b)], head_of_row == head_of_lane) for b in seqs]
        m_alls = [(s * decay[seq_rows(b)]).astype(BF16) for b, s in zip(seqs, scores)]
        y_diags += [_dot(m, xb) for m, xb in zip(m_alls, x_blks)]
    for b in range(nb):
        a_last = acs_e[(b + 1) * rows - 1:(b + 1) * rows, :]
        d_hi, d_mid, d_lo = _bf16_parts(jnp.exp(a_last))
        tr_scr[b * slot:b * slot + rows, :] = xdt[seq_rows(b)] * jnp.exp(a_last - acs_e[seq_rows(b)])
        tr_scr[b * slot + rows:b * slot + rows + 1, :] = d_hi.astype(F32)
        tr_scr[b * slot + rows + 1:b * slot + rows + 2, :] = d_mid.astype(F32)
        tr_scr[b * slot + rows + 2:b * slot + rows + 3, :] = d_lo.astype(F32)
    tr_t = tr_scr[...].T.astype(BF16)

    ys = []
    for b in range(nb):
        rs = slice(b * rows, (b + 1) * rows)
        bm_b, cm_bb = bm[rs], cm[rs].astype(BF16)
        y_parts = []
        for gi in range(n_groups):
            ns = slice(gi * n_state, (gi + 1) * n_state)
            gs = slice(gi * gw, (gi + 1) * gw)
            h0_g = h0_ref[b, gs, :]
            y_parts.append(_dot_nt(cm_bb[:, ns], h0_g.astype(BF16)))
            w_seq = jnp.concatenate(
                [jnp.concatenate([bm_b[:, ns], jnp.zeros((slot - rows, n_state), F32)], axis=0), ones_blk], axis=1)
            pieces = ([jnp.zeros((b * slot, 2 * n_state), F32)] if b else []) + [w_seq]
            if b < nb - 1:
                pieces.append(jnp.zeros((LANES - (b + 1) * slot, 2 * n_state), F32))
            upd = _dot(tr_t[gs, :], jnp.concatenate(pieces, axis=0).astype(BF16))
            hnew_ref[b, gs, :] = h0_g * upd[:, n_state:] + upd[:, :n_state]
        ys.append(y_diags[b] + jnp.concatenate(y_parts, axis=1) * decay_in[rs])
    y = jnp.concatenate(ys, axis=0) + xs * dsk_ref[...]
    y_ref[...] = y.reshape(nb, rows, d_inner)


def _bfin_kernel(x_ref, y_ref, min_ref, g_ref, wz_ref, wmb_ref, ng_ref, wpb_ref, wout_ref, fg_ref, out_ref,
                 *, n_groups, final_norm):
    out_ref[...] = _finish_b(x_ref[...], y_ref[...], g_ref, wz_ref, wmb_ref, ng_ref, wpb_ref, min_ref[...],
                             wout_ref, fg_ref, n_groups=n_groups, final_norm=final_norm)


def _branch_b_sample(x3, m3, conv0, h0, w, *, layer, nb, tm, final_norm):
    bsz, rows, d = x3.shape
    conv_dim = w["cw"].shape[1]
    d_inner = w["ng"].shape[1]
    n_heads = d_inner // SSD_HEAD_DIM
    conv_w = w["cw"].shape[0]
    kern = functools.partial(_bs_kernel, nb=nb, rows=rows, n_groups=SSD_GROUPS, n_heads=n_heads,
                             head_dim=SSD_HEAD_DIM, n_state=SSD_STATE, conv_w=conv_w)

    def seq_spec(width):
        return pl.BlockSpec((nb, rows, width), lambda i: (i, 0, 0))

    conv_spec = pl.BlockSpec((nb, conv_w - 1, conv_dim), lambda i: (i, 0, 0))
    h_spec = pl.BlockSpec((nb, d_inner, SSD_STATE), lambda i: (i, 0, 0))
    consts = [w[k] for k in ("g", "wxbc", "wdt", "cw", "cb", "dtb", "alog", "dsk", "tri_s", "e3", "ecol")]
    y, h_new, conv_new = pl.pallas_call(
        _regroup(kern, 1, consts),
        out_shape=(jax.ShapeDtypeStruct((bsz, rows, d_inner), F32),
                   jax.ShapeDtypeStruct((bsz, d_inner, SSD_STATE), F32),
                   jax.ShapeDtypeStruct((bsz, conv_w - 1, conv_dim), F32)),
        grid=(bsz // nb,),
        in_specs=[seq_spec(d)] + _const_specs(consts) + [
            pl.BlockSpec((None, nb, conv_w - 1, conv_dim), lambda i: (layer, i, 0, 0)), h_spec],
        out_specs=(seq_spec(d_inner), h_spec, conv_spec),
        scratch_shapes=[pltpu.VMEM((nb, SUBLANES, conv_dim), F32), pltpu.VMEM((LANES, d_inner), F32)],
        compiler_params=_params(1),
        name="branch_b_sample",
    )(x3, *_arrays(consts), conv0, h0)

    n_rows = bsz * rows
    fin_consts = [w[k] for k in ("g", "wz", "wmb", "ng", "wpb", "wout", "fg")]

    def rows_spec(width):
        return pl.BlockSpec((tm, width), lambda i: (i, 0))

    out = pl.pallas_call(
        _regroup(functools.partial(_bfin_kernel, n_groups=SSD_GROUPS, final_norm=final_norm), 3, fin_consts),
        out_shape=jax.ShapeDtypeStruct((n_rows, d), F32),
        grid=(n_rows // tm,),
        in_specs=[rows_spec(d), rows_spec(d_inner), rows_spec(d)] + _const_specs(fin_consts),
        out_specs=rows_spec(d),
        compiler_params=_params(1),
        name="branch_b_sample_out",
    )(x3.reshape(n_rows, d), y.reshape(n_rows, d_inner), m3.reshape(n_rows, d), *_arrays(fin_consts))
    return out.reshape(bsz, rows, d), h_new, conv_new


def _tile(n, pref):
    return pref if n % pref == 0 else n


def _chunk_tokens():
    q = np.arange(CHUNK)
    return (q % SUBLANES) * (CHUNK // SUBLANES) + q // SUBLANES


def _head_expand(n_heads, width, n_rep):
    j = np.arange(LANES)[:, None]
    c = np.arange(n_heads * width)[None, :]
    return jnp.asarray(((j < n_rep * n_heads) & (j % n_heads == c // width)).astype(np.float32), dtype=BF16)


def _layer_weights(l, d, dec_rows, nb_s, norm_g, w_in, conv_w, conv_b, dt_bias, a_log, d_skip, ssd_norm_g, ln_v_g,
                   ln_v_b, w_spatial, b_spatial, w_proj_a, w_proj_b, w_proj_x, w_out, final_norm_g):
    d_a = d
    d_inner = w_proj_b.shape[1]
    conv_dim = conv_w.shape[2]
    n_heads = a_log.shape[1]
    d_x = w_proj_x.shape[1]
    sizes = (d_a, d_a, d_a, d_inner, conv_dim, n_heads, d_x, d_x, N_BRANCH * d)
    offs = np.concatenate([[0], np.cumsum(sizes)])
    wl = w_in[l]
    sec = lambda i: wl[:, offs[i]:offs[i + 1]]
    merge = sec(8)
    row = lambda v: v.reshape(1, -1).astype(F32)
    rep3 = lambda v, fill: jnp.concatenate(
        [v] * DT_REPLICAS + [jnp.full(v.shape[:-1] + (LANES - DT_REPLICAS * v.shape[-1],), fill, v.dtype)], axis=-1)

    tril = jnp.tril(jnp.ones((CHUNK, CHUNK), F32))
    ws_p = jnp.where(tril[None] > 0, w_spatial[l], 0.0)
    bs_p = jnp.repeat(b_spatial[l].T, d_a // A_GROUPS, axis=1)
    n_seq = CHUNK // dec_rows
    eye = np.eye(n_seq, dtype=np.float32)
    ws_s = (eye[None, :, None, :, None] * ws_p[:, None, :dec_rows, None, :dec_rows]).reshape(A_GROUPS, CHUNK, CHUNK)
    bs_s = jnp.tile(bs_p[:dec_rows], (n_seq, 1))

    w = {
        "g": row(norm_g[l]),
        "lng": row(ln_v_g[l]), "lnb": row(ln_v_b[l]),
        "ws_p": ws_p.astype(BF16), "bs_p": bs_p, "ws_s": ws_s.astype(BF16), "bs_s": bs_s,
        "wpa": _pack_w(w_proj_a[l]),
        "cw": conv_w[l].astype(F32), "cb": row(conv_b[l]),
        "dtb": rep3(row(dt_bias[l]), 0.0), "alog": rep3(row(a_log[l]), 0.0),
        "dsk": jnp.repeat(row(d_skip[l]), SSD_HEAD_DIM, axis=1),
        "ng": row(ssd_norm_g[l]),
        "wpb": _pack_w(w_proj_b[l]),
        "wout": _pack_w(w_out[l]), "fg": row(final_norm_g),
        "tri": jnp.asarray(_chunk_tokens()[:, None] >= _chunk_tokens()[None, :], dtype=BF16),
        "tri_s": jnp.asarray(np.kron(np.eye(nb_s), np.tril(np.ones((dec_rows, dec_rows)))), dtype=BF16),
        "e3": _head_expand(n_heads, SSD_HEAD_DIM, DT_REPLICAS),
        "ecol": _head_expand(n_heads, dec_rows, DT_REPLICAS),
        "wpx": _pack_w(w_proj_x[l]),
    }
    w_bf = _pack_w(wl)
    tail = w_bf[:, offs[6]:]
    assert all(o % d == 0 for o in offs[:5]) and d_x % LANES == 0 and d % d_x == 0
    for name, i in (("wu", 0), ("wv", 1), ("wga", 2), ("wz", 3), ("wxbc", 4)):
        windows = [_Cols(w_bf, c, d) for c in range(offs[i], offs[i + 1], d)]
        w[name] = windows if len(windows) > 1 else windows[0]
    w["wdt"] = _pack_w(rep3(sec(5), 0.0))
    w["wq"], w["wgx"] = _Cols(tail, 0, d_x), _Cols(tail, d_x, d_x)
    for k, name in enumerate(("wma", "wmb", "wmx")):
        w[name] = _Cols(tail, 2 * d_x + k * d, d)
    return w


def _layer(x3, k3, v3, w, *, nb_x, rows_x, heads_in_rows, tm_a, ws, bs, want_v, b_fn):
    bsz, seq, d = x3.shape
    m = _branch_x(x3, w["g"], w["wq"], w["wgx"], w["wmx"], k3, v3, w["wpx"], nb=nb_x, rows=rows_x,
                  heads_in_rows=heads_in_rows)
    m, vn = _branch_a(x3.reshape(bsz * seq, d), w["g"], w["wu"], w["wv"], w["wga"], w["wma"], w["lng"], w["lnb"],
                      ws, bs, w["wpa"], m.reshape(bsz * seq, d), tm=tm_a, want_v=want_v)
    return b_fn(x3, m.reshape(bsz, seq, d)), vn


def kernel(x_prompt, x_sample, mem_prompt, cache_mem_k, cache_mem_v, state_ssm, state_conv, norm_g, w_in, conv_w,
           conv_b, dt_bias, a_log, d_skip, ssd_norm_g, ln_v_g, ln_v_b, w_spatial, b_spatial, mem_norm_g, w_mem_kv,
           w_proj_a, w_proj_b, w_proj_x, w_out, final_norm_g):
    depth = w_in.shape[0]
    bsz, seq, d = x_prompt.shape
    dec_b, dec_rows, _ = x_sample.shape
    n_mem = mem_prompt.shape[1]
    d_x = w_proj_x.shape[1]
    d_inner = w_proj_b.shape[1]
    n_heads = a_log.shape[1]
    assert seq % CHUNK == 0 and CHUNK % dec_rows == 0 and dec_rows == SUBLANES
    assert DT_REPLICAS * n_heads <= LANES and n_heads * dec_rows % LANES == 0

    nb_s = _tile(dec_b, 8)
    nb_xs = _tile(dec_b, 8)
    yp, ys = x_prompt, x_sample
    outs = {k: [] for k in ("mk", "mv", "hp", "cp", "hs", "cs", "vs")}
    for l in range(depth):
        w = _layer_weights(l, d, dec_rows, nb_s, norm_g, w_in, conv_w, conv_b, dt_bias, a_log, d_skip, ssd_norm_g,
                           ln_v_g, ln_v_b, w_spatial, b_spatial, w_proj_a, w_proj_b, w_proj_x, w_out, final_norm_g)
        final_norm = l == depth - 1
        mk, mv = _mem_kv(mem_prompt.reshape(bsz * n_mem, d), mem_norm_g[l].reshape(1, d).astype(F32),
                         _pack_w(w_mem_kv[l]), _tile(bsz * n_mem, 512))
        (yp, hp, cp), _ = _layer(
            yp, mk.reshape(bsz, n_mem * (d_x // X_HEAD_DIM), X_HEAD_DIM),
            mv.reshape(bsz, n_mem * (d_x // X_HEAD_DIM), X_HEAD_DIM), w,
            nb_x=1, rows_x=_tile(seq, 512), heads_in_rows=False, tm_a=_tile(bsz * seq, 512), ws=w["ws_p"], bs=w["bs_p"], want_v=False,
            b_fn=functools.partial(_branch_b_prompt, w=w, tm=_tile(seq, 256), final_norm=final_norm))
        (ys, hs, cs), vs = _layer(
            ys, cache_mem_k[l].reshape(dec_b, n_mem * (d_x // X_HEAD_DIM), X_HEAD_DIM),
            cache_mem_v[l].reshape(dec_b, n_mem * (d_x // X_HEAD_DIM), X_HEAD_DIM), w,
            nb_x=nb_xs, rows_x=dec_rows, heads_in_rows=True, tm_a=_tile(dec_b * dec_rows, 512), ws=w["ws_s"], bs=w["bs_s"], want_v=True,
            b_fn=functools.partial(_branch_b_sample, conv0=state_conv, layer=l,
                                   h0=state_ssm[l].reshape(dec_b, d_inner, SSD_STATE), w=w, nb=nb_s,
                                   tm=_tile(dec_b * dec_rows, 512), final_norm=final_norm))
        outs["mk"].append(mk.reshape(bsz, n_mem, d_x // X_HEAD_DIM, X_HEAD_DIM))
        outs["mv"].append(mv.reshape(bsz, n_mem, d_x // X_HEAD_DIM, X_HEAD_DIM))
        outs["hp"].append(hp.reshape(bsz, n_heads, SSD_HEAD_DIM, SSD_STATE))
        outs["cp"].append(cp)
        outs["hs"].append(hs.reshape(dec_b, n_heads, SSD_HEAD_DIM, SSD_STATE))
        outs["cs"].append(cs)
        outs["vs"].append(vs.reshape(dec_b, dec_rows, d))
    st = lambda k: jnp.stack(outs[k])
    return (yp, ys, st("mk"), st("mv"), st("hp"), st("cp"), st("hs"), st("cs"), st("vs"))
```

```python
import functools
import math
from typing import NamedTuple

import jax
import jax.numpy as jnp
import numpy as np
from jax import lax
from jax.experimental import pallas as pl
from jax.experimental.pallas import tpu as pltpu

F32 = jnp.float32
BF16 = jnp.bfloat16
EPS = 1e-6
SQRT_HALF = math.sqrt(0.5)

LANES = 128
SUBLANES = 8
VMEM_LIMIT_BYTES = 56 * 1024 * 1024

CHUNK = 128
A_GROUPS = 8
SSD_HEAD_DIM = 64
SSD_GROUPS = 8
SSD_STATE = 128
X_HEAD_DIM = 128
N_BRANCH = 3
DT_REPLICAS = 3
SEQ_BATCH = 4

def _dot(a, b):
    return jnp.dot(a, b, preferred_element_type=F32)


def _dot_nt(a, b):
    return lax.dot_general(a, b, (((1,), (1,)), ((), ())), preferred_element_type=F32)


def _dot_tn(a, b):
    return lax.dot_general(a, b, (((0,), (0,)), ((), ())), preferred_element_type=F32)


MXU_COLS = 256


def _pack_w(w):
    return w.astype(BF16)


def _ncols(w):
    return sum(r.shape[1] for r in w) if isinstance(w, tuple) else w.shape[1]


def _dotw(a, w, cols=None):
    if not isinstance(w, tuple):
        return _dot(a, w[...] if cols is None else w[:, cols])
    width = w[0].shape[1]
    lo, hi = (0, width * len(w)) if cols is None else (cols.start, cols.stop)
    parts = []
    while lo < hi:
        k, off = divmod(lo, width)
        n = min(width - off, hi - lo)
        parts.append(_dot(a, w[k][:, off:off + n]))
        lo += n
    return parts[0] if len(parts) == 1 else jnp.concatenate(parts, axis=1)


def _dot_cols(a, w_ref, fn, width=MXU_COLS):
    n = w_ref.shape[1]
    width = min(width, n)
    blocks = [fn(_dotw(a, w_ref, slice(c, c + width))) for c in range(0, n, width)]
    return blocks[0] if len(blocks) == 1 else jnp.concatenate(blocks, axis=1)


def _rmsnorm(x, g):
    return x * lax.rsqrt(jnp.mean(x * x, axis=-1, keepdims=True) + EPS) * g


def _gelu(x):
    return 0.5 * x * (1.0 + lax.erf(x * SQRT_HALF))


def _sigmoid(x):
    return 0.5 * jnp.tanh(0.5 * x) + 0.5


def _silu(x):
    return x * _sigmoid(x)


def _bf16_parts(x):
    hi = x.astype(BF16)
    r1 = x - hi.astype(F32)
    mid = r1.astype(BF16)
    lo = (r1 - mid.astype(F32)).astype(BF16)
    return hi, mid, lo


def _split_by_replica(x, n_heads):
    hi, mid, lo = _bf16_parts(x)
    lane = lax.broadcasted_iota(jnp.int32, x.shape, 1)
    return jnp.where(lane < n_heads, hi, jnp.where(lane < 2 * n_heads, mid, lo))


def _cumsum_rows(tri, x):
    hi, mid, lo = _bf16_parts(x)
    return _dot(tri, hi) + _dot(tri, mid) + _dot(tri, lo)


def _memkv_kernel(x_ref, g_ref, w_ref, k_ref, v_ref, *, n_heads):
    tm = x_ref.shape[0]
    hd = k_ref.shape[-1]
    hn = _rmsnorm(x_ref[...], g_ref[...]).astype(BF16)
    kv = _dotw(hn, w_ref)
    for h in range(n_heads):
        k_ref[pl.ds(h, tm, stride=n_heads), :] = kv[:, h * hd:(h + 1) * hd]
        v_ref[pl.ds(h, tm, stride=n_heads), :] = kv[:, (n_heads + h) * hd:(n_heads + h + 1) * hd]


class _Cols(NamedTuple):
    mat: jax.Array
    start: int
    width: int

    @property
    def shape(self):
        return (self.mat.shape[0], self.width)


def _const_spec(op=None):
    if isinstance(op, _Cols):
        idx = op.start // op.width
        return pl.BlockSpec(op.shape, lambda *_: (0, idx), pipeline_mode=pl.Buffered(1))
    return pl.BlockSpec(memory_space=pltpu.VMEM)


def _flatten(ops):
    return [o for op in ops for o in (op if isinstance(op, list) else [op])]


def _const_specs(ops):
    return [_const_spec(op) for op in _flatten(ops)]


def _arrays(ops):
    return [op.mat if isinstance(op, _Cols) else op for op in _flatten(ops)]


def _regroup(kernel_fn, n_lead, ops):
    sizes = [len(op) if isinstance(op, list) else 0 for op in ops]

    def wrapped(*refs):
        refs = list(refs)
        args, pos = refs[:n_lead], n_lead
        for n in sizes:
            args.append(tuple(refs[pos:pos + n]) if n else refs[pos])
            pos += max(n, 1)
        return kernel_fn(*args, *refs[pos:])

    return wrapped


def _params(n_grid):
    return pltpu.CompilerParams(dimension_semantics=("arbitrary",) * n_grid, vmem_limit_bytes=VMEM_LIMIT_BYTES)


def _mem_kv(mem2d, g, w, tm):
    rows, d = mem2d.shape
    n_heads = w.shape[1] // 2 // X_HEAD_DIM
    out = jax.ShapeDtypeStruct((rows * n_heads, X_HEAD_DIM), F32)
    out_spec = pl.BlockSpec((tm * n_heads, X_HEAD_DIM), lambda i: (i, 0))
    return pl.pallas_call(
        functools.partial(_memkv_kernel, n_heads=n_heads),
        out_shape=(out, out),
        grid=(rows // tm,),
        in_specs=[pl.BlockSpec((tm, d), lambda i: (i, 0)), _const_spec(), _const_spec()],
        out_specs=(out_spec, out_spec),
        compiler_params=_params(1),
        name="mem_kv",
    )(mem2d, g, w)


def _x_kernel(x_ref, g_ref, wq_ref, wgx_ref, wmx_ref, k_ref, v_ref, wpx_ref, m_ref, *, nb, rows, head_dim):
    d = x_ref.shape[-1]
    x = x_ref[...].reshape(nb * rows, d)
    hn = _rmsnorm(x, g_ref[...]).astype(BF16)
    q = _dotw(hn, wq_ref)
    gate = _silu(_dotw(hn, wgx_ref))
    gm = _sigmoid(_dotw(hn, wmx_ref))
    n_heads = q.shape[1] // head_dim
    scale = head_dim ** -0.5
    outs = []
    n_mem = k_ref.shape[1] // n_heads
    for b in range(nb):
        qb = q[b * rows:(b + 1) * rows].astype(BF16)
        heads = range(n_heads)
        kb = [k_ref[b, pl.ds(h, n_mem, stride=n_heads), :].astype(BF16) for h in heads]
        vb = [v_ref[b, pl.ds(h, n_mem, stride=n_heads), :].astype(BF16) for h in heads]
        scores = [_dot_nt(qb[:, h * head_dim:(h + 1) * head_dim], kb[h]) * scale for h in heads]
        exps = [jnp.exp(s - jnp.max(s, axis=-1, keepdims=True)) for s in scores]
        probs = [(e / jnp.sum(e, axis=-1, keepdims=True)).astype(BF16) for e in exps]
        outs.append(jnp.concatenate([_dot(probs[h], vb[h]) for h in heads], axis=-1))
    o = outs[0] if nb == 1 else jnp.concatenate(outs, axis=0)
    hx = (o * gate).astype(BF16)
    m = gm * _dotw(hx, wpx_ref)
    m_ref[...] = m.reshape(nb, rows, d)


def _xs_kernel(x_ref, g_ref, wq_ref, wgx_ref, wmx_ref, k_ref, v_ref, wpx_ref, m_ref, *, nb, rows, head_dim):
    d = x_ref.shape[-1]
    x = x_ref[...].reshape(nb * rows, d)
    hn = _rmsnorm(x, g_ref[...]).astype(BF16)
    q = _dotw(hn, wq_ref)
    gate = _silu(_dotw(hn, wgx_ref))
    gm = _sigmoid(_dotw(hn, wmx_ref))
    n_heads = q.shape[1] // head_dim
    n_kv = k_ref.shape[1]
    scale = head_dim ** -0.5
    row_head = lax.broadcasted_iota(jnp.int32, (n_heads * rows, n_kv), 0) // rows
    col_head = lax.broadcasted_iota(jnp.int32, (n_heads * rows, n_kv), 1) % n_heads
    same_head = row_head == col_head
    seqs = range(nb)
    scores = []
    for b in seqs:
        qb = q[b * rows:(b + 1) * rows]
        q_heads = jnp.concatenate([qb[:, h * head_dim:(h + 1) * head_dim] for h in range(n_heads)], axis=0)
        scores.append(_dot_nt(q_heads.astype(BF16), k_ref[b].astype(BF16)))
    masked = [jnp.where(same_head, s * scale, -jnp.inf) for s in scores]
    exps = [jnp.exp(s - jnp.max(s, axis=-1, keepdims=True)) for s in masked]
    probs = [(e / jnp.sum(e, axis=-1, keepdims=True)).astype(BF16) for e in exps]
    o_heads = [_dot(probs[b], v_ref[b].astype(BF16)) for b in seqs]
    outs = [jnp.concatenate([oh[h * rows:(h + 1) * rows] for h in range(n_heads)], axis=1) for oh in o_heads]
    o = outs[0] if nb == 1 else jnp.concatenate(outs, axis=0)
    hx = (o * gate).astype(BF16)
    m = gm * _dotw(hx, wpx_ref)
    m_ref[...] = m.reshape(nb, rows, d)


def _branch_x(x3, g, wq, wgx, wmx, k3, v3, wpx, *, nb, rows, heads_in_rows):
    bsz, seq, d = x3.shape
    n_mem, dx = k3.shape[1], k3.shape[2]
    kern = functools.partial(_xs_kernel if heads_in_rows else _x_kernel, nb=nb, rows=rows, head_dim=X_HEAD_DIM)
    return pl.pallas_call(
        kern,
        out_shape=jax.ShapeDtypeStruct((bsz, seq, d), F32),
        grid=(bsz // nb, seq // rows),
        in_specs=[pl.BlockSpec((nb, rows, d), lambda i, j: (i, j, 0))] + _const_specs([g, wq, wgx, wmx]) + [
            pl.BlockSpec((nb, n_mem, dx), lambda i, j: (i, 0, 0)),
            pl.BlockSpec((nb, n_mem, dx), lambda i, j: (i, 0, 0)),
            _const_spec(wpx),
        ],
        out_specs=pl.BlockSpec((nb, rows, d), lambda i, j: (i, j, 0)),
        compiler_params=_params(2),
        name="branch_x",
    )(x3, *_arrays([g, wq, wgx, wmx]), k3, v3, wpx)


def _a_kernel(x_ref, g_ref, wu_ref, wv_ref, wga_ref, wma_ref, lng_ref, lnb_ref, ws_ref, bs_ref, wpa_ref, min_ref,
              mout_ref, *maybe_v_ref, chunk, groups, sub):
    tm, d = x_ref.shape
    gd = d // groups
    bias = bs_ref[...]
    for t in range(tm // sub):
        ts = slice(t * sub, (t + 1) * sub)
        hn = _rmsnorm(x_ref[ts, :], g_ref[...]).astype(BF16)
        v = _dot_cols(hn, wv_ref, _gelu)
        u = _dot_cols(hn, wu_ref, _gelu)
        vc = v - jnp.mean(v, axis=-1, keepdims=True)
        vn = vc * lax.rsqrt(jnp.mean(vc * vc, axis=-1, keepdims=True) + EPS) * lng_ref[...] + lnb_ref[...]
        if maybe_v_ref:
            maybe_v_ref[0][ts, :] = vn
        ga = _dot_cols(hn, wga_ref, _silu)
        vb = vn.astype(BF16)
        n_c = sub // chunk
        wide = [_dot(ws_ref[gi], jnp.concatenate([vb[c * chunk:(c + 1) * chunk, gi * gd:(gi + 1) * gd]
                                                  for c in range(n_c)], axis=1)) for gi in range(groups)]
        mixed = [jnp.concatenate([wg[:, c * gd:(c + 1) * gd] for wg in wide], axis=1) + bias for c in range(n_c)]
        s = mixed[0] if n_c == 1 else jnp.concatenate(mixed, axis=0)
        ha = (u * s * ga).astype(BF16)
        gm = _sigmoid(_dotw(hn, wma_ref))
        mout_ref[ts, :] = min_ref[ts, :] + gm * _dotw(ha, wpa_ref)


def _branch_a(x2, g, wu, wv, wga, wma, lng, lnb, ws, bs, wpa, m_in, *, tm, want_v):
    rows, d = x2.shape
    row_spec = pl.BlockSpec((tm, d), lambda i: (i, 0))
    out_shape = [jax.ShapeDtypeStruct((rows, d), F32)]
    out_specs = [row_spec]
    if want_v:
        out_shape.append(jax.ShapeDtypeStruct((rows, d), F32))
        out_specs.append(row_spec)
    kern = functools.partial(_a_kernel, chunk=CHUNK, groups=A_GROUPS, sub=min(tm, 256))
    consts = [g, wu, wv, wga, wma, lng, lnb, ws, bs, wpa]
    res = pl.pallas_call(
        kern,
        out_shape=tuple(out_shape),
        grid=(rows // tm,),
        in_specs=[row_spec] + _const_specs(consts) + [row_spec],
        out_specs=tuple(out_specs),
        compiler_params=_params(1),
        name="branch_a",
    )(x2, *_arrays(consts), m_in)
    return res if want_v else (res[0], None)


def _finish_b(x, y, g_ref, wz_ref, wmb_ref, ng_ref, wpb_ref, m_in, wout_ref, fg_ref, *, n_groups, final_norm):
    hn = _rmsnorm(x, g_ref[...]).astype(BF16)
    gm = _sigmoid(_dotw(hn, wmb_ref))
    yf = y * _silu(_dotw(hn, wz_ref))
    gw = yf.shape[1] // n_groups
    parts = []
    for gi in range(n_groups):
        yg = yf[:, gi * gw:(gi + 1) * gw]
        parts.append(yg * lax.rsqrt(jnp.mean(yg * yg, axis=-1, keepdims=True) + EPS))
    hb = (jnp.concatenate(parts, axis=-1) * ng_ref[...]).astype(BF16)
    m = m_in + gm * _dotw(hb, wpb_ref)
    out = x + _dotw(m.astype(BF16), wout_ref)
    return _rmsnorm(out, fg_ref[...]) if final_norm else out


HANDOFF = ("xs", "bm", "cm", "zs", "dt", "gm", "tail")


def _bp_stage1(x_ref, g_ref, wz_ref, wxbc_ref, wdt_ref, wmb_ref, cw_ref, cb_ref, dtb_ref, perm_ref, xs_scr, out,
               first_tile, *, tm, chunk, d_inner, bc_width, conv_w):
    pad = SUBLANES
    vpc = chunk // pad
    n_chunks = tm // chunk
    width = MXU_COLS
    hn = _rmsnorm(x_ref[0], g_ref[...]).astype(BF16)
    for c0 in range(0, out["gm"].shape[1], width):
        cols = slice(c0, c0 + width)
        out["gm"][:, cols] = _sigmoid(_dotw(hn, wmb_ref, cols))
        yield
    hn = _dot(perm_ref[0], hn).astype(BF16)
    out["dt"][...] = jax.nn.softplus(_dotw(hn, wdt_ref) + dtb_ref[...])
    yield
    for c0 in range(0, d_inner, width):
        cols = slice(c0, c0 + width)
        out["zs"][:, cols] = _silu(_dotw(hn, wz_ref, cols))
        yield
    first_sublane = lax.broadcasted_iota(jnp.int32, (pad, width), 0) == 0
    for c0 in range(0, _ncols(wxbc_ref), width):
        cols = slice(c0, c0 + width)
        raw_all = _dotw(hn, wxbc_ref, cols)
        prev_tail = jnp.where(first_tile, 0.0, xs_scr[:, cols])
        acts = []
        for c in range(n_chunks):
            raw = raw_all[c * chunk:(c + 1) * chunk]
            conv = cb_ref[:, cols] + raw * cw_ref[conv_w - 1:conv_w, cols]
            for back in range(1, conv_w):
                head = []
                for v in range(back):
                    w = vpc - back + v
                    cur = pltpu.roll(raw[w * pad:(w + 1) * pad], 1, axis=0)
                    prv = pltpu.roll(prev_tail[(w - vpc + conv_w - 1) * pad:(w - vpc + conv_w) * pad], 1, axis=0)
                    head.append(jnp.where(first_sublane, prv, cur))
                shifted = jnp.concatenate(head + [raw[:chunk - back * pad]], axis=0)
                conv = conv + shifted * cw_ref[conv_w - 1 - back:conv_w - back, cols]
            acts.append(_silu(conv))
            prev_tail = raw[chunk - (conv_w - 1) * pad:]
        xs_scr[:, cols] = prev_tail
        out["tail"][:, cols] = prev_tail
        act = acts[0] if n_chunks == 1 else jnp.concatenate(acts, axis=0)
        if c0 < d_inner:
            out["xs"][:, cols] = act
        elif c0 < d_inner + bc_width:
            out["bm"][:, c0 - d_inner:c0 - d_inner + width] = act.astype(BF16)
        else:
            out["cm"][:, c0 - d_inner - bc_width:c0 - d_inner - bc_width + width] = act.astype(BF16)
        yield


def _bp_stage2(x_ref, min_ref, alog_ref, dsk_ref, ng_ref, wpb_ref, wout_ref, fg_ref, tri_ref, e3_ref, perm_ref,
               y_ref, st_scr, inp, first_tile, *, tm, chunk, n_groups, n_heads, head_dim, n_state, final_norm):
    pad = SUBLANES
    vpc = chunk // pad
    hpg = n_heads // n_groups
    gw = hpg * head_dim
    width = MXU_COLS

    def token_of(q):
        return (q % pad) * vpc + q // pad

    li = token_of(lax.broadcasted_iota(jnp.int32, (chunk, chunk), 0))
    si = token_of(lax.broadcasted_iota(jnp.int32, (chunk, chunk), 1))
    causal = li >= si
    head_of_lane = lax.broadcasted_iota(jnp.int32, (chunk, gw), 1) // head_dim
    tri = tri_ref[...]
    e3 = e3_ref[...]
    a_row = -jnp.exp(alog_ref[...])

    hb_rows = []
    for c in range(tm // chunk):
        rs = slice(c * chunk, (c + 1) * chunk)
        dt = inp["dt"][rs, :]
        acs = _cumsum_rows(tri, dt * a_row)
        acs_t = acs.T
        acs_e = _dot(_split_by_replica(acs, n_heads).astype(BF16), e3)
        a_last = acs_e[chunk - 1:chunk, :]
        xs_c = inp["xs"][rs, :]
        xdt_c = xs_c * _dot(_split_by_replica(dt, n_heads).astype(BF16), e3)
        xdt_b = xdt_c.astype(BF16)
        xd_state = (xdt_c * jnp.exp(a_last - acs_e)).astype(BF16)
        decay_in = jnp.exp(acs_e)
        st_prev = st_scr[...]
        if c == 0:
            st_prev = jnp.where(first_tile, 0.0, st_prev)
        st_prev_b = st_prev.astype(BF16)
        yield
        groups = range(n_groups)
        n_sl = [slice(gi * n_state, (gi + 1) * n_state) for gi in groups]
        g_sl = [slice(gi * gw, (gi + 1) * gw) for gi in groups]
        scores = [_dot_nt(inp["cm"][rs, n_sl[gi]], inp["bm"][rs, n_sl[gi]]) for gi in groups]
        yield
        y_offs = [_dot(inp["cm"][rs, n_sl[gi]], st_prev_b[:, g_sl[gi]]) for gi in groups]
        st_parts = [_dot_tn(inp["bm"][rs, n_sl[gi]], xd_state[:, g_sl[gi]]) for gi in groups]
        st_scr[...] = st_prev * jnp.exp(a_last) + jnp.concatenate(st_parts, axis=1)
        yield
        y_diags = []
        for gi in groups:
            m_heads, x_blocks = [], []
            xg = xdt_b[:, g_sl[gi]]
            for r in range(hpg):
                h = gi * hpg + r
                diff = acs[:, h:h + 1] - acs_t[h:h + 1, :]
                decay = jnp.exp(jnp.where(causal, diff, -jnp.inf))
                m_heads.append((scores[gi] * decay).astype(BF16))
                x_blocks.append(jnp.where(head_of_lane == r, xg, jnp.zeros_like(xg)))
            y_diags.append(_dot(jnp.concatenate(m_heads, axis=1), jnp.concatenate(x_blocks, axis=0)))
            yield
        hb_parts = []
        for gi in groups:
            gs = g_sl[gi]
            y = y_diags[gi] + y_offs[gi] * decay_in[:, gs] + xs_c[:, gs] * dsk_ref[:, gs]
            yf = y * inp["zs"][rs, gs]
            yn = yf * lax.rsqrt(jnp.mean(yf * yf, axis=-1, keepdims=True) + EPS)
            hb_parts.append((yn * ng_ref[:, gs]).astype(BF16))
            if gi % 2:
                yield
        hb_rows.append(jnp.concatenate(hb_parts, axis=1))
    hb = hb_rows[0] if len(hb_rows) == 1 else jnp.concatenate(hb_rows, axis=0)
    hb = _dot(perm_ref[1], hb).astype(BF16)
    yield
    m_parts = []
    for c0 in range(0, wpb_ref.shape[1], width):
        cols = slice(c0, c0 + width)
        m_parts.append((min_ref[0, :, cols] + inp["gm"][:, cols] * _dotw(hb, wpb_ref, cols)).astype(BF16))
        yield
    m = jnp.concatenate(m_parts, axis=1)
    out = x_ref[0] + _dotw(m, wout_ref)
    y_ref[0] = _rmsnorm(out, fg_ref[...]) if final_norm else out


def _bp_kernel(x1_ref, x2_ref, min_ref, g_ref, wz_ref, wxbc_ref, wdt_ref, wmb_ref, cw_ref, cb_ref, dtb_ref, alog_ref,
               dsk_ref, ng_ref, wpb_ref, wout_ref, fg_ref, tri_ref, e3_ref, perm_ref,
               y_ref, ssm_ref, conv_ref, xs_scr, st_scr, *slot_refs,
               nj, n_tiles, tm, chunk, n_groups, n_heads, head_dim, n_state, conv_w, final_norm):
    t = pl.program_id(0)
    d_inner = n_heads * head_dim
    pad = SUBLANES
    tile1 = jnp.minimum(t, n_tiles - 1)
    tile2 = jnp.maximum(t - 1, 0)
    n_h = len(HANDOFF)
    slots = [dict(zip(HANDOFF, slot_refs[i * n_h:(i + 1) * n_h])) for i in range(2)]

    @pl.when(t == 0)
    def _init():
        for ref in slots[1].values():
            ref[...] = jnp.zeros_like(ref)
        st_scr[...] = jnp.zeros_like(st_scr)
        xs_scr[...] = jnp.zeros_like(xs_scr)

    def step(slot_w, slot_r):
        s1 = _bp_stage1(x1_ref, g_ref, wz_ref, wxbc_ref, wdt_ref, wmb_ref, cw_ref, cb_ref, dtb_ref, perm_ref, xs_scr,
                        slot_w, tile1 % nj == 0, tm=tm, chunk=chunk, d_inner=d_inner,
                        bc_width=n_groups * n_state, conv_w=conv_w)
        s2 = _bp_stage2(x2_ref, min_ref, alog_ref, dsk_ref, ng_ref, wpb_ref, wout_ref, fg_ref, tri_ref, e3_ref,
                        perm_ref, y_ref, st_scr, slot_r, tile2 % nj == 0, tm=tm, chunk=chunk, n_groups=n_groups,
                        n_heads=n_heads, head_dim=head_dim, n_state=n_state, final_norm=final_norm)
        live = [s2, s1]
        while live:
            for gen in list(live):
                if next(gen, StopIteration) is StopIteration:
                    live.remove(gen)

        @pl.when((t > 0) & (tile2 % nj == nj - 1))
        def _seq_end():
            tail = slot_r["tail"][...]
            conv_ref[0] = jnp.concatenate([tail[(v + 1) * pad - 1:(v + 1) * pad] for v in range(conv_w - 1)], axis=0)
            ssm_ref[0] = st_scr[...].T

    for parity in range(2):
        pl.when(t % 2 == parity)(functools.partial(step, slots[parity], slots[1 - parity]))


def _branch_b_prompt(x3, m3, w, *, tm, final_norm):
    bsz, seq, d = x3.shape
    conv_dim = w["cw"].shape[1]
    d_inner = w["ng"].shape[1]
    n_heads = d_inner // SSD_HEAD_DIM
    conv_w = w["cw"].shape[0]
    nj = seq // tm
    n_tiles = bsz * nj
    bc_width = SSD_GROUPS * SSD_STATE
    kern = functools.partial(_bp_kernel, nj=nj, n_tiles=n_tiles, tm=tm, chunk=CHUNK, n_groups=SSD_GROUPS,
                             n_heads=n_heads, head_dim=SSD_HEAD_DIM, n_state=SSD_STATE, conv_w=conv_w,
                             final_norm=final_norm)

    def tile1(t):
        return jnp.minimum(t, n_tiles - 1)

    def tile2(t):
        return jnp.maximum(t - 1, 0)

    spec1 = pl.BlockSpec((1, tm, d), lambda t: (tile1(t) // nj, tile1(t) % nj, 0))
    spec2 = pl.BlockSpec((1, tm, d), lambda t: (tile2(t) // nj, tile2(t) % nj, 0))
    consts = [w[k] for k in ("g", "wz", "wxbc", "wdt", "wmb", "cw", "cb", "dtb", "alog", "dsk", "ng", "wpb", "wout",
                             "fg", "tri", "e3")]
    tok = (np.arange(tm) // CHUNK) * CHUNK + np.tile(_chunk_tokens(), tm // CHUNK)
    p = (tok[:, None] == np.arange(tm)[None, :]).astype(np.float32)
    perm = jnp.asarray(np.stack([p, p.T]), dtype=BF16)
    handoff = {"xs": ((tm, d_inner), F32), "bm": ((tm, bc_width), BF16), "cm": ((tm, bc_width), BF16),
               "zs": ((tm, d_inner), F32), "dt": ((tm, LANES), F32), "gm": ((tm, d), F32),
               "tail": (((conv_w - 1) * SUBLANES, conv_dim), F32)}
    return pl.pallas_call(
        _regroup(kern, 3, consts),
        out_shape=(jax.ShapeDtypeStruct((bsz, seq, d), F32),
                   jax.ShapeDtypeStruct((bsz, d_inner, SSD_STATE), F32),
                   jax.ShapeDtypeStruct((bsz, conv_w - 1, conv_dim), F32)),
        grid=(n_tiles + 1,),
        in_specs=[spec1, spec2, spec2] + _const_specs(consts + [perm]),
        out_specs=(spec2,
                   pl.BlockSpec((1, d_inner, SSD_STATE), lambda t: (tile2(t) // nj, 0, 0)),
                   pl.BlockSpec((1, conv_w - 1, conv_dim), lambda t: (tile2(t) // nj, 0, 0))),
        scratch_shapes=([pltpu.VMEM(((conv_w - 1) * SUBLANES, conv_dim), F32), pltpu.VMEM((SSD_STATE, d_inner), F32)]
                        + [pltpu.VMEM(*handoff[k]) for _ in range(2) for k in HANDOFF]),
        compiler_params=_params(1),
        name="branch_b_prompt",
    )(x3, x3, m3, *_arrays(consts), perm)


def _bs_kernel(x_ref, g_ref, wxbc_ref, wdt_ref, cw_ref, cb_ref, dtb_ref, alog_ref, dsk_ref,
               tri_ref, e3_ref, ecol_ref, conv0_ref, h0_ref,
               y_ref, hnew_ref, convnew_ref, xs_scr, tr_scr,
               *, nb, rows, n_groups, n_heads, head_dim, n_state, conv_w):
    hpg = n_heads // n_groups
    gw = hpg * head_dim
    d_inner = n_heads * head_dim
    d = x_ref.shape[-1]
    r_all = nb * rows
    pad = SUBLANES
    hs_w = n_heads * rows

    @pl.when(pl.program_id(0) == 0)
    def _init():
        tr_scr[...] = jnp.zeros_like(tr_scr)
        xs_scr[...] = jnp.zeros_like(xs_scr)

    x = x_ref[...].reshape(r_all, d)
    hn = _rmsnorm(x, g_ref[...]).astype(BF16)
    xbc_raw = _dotw(hn, wxbc_ref)
    dt = jax.nn.softplus(_dotw(hn, wdt_ref) + dtb_ref[...])

    raw = xbc_raw.reshape(nb, rows, xbc_raw.shape[1])
    xs_scr[:, rows - (conv_w - 1):, :] = conv0_ref[...]
    state = xs_scr[...]
    row_in_seq = lax.broadcasted_iota(jnp.int32, raw.shape, 1)
    conv = cb_ref[...] + raw * cw_ref[conv_w - 1:conv_w, :]
    for back in range(1, conv_w):
        shifted = jnp.where(row_in_seq < back, pltpu.roll(state, back, axis=1), pltpu.roll(raw, back, axis=1))
        conv = conv + shifted * cw_ref[conv_w - 1 - back:conv_w - back, :]
    convnew_ref[...] = raw[:, rows - (conv_w - 1):, :]
    xbc = _silu(conv.reshape(r_all, xbc_raw.shape[1]))
    xs = xbc[:, :d_inner]
    bm = xbc[:, d_inner:d_inner + n_groups * n_state]
    cm = xbc[:, d_inner + n_groups * n_state:]

    a_row = -jnp.exp(alog_ref[...])
    acs = _cumsum_rows(tri_ref[...], dt * a_row)
    e3 = e3_ref[...]
    acs_split = _split_by_replica(acs, n_heads).astype(BF16)
    acs_e = _dot(acs_split, e3)
    xdt = xs * _dot(_split_by_replica(dt, n_heads).astype(BF16), e3)
    decay_in = jnp.exp(acs_e)

    acs_col = _dot(acs_split, ecol_ref[...])
    l_idx = lax.broadcasted_iota(jnp.int32, (r_all, hs_w), 0) % rows
    s_idx = lax.broadcasted_iota(jnp.int32, (r_all, hs_w), 1) % rows
    on_diag = jnp.where(l_idx == s_idx, acs_col, 0.0).reshape(nb, rows, hs_w)
    acs_row = jnp.broadcast_to(jnp.sum(on_diag, axis=1, keepdims=True), (nb, rows, hs_w)).reshape(r_all, hs_w)
    decay = jnp.exp(jnp.where(l_idx >= s_idx, acs_col - acs_row, -jnp.inf))

    slot = LANES // nb
    ones_rows = lax.broadcasted_iota(jnp.int32, (slot, n_state), 0)
    ones_blk = jnp.where((ones_rows >= rows) & (ones_rows < rows + 3), 1.0, 0.0)

    def seq_rows(b):
        return slice(b * rows, (b + 1) * rows)

    lane_in_tile = lax.broadcasted_iota(jnp.int32, (rows, LANES), 1)

    def expand_rows(v, start_of, width):
        c = v.shape[1]
        blocks = []
        for h in range(n_heads):
            start = start_of(h)
            lo = start // LANES * LANES
            hi = max(lo + LANES, start + width)
            piece = v[:, lo:hi]
            if width < LANES:
                piece = jnp.where(lane_in_tile // width == (start - lo) // width, piece, 0.0)
            parts = [jnp.zeros((rows, lo), F32)] if lo else []
            parts.append(piece)
            if hi < c:
                parts.append(jnp.zeros((rows, c - hi), F32))
            blocks.append(jnp.concatenate(parts, axis=1) if len(parts) > 1 else piece)
        return jnp.concatenate(blocks, axis=0).astype(BF16)

    y_diags = []
    for b0 in range(0, nb, SEQ_BATCH):
        seqs = range(b0, min(b0 + SEQ_BATCH, nb))
        b_exps = [expand_rows(bm[seq_rows(b)], lambda h: h // hpg * n_state, n_state) for b in seqs]
        scores = [_dot_nt(cm[seq_rows(b)].astype(BF16), be) for b, be in zip(seqs, b_exps)]
        x_blks = [expand_rows(xdt[seq_rows(b)], lambda h: h * head_dim, head_dim) for b in seqs]
        m_alls = [(s * decay[seq_rows(b)]).astype(BF16) for b, s in zip(seqs, scores)]
        y_diags += [_dot(m, xb) for m, xb in zip(m_alls, x_blks)]
    for b in range(nb):
        a_last = acs_e[(b + 1) * rows - 1:(b + 1) * rows, :]
        d_hi, d_mid, d_lo = _bf16_parts(jnp.exp(a_last))
        tr_scr[b * slot:b * slot + rows, :] = xdt[seq_rows(b)] * jnp.exp(a_last - acs_e[seq_rows(b)])
        tr_scr[b * slot + rows:b * slot + rows + 1, :] = d_hi.astype(F32)
        tr_scr[b * slot + rows + 1:b * slot + rows + 2, :] = d_mid.astype(F32)
        tr_scr[b * slot + rows + 2:b * slot + rows + 3, :] = d_lo.astype(F32)
    tr_t = tr_scr[...].T.astype(BF16)

    ys = []
    for b in range(nb):
        rs = slice(b * rows, (b + 1) * rows)
        bm_b, cm_bb = bm[rs], cm[rs].astype(BF16)
        y_parts = []
        for gi in range(n_groups):
            ns = slice(gi * n_state, (gi + 1) * n_state)
            gs = slice(gi * gw, (gi + 1) * gw)
            h0_g = h0_ref[b, gs, :]
            y_parts.append(_dot_nt(cm_bb[:, ns], h0_g.astype(BF16)))
            w_seq = jnp.concatenate(
                [jnp.concatenate([bm_b[:, ns], jnp.zeros((slot - rows, n_state), F32)], axis=0), ones_blk], axis=1)
            pieces = ([jnp.zeros((b * slot, 2 * n_state), F32)] if b else []) + [w_seq]
            if b < nb - 1:
                pieces.append(jnp.zeros((LANES - (b + 1) * slot, 2 * n_state), F32))
            upd = _dot(tr_t[gs, :], jnp.concatenate(pieces, axis=0).astype(BF16))
            hnew_ref[b, gs, :] = h0_g * upd[:, n_state:] + upd[:, :n_state]
        ys.append(y_diags[b] + jnp.concatenate(y_parts, axis=1) * decay_in[rs])
    y = jnp.concatenate(ys, axis=0) + xs * dsk_ref[...]
    y_ref[...] = y.reshape(nb, rows, d_inner)


def _bfin_kernel(x_ref, y_ref, min_ref, g_ref, wz_ref, wmb_ref, ng_ref, wpb_ref, wout_ref, fg_ref, out_ref,
                 *, n_groups, final_norm):
    out_ref[...] = _finish_b(x_ref[...], y_ref[...], g_ref, wz_ref, wmb_ref, ng_ref, wpb_ref, min_ref[...],
                             wout_ref, fg_ref, n_groups=n_groups, final_norm=final_norm)


def _branch_b_sample(x3, m3, conv0, h0, w, *, layer, nb, tm, final_norm):
    bsz, rows, d = x3.shape
    conv_dim = w["cw"].shape[1]
    d_inner = w["ng"].shape[1]
    n_heads = d_inner // SSD_HEAD_DIM
    conv_w = w["cw"].shape[0]
    kern = functools.partial(_bs_kernel, nb=nb, rows=rows, n_groups=SSD_GROUPS, n_heads=n_heads,
                             head_dim=SSD_HEAD_DIM, n_state=SSD_STATE, conv_w=conv_w)

    def seq_spec(width):
        return pl.BlockSpec((nb, rows, width), lambda i: (i, 0, 0))

    conv_spec = pl.BlockSpec((nb, conv_w - 1, conv_dim), lambda i: (i, 0, 0))
    h_spec = pl.BlockSpec((nb, d_inner, SSD_STATE), lambda i: (i, 0, 0))
    consts = [w[k] for k in ("g", "wxbc", "wdt", "cw", "cb", "dtb", "alog", "dsk", "tri_s", "e3", "ecol")]
    y, h_new, conv_new = pl.pallas_call(
        _regroup(kern, 1, consts),
        out_shape=(jax.ShapeDtypeStruct((bsz, rows, d_inner), F32),
                   jax.ShapeDtypeStruct((bsz, d_inner, SSD_STATE), F32),
                   jax.ShapeDtypeStruct((bsz, conv_w - 1, conv_dim), F32)),
        grid=(bsz // nb,),
        in_specs=[seq_spec(d)] + _const_specs(consts) + [
            pl.BlockSpec((None, nb, conv_w - 1, conv_dim), lambda i: (layer, i, 0, 0)), h_spec],
        out_specs=(seq_spec(d_inner), h_spec, conv_spec),
        scratch_shapes=[pltpu.VMEM((nb, SUBLANES, conv_dim), F32), pltpu.VMEM((LANES, d_inner), F32)],
        compiler_params=_params(1),
        name="branch_b_sample",
    )(x3, *_arrays(consts), conv0, h0)

    n_rows = bsz * rows
    fin_consts = [w[k] for k in ("g", "wz", "wmb", "ng", "wpb", "wout", "fg")]

    def rows_spec(width):
        return pl.BlockSpec((tm, width), lambda i: (i, 0))

    out = pl.pallas_call(
        _regroup(functools.partial(_bfin_kernel, n_groups=SSD_GROUPS, final_norm=final_norm), 3, fin_consts),
        out_shape=jax.ShapeDtypeStruct((n_rows, d), F32),
        grid=(n_rows // tm,),
        in_specs=[rows_spec(d), rows_spec(d_inner), rows_spec(d)] + _const_specs(fin_consts),
        out_specs=rows_spec(d),
        compiler_params=_params(1),
        name="branch_b_sample_out",
    )(x3.reshape(n_rows, d), y.reshape(n_rows, d_inner), m3.reshape(n_rows, d), *_arrays(fin_consts))
    return out.reshape(bsz, rows, d), h_new, conv_new


def _tile(n, pref):
    return pref if n % pref == 0 else n


def _chunk_tokens():
    q = np.arange(CHUNK)
    return (q % SUBLANES) * (CHUNK // SUBLANES) + q // SUBLANES


def _head_expand(n_heads, width, n_rep):
    j = np.arange(LANES)[:, None]
    c = np.arange(n_heads * width)[None, :]
    return jnp.asarray(((j < n_rep * n_heads) & (j % n_heads == c // width)).astype(np.float32), dtype=BF16)


def _layer_weights(l, d, dec_rows, nb_s, norm_g, w_in, conv_w, conv_b, dt_bias, a_log, d_skip, ssd_norm_g, ln_v_g,
                   ln_v_b, w_spatial, b_spatial, w_proj_a, w_proj_b, w_proj_x, w_out, final_norm_g):
    d_a = d
    d_inner = w_proj_b.shape[1]
    conv_dim = conv_w.shape[2]
    n_heads = a_log.shape[1]
    d_x = w_proj_x.shape[1]
    sizes = (d_a, d_a, d_a, d_inner, conv_dim, n_heads, d_x, d_x, N_BRANCH * d)
    offs = np.concatenate([[0], np.cumsum(sizes)])
    wl = w_in[l]
    sec = lambda i: wl[:, offs[i]:offs[i + 1]]
    merge = sec(8)
    row = lambda v: v.reshape(1, -1).astype(F32)
    rep3 = lambda v, fill: jnp.concatenate(
        [v] * DT_REPLICAS + [jnp.full(v.shape[:-1] + (LANES - DT_REPLICAS * v.shape[-1],), fill, v.dtype)], axis=-1)

    tril = jnp.tril(jnp.ones((CHUNK, CHUNK), F32))
    ws_p = jnp.where(tril[None] > 0, w_spatial[l], 0.0)
    bs_p = jnp.repeat(b_spatial[l].T, d_a // A_GROUPS, axis=1)
    n_seq = CHUNK // dec_rows
    eye = np.eye(n_seq, dtype=np.float32)
    ws_s = (eye[None, :, None, :, None] * ws_p[:, None, :dec_rows, None, :dec_rows]).reshape(A_GROUPS, CHUNK, CHUNK)
    bs_s = jnp.tile(bs_p[:dec_rows], (n_seq, 1))

    w = {
        "g": row(norm_g[l]),
        "lng": row(ln_v_g[l]), "lnb": row(ln_v_b[l]),
        "ws_p": ws_p.astype(BF16), "bs_p": bs_p, "ws_s": ws_s.astype(BF16), "bs_s": bs_s,
        "wpa": _pack_w(w_proj_a[l]),
        "cw": conv_w[l].astype(F32), "cb": row(conv_b[l]),
        "dtb": rep3(row(dt_bias[l]), 0.0), "alog": rep3(row(a_log[l]), 0.0),
        "dsk": jnp.repeat(row(d_skip[l]), SSD_HEAD_DIM, axis=1),
        "ng": row(ssd_norm_g[l]),
        "wpb": _pack_w(w_proj_b[l]),
        "wout": _pack_w(w_out[l]), "fg": row(final_norm_g),
        "tri": jnp.asarray(_chunk_tokens()[:, None] >= _chunk_tokens()[None, :], dtype=BF16),
        "tri_s": jnp.asarray(np.kron(np.eye(nb_s), np.tril(np.ones((dec_rows, dec_rows)))), dtype=BF16),
        "e3": _head_expand(n_heads, SSD_HEAD_DIM, DT_REPLICAS),
        "ecol": _head_expand(n_heads, dec_rows, DT_REPLICAS),
        "wpx": _pack_w(w_proj_x[l]),
    }
    w_bf = _pack_w(wl)
    tail = w_bf[:, offs[6]:]
    assert all(o % d == 0 for o in offs[:5]) and d_x % LANES == 0 and d % d_x == 0
    for name, i in (("wu", 0), ("wv", 1), ("wga", 2), ("wz", 3), ("wxbc", 4)):
        windows = [_Cols(w_bf, c, d) for c in range(offs[i], offs[i + 1], d)]
        w[name] = windows if len(windows) > 1 else windows[0]
    w["wdt"] = _pack_w(rep3(sec(5), 0.0))
    w["wq"], w["wgx"] = _Cols(tail, 0, d_x), _Cols(tail, d_x, d_x)
    for k, name in enumerate(("wma", "wmb", "wmx")):
        w[name] = _Cols(tail, 2 * d_x + k * d, d)
    return w


def _layer(x3, k3, v3, w, *, nb_x, rows_x, heads_in_rows, tm_a, ws, bs, want_v, b_fn):
    bsz, seq, d = x3.shape
    m = _branch_x(x3, w["g"], w["wq"], w["wgx"], w["wmx"], k3, v3, w["wpx"], nb=nb_x, rows=rows_x,
                  heads_in_rows=heads_in_rows)
    m, vn = _branch_a(x3.reshape(bsz * seq, d), w["g"], w["wu"], w["wv"], w["wga"], w["wma"], w["lng"], w["lnb"],
                      ws, bs, w["wpa"], m.reshape(bsz * seq, d), tm=tm_a, want_v=want_v)
    return b_fn(x3, m.reshape(bsz, seq, d)), vn


def kernel(x_prompt, x_sample, mem_prompt, cache_mem_k, cache_mem_v, state_ssm, state_conv, norm_g, w_in, conv_w,
           conv_b, dt_bias, a_log, d_skip, ssd_norm_g, ln_v_g, ln_v_b, w_spatial, b_spatial, mem_norm_g, w_mem_kv,
           w_proj_a, w_proj_b, w_proj_x, w_out, final_norm_g):
    depth = w_in.shape[0]
    bsz, seq, d = x_prompt.shape
    dec_b, dec_rows, _ = x_sample.shape
    n_mem = mem_prompt.shape[1]
    d_x = w_proj_x.shape[1]
    d_inner = w_proj_b.shape[1]
    n_heads = a_log.shape[1]
    assert seq % CHUNK == 0 and CHUNK % dec_rows == 0 and dec_rows == SUBLANES
    assert DT_REPLICAS * n_heads <= LANES and n_heads * dec_rows % LANES == 0

    nb_s = _tile(dec_b, 8)
    nb_xs = _tile(dec_b, 8)
    yp, ys = x_prompt, x_sample
    outs = {k: [] for k in ("mk", "mv", "hp", "cp", "hs", "cs", "vs")}
    for l in range(depth):
        w = _layer_weights(l, d, dec_rows, nb_s, norm_g, w_in, conv_w, conv_b, dt_bias, a_log, d_skip, ssd_norm_g,
                           ln_v_g, ln_v_b, w_spatial, b_spatial, w_proj_a, w_proj_b, w_proj_x, w_out, final_norm_g)
        final_norm = l == depth - 1
        mk, mv = _mem_kv(mem_prompt.reshape(bsz * n_mem, d), mem_norm_g[l].reshape(1, d).astype(F32),
                         _pack_w(w_mem_kv[l]), _tile(bsz * n_mem, 512))
        (yp, hp, cp), _ = _layer(
            yp, mk.reshape(bsz, n_mem * (d_x // X_HEAD_DIM), X_HEAD_DIM),
            mv.reshape(bsz, n_mem * (d_x // X_HEAD_DIM), X_HEAD_DIM), w,
            nb_x=1, rows_x=_tile(seq, 512), heads_in_rows=False, tm_a=_tile(bsz * seq, 512), ws=w["ws_p"], bs=w["bs_p"], want_v=False,
            b_fn=functools.partial(_branch_b_prompt, w=w, tm=_tile(seq, 256), final_norm=final_norm))
        (ys, hs, cs), vs = _layer(
            ys, cache_mem_k[l].reshape(dec_b, n_mem * (d_x // X_HEAD_DIM), X_HEAD_DIM),
            cache_mem_v[l].reshape(dec_b, n_mem * (d_x // X_HEAD_DIM), X_HEAD_DIM), w,
            nb_x=nb_xs, rows_x=dec_rows, heads_in_rows=True, tm_a=_tile(dec_b * dec_rows, 512), ws=w["ws_s"], bs=w["bs_s"], want_v=True,
            b_fn=functools.partial(_branch_b_sample, conv0=state_conv, layer=l,
                                   h0=state_ssm[l].reshape(dec_b, d_inner, SSD_STATE), w=w, nb=nb_s,
                                   tm=_tile(dec_b * dec_rows, 512), final_norm=final_norm))
        outs["mk"].append(mk.reshape(bsz, n_mem, d_x // X_HEAD_DIM, X_HEAD_DIM))
        outs["mv"].append(mv.reshape(bsz, n_mem, d_x // X_HEAD_DIM, X_HEAD_DIM))
        outs["hp"].append(hp.reshape(bsz, n_heads, SSD_HEAD_DIM, SSD_STATE))
        outs["cp"].append(cp)
        outs["hs"].append(hs.reshape(dec_b, n_heads, SSD_HEAD_DIM, SSD_STATE))
        outs["cs"].append(cs)
        outs["vs"].append(vs.reshape(dec_b, dec_rows, d))
    st = lambda k: jnp.stack(outs[k])
    return (yp, ys, st("mk"), st("mv"), st("hp"), st("cp"), st("hs"), st("cs"), st("vs"))
```

```python
import functools
import math
from typing import NamedTuple

import jax
import jax.numpy as jnp
import numpy as np
from jax import lax
from jax.experimental import pallas as pl
from jax.experimental.pallas import tpu as pltpu

F32 = jnp.float32
BF16 = jnp.bfloat16
EPS = 1e-6
SQRT_HALF = math.sqrt(0.5)

LANES = 128
SUBLANES = 8
VMEM_LIMIT_BYTES = 56 * 1024 * 1024

CHUNK = 128
A_GROUPS = 8
SSD_HEAD_DIM = 64
SSD_GROUPS = 8
SSD_STATE = 128
X_HEAD_DIM = 128
N_BRANCH = 3
DT_REPLICAS = 3
SEQ_BATCH = 4

def _dot(a, b):
    return jnp.dot(a, b, preferred_element_type=F32)


def _dot_nt(a, b):
    return lax.dot_general(a, b, (((1,), (1,)), ((), ())), preferred_element_type=F32)


def _dot_tn(a, b):
    return lax.dot_general(a, b, (((0,), (0,)), ((), ())), preferred_element_type=F32)


MXU_COLS = 256


def _pack_w(w):
    return w.astype(BF16)


def _ncols(w):
    return sum(r.shape[1] for r in w) if isinstance(w, tuple) else w.shape[1]


def _dotw(a, w, cols=None):
    if not isinstance(w, tuple):
        return _dot(a, w[...] if cols is None else w[:, cols])
    width = w[0].shape[1]
    lo, hi = (0, width * len(w)) if cols is None else (cols.start, cols.stop)
    parts = []
    while lo < hi:
        k, off = divmod(lo, width)
        n = min(width - off, hi - lo)
        parts.append(_dot(a, w[k][:, off:off + n]))
        lo += n
    return parts[0] if len(parts) == 1 else jnp.concatenate(parts, axis=1)


def _dot_cols(a, w_ref, fn, width=MXU_COLS):
    n = w_ref.shape[1]
    width = min(width, n)
    blocks = [fn(_dotw(a, w_ref, slice(c, c + width))) for c in range(0, n, width)]
    return blocks[0] if len(blocks) == 1 else jnp.concatenate(blocks, axis=1)


def _rmsnorm(x, g):
    return x * lax.rsqrt(jnp.mean(x * x, axis=-1, keepdims=True) + EPS) * g


def _gelu(x):
    return 0.5 * x * (1.0 + lax.erf(x * SQRT_HALF))


def _sigmoid(x):
    return 0.5 * jnp.tanh(0.5 * x) + 0.5


def _silu(x):
    return x * _sigmoid(x)


def _bf16_parts(x):
    hi = x.astype(BF16)
    r1 = x - hi.astype(F32)
    mid = r1.astype(BF16)
    lo = (r1 - mid.astype(F32)).astype(BF16)
    return hi, mid, lo


def _split_by_replica(x, n_heads):
    hi, mid, lo = _bf16_parts(x)
    lane = lax.broadcasted_iota(jnp.int32, x.shape, 1)
    return jnp.where(lane < n_heads, hi, jnp.where(lane < 2 * n_heads, mid, lo))


def _cumsum_rows(tri, x):
    hi, mid, lo = _bf16_parts(x)
    return _dot(tri, hi) + _dot(tri, mid) + _dot(tri, lo)


def _memkv_kernel(x_ref, g_ref, w_ref, k_ref, v_ref, *, n_heads):
    tm = x_ref.shape[0]
    hd = k_ref.shape[-1]
    hn = _rmsnorm(x_ref[...], g_ref[...]).astype(BF16)
    kv = _dotw(hn, w_ref)
    for h in range(n_heads):
        k_ref[pl.ds(h, tm, stride=n_heads), :] = kv[:, h * hd:(h + 1) * hd]
        v_ref[pl.ds(h, tm, stride=n_heads), :] = kv[:, (n_heads + h) * hd:(n_heads + h + 1) * hd]


class _Cols(NamedTuple):
    mat: jax.Array
    start: int
    width: int

    @property
    def shape(self):
        return (self.mat.shape[0], self.width)


def _const_spec(op=None):
    if isinstance(op, _Cols):
        idx = op.start // op.width
        return pl.BlockSpec(op.shape, lambda *_: (0, idx), pipeline_mode=pl.Buffered(1))
    return pl.BlockSpec(memory_space=pltpu.VMEM)


def _flatten(ops):
    return [o for op in ops for o in (op if isinstance(op, list) else [op])]


def _const_specs(ops):
    return [_const_spec(op) for op in _flatten(ops)]


def _arrays(ops):
    return [op.mat if isinstance(op, _Cols) else op for op in _flatten(ops)]


def _regroup(kernel_fn, n_lead, ops):
    sizes = [len(op) if isinstance(op, list) else 0 for op in ops]

    def wrapped(*refs):
        refs = list(refs)
        args, pos = refs[:n_lead], n_lead
        for n in sizes:
            args.append(tuple(refs[pos:pos + n]) if n else refs[pos])
            pos += max(n, 1)
        return kernel_fn(*args, *refs[pos:])

    return wrapped


def _params(n_grid):
    return pltpu.CompilerParams(dimension_semantics=("arbitrary",) * n_grid, vmem_limit_bytes=VMEM_LIMIT_BYTES)


def _mem_kv(mem2d, g, w, tm):
    rows, d = mem2d.shape
    n_heads = w.shape[1] // 2 // X_HEAD_DIM
    out = jax.ShapeDtypeStruct((rows * n_heads, X_HEAD_DIM), F32)
    out_spec = pl.BlockSpec((tm * n_heads, X_HEAD_DIM), lambda i: (i, 0))
    return pl.pallas_call(
        functools.partial(_memkv_kernel, n_heads=n_heads),
        out_shape=(out, out),
        grid=(rows // tm,),
        in_specs=[pl.BlockSpec((tm, d), lambda i: (i, 0)), _const_spec(), _const_spec()],
        out_specs=(out_spec, out_spec),
        compiler_params=_params(1),
        name="mem_kv",
    )(mem2d, g, w)


def _x_kernel(x_ref, g_ref, wq_ref, wgx_ref, wmx_ref, k_ref, v_ref, wpx_ref, m_ref, *, nb, rows, head_dim):
    d = x_ref.shape[-1]
    x = x_ref[...].reshape(nb * rows, d)
    hn = _rmsnorm(x, g_ref[...]).astype(BF16)
    q = _dotw(hn, wq_ref)
    gate = _silu(_dotw(hn, wgx_ref))
    gm = _sigmoid(_dotw(hn, wmx_ref))
    n_heads = q.shape[1] // head_dim
    scale = head_dim ** -0.5
    outs = []
    n_mem = k_ref.shape[1] // n_heads
    for b in range(nb):
        qb = q[b * rows:(b + 1) * rows].astype(BF16)
        heads = range(n_heads)
        kb = [k_ref[b, pl.ds(h, n_mem, stride=n_heads), :].astype(BF16) for h in heads]
        vb = [v_ref[b, pl.ds(h, n_mem, stride=n_heads), :].astype(BF16) for h in heads]
        scores = [_dot_nt(qb[:, h * head_dim:(h + 1) * head_dim], kb[h]) * scale for h in heads]
        exps = [jnp.exp(s - jnp.max(s, axis=-1, keepdims=True)) for s in scores]
        probs = [(e / jnp.sum(e, axis=-1, keepdims=True)).astype(BF16) for e in exps]
        outs.append(jnp.concatenate([_dot(probs[h], vb[h]) for h in heads], axis=-1))
    o = outs[0] if nb == 1 else jnp.concatenate(outs, axis=0)
    hx = (o * gate).astype(BF16)
    m = gm * _dotw(hx, wpx_ref)
    m_ref[...] = m.reshape(nb, rows, d)


def _xs_kernel(x_ref, g_ref, wq_ref, wgx_ref, wmx_ref, k_ref, v_ref, wpx_ref, m_ref, *, nb, rows, head_dim):
    d = x_ref.shape[-1]
    x = x_ref[...].reshape(nb * rows, d)
    hn = _rmsnorm(x, g_ref[...]).astype(BF16)
    q = _dotw(hn, wq_ref)
    gate = _silu(_dotw(hn, wgx_ref))
    gm = _sigmoid(_dotw(hn, wmx_ref))
    n_heads = q.shape[1] // head_dim
    n_kv = k_ref.shape[1]
    scale = head_dim ** -0.5
    row_head = lax.broadcasted_iota(jnp.int32, (n_heads * rows, n_kv), 0) // rows
    col_head = lax.broadcasted_iota(jnp.int32, (n_heads * rows, n_kv), 1) % n_heads
    same_head = row_head == col_head
    seqs = range(nb)
    scores = []
    for b in seqs:
        qb = q[b * rows:(b + 1) * rows]
        q_heads = jnp.concatenate([qb[:, h * head_dim:(h + 1) * head_dim] for h in range(n_heads)], axis=0)
        scores.append(_dot_nt(q_heads.astype(BF16), k_ref[b].astype(BF16)))
    masked = [jnp.where(same_head, s * scale, -jnp.inf) for s in scores]
    exps = [jnp.exp(s - jnp.max(s, axis=-1, keepdims=True)) for s in masked]
    probs = [(e / jnp.sum(e, axis=-1, keepdims=True)).astype(BF16) for e in exps]
    o_heads = [_dot(probs[b], v_ref[b].astype(BF16)) for b in seqs]
    outs = [jnp.concatenate([oh[h * rows:(h + 1) * rows] for h in range(n_heads)], axis=1) for oh in o_heads]
    o = outs[0] if nb == 1 else jnp.concatenate(outs, axis=0)
    hx = (o * gate).astype(BF16)
    m = gm * _dotw(hx, wpx_ref)
    m_ref[...] = m.reshape(nb, rows, d)


def _branch_x(x3, g, wq, wgx, wmx, k3, v3, wpx, *, nb, rows, heads_in_rows):
    bsz, seq, d = x3.shape
    n_mem, dx = k3.shape[1], k3.shape[2]
    kern = functools.partial(_xs_kernel if heads_in_rows else _x_kernel, nb=nb, rows=rows, head_dim=X_HEAD_DIM)
    return pl.pallas_call(
        kern,
        out_shape=jax.ShapeDtypeStruct((bsz, seq, d), F32),
        grid=(bsz // nb, seq // rows),
        in_specs=[pl.BlockSpec((nb, rows, d), lambda i, j: (i, j, 0))] + _const_specs([g, wq, wgx, wmx]) + [
            pl.BlockSpec((nb, n_mem, dx), lambda i, j: (i, 0, 0)),
            pl.BlockSpec((nb, n_mem, dx), lambda i, j: (i, 0, 0)),
            _const_spec(wpx),
        ],
        out_specs=pl.BlockSpec((nb, rows, d), lambda i, j: (i, j, 0)),
        compiler_params=_params(2),
        name="branch_x",
    )(x3, *_arrays([g, wq, wgx, wmx]), k3, v3, wpx)


def _a_kernel(x_ref, g_ref, wu_ref, wv_ref, wga_ref, wma_ref, lng_ref, lnb_ref, ws_ref, bs_ref, wpa_ref, min_ref,
              mout_ref, *maybe_v_ref, chunk, groups, sub):
    tm, d = x_ref.shape
    gd = d // groups
    bias = bs_ref[...]
    for t in range(tm // sub):
        ts = slice(t * sub, (t + 1) * sub)
        hn = _rmsnorm(x_ref[ts, :], g_ref[...]).astype(BF16)
        v = _dot_cols(hn, wv_ref, _gelu)
        u = _dot_cols(hn, wu_ref, _gelu)
        vc = v - jnp.mean(v, axis=-1, keepdims=True)
        vn = vc * lax.rsqrt(jnp.mean(vc * vc, axis=-1, keepdims=True) + EPS) * lng_ref[...] + lnb_ref[...]
        if maybe_v_ref:
            maybe_v_ref[0][ts, :] = vn
        ga = _dot_cols(hn, wga_ref, _silu)
        vb = vn.astype(BF16)
        n_c = sub // chunk
        wide = [_dot(ws_ref[gi], jnp.concatenate([vb[c * chunk:(c + 1) * chunk, gi * gd:(gi + 1) * gd]
                                                  for c in range(n_c)], axis=1)) for gi in range(groups)]
        mixed = [jnp.concatenate([wg[:, c * gd:(c + 1) * gd] for wg in wide], axis=1) + bias for c in range(n_c)]
        s = mixed[0] if n_c == 1 else jnp.concatenate(mixed, axis=0)
        ha = (u * s * ga).astype(BF16)
        gm = _sigmoid(_dotw(hn, wma_ref))
        mout_ref[ts, :] = min_ref[ts, :] + gm * _dotw(ha, wpa_ref)


def _branch_a(x2, g, wu, wv, wga, wma, lng, lnb, ws, bs, wpa, m_in, *, tm, want_v):
    rows, d = x2.shape
    row_spec = pl.BlockSpec((tm, d), lambda i: (i, 0))
    out_shape = [jax.ShapeDtypeStruct((rows, d), F32)]
    out_specs = [row_spec]
    if want_v:
        out_shape.append(jax.ShapeDtypeStruct((rows, d), F32))
        out_specs.append(row_spec)
    kern = functools.partial(_a_kernel, chunk=CHUNK, groups=A_GROUPS, sub=min(tm, 256))
    consts = [g, wu, wv, wga, wma, lng, lnb, ws, bs, wpa]
    res = pl.pallas_call(
        kern,
        out_shape=tuple(out_shape),
        grid=(rows // tm,),
        in_specs=[row_spec] + _const_specs(consts) + [row_spec],
        out_specs=tuple(out_specs),
        compiler_params=_params(1),
        name="branch_a",
    )(x2, *_arrays(consts), m_in)
    return res if want_v else (res[0], None)


def _finish_b(x, y, g_ref, wz_ref, wmb_ref, ng_ref, wpb_ref, m_in, wout_ref, fg_ref, *, n_groups, final_norm):
    hn = _rmsnorm(x, g_ref[...]).astype(BF16)
    gm = _sigmoid(_dotw(hn, wmb_ref))
    yf = y * _silu(_dotw(hn, wz_ref))
    gw = yf.shape[1] // n_groups
    parts = []
    for gi in range(n_groups):
        yg = yf[:, gi * gw:(gi + 1) * gw]
        parts.append(yg * lax.rsqrt(jnp.mean(yg * yg, axis=-1, keepdims=True) + EPS))
    hb = (jnp.concatenate(parts, axis=-1) * ng_ref[...]).astype(BF16)
    m = m_in + gm * _dotw(hb, wpb_ref)
    out = x + _dotw(m.astype(BF16), wout_ref)
    return _rmsnorm(out, fg_ref[...]) if final_norm else out


HANDOFF = ("xs", "bm", "cm", "zs", "dt", "gm", "tail")


def _bp_stage1(x_ref, g_ref, wz_ref, wxbc_ref, wdt_ref, wmb_ref, cw_ref, cb_ref, dtb_ref, perm_ref, xs_scr, out,
               first_tile, *, tm, chunk, d_inner, bc_width, conv_w):
    pad = SUBLANES
    vpc = chunk // pad
    n_chunks = tm // chunk
    width = MXU_COLS
    hn = _rmsnorm(x_ref[0], g_ref[...]).astype(BF16)
    for c0 in range(0, out["gm"].shape[1], width):
        cols = slice(c0, c0 + width)
        out["gm"][:, cols] = _sigmoid(_dotw(hn, wmb_ref, cols))
        yield
    hn = _dot(perm_ref[0], hn).astype(BF16)
    out["dt"][...] = jax.nn.softplus(_dotw(hn, wdt_ref) + dtb_ref[...])
    yield
    for c0 in range(0, d_inner, width):
        cols = slice(c0, c0 + width)
        out["zs"][:, cols] = _silu(_dotw(hn, wz_ref, cols))
        yield
    first_sublane = lax.broadcasted_iota(jnp.int32, (pad, width), 0) == 0
    for c0 in range(0, _ncols(wxbc_ref), width):
        cols = slice(c0, c0 + width)
        raw_all = _dotw(hn, wxbc_ref, cols)
        prev_tail = jnp.where(first_tile, 0.0, xs_scr[:, cols])
        acts = []
        for c in range(n_chunks):
            raw = raw_all[c * chunk:(c + 1) * chunk]
            conv = cb_ref[:, cols] + raw * cw_ref[conv_w - 1:conv_w, cols]
            for back in range(1, conv_w):
                head = []
                for v in range(back):
                    w = vpc - back + v
                    cur = pltpu.roll(raw[w * pad:(w + 1) * pad], 1, axis=0)
                    prv = pltpu.roll(prev_tail[(w - vpc + conv_w - 1) * pad:(w - vpc + conv_w) * pad], 1, axis=0)
                    head.append(jnp.where(first_sublane, prv, cur))
                shifted = jnp.concatenate(head + [raw[:chunk - back * pad]], axis=0)
                conv = conv + shifted * cw_ref[conv_w - 1 - back:conv_w - back, cols]
            acts.append(_silu(conv))
            prev_tail = raw[chunk - (conv_w - 1) * pad:]
        xs_scr[:, cols] = prev_tail
        out["tail"][:, cols] = prev_tail
        act = acts[0] if n_chunks == 1 else jnp.concatenate(acts, axis=0)
        if c0 < d_inner:
            out["xs"][:, cols] = act
        elif c0 < d_inner + bc_width:
            out["bm"][:, c0 - d_inner:c0 - d_inner + width] = act.astype(BF16)
        else:
            out["cm"][:, c0 - d_inner - bc_width:c0 - d_inner - bc_width + width] = act.astype(BF16)
        yield


def _bp_stage2(x_ref, min_ref, alog_ref, dsk_ref, ng_ref, wpb_ref, wout_ref, fg_ref, tri_ref, e3_ref, perm_ref,
               y_ref, st_scr, inp, first_tile, *, tm, chunk, n_groups, n_heads, head_dim, n_state, final_norm):
    pad = SUBLANES
    vpc = chunk // pad
    hpg = n_heads // n_groups
    gw = hpg * head_dim
    width = MXU_COLS

    def token_of(q):
        return (q % pad) * vpc + q // pad

    li = token_of(lax.broadcasted_iota(jnp.int32, (chunk, chunk), 0))
    si = token_of(lax.broadcasted_iota(jnp.int32, (chunk, chunk), 1))
    causal = li >= si
    head_of_lane = lax.broadcasted_iota(jnp.int32, (chunk, gw), 1) // head_dim
    tri = tri_ref[...]
    e3 = e3_ref[...]
    a_row = -jnp.exp(alog_ref[...])

    hb_rows = []
    for c in range(tm // chunk):
        rs = slice(c * chunk, (c + 1) * chunk)
        dt = inp["dt"][rs, :]
        acs = _cumsum_rows(tri, dt * a_row)
        acs_t = acs.T
        acs_e = _dot(_split_by_replica(acs, n_heads).astype(BF16), e3)
        a_last = acs_e[chunk - 1:chunk, :]
        xs_c = inp["xs"][rs, :]
        xdt_c = xs_c * _dot(_split_by_replica(dt, n_heads).astype(BF16), e3)
        xdt_b = xdt_c.astype(BF16)
        xd_state = (xdt_c * jnp.exp(a_last - acs_e)).astype(BF16)
        decay_in = jnp.exp(acs_e)
        st_prev = st_scr[...]
        if c == 0:
            st_prev = jnp.where(first_tile, 0.0, st_prev)
        st_prev_b = st_prev.astype(BF16)
        yield
        groups = range(n_groups)
        n_sl = [slice(gi * n_state, (gi + 1) * n_state) for gi in groups]
        g_sl = [slice(gi * gw, (gi + 1) * gw) for gi in groups]
        scores = [_dot_nt(inp["cm"][rs, n_sl[gi]], inp["bm"][rs, n_sl[gi]]) for gi in groups]
        yield
        y_offs = [_dot(inp["cm"][rs, n_sl[gi]], st_prev_b[:, g_sl[gi]]) for gi in groups]
        st_parts = [_dot_tn(inp["bm"][rs, n_sl[gi]], xd_state[:, g_sl[gi]]) for gi in groups]
        st_scr[...] = st_prev * jnp.exp(a_last) + jnp.concatenate(st_parts, axis=1)
        yield
        y_diags = []
        for gi in groups:
            m_heads, x_blocks = [], []
            xg = xdt_b[:, g_sl[gi]]
            for r in range(hpg):
                h = gi * hpg + r
                diff = acs[:, h:h + 1] - acs_t[h:h + 1, :]
                decay = jnp.exp(jnp.where(causal, diff, -jnp.inf))
                m_heads.append((scores[gi] * decay).astype(BF16))
                x_blocks.append(jnp.where(head_of_lane == r, xg, jnp.zeros_like(xg)))
            y_diags.append(_dot(jnp.concatenate(m_heads, axis=1), jnp.concatenate(x_blocks, axis=0)))
            yield
        hb_parts = []
        for gi in groups:
            gs = g_sl[gi]
            y = y_diags[gi] + y_offs[gi] * decay_in[:, gs] + xs_c[:, gs] * dsk_ref[:, gs]
            yf = y * inp["zs"][rs, gs]
            yn = yf * lax.rsqrt(jnp.mean(yf * yf, axis=-1, keepdims=True) + EPS)
            hb_parts.append((yn * ng_ref[:, gs]).astype(BF16))
            if gi % 2:
                yield
        hb_rows.append(jnp.concatenate(hb_parts, axis=1))
    hb = hb_rows[0] if len(hb_rows) == 1 else jnp.concatenate(hb_rows, axis=0)
    hb = _dot(perm_ref[1], hb).astype(BF16)
    yield
    m_parts = []
    for c0 in range(0, wpb_ref.shape[1], width):
        cols = slice(c0, c0 + width)
        m_parts.append((min_ref[0, :, cols] + inp["gm"][:, cols] * _dotw(hb, wpb_ref, cols)).astype(BF16))
        yield
    m = jnp.concatenate(m_parts, axis=1)
    out = x_ref[0] + _dotw(m, wout_ref)
    y_ref[0] = _rmsnorm(out, fg_ref[...]) if final_norm else out


def _bp_kernel(x1_ref, x2_ref, min_ref, g_ref, wz_ref, wxbc_ref, wdt_ref, wmb_ref, cw_ref, cb_ref, dtb_ref, alog_ref,
               dsk_ref, ng_ref, wpb_ref, wout_ref, fg_ref, tri_ref, e3_ref, perm_ref,
               y_ref, ssm_ref, conv_ref, xs_scr, st_scr, *slot_refs,
               nj, n_tiles, tm, chunk, n_groups, n_heads, head_dim, n_state, conv_w, final_norm):
    t = pl.program_id(0)
    d_inner = n_heads * head_dim
    pad = SUBLANES
    tile1 = jnp.minimum(t, n_tiles - 1)
    tile2 = jnp.maximum(t - 1, 0)
    n_h = len(HANDOFF)
    slots = [dict(zip(HANDOFF, slot_refs[i * n_h:(i + 1) * n_h])) for i in range(2)]

    @pl.when(t == 0)
    def _init():
        for ref in slots[1].values():
            ref[...] = jnp.zeros_like(ref)
        st_scr[...] = jnp.zeros_like(st_scr)
        xs_scr[...] = jnp.zeros_like(xs_scr)

    def step(slot_w, slot_r):
        s1 = _bp_stage1(x1_ref, g_ref, wz_ref, wxbc_ref, wdt_ref, wmb_ref, cw_ref, cb_ref, dtb_ref, perm_ref, xs_scr,
                        slot_w, tile1 % nj == 0, tm=tm, chunk=chunk, d_inner=d_inner,
                        bc_width=n_groups * n_state, conv_w=conv_w)
        s2 = _bp_stage2(x2_ref, min_ref, alog_ref, dsk_ref, ng_ref, wpb_ref, wout_ref, fg_ref, tri_ref, e3_ref,
                        perm_ref, y_ref, st_scr, slot_r, tile2 % nj == 0, tm=tm, chunk=chunk, n_groups=n_groups,
                        n_heads=n_heads, head_dim=head_dim, n_state=n_state, final_norm=final_norm)
        live = [s2, s1]
        while live:
            for gen in list(live):
                if next(gen, StopIteration) is StopIteration:
                    live.remove(gen)

        @pl.when((t > 0) & (tile2 % nj == nj - 1))
        def _seq_end():
            tail = slot_r["tail"][...]
            conv_ref[0] = jnp.concatenate([tail[(v + 1) * pad - 1:(v + 1) * pad] for v in range(conv_w - 1)], axis=0)
            ssm_ref[0] = st_scr[...].T

    for parity in range(2):
        pl.when(t % 2 == parity)(functools.partial(step, slots[parity], slots[1 - parity]))


def _branch_b_prompt(x3, m3, w, *, tm, final_norm):
    bsz, seq, d = x3.shape
    conv_dim = w["cw"].shape[1]
    d_inner = w["ng"].shape[1]
    n_heads = d_inner // SSD_HEAD_DIM
    conv_w = w["cw"].shape[0]
    nj = seq // tm
    n_tiles = bsz * nj
    bc_width = SSD_GROUPS * SSD_STATE
    kern = functools.partial(_bp_kernel, nj=nj, n_tiles=n_tiles, tm=tm, chunk=CHUNK, n_groups=SSD_GROUPS,
                             n_heads=n_heads, head_dim=SSD_HEAD_DIM, n_state=SSD_STATE, conv_w=conv_w,
                             final_norm=final_norm)

    def tile1(t):
        return jnp.minimum(t, n_tiles - 1)

    def tile2(t):
        return jnp.maximum(t - 1, 0)

    spec1 = pl.BlockSpec((1, tm, d), lambda t: (tile1(t) // nj, tile1(t) % nj, 0))
    spec2 = pl.BlockSpec((1, tm, d), lambda t: (tile2(t) // nj, tile2(t) % nj, 0))
    consts = [w[k] for k in ("g", "wz", "wxbc", "wdt", "wmb", "cw", "cb", "dtb", "alog", "dsk", "ng", "wpb", "wout",
                             "fg", "tri", "e3")]
    tok = (np.arange(tm) // CHUNK) * CHUNK + np.tile(_chunk_tokens(), tm // CHUNK)
    p = (tok[:, None] == np.arange(tm)[None, :]).astype(np.float32)
    perm = jnp.asarray(np.stack([p, p.T]), dtype=BF16)
    handoff = {"xs": ((tm, d_inner), F32), "bm": ((tm, bc_width), BF16), "cm": ((tm, bc_width), BF16),
               "zs": ((tm, d_inner), F32), "dt": ((tm, LANES), F32), "gm": ((tm, d), F32),
               "tail": (((conv_w - 1) * SUBLANES, conv_dim), F32)}
    return pl.pallas_call(
        _regroup(kern, 3, consts),
        out_shape=(jax.ShapeDtypeStruct((bsz, seq, d), F32),
                   jax.ShapeDtypeStruct((bsz, d_inner, SSD_STATE), F32),
                   jax.ShapeDtypeStruct((bsz, conv_w - 1, conv_dim), F32)),
        grid=(n_tiles + 1,),
        in_specs=[spec1, spec2, spec2] + _const_specs(consts + [perm]),
        out_specs=(spec2,
                   pl.BlockSpec((1, d_inner, SSD_STATE), lambda t: (tile2(t) // nj, 0, 0)),
                   pl.BlockSpec((1, conv_w - 1, conv_dim), lambda t: (tile2(t) // nj, 0, 0))),
        scratch_shapes=([pltpu.VMEM(((conv_w - 1) * SUBLANES, conv_dim), F32), pltpu.VMEM((SSD_STATE, d_inner), F32)]
                        + [pltpu.VMEM(*handoff[k]) for _ in range(2) for k in HANDOFF]),
        compiler_params=_params(1),
        name="branch_b_prompt",
    )(x3, x3, m3, *_arrays(consts), perm)


def _bs_kernel(x_ref, g_ref, wxbc_ref, wdt_ref, cw_ref, cb_ref, dtb_ref, alog_ref, dsk_ref,
               tri_ref, e3_ref, ecol_ref, conv0_ref, h0_ref,
               y_ref, hnew_ref, convnew_ref, xs_scr, tr_scr,
               *, nb, rows, n_groups, n_heads, head_dim, n_state, conv_w):
    hpg = n_heads // n_groups
    gw = hpg * head_dim
    d_inner = n_heads * head_dim
    d = x_ref.shape[-1]
    r_all = nb * rows
    hs_w = n_heads * rows

    @pl.when(pl.program_id(0) == 0)
    def _init():
        tr_scr[...] = jnp.zeros_like(tr_scr)
        xs_scr[...] = jnp.zeros_like(xs_scr)

    x = x_ref[...].reshape(r_all, d)
    hn = _rmsnorm(x, g_ref[...]).astype(BF16)
    xbc_raw = _dotw(hn, wxbc_ref)
    dt = jax.nn.softplus(_dotw(hn, wdt_ref) + dtb_ref[...])

    raw = xbc_raw.reshape(nb, rows, xbc_raw.shape[1])
    xs_scr[:, rows - (conv_w - 1):, :] = conv0_ref[...]
    state = xs_scr[...]
    row_in_seq = lax.broadcasted_iota(jnp.int32, raw.shape, 1)
    conv = cb_ref[...] + raw * cw_ref[conv_w - 1:conv_w, :]
    for back in range(1, conv_w):
        shifted = jnp.where(row_in_seq < back, pltpu.roll(state, back, axis=1), pltpu.roll(raw, back, axis=1))
        conv = conv + shifted * cw_ref[conv_w - 1 - back:conv_w - back, :]
    convnew_ref[...] = raw[:, rows - (conv_w - 1):, :]
    xbc = _silu(conv.reshape(r_all, xbc_raw.shape[1]))
    xs = xbc[:, :d_inner]
    bm = xbc[:, d_inner:d_inner + n_groups * n_state]
    cm = xbc[:, d_inner + n_groups * n_state:]

    a_row = -jnp.exp(alog_ref[...])
    acs = _cumsum_rows(tri_ref[...], dt * a_row)
    e3 = e3_ref[...]
    acs_split = _split_by_replica(acs, n_heads).astype(BF16)
    acs_e = _dot(acs_split, e3)
    xdt = xs * _dot(_split_by_replica(dt, n_heads).astype(BF16), e3)
    decay_in = jnp.exp(acs_e)

    acs_col = _dot(acs_split, ecol_ref[...])
    l_idx = lax.broadcasted_iota(jnp.int32, (r_all, hs_w), 0) % rows
    s_idx = lax.broadcasted_iota(jnp.int32, (r_all, hs_w), 1) % rows
    on_diag = jnp.where(l_idx == s_idx, acs_col, 0.0).reshape(nb, rows, hs_w)
    acs_row = jnp.broadcast_to(jnp.sum(on_diag, axis=1, keepdims=True), (nb, rows, hs_w)).reshape(r_all, hs_w)
    decay = jnp.exp(jnp.where(l_idx >= s_idx, acs_col - acs_row, -jnp.inf))

    slot = LANES // nb
    ones_rows = lax.broadcasted_iota(jnp.int32, (slot, n_state), 0)
    ones_blk = jnp.where((ones_rows >= rows) & (ones_rows < rows + 3), 1.0, 0.0)

    def seq_rows(b):
        return slice(b * rows, (b + 1) * rows)

    lane_in_tile = lax.broadcasted_iota(jnp.int32, (rows, LANES), 1)

    def expand_rows(v, start_of, width):
        c = v.shape[1]
        blocks = []
        for h in range(n_heads):
            start = start_of(h)
            lo = start // LANES * LANES
            hi = max(lo + LANES, start + width)
            piece = v[:, lo:hi]
            if width < LANES:
                piece = jnp.where(lane_in_tile // width == (start - lo) // width, piece, 0.0)
            parts = [jnp.zeros((rows, lo), F32)] if lo else []
            parts.append(piece)
            if hi < c:
                parts.append(jnp.zeros((rows, c - hi), F32))
            blocks.append(jnp.concatenate(parts, axis=1) if len(parts) > 1 else piece)
        return jnp.concatenate(blocks, axis=0).astype(BF16)

    y_diags = []
    for b0 in range(0, nb, SEQ_BATCH):
        seqs = range(b0, min(b0 + SEQ_BATCH, nb))
        b_exps = [expand_rows(bm[seq_rows(b)], lambda h: h // hpg * n_state, n_state) for b in seqs]
        scores = [_dot_nt(cm[seq_rows(b)].astype(BF16), be) for b, be in zip(seqs, b_exps)]
        x_blks = [expand_rows(xdt[seq_rows(b)], lambda h: h * head_dim, head_dim) for b in seqs]
        m_alls = [(s * decay[seq_rows(b)]).astype(BF16) for b, s in zip(seqs, scores)]
        y_diags += [_dot(m, xb) for m, xb in zip(m_alls, x_blks)]
    for b in range(nb):
        a_last = acs_e[(b + 1) * rows - 1:(b + 1) * rows, :]
        d_hi, d_mid, d_lo = _bf16_parts(jnp.exp(a_last))
        tr_scr[b * slot:b * slot + rows, :] = xdt[seq_rows(b)] * jnp.exp(a_last - acs_e[seq_rows(b)])
        tr_scr[b * slot + rows:b * slot + rows + 1, :] = d_hi.astype(F32)
        tr_scr[b * slot + rows + 1:b * slot + rows + 2, :] = d_mid.astype(F32)
        tr_scr[b * slot + rows + 2:b * slot + rows + 3, :] = d_lo.astype(F32)
    tr_t = tr_scr[...].T.astype(BF16)

    ys = []
    for b in range(nb):
        rs = slice(b * rows, (b + 1) * rows)
        bm_b, cm_bb = bm[rs], cm[rs].astype(BF16)
        y_parts = []
        for gi in range(n_groups):
            ns = slice(gi * n_state, (gi + 1) * n_state)
            gs = slice(gi * gw, (gi + 1) * gw)
            h0_g = h0_ref[b, gs, :]
            y_parts.append(_dot_nt(cm_bb[:, ns], h0_g.astype(BF16)))
            w_seq = jnp.concatenate(
                [jnp.concatenate([bm_b[:, ns], jnp.zeros((slot - rows, n_state), F32)], axis=0), ones_blk], axis=1)
            pieces = ([jnp.zeros((b * slot, 2 * n_state), F32)] if b else []) + [w_seq]
            if b < nb - 1:
                pieces.append(jnp.zeros((LANES - (b + 1) * slot, 2 * n_state), F32))
            upd = _dot(tr_t[gs, :], jnp.concatenate(pieces, axis=0).astype(BF16))
            hnew_ref[b, gs, :] = h0_g * upd[:, n_state:] + upd[:, :n_state]
        ys.append(y_diags[b] + jnp.concatenate(y_parts, axis=1) * decay_in[rs])
    y = jnp.concatenate(ys, axis=0) + xs * dsk_ref[...]
    y_ref[...] = y.reshape(nb, rows, d_inner)


def _bfin_kernel(x_ref, y_ref, min_ref, g_ref, wz_ref, wmb_ref, ng_ref, wpb_ref, wout_ref, fg_ref, out_ref,
                 *, n_groups, final_norm):
    out_ref[...] = _finish_b(x_ref[...], y_ref[...], g_ref, wz_ref, wmb_ref, ng_ref, wpb_ref, min_ref[...],
                             wout_ref, fg_ref, n_groups=n_groups, final_norm=final_norm)


def _branch_b_sample(x3, m3, conv0, h0, w, *, layer, nb, tm, final_norm):
    bsz, rows, d = x3.shape
    conv_dim = w["cw"].shape[1]
    d_inner = w["ng"].shape[1]
    n_heads = d_inner // SSD_HEAD_DIM
    conv_w = w["cw"].shape[0]
    kern = functools.partial(_bs_kernel, nb=nb, rows=rows, n_groups=SSD_GROUPS, n_heads=n_heads,
                             head_dim=SSD_HEAD_DIM, n_state=SSD_STATE, conv_w=conv_w)

    def seq_spec(width):
        return pl.BlockSpec((nb, rows, width), lambda i: (i, 0, 0))

    conv_spec = pl.BlockSpec((nb, conv_w - 1, conv_dim), lambda i: (i, 0, 0))
    h_spec = pl.BlockSpec((nb, d_inner, SSD_STATE), lambda i: (i, 0, 0))
    consts = [w[k] for k in ("g", "wxbc", "wdt", "cw", "cb", "dtb", "alog", "dsk", "tri_s", "e3", "ecol")]
    y, h_new, conv_new = pl.pallas_call(
        _regroup(kern, 1, consts),
        out_shape=(jax.ShapeDtypeStruct((bsz, rows, d_inner), F32),
                   jax.ShapeDtypeStruct((bsz, d_inner, SSD_STATE), F32),
                   jax.ShapeDtypeStruct((bsz, conv_w - 1, conv_dim), F32)),
        grid=(bsz // nb,),
        in_specs=[seq_spec(d)] + _const_specs(consts) + [
            pl.BlockSpec((None, nb, conv_w - 1, conv_dim), lambda i: (layer, i, 0, 0)), h_spec],
        out_specs=(seq_spec(d_inner), h_spec, conv_spec),
        scratch_shapes=[pltpu.VMEM((nb, SUBLANES, conv_dim), F32), pltpu.VMEM((LANES, d_inner), F32)],
        compiler_params=_params(1),
        name="branch_b_sample",
    )(x3, *_arrays(consts), conv0, h0)

    n_rows = bsz * rows
    fin_consts = [w[k] for k in ("g", "wz", "wmb", "ng", "wpb", "wout", "fg")]

    def rows_spec(width):
        return pl.BlockSpec((tm, width), lambda i: (i, 0))

    out = pl.pallas_call(
        _regroup(functools.partial(_bfin_kernel, n_groups=SSD_GROUPS, final_norm=final_norm), 3, fin_consts),
        out_shape=jax.ShapeDtypeStruct((n_rows, d), F32),
        grid=(n_rows // tm,),
        in_specs=[rows_spec(d), rows_spec(d_inner), rows_spec(d)] + _const_specs(fin_consts),
        out_specs=rows_spec(d),
        compiler_params=_params(1),
        name="branch_b_sample_out",
    )(x3.reshape(n_rows, d), y.reshape(n_rows, d_inner), m3.reshape(n_rows, d), *_arrays(fin_consts))
    return out.reshape(bsz, rows, d), h_new, conv_new


def _tile(n, pref):
    return pref if n % pref == 0 else n


def _chunk_tokens():
    q = np.arange(CHUNK)
    return (q % SUBLANES) * (CHUNK // SUBLANES) + q // SUBLANES


def _head_expand(n_heads, width, n_rep):
    j = np.arange(LANES)[:, None]
    c = np.arange(n_heads * width)[None, :]
    return jnp.asarray(((j < n_rep * n_heads) & (j % n_heads == c // width)).astype(np.float32), dtype=BF16)


def _layer_weights(l, d, dec_rows, nb_s, norm_g, w_in, conv_w, conv_b, dt_bias, a_log, d_skip, ssd_norm_g, ln_v_g,
                   ln_v_b, w_spatial, b_spatial, w_proj_a, w_proj_b, w_proj_x, w_out, final_norm_g):
    d_a = d
    d_inner = w_proj_b.shape[1]
    conv_dim = conv_w.shape[2]
    n_heads = a_log.shape[1]
    d_x = w_proj_x.shape[1]
    sizes = (d_a, d_a, d_a, d_inner, conv_dim, n_heads, d_x, d_x, N_BRANCH * d)
    offs = np.concatenate([[0], np.cumsum(sizes)])
    wl = w_in[l]
    sec = lambda i: wl[:, offs[i]:offs[i + 1]]
    row = lambda v: v.reshape(1, -1).astype(F32)
    rep3 = lambda v, fill: jnp.concatenate(
        [v] * DT_REPLICAS + [jnp.full(v.shape[:-1] + (LANES - DT_REPLICAS * v.shape[-1],), fill, v.dtype)], axis=-1)

    tril = jnp.tril(jnp.ones((CHUNK, CHUNK), F32))
    ws_p = jnp.where(tril[None] > 0, w_spatial[l], 0.0)
    bs_p = jnp.repeat(b_spatial[l].T, d_a // A_GROUPS, axis=1)
    n_seq = CHUNK // dec_rows
    eye = np.eye(n_seq, dtype=np.float32)
    ws_s = (eye[None, :, None, :, None] * ws_p[:, None, :dec_rows, None, :dec_rows]).reshape(A_GROUPS, CHUNK, CHUNK)
    bs_s = jnp.tile(bs_p[:dec_rows], (n_seq, 1))

    w = {
        "g": row(norm_g[l]),
        "lng": row(ln_v_g[l]), "lnb": row(ln_v_b[l]),
        "ws_p": ws_p.astype(BF16), "bs_p": bs_p, "ws_s": ws_s.astype(BF16), "bs_s": bs_s,
        "wpa": _pack_w(w_proj_a[l]),
        "cw": conv_w[l].astype(F32), "cb": row(conv_b[l]),
        "dtb": rep3(row(dt_bias[l]), 0.0), "alog": rep3(row(a_log[l]), 0.0),
        "dsk": jnp.repeat(row(d_skip[l]), SSD_HEAD_DIM, axis=1),
        "ng": row(ssd_norm_g[l]),
        "wpb": _pack_w(w_proj_b[l]),
        "wout": _pack_w(w_out[l]), "fg": row(final_norm_g),
        "tri": jnp.asarray(_chunk_tokens()[:, None] >= _chunk_tokens()[None, :], dtype=BF16),
        "tri_s": jnp.asarray(np.kron(np.eye(nb_s), np.tril(np.ones((dec_rows, dec_rows)))), dtype=BF16),
        "e3": _head_expand(n_heads, SSD_HEAD_DIM, DT_REPLICAS),
        "ecol": _head_expand(n_heads, dec_rows, DT_REPLICAS),
        "wpx": _pack_w(w_proj_x[l]),
    }
    w_bf = _pack_w(wl)
    tail = w_bf[:, offs[6]:]
    assert all(o % d == 0 for o in offs[:5]) and d_x % LANES == 0 and d % d_x == 0
    for name, i in (("wu", 0), ("wv", 1), ("wga", 2), ("wz", 3), ("wxbc", 4)):
        windows = [_Cols(w_bf, c, d) for c in range(offs[i], offs[i + 1], d)]
        w[name] = windows if len(windows) > 1 else windows[0]
    w["wdt"] = _pack_w(rep3(sec(5), 0.0))
    w["wq"], w["wgx"] = _Cols(tail, 0, d_x), _Cols(tail, d_x, d_x)
    for k, name in enumerate(("wma", "wmb", "wmx")):
        w[name] = _Cols(tail, 2 * d_x + k * d, d)
    return w


def _layer(x3, k3, v3, w, *, nb_x, rows_x, heads_in_rows, tm_a, ws, bs, want_v, b_fn):
    bsz, seq, d = x3.shape
    m = _branch_x(x3, w["g"], w["wq"], w["wgx"], w["wmx"], k3, v3, w["wpx"], nb=nb_x, rows=rows_x,
                  heads_in_rows=heads_in_rows)
    m, vn = _branch_a(x3.reshape(bsz * seq, d), w["g"], w["wu"], w["wv"], w["wga"], w["wma"], w["lng"], w["lnb"],
                      ws, bs, w["wpa"], m.reshape(bsz * seq, d), tm=tm_a, want_v=want_v)
    return b_fn(x3, m.reshape(bsz, seq, d)), vn


def kernel(x_prompt, x_sample, mem_prompt, cache_mem_k, cache_mem_v, state_ssm, state_conv, norm_g, w_in, conv_w,
           conv_b, dt_bias, a_log, d_skip, ssd_norm_g, ln_v_g, ln_v_b, w_spatial, b_spatial, mem_norm_g, w_mem_kv,
           w_proj_a, w_proj_b, w_proj_x, w_out, final_norm_g):
    depth = w_in.shape[0]
    bsz, seq, d = x_prompt.shape
    dec_b, dec_rows, _ = x_sample.shape
    n_mem = mem_prompt.shape[1]
    d_x = w_proj_x.shape[1]
    d_inner = w_proj_b.shape[1]
    n_heads = a_log.shape[1]
    assert seq % CHUNK == 0 and CHUNK % dec_rows == 0 and dec_rows == SUBLANES
    assert DT_REPLICAS * n_heads <= LANES and n_heads * dec_rows % LANES == 0

    nb_s = _tile(dec_b, 8)
    nb_xs = _tile(dec_b, 8)
    yp, ys = x_prompt, x_sample
    outs = {k: [] for k in ("mk", "mv", "hp", "cp", "hs", "cs", "vs")}
    for l in range(depth):
        w = _layer_weights(l, d, dec_rows, nb_s, norm_g, w_in, conv_w, conv_b, dt_bias, a_log, d_skip, ssd_norm_g,
                           ln_v_g, ln_v_b, w_spatial, b_spatial, w_proj_a, w_proj_b, w_proj_x, w_out, final_norm_g)
        final_norm = l == depth - 1
        mk, mv = _mem_kv(mem_prompt.reshape(bsz * n_mem, d), mem_norm_g[l].reshape(1, d).astype(F32),
                         _pack_w(w_mem_kv[l]), _tile(bsz * n_mem, 512))
        (yp, hp, cp), _ = _layer(
            yp, mk.reshape(bsz, n_mem * (d_x // X_HEAD_DIM), X_HEAD_DIM),
            mv.reshape(bsz, n_mem * (d_x // X_HEAD_DIM), X_HEAD_DIM), w,
            nb_x=1, rows_x=_tile(seq, 512), heads_in_rows=False, tm_a=_tile(bsz * seq, 512), ws=w["ws_p"], bs=w["bs_p"], want_v=False,
            b_fn=functools.partial(_branch_b_prompt, w=w, tm=_tile(seq, 256), final_norm=final_norm))
        (ys, hs, cs), vs = _layer(
            ys, cache_mem_k[l].reshape(dec_b, n_mem * (d_x // X_HEAD_DIM), X_HEAD_DIM),
            cache_mem_v[l].reshape(dec_b, n_mem * (d_x // X_HEAD_DIM), X_HEAD_DIM), w,
            nb_x=nb_xs, rows_x=dec_rows, heads_in_rows=True, tm_a=_tile(dec_b * dec_rows, 512), ws=w["ws_s"], bs=w["bs_s"], want_v=True,
            b_fn=functools.partial(_branch_b_sample, conv0=state_conv, layer=l,
                                   h0=state_ssm[l].reshape(dec_b, d_inner, SSD_STATE), w=w, nb=nb_s,
                                   tm=_tile(dec_b * dec_rows, 512), final_norm=final_norm))
        outs["mk"].append(mk.reshape(bsz, n_mem, d_x // X_HEAD_DIM, X_HEAD_DIM))
        outs["mv"].append(mv.reshape(bsz, n_mem, d_x // X_HEAD_DIM, X_HEAD_DIM))
        outs["hp"].append(hp.reshape(bsz, n_heads, SSD_HEAD_DIM, SSD_STATE))
        outs["cp"].append(cp)
        outs["hs"].append(hs.reshape(dec_b, n_heads, SSD_HEAD_DIM, SSD_STATE))
        outs["cs"].append(cs)
        outs["vs"].append(vs.reshape(dec_b, dec_rows, d))
    st = lambda k: jnp.stack(outs[k])
    return (yp, ys, st("mk"), st("mv"), st("hp"), st("cp"), st("hs"), st("cs"), st("vs"))
```

```python
import functools
import math
from typing import NamedTuple

import jax
import jax.numpy as jnp
import numpy as np
from jax import lax
from jax.experimental import pallas as pl
from jax.experimental.pallas import tpu as pltpu

F32 = jnp.float32
BF16 = jnp.bfloat16
EPS = 1e-6
SQRT_HALF = math.sqrt(0.5)

LANES = 128
SUBLANES = 8
VMEM_LIMIT_BYTES = 56 * 1024 * 1024

CHUNK = 128
A_GROUPS = 8
SSD_HEAD_DIM = 64
SSD_GROUPS = 8
SSD_STATE = 128
X_HEAD_DIM = 128
N_BRANCH = 3
DT_REPLICAS = 3
SEQ_BATCH = 4

def _dot(a, b):
    return jnp.dot(a, b, preferred_element_type=F32)


def _dot_nt(a, b):
    return lax.dot_general(a, b, (((1,), (1,)), ((), ())), preferred_element_type=F32)


def _dot_tn(a, b):
    return lax.dot_general(a, b, (((0,), (0,)), ((), ())), preferred_element_type=F32)


MXU_COLS = 256


def _pack_w(w):
    return w.astype(BF16)


def _ncols(w):
    return sum(r.shape[1] for r in w) if isinstance(w, tuple) else w.shape[1]


def _dotw(a, w, cols=None):
    if not isinstance(w, tuple):
        return _dot(a, w[...] if cols is None else w[:, cols])
    width = w[0].shape[1]
    lo, hi = (0, width * len(w)) if cols is None else (cols.start, cols.stop)
    parts = []
    while lo < hi:
        k, off = divmod(lo, width)
        n = min(width - off, hi - lo)
        parts.append(_dot(a, w[k][:, off:off + n]))
        lo += n
    return parts[0] if len(parts) == 1 else jnp.concatenate(parts, axis=1)


def _dot_cols(a, w_ref, fn, width=MXU_COLS):
    n = w_ref.shape[1]
    width = min(width, n)
    blocks = [fn(_dotw(a, w_ref, slice(c, c + width))) for c in range(0, n, width)]
    return blocks[0] if len(blocks) == 1 else jnp.concatenate(blocks, axis=1)


def _rmsnorm(x, g):
    return x * lax.rsqrt(jnp.mean(x * x, axis=-1, keepdims=True) + EPS) * g


def _gelu(x):
    return 0.5 * x * (1.0 + lax.erf(x * SQRT_HALF))


def _sigmoid(x):
    return 0.5 * jnp.tanh(0.5 * x) + 0.5


def _silu(x):
    return x * _sigmoid(x)


def _bf16_parts(x):
    hi = x.astype(BF16)
    r1 = x - hi.astype(F32)
    mid = r1.astype(BF16)
    lo = (r1 - mid.astype(F32)).astype(BF16)
    return hi, mid, lo


def _split_by_replica(x, n_heads):
    hi, mid, lo = _bf16_parts(x)
    lane = lax.broadcasted_iota(jnp.int32, x.shape, 1)
    return jnp.where(lane < n_heads, hi, jnp.where(lane < 2 * n_heads, mid, lo))


def _cumsum_rows(tri, x):
    hi, mid, lo = _bf16_parts(x)
    return _dot(tri, hi) + _dot(tri, mid) + _dot(tri, lo)


def _memkv_kernel(x_ref, g_ref, w_ref, k_ref, v_ref, *, n_heads):
    tm = x_ref.shape[0]
    hd = k_ref.shape[-1]
    hn = _rmsnorm(x_ref[...], g_ref[...]).astype(BF16)
    kv = _dotw(hn, w_ref)
    for h in range(n_heads):
        k_ref[pl.ds(h, tm, stride=n_heads), :] = kv[:, h * hd:(h + 1) * hd]
        v_ref[pl.ds(h, tm, stride=n_heads), :] = kv[:, (n_heads + h) * hd:(n_heads + h + 1) * hd]


class _Cols(NamedTuple):
    mat: jax.Array
    start: int
    width: int

    @property
    def shape(self):
        return (self.mat.shape[0], self.width)


def _const_spec(op=None):
    if isinstance(op, _Cols):
        idx = op.start // op.width
        return pl.BlockSpec(op.shape, lambda *_: (0, idx), pipeline_mode=pl.Buffered(1))
    return pl.BlockSpec(memory_space=pltpu.VMEM)


def _flatten(ops):
    return [o for op in ops for o in (op if isinstance(op, list) else [op])]


def _const_specs(ops):
    return [_const_spec(op) for op in _flatten(ops)]


def _arrays(ops):
    return [op.mat if isinstance(op, _Cols) else op for op in _flatten(ops)]


def _regroup(kernel_fn, n_lead, ops):
    sizes = [len(op) if isinstance(op, list) else 0 for op in ops]

    def wrapped(*refs):
        refs = list(refs)
        args, pos = refs[:n_lead], n_lead
        for n in sizes:
            args.append(tuple(refs[pos:pos + n]) if n else refs[pos])
            pos += max(n, 1)
        return kernel_fn(*args, *refs[pos:])

    return wrapped


def _params(n_grid):
    return pltpu.CompilerParams(dimension_semantics=("arbitrary",) * n_grid, vmem_limit_bytes=VMEM_LIMIT_BYTES)


def _mem_kv(mem2d, g, w, tm):
    rows, d = mem2d.shape
    n_heads = w.shape[1] // 2 // X_HEAD_DIM
    out = jax.ShapeDtypeStruct((rows * n_heads, X_HEAD_DIM), F32)
    out_spec = pl.BlockSpec((tm * n_heads, X_HEAD_DIM), lambda i: (i, 0))
    return pl.pallas_call(
        functools.partial(_memkv_kernel, n_heads=n_heads),
        out_shape=(out, out),
        grid=(rows // tm,),
        in_specs=[pl.BlockSpec((tm, d), lambda i: (i, 0)), _const_spec(), _const_spec()],
        out_specs=(out_spec, out_spec),
        compiler_params=_params(1),
        name="mem_kv",
    )(mem2d, g, w)


def _x_kernel(x_ref, g_ref, wq_ref, wgx_ref, wmx_ref, k_ref, v_ref, wpx_ref, m_ref, *, nb, rows, head_dim):
    d = x_ref.shape[-1]
    x = x_ref[...].reshape(nb * rows, d)
    hn = _rmsnorm(x, g_ref[...]).astype(BF16)
    q = _dotw(hn, wq_ref)
    gate = _silu(_dotw(hn, wgx_ref))
    gm = _sigmoid(_dotw(hn, wmx_ref))
    n_heads = q.shape[1] // head_dim
    scale = head_dim ** -0.5
    outs = []
    n_mem = k_ref.shape[1] // n_heads
    for b in range(nb):
        qb = q[b * rows:(b + 1) * rows].astype(BF16)
        heads = range(n_heads)
        kb = [k_ref[b, pl.ds(h, n_mem, stride=n_heads), :].astype(BF16) for h in heads]
        vb = [v_ref[b, pl.ds(h, n_mem, stride=n_heads), :].astype(BF16) for h in heads]
        scores = [_dot_nt(qb[:, h * head_dim:(h + 1) * head_dim], kb[h]) * scale for h in heads]
        exps = [jnp.exp(s - jnp.max(s, axis=-1, keepdims=True)) for s in scores]
        probs = [(e / jnp.sum(e, axis=-1, keepdims=True)).astype(BF16) for e in exps]
        outs.append(jnp.concatenate([_dot(probs[h], vb[h]) for h in heads], axis=-1))
    o = outs[0] if nb == 1 else jnp.concatenate(outs, axis=0)
    hx = (o * gate).astype(BF16)
    m = gm * _dotw(hx, wpx_ref)
    m_ref[...] = m.reshape(nb, rows, d)


def _xs_kernel(x_ref, g_ref, wq_ref, wgx_ref, wmx_ref, k_ref, v_ref, wpx_ref, m_ref, *, nb, rows, head_dim):
    d = x_ref.shape[-1]
    x = x_ref[...].reshape(nb * rows, d)
    hn = _rmsnorm(x, g_ref[...]).astype(BF16)
    q = _dotw(hn, wq_ref)
    gate = _silu(_dotw(hn, wgx_ref))
    gm = _sigmoid(_dotw(hn, wmx_ref))
    n_heads = q.shape[1] // head_dim
    n_kv = k_ref.shape[1]
    scale = head_dim ** -0.5
    row_head = lax.broadcasted_iota(jnp.int32, (n_heads * rows, n_kv), 0) // rows
    col_head = lax.broadcasted_iota(jnp.int32, (n_heads * rows, n_kv), 1) % n_heads
    same_head = row_head == col_head
    seqs = range(nb)
    scores = []
    for b in seqs:
        qb = q[b * rows:(b + 1) * rows]
        q_heads = jnp.concatenate([qb[:, h * head_dim:(h + 1) * head_dim] for h in range(n_heads)], axis=0)
        scores.append(_dot_nt(q_heads.astype(BF16), k_ref[b].astype(BF16)))
    masked = [jnp.where(same_head, s * scale, -jnp.inf) for s in scores]
    exps = [jnp.exp(s - jnp.max(s, axis=-1, keepdims=True)) for s in masked]
    probs = [(e / jnp.sum(e, axis=-1, keepdims=True)).astype(BF16) for e in exps]
    o_heads = [_dot(probs[b], v_ref[b].astype(BF16)) for b in seqs]
    outs = [jnp.concatenate([oh[h * rows:(h + 1) * rows] for h in range(n_heads)], axis=1) for oh in o_heads]
    o = outs[0] if nb == 1 else jnp.concatenate(outs, axis=0)
    hx = (o * gate).astype(BF16)
    m = gm * _dotw(hx, wpx_ref)
    m_ref[...] = m.reshape(nb, rows, d)


def _branch_x(x3, g, wq, wgx, wmx, k3, v3, wpx, *, nb, rows, heads_in_rows):
    bsz, seq, d = x3.shape
    n_mem, dx = k3.shape[1], k3.shape[2]
    kern = functools.partial(_xs_kernel if heads_in_rows else _x_kernel, nb=nb, rows=rows, head_dim=X_HEAD_DIM)
    return pl.pallas_call(
        kern,
        out_shape=jax.ShapeDtypeStruct((bsz, seq, d), F32),
        grid=(bsz // nb, seq // rows),
        in_specs=[pl.BlockSpec((nb, rows, d), lambda i, j: (i, j, 0))] + _const_specs([g, wq, wgx, wmx]) + [
            pl.BlockSpec((nb, n_mem, dx), lambda i, j: (i, 0, 0)),
            pl.BlockSpec((nb, n_mem, dx), lambda i, j: (i, 0, 0)),
            _const_spec(wpx),
        ],
        out_specs=pl.BlockSpec((nb, rows, d), lambda i, j: (i, j, 0)),
        compiler_params=_params(2),
        name="branch_x",
    )(x3, *_arrays([g, wq, wgx, wmx]), k3, v3, wpx)


def _a_kernel(x_ref, g_ref, wu_ref, wv_ref, wga_ref, wma_ref, lng_ref, lnb_ref, ws_ref, bs_ref, wpa_ref, min_ref,
              mout_ref, *maybe_v_ref, chunk, groups, sub):
    tm, d = x_ref.shape
    gd = d // groups
    bias = bs_ref[...]
    for t in range(tm // sub):
        ts = slice(t * sub, (t + 1) * sub)
        hn = _rmsnorm(x_ref[ts, :], g_ref[...]).astype(BF16)
        v = _dot_cols(hn, wv_ref, _gelu)
        u = _dot_cols(hn, wu_ref, _gelu)
        vc = v - jnp.mean(v, axis=-1, keepdims=True)
        vn = vc * lax.rsqrt(jnp.mean(vc * vc, axis=-1, keepdims=True) + EPS) * lng_ref[...] + lnb_ref[...]
        if maybe_v_ref:
            maybe_v_ref[0][ts, :] = vn
        ga = _dot_cols(hn, wga_ref, _silu)
        vb = vn.astype(BF16)
        n_c = sub // chunk
        wide = [_dot(ws_ref[gi], jnp.concatenate([vb[c * chunk:(c + 1) * chunk, gi * gd:(gi + 1) * gd]
                                                  for c in range(n_c)], axis=1)) for gi in range(groups)]
        mixed = [jnp.concatenate([wg[:, c * gd:(c + 1) * gd] for wg in wide], axis=1) + bias for c in range(n_c)]
        s = mixed[0] if n_c == 1 else jnp.concatenate(mixed, axis=0)
        ha = (u * s * ga).astype(BF16)
        gm = _sigmoid(_dotw(hn, wma_ref))
        mout_ref[ts, :] = min_ref[ts, :] + gm * _dotw(ha, wpa_ref)


def _branch_a(x2, g, wu, wv, wga, wma, lng, lnb, ws, bs, wpa, m_in, *, tm, want_v):
    rows, d = x2.shape
    row_spec = pl.BlockSpec((tm, d), lambda i: (i, 0))
    out_shape = [jax.ShapeDtypeStruct((rows, d), F32)]
    out_specs = [row_spec]
    if want_v:
        out_shape.append(jax.ShapeDtypeStruct((rows, d), F32))
        out_specs.append(row_spec)
    kern = functools.partial(_a_kernel, chunk=CHUNK, groups=A_GROUPS, sub=min(tm, 256))
    consts = [g, wu, wv, wga, wma, lng, lnb, ws, bs, wpa]
    res = pl.pallas_call(
        kern,
        out_shape=tuple(out_shape),
        grid=(rows // tm,),
        in_specs=[row_spec] + _const_specs(consts) + [row_spec],
        out_specs=tuple(out_specs),
        compiler_params=_params(1),
        name="branch_a",
    )(x2, *_arrays(consts), m_in)
    return res if want_v else (res[0], None)


def _finish_b(x, y, g_ref, wz_ref, wmb_ref, ng_ref, wpb_ref, m_in, wout_ref, fg_ref, *, n_groups, final_norm):
    hn = _rmsnorm(x, g_ref[...]).astype(BF16)
    gm = _sigmoid(_dotw(hn, wmb_ref))
    yf = y * _silu(_dotw(hn, wz_ref))
    gw = yf.shape[1] // n_groups
    parts = []
    for gi in range(n_groups):
        yg = yf[:, gi * gw:(gi + 1) * gw]
        parts.append(yg * lax.rsqrt(jnp.mean(yg * yg, axis=-1, keepdims=True) + EPS))
    hb = (jnp.concatenate(parts, axis=-1) * ng_ref[...]).astype(BF16)
    m = m_in + gm * _dotw(hb, wpb_ref)
    out = x + _dotw(m.astype(BF16), wout_ref)
    return _rmsnorm(out, fg_ref[...]) if final_norm else out


HANDOFF = ("xs", "bm", "cm", "zs", "dt", "gm", "tail")


def _bp_stage1(x_ref, g_ref, wz_ref, wxbc_ref, wdt_ref, wmb_ref, cw_ref, cb_ref, dtb_ref, perm_ref, xs_scr, out,
               first_tile, *, tm, chunk, d_inner, bc_width, conv_w):
    pad = SUBLANES
    vpc = chunk // pad
    n_chunks = tm // chunk
    width = MXU_COLS
    hn = _rmsnorm(x_ref[0], g_ref[...]).astype(BF16)
    for c0 in range(0, out["gm"].shape[1], width):
        cols = slice(c0, c0 + width)
        out["gm"][:, cols] = _sigmoid(_dotw(hn, wmb_ref, cols))
        yield
    hn = _dot(perm_ref[0], hn).astype(BF16)
    out["dt"][...] = jax.nn.softplus(_dotw(hn, wdt_ref) + dtb_ref[...])
    yield
    for c0 in range(0, d_inner, width):
        cols = slice(c0, c0 + width)
        out["zs"][:, cols] = _silu(_dotw(hn, wz_ref, cols))
        yield
    first_sublane = lax.broadcasted_iota(jnp.int32, (pad, width), 0) == 0
    for c0 in range(0, _ncols(wxbc_ref), width):
        cols = slice(c0, c0 + width)
        raw_all = _dotw(hn, wxbc_ref, cols)
        prev_tail = jnp.where(first_tile, 0.0, xs_scr[:, cols])
        acts = []
        for c in range(n_chunks):
            raw = raw_all[c * chunk:(c + 1) * chunk]
            conv = cb_ref[:, cols] + raw * cw_ref[conv_w - 1:conv_w, cols]
            for back in range(1, conv_w):
                head = []
                for v in range(back):
                    w = vpc - back + v
                    cur = pltpu.roll(raw[w * pad:(w + 1) * pad], 1, axis=0)
                    prv = pltpu.roll(prev_tail[(w - vpc + conv_w - 1) * pad:(w - vpc + conv_w) * pad], 1, axis=0)
                    head.append(jnp.where(first_sublane, prv, cur))
                shifted = jnp.concatenate(head + [raw[:chunk - back * pad]], axis=0)
                conv = conv + shifted * cw_ref[conv_w - 1 - back:conv_w - back, cols]
            acts.append(_silu(conv))
            prev_tail = raw[chunk - (conv_w - 1) * pad:]
        xs_scr[:, cols] = prev_tail
        out["tail"][:, cols] = prev_tail
        act = acts[0] if n_chunks == 1 else jnp.concatenate(acts, axis=0)
        if c0 < d_inner:
            out["xs"][:, cols] = act
        elif c0 < d_inner + bc_width:
            out["bm"][:, c0 - d_inner:c0 - d_inner + width] = act.astype(BF16)
        else:
            out["cm"][:, c0 - d_inner - bc_width:c0 - d_inner - bc_width + width] = act.astype(BF16)
        yield


def _bp_stage2(x_ref, min_ref, alog_ref, dsk_ref, ng_ref, wpb_ref, wout_ref, fg_ref, tri_ref, e3_ref, perm_ref,
               y_ref, st_scr, inp, first_tile, *, tm, chunk, n_groups, n_heads, head_dim, n_state, final_norm):
    pad = SUBLANES
    vpc = chunk // pad
    hpg = n_heads // n_groups
    gw = hpg * head_dim
    width = MXU_COLS

    def token_of(q):
        return (q % pad) * vpc + q // pad

    li = token_of(lax.broadcasted_iota(jnp.int32, (chunk, chunk), 0))
    si = token_of(lax.broadcasted_iota(jnp.int32, (chunk, chunk), 1))
    causal = li >= si
    head_of_lane = lax.broadcasted_iota(jnp.int32, (chunk, gw), 1) // head_dim
    tri = tri_ref[...]
    e3 = e3_ref[...]
    a_row = -jnp.exp(alog_ref[...])

    hb_rows = []
    for c in range(tm // chunk):
        rs = slice(c * chunk, (c + 1) * chunk)
        dt = inp["dt"][rs, :]
        acs = _cumsum_rows(tri, dt * a_row)
        acs_t = acs.T
        acs_e = _dot(_split_by_replica(acs, n_heads).astype(BF16), e3)
        a_last = acs_e[chunk - 1:chunk, :]
        xs_c = inp["xs"][rs, :]
        xdt_c = xs_c * _dot(_split_by_replica(dt, n_heads).astype(BF16), e3)
        xdt_b = xdt_c.astype(BF16)
        xd_state = (xdt_c * jnp.exp(a_last - acs_e)).astype(BF16)
        decay_in = jnp.exp(acs_e)
        st_prev = st_scr[...]
        if c == 0:
            st_prev = jnp.where(first_tile, 0.0, st_prev)
        st_prev_b = st_prev.astype(BF16)
        yield
        groups = range(n_groups)
        n_sl = [slice(gi * n_state, (gi + 1) * n_state) for gi in groups]
        g_sl = [slice(gi * gw, (gi + 1) * gw) for gi in groups]
        scores = [_dot_nt(inp["cm"][rs, n_sl[gi]], inp["bm"][rs, n_sl[gi]]) for gi in groups]
        yield
        y_offs = [_dot(inp["cm"][rs, n_sl[gi]], st_prev_b[:, g_sl[gi]]) for gi in groups]
        st_parts = [_dot_tn(inp["bm"][rs, n_sl[gi]], xd_state[:, g_sl[gi]]) for gi in groups]
        st_scr[...] = st_prev * jnp.exp(a_last) + jnp.concatenate(st_parts, axis=1)
        yield
        y_diags = []
        for gi in groups:
            m_heads, x_blocks = [], []
            xg = xdt_b[:, g_sl[gi]]
            for r in range(hpg):
                h = gi * hpg + r
                diff = acs[:, h:h + 1] - acs_t[h:h + 1, :]
                decay = jnp.exp(jnp.where(causal, diff, -jnp.inf))
                m_heads.append((scores[gi] * decay).astype(BF16))
                x_blocks.append(jnp.where(head_of_lane == r, xg, jnp.zeros_like(xg)))
            y_diags.append(_dot(jnp.concatenate(m_heads, axis=1), jnp.concatenate(x_blocks, axis=0)))
            yield
        hb_parts = []
        for gi in groups:
            gs = g_sl[gi]
            y = y_diags[gi] + y_offs[gi] * decay_in[:, gs] + xs_c[:, gs] * dsk_ref[:, gs]
            yf = y * inp["zs"][rs, gs]
            yn = yf * lax.rsqrt(jnp.mean(yf * yf, axis=-1, keepdims=True) + EPS)
            hb_parts.append((yn * ng_ref[:, gs]).astype(BF16))
            if gi % 2:
                yield
        hb_rows.append(jnp.concatenate(hb_parts, axis=1))
    hb = hb_rows[0] if len(hb_rows) == 1 else jnp.concatenate(hb_rows, axis=0)
    hb = _dot(perm_ref[1], hb).astype(BF16)
    yield
    m_parts = []
    for c0 in range(0, wpb_ref.shape[1], width):
        cols = slice(c0, c0 + width)
        m_parts.append((min_ref[0, :, cols] + inp["gm"][:, cols] * _dotw(hb, wpb_ref, cols)).astype(BF16))
        yield
    m = jnp.concatenate(m_parts, axis=1)
    out = x_ref[0] + _dotw(m, wout_ref)
    y_ref[0] = _rmsnorm(out, fg_ref[...]) if final_norm else out


def _bp_kernel(x1_ref, x2_ref, min_ref, g_ref, wz_ref, wxbc_ref, wdt_ref, wmb_ref, cw_ref, cb_ref, dtb_ref, alog_ref,
               dsk_ref, ng_ref, wpb_ref, wout_ref, fg_ref, tri_ref, e3_ref, perm_ref,
               y_ref, ssm_ref, conv_ref, xs_scr, st_scr, *slot_refs,
               nj, n_tiles, tm, chunk, n_groups, n_heads, head_dim, n_state, conv_w, final_norm):
    t = pl.program_id(0)
    d_inner = n_heads * head_dim
    pad = SUBLANES
    tile1 = jnp.minimum(t, n_tiles - 1)
    tile2 = jnp.maximum(t - 1, 0)
    n_h = len(HANDOFF)
    slots = [dict(zip(HANDOFF, slot_refs[i * n_h:(i + 1) * n_h])) for i in range(2)]

    @pl.when(t == 0)
    def _init():
        for ref in slots[1].values():
            ref[...] = jnp.zeros_like(ref)
        st_scr[...] = jnp.zeros_like(st_scr)
        xs_scr[...] = jnp.zeros_like(xs_scr)

    def step(slot_w, slot_r):
        s1 = _bp_stage1(x1_ref, g_ref, wz_ref, wxbc_ref, wdt_ref, wmb_ref, cw_ref, cb_ref, dtb_ref, perm_ref, xs_scr,
                        slot_w, tile1 % nj == 0, tm=tm, chunk=chunk, d_inner=d_inner,
                        bc_width=n_groups * n_state, conv_w=conv_w)
        s2 = _bp_stage2(x2_ref, min_ref, alog_ref, dsk_ref, ng_ref, wpb_ref, wout_ref, fg_ref, tri_ref, e3_ref,
                        perm_ref, y_ref, st_scr, slot_r, tile2 % nj == 0, tm=tm, chunk=chunk, n_groups=n_groups,
                        n_heads=n_heads, head_dim=head_dim, n_state=n_state, final_norm=final_norm)
        live = [s2, s1]
        while live:
            for gen in list(live):
                if next(gen, StopIteration) is StopIteration:
                    live.remove(gen)

        @pl.when((t > 0) & (tile2 % nj == nj - 1))
        def _seq_end():
            tail = slot_r["tail"][...]
            conv_ref[0] = jnp.concatenate([tail[(v + 1) * pad - 1:(v + 1) * pad] for v in range(conv_w - 1)], axis=0)
            ssm_ref[0] = st_scr[...].T

    for parity in range(2):
        pl.when(t % 2 == parity)(functools.partial(step, slots[parity], slots[1 - parity]))


def _branch_b_prompt(x3, m3, w, *, tm, final_norm):
    bsz, seq, d = x3.shape
    conv_dim = w["cw"].shape[1]
    d_inner = w["ng"].shape[1]
    n_heads = d_inner // SSD_HEAD_DIM
    conv_w = w["cw"].shape[0]
    nj = seq // tm
    n_tiles = bsz * nj
    bc_width = SSD_GROUPS * SSD_STATE
    kern = functools.partial(_bp_kernel, nj=nj, n_tiles=n_tiles, tm=tm, chunk=CHUNK, n_groups=SSD_GROUPS,
                             n_heads=n_heads, head_dim=SSD_HEAD_DIM, n_state=SSD_STATE, conv_w=conv_w,
                             final_norm=final_norm)

    def tile1(t):
        return jnp.minimum(t, n_tiles - 1)

    def tile2(t):
        return jnp.maximum(t - 1, 0)

    spec1 = pl.BlockSpec((1, tm, d), lambda t: (tile1(t) // nj, tile1(t) % nj, 0))
    spec2 = pl.BlockSpec((1, tm, d), lambda t: (tile2(t) // nj, tile2(t) % nj, 0))
    consts = [w[k] for k in ("g", "wz", "wxbc", "wdt", "wmb", "cw", "cb", "dtb", "alog", "dsk", "ng", "wpb", "wout",
                             "fg", "tri", "e3")]
    tok = (np.arange(tm) // CHUNK) * CHUNK + np.tile(_chunk_tokens(), tm // CHUNK)
    p = (tok[:, None] == np.arange(tm)[None, :]).astype(np.float32)
    perm = jnp.asarray(np.stack([p, p.T]), dtype=BF16)
    handoff = {"xs": ((tm, d_inner), F32), "bm": ((tm, bc_width), BF16), "cm": ((tm, bc_width), BF16),
               "zs": ((tm, d_inner), F32), "dt": ((tm, LANES), F32), "gm": ((tm, d), F32),
               "tail": (((conv_w - 1) * SUBLANES, conv_dim), F32)}
    return pl.pallas_call(
        _regroup(kern, 3, consts),
        out_shape=(jax.ShapeDtypeStruct((bsz, seq, d), F32),
                   jax.ShapeDtypeStruct((bsz, d_inner, SSD_STATE), F32),
                   jax.ShapeDtypeStruct((bsz, conv_w - 1, conv_dim), F32)),
        grid=(n_tiles + 1,),
        in_specs=[spec1, spec2, spec2] + _const_specs(consts + [perm]),
        out_specs=(spec2,
                   pl.BlockSpec((1, d_inner, SSD_STATE), lambda t: (tile2(t) // nj, 0, 0)),
                   pl.BlockSpec((1, conv_w - 1, conv_dim), lambda t: (tile2(t) // nj, 0, 0))),
        scratch_shapes=([pltpu.VMEM(((conv_w - 1) * SUBLANES, conv_dim), F32), pltpu.VMEM((SSD_STATE, d_inner), F32)]
                        + [pltpu.VMEM(*handoff[k]) for _ in range(2) for k in HANDOFF]),
        compiler_params=_params(1),
        name="branch_b_prompt",
    )(x3, x3, m3, *_arrays(consts), perm)


def _bs_kernel(x_ref, g_ref, wxbc_ref, wdt_ref, cw_ref, cb_ref, dtb_ref, alog_ref, dsk_ref,
               tri_ref, e3_ref, ecol_ref, conv0_ref, h0_ref,
               y_ref, hnew_ref, convnew_ref, xs_scr, tr_scr,
               *, nb, rows, n_groups, n_heads, head_dim, n_state, conv_w):
    hpg = n_heads // n_groups
    gw = hpg * head_dim
    d_inner = n_heads * head_dim
    d = x_ref.shape[-1]
    r_all = nb * rows
    hs_w = n_heads * rows

    @pl.when(pl.program_id(0) == 0)
    def _init():
        tr_scr[...] = jnp.zeros_like(tr_scr)
        xs_scr[...] = jnp.zeros_like(xs_scr)

    x = x_ref[...].reshape(r_all, d)
    hn = _rmsnorm(x, g_ref[...]).astype(BF16)
    xbc_raw = _dotw(hn, wxbc_ref)
    dt = jax.nn.softplus(_dotw(hn, wdt_ref) + dtb_ref[...])

    raw = xbc_raw.reshape(nb, rows, xbc_raw.shape[1])
    xs_scr[:, rows - (conv_w - 1):, :] = conv0_ref[...]
    state = xs_scr[...]
    row_in_seq = lax.broadcasted_iota(jnp.int32, raw.shape, 1)
    conv = cb_ref[...] + raw * cw_ref[conv_w - 1:conv_w, :]
    for back in range(1, conv_w):
        shifted = jnp.where(row_in_seq < back, pltpu.roll(state, back, axis=1), pltpu.roll(raw, back, axis=1))
        conv = conv + shifted * cw_ref[conv_w - 1 - back:conv_w - back, :]
    convnew_ref[...] = raw[:, rows - (conv_w - 1):, :]
    xbc = _silu(conv.reshape(r_all, xbc_raw.shape[1]))
    xs = xbc[:, :d_inner]
    bm = xbc[:, d_inner:d_inner + n_groups * n_state]
    cm = xbc[:, d_inner + n_groups * n_state:]

    a_row = -jnp.exp(alog_ref[...])
    acs = _cumsum_rows(tri_ref[...], dt * a_row)
    e3 = e3_ref[...]
    acs_split = _split_by_replica(acs, n_heads).astype(BF16)
    acs_e = _dot(acs_split, e3)
    xdt = xs * _dot(_split_by_replica(dt, n_heads).astype(BF16), e3)
    decay_in = jnp.exp(acs_e)

    acs_col = _dot(acs_split, ecol_ref[...])
    l_idx = lax.broadcasted_iota(jnp.int32, (r_all, hs_w), 0) % rows
    s_idx = lax.broadcasted_iota(jnp.int32, (r_all, hs_w), 1) % rows
    on_diag = jnp.where(l_idx == s_idx, acs_col, 0.0).reshape(nb, rows, hs_w)
    acs_row = jnp.broadcast_to(jnp.sum(on_diag, axis=1, keepdims=True), (nb, rows, hs_w)).reshape(r_all, hs_w)
    decay = jnp.exp(jnp.where(l_idx >= s_idx, acs_col - acs_row, -jnp.inf))

    slot = LANES // nb
    ones_rows = lax.broadcasted_iota(jnp.int32, (slot, n_state), 0)
    ones_blk = jnp.where((ones_rows >= rows) & (ones_rows < rows + 3), 1.0, 0.0)

    def seq_rows(b):
        return slice(b * rows, (b + 1) * rows)

    lane_in_tile = lax.broadcasted_iota(jnp.int32, (rows, LANES), 1)

    def expand_rows(v, start_of, width):
        c = v.shape[1]
        blocks = []
        for h in range(n_heads):
            start = start_of(h)
            lo = start // LANES * LANES
            hi = max(lo + LANES, start + width)
            piece = v[:, lo:hi]
            if width < LANES:
                piece = jnp.where(lane_in_tile // width == (start - lo) // width, piece, 0.0)
            parts = [jnp.zeros((rows, lo), F32)] if lo else []
            parts.append(piece)
            if hi < c:
                parts.append(jnp.zeros((rows, c - hi), F32))
            blocks.append(jnp.concatenate(parts, axis=1) if len(parts) > 1 else piece)
        return jnp.concatenate(blocks, axis=0).astype(BF16)

    y_diags = []
    for b0 in range(0, nb, SEQ_BATCH):
        seqs = range(b0, min(b0 + SEQ_BATCH, nb))
        b_exps = [expand_rows(bm[seq_rows(b)], lambda h: h // hpg * n_state, n_state) for b in seqs]
        scores = [_dot_nt(cm[seq_rows(b)].astype(BF16), be) for b, be in zip(seqs, b_exps)]
        x_blks = [expand_rows(xdt[seq_rows(b)], lambda h: h * head_dim, head_dim) for b in seqs]
        m_alls = [(s * decay[seq_rows(b)]).astype(BF16) for b, s in zip(seqs, scores)]
        y_diags += [_dot(m, xb) for m, xb in zip(m_alls, x_blks)]
    for b in range(nb):
        a_last = acs_e[(b + 1) * rows - 1:(b + 1) * rows, :]
        d_hi, d_mid, d_lo = _bf16_parts(jnp.exp(a_last))
        tr_scr[b * slot:b * slot + rows, :] = xdt[seq_rows(b)] * jnp.exp(a_last - acs_e[seq_rows(b)])
        tr_scr[b * slot + rows:b * slot + rows + 1, :] = d_hi.astype(F32)
        tr_scr[b * slot + rows + 1:b * slot + rows + 2, :] = d_mid.astype(F32)
        tr_scr[b * slot + rows + 2:b * slot + rows + 3, :] = d_lo.astype(F32)
    tr_t = tr_scr[...].T.astype(BF16)

    ys = []
    for b in range(nb):
        rs = slice(b * rows, (b + 1) * rows)
        bm_b, cm_bb = bm[rs], cm[rs].astype(BF16)
        y_parts = []
        for gi in range(n_groups):
            ns = slice(gi * n_state, (gi + 1) * n_state)
            gs = slice(gi * gw, (gi + 1) * gw)
            h0_g = h0_ref[b, gs, :]
            y_parts.append(_dot_nt(cm_bb[:, ns], h0_g.astype(BF16)))
            w_seq = jnp.concatenate(
                [jnp.concatenate([bm_b[:, ns], jnp.zeros((slot - rows, n_state), F32)], axis=0), ones_blk], axis=1)
            pieces = ([jnp.zeros((b * slot, 2 * n_state), F32)] if b else []) + [w_seq]
            if b < nb - 1:
                pieces.append(jnp.zeros((LANES - (b + 1) * slot, 2 * n_state), F32))
            upd = _dot(tr_t[gs, :], jnp.concatenate(pieces, axis=0).astype(BF16))
            hnew_ref[b, gs, :] = h0_g * upd[:, n_state:] + upd[:, :n_state]
        ys.append(y_diags[b] + jnp.concatenate(y_parts, axis=1) * decay_in[rs])
    y = jnp.concatenate(ys, axis=0) + xs * dsk_ref[...]
    y_ref[...] = y.reshape(nb, rows, d_inner)


def _bfin_kernel(x_ref, y_ref, min_ref, g_ref, wz_ref, wmb_ref, ng_ref, wpb_ref, wout_ref, fg_ref, out_ref,
                 *, n_groups, final_norm):
    out_ref[...] = _finish_b(x_ref[...], y_ref[...], g_ref, wz_ref, wmb_ref, ng_ref, wpb_ref, min_ref[...],
                             wout_ref, fg_ref, n_groups=n_groups, final_norm=final_norm)


def _branch_b_sample(x3, m3, conv0, h0, w, *, layer, nb, tm, final_norm):
    bsz, rows, d = x3.shape
    conv_dim = w["cw"].shape[1]
    d_inner = w["ng"].shape[1]
    n_heads = d_inner // SSD_HEAD_DIM
    conv_w = w["cw"].shape[0]
    kern = functools.partial(_bs_kernel, nb=nb, rows=rows, n_groups=SSD_GROUPS, n_heads=n_heads,
                             head_dim=SSD_HEAD_DIM, n_state=SSD_STATE, conv_w=conv_w)

    def seq_spec(width):
        return pl.BlockSpec((nb, rows, width), lambda i: (i, 0, 0))

    conv_spec = pl.BlockSpec((nb, conv_w - 1, conv_dim), lambda i: (i, 0, 0))
    h_spec = pl.BlockSpec((nb, d_inner, SSD_STATE), lambda i: (i, 0, 0))
    consts = [w[k] for k in ("g", "wxbc", "wdt", "cw", "cb", "dtb", "alog", "dsk", "tri_s", "e3", "ecol")]
    y, h_new, conv_new = pl.pallas_call(
        _regroup(kern, 1, consts),
        out_shape=(jax.ShapeDtypeStruct((bsz, rows, d_inner), F32),
                   jax.ShapeDtypeStruct((bsz, d_inner, SSD_STATE), F32),
                   jax.ShapeDtypeStruct((bsz, conv_w - 1, conv_dim), F32)),
        grid=(bsz // nb,),
        in_specs=[seq_spec(d)] + _const_specs(consts) + [
            pl.BlockSpec((None, nb, conv_w - 1, conv_dim), lambda i: (layer, i, 0, 0)), h_spec],
        out_specs=(seq_spec(d_inner), h_spec, conv_spec),
        scratch_shapes=[pltpu.VMEM((nb, SUBLANES, conv_dim), F32), pltpu.VMEM((LANES, d_inner), F32)],
        compiler_params=_params(1),
        name="branch_b_sample",
    )(x3, *_arrays(consts), conv0, h0)

    n_rows = bsz * rows
    fin_consts = [w[k] for k in ("g", "wz", "wmb", "ng", "wpb", "wout", "fg")]

    def rows_spec(width):
        return pl.BlockSpec((tm, width), lambda i: (i, 0))

    out = pl.pallas_call(
        _regroup(functools.partial(_bfin_kernel, n_groups=SSD_GROUPS, final_norm=final_norm), 3, fin_consts),
        out_shape=jax.ShapeDtypeStruct((n_rows, d), F32),
        grid=(n_rows // tm,),
        in_specs=[rows_spec(d), rows_spec(d_inner), rows_spec(d)] + _const_specs(fin_consts),
        out_specs=rows_spec(d),
        compiler_params=_params(1),
        name="branch_b_sample_out",
    )(x3.reshape(n_rows, d), y.reshape(n_rows, d_inner), m3.reshape(n_rows, d), *_arrays(fin_consts))
    return out.reshape(bsz, rows, d), h_new, conv_new


CAST_COLS = 512


def _cast_kernel(w_ref, head_ref, tail_ref, prev_scr, *, first_tail_block, shift):
    cur = w_ref[...]
    head_ref[...] = cur.astype(BF16)

    @pl.when(pl.program_id(0) > first_tail_block)
    def _tail():
        both = jnp.concatenate([prev_scr[...], cur], axis=1)
        tail_ref[...] = both[:, shift:shift + cur.shape[1]].astype(BF16)

    prev_scr[...] = cur


def _cast_w_in(w_in, layer, tail_start):
    _, k, n = w_in.shape
    cb = CAST_COLS
    first_tail_block, shift = divmod(tail_start, cb)
    n_blocks = pl.cdiv(n, cb)
    assert (n - tail_start) % cb == 0 and first_tail_block + 1 + (n - tail_start) // cb == n_blocks
    return pl.pallas_call(
        functools.partial(_cast_kernel, first_tail_block=first_tail_block, shift=shift),
        out_shape=(jax.ShapeDtypeStruct((k, n), BF16), jax.ShapeDtypeStruct((k, n - tail_start), BF16)),
        grid=(n_blocks,),
        in_specs=[pl.BlockSpec((None, k, cb), lambda j: (layer, 0, j))],
        out_specs=(pl.BlockSpec((k, cb), lambda j: (0, j)),
                   pl.BlockSpec((k, cb), lambda j: (0, jnp.maximum(j - first_tail_block - 1, 0)))),
        scratch_shapes=[pltpu.VMEM((k, cb), F32)],
        compiler_params=_params(1),
        name="cast_w_in",
    )(w_in)


def _tile(n, pref):
    return pref if n % pref == 0 else n


def _chunk_tokens():
    q = np.arange(CHUNK)
    return (q % SUBLANES) * (CHUNK // SUBLANES) + q // SUBLANES


def _head_expand(n_heads, width, n_rep):
    j = np.arange(LANES)[:, None]
    c = np.arange(n_heads * width)[None, :]
    return jnp.asarray(((j < n_rep * n_heads) & (j % n_heads == c // width)).astype(np.float32), dtype=BF16)


def _layer_weights(l, d, dec_rows, nb_s, norm_g, w_in, conv_w, conv_b, dt_bias, a_log, d_skip, ssd_norm_g, ln_v_g,
                   ln_v_b, w_spatial, b_spatial, w_proj_a, w_proj_b, w_proj_x, w_out, final_norm_g):
    d_a = d
    d_inner = w_proj_b.shape[1]
    conv_dim = conv_w.shape[2]
    n_heads = a_log.shape[1]
    d_x = w_proj_x.shape[1]
    sizes = (d_a, d_a, d_a, d_inner, conv_dim, n_heads, d_x, d_x, N_BRANCH * d)
    offs = np.concatenate([[0], np.cumsum(sizes)])
    wl = w_in[l]
    sec = lambda i: wl[:, offs[i]:offs[i + 1]]
    row = lambda v: v.reshape(1, -1).astype(F32)
    rep3 = lambda v, fill: jnp.concatenate(
        [v] * DT_REPLICAS + [jnp.full(v.shape[:-1] + (LANES - DT_REPLICAS * v.shape[-1],), fill, v.dtype)], axis=-1)

    tril = jnp.tril(jnp.ones((CHUNK, CHUNK), F32))
    ws_p = jnp.where(tril[None] > 0, w_spatial[l], 0.0)
    bs_p = jnp.repeat(b_spatial[l].T, d_a // A_GROUPS, axis=1)
    n_seq = CHUNK // dec_rows
    eye = np.eye(n_seq, dtype=np.float32)
    ws_s = (eye[None, :, None, :, None] * ws_p[:, None, :dec_rows, None, :dec_rows]).reshape(A_GROUPS, CHUNK, CHUNK)
    bs_s = jnp.tile(bs_p[:dec_rows], (n_seq, 1))

    w = {
        "g": row(norm_g[l]),
        "lng": row(ln_v_g[l]), "lnb": row(ln_v_b[l]),
        "ws_p": ws_p.astype(BF16), "bs_p": bs_p, "ws_s": ws_s.astype(BF16), "bs_s": bs_s,
        "wpa": _pack_w(w_proj_a[l]),
        "cw": conv_w[l].astype(F32), "cb": row(conv_b[l]),
        "dtb": rep3(row(dt_bias[l]), 0.0), "alog": rep3(row(a_log[l]), 0.0),
        "dsk": jnp.repeat(row(d_skip[l]), SSD_HEAD_DIM, axis=1),
        "ng": row(ssd_norm_g[l]),
        "wpb": _pack_w(w_proj_b[l]),
        "wout": _pack_w(w_out[l]), "fg": row(final_norm_g),
        "tri": jnp.asarray(_chunk_tokens()[:, None] >= _chunk_tokens()[None, :], dtype=BF16),
        "tri_s": jnp.asarray(np.kron(np.eye(nb_s), np.tril(np.ones((dec_rows, dec_rows)))), dtype=BF16),
        "e3": _head_expand(n_heads, SSD_HEAD_DIM, DT_REPLICAS),
        "ecol": _head_expand(n_heads, dec_rows, DT_REPLICAS),
        "wpx": _pack_w(w_proj_x[l]),
    }
    w_bf, tail = _cast_w_in(w_in, l, int(offs[6]))
    assert all(o % d == 0 for o in offs[:5]) and d_x % LANES == 0 and d % d_x == 0
    for name, i in (("wu", 0), ("wv", 1), ("wga", 2), ("wz", 3), ("wxbc", 4)):
        windows = [_Cols(w_bf, c, d) for c in range(offs[i], offs[i + 1], d)]
        w[name] = windows if len(windows) > 1 else windows[0]
    w["wdt"] = _pack_w(rep3(sec(5), 0.0))
    w["wq"], w["wgx"] = _Cols(tail, 0, d_x), _Cols(tail, d_x, d_x)
    for k, name in enumerate(("wma", "wmb", "wmx")):
        w[name] = _Cols(tail, 2 * d_x + k * d, d)
    return w


def _layer(x3, k3, v3, w, *, nb_x, rows_x, heads_in_rows, tm_a, ws, bs, want_v, b_fn):
    bsz, seq, d = x3.shape
    m = _branch_x(x3, w["g"], w["wq"], w["wgx"], w["wmx"], k3, v3, w["wpx"], nb=nb_x, rows=rows_x,
                  heads_in_rows=heads_in_rows)
    m, vn = _branch_a(x3.reshape(bsz * seq, d), w["g"], w["wu"], w["wv"], w["wga"], w["wma"], w["lng"], w["lnb"],
                      ws, bs, w["wpa"], m.reshape(bsz * seq, d), tm=tm_a, want_v=want_v)
    return b_fn(x3, m.reshape(bsz, seq, d)), vn


def kernel(x_prompt, x_sample, mem_prompt, cache_mem_k, cache_mem_v, state_ssm, state_conv, norm_g, w_in, conv_w,
           conv_b, dt_bias, a_log, d_skip, ssd_norm_g, ln_v_g, ln_v_b, w_spatial, b_spatial, mem_norm_g, w_mem_kv,
           w_proj_a, w_proj_b, w_proj_x, w_out, final_norm_g):
    depth = w_in.shape[0]
    bsz, seq, d = x_prompt.shape
    dec_b, dec_rows, _ = x_sample.shape
    n_mem = mem_prompt.shape[1]
    d_x = w_proj_x.shape[1]
    d_inner = w_proj_b.shape[1]
    n_heads = a_log.shape[1]
    assert seq % CHUNK == 0 and CHUNK % dec_rows == 0 and dec_rows == SUBLANES
    assert DT_REPLICAS * n_heads <= LANES and n_heads * dec_rows % LANES == 0

    nb_s = _tile(dec_b, 8)
    nb_xs = _tile(dec_b, 8)
    yp, ys = x_prompt, x_sample
    outs = {k: [] for k in ("mk", "mv", "hp", "cp", "hs", "cs", "vs")}
    for l in range(depth):
        w = _layer_weights(l, d, dec_rows, nb_s, norm_g, w_in, conv_w, conv_b, dt_bias, a_log, d_skip, ssd_norm_g,
                           ln_v_g, ln_v_b, w_spatial, b_spatial, w_proj_a, w_proj_b, w_proj_x, w_out, final_norm_g)
        final_norm = l == depth - 1
        mk, mv = _mem_kv(mem_prompt.reshape(bsz * n_mem, d), mem_norm_g[l].reshape(1, d).astype(F32),
                         _pack_w(w_mem_kv[l]), _tile(bsz * n_mem, 512))
        (yp, hp, cp), _ = _layer(
            yp, mk.reshape(bsz, n_mem * (d_x // X_HEAD_DIM), X_HEAD_DIM),
            mv.reshape(bsz, n_mem * (d_x // X_HEAD_DIM), X_HEAD_DIM), w,
            nb_x=1, rows_x=_tile(seq, 512), heads_in_rows=False, tm_a=_tile(bsz * seq, 512), ws=w["ws_p"], bs=w["bs_p"], want_v=False,
            b_fn=functools.partial(_branch_b_prompt, w=w, tm=_tile(seq, 256), final_norm=final_norm))
        (ys, hs, cs), vs = _layer(
            ys, cache_mem_k[l].reshape(dec_b, n_mem * (d_x // X_HEAD_DIM), X_HEAD_DIM),
            cache_mem_v[l].reshape(dec_b, n_mem * (d_x // X_HEAD_DIM), X_HEAD_DIM), w,
            nb_x=nb_xs, rows_x=dec_rows, heads_in_rows=True, tm_a=_tile(dec_b * dec_rows, 512), ws=w["ws_s"], bs=w["bs_s"], want_v=True,
            b_fn=functools.partial(_branch_b_sample, conv0=state_conv, layer=l,
                                   h0=state_ssm[l].reshape(dec_b, d_inner, SSD_STATE), w=w, nb=nb_s,
                                   tm=_tile(dec_b * dec_rows, 512), final_norm=final_norm))
        outs["mk"].append(mk.reshape(bsz, n_mem, d_x // X_HEAD_DIM, X_HEAD_DIM))
        outs["mv"].append(mv.reshape(bsz, n_mem, d_x // X_HEAD_DIM, X_HEAD_DIM))
        outs["hp"].append(hp.reshape(bsz, n_heads, SSD_HEAD_DIM, SSD_STATE))
        outs["cp"].append(cp)
        outs["hs"].append(hs.reshape(dec_b, n_heads, SSD_HEAD_DIM, SSD_STATE))
        outs["cs"].append(cs)
        outs["vs"].append(vs.reshape(dec_b, dec_rows, d))
    st = lambda k: jnp.stack(outs[k])
    return (yp, ys, st("mk"), st("mv"), st("hp"), st("cp"), st("hs"), st("cs"), st("vs"))
```

```python
import functools
import math
from typing import NamedTuple

import jax
import jax.numpy as jnp
import numpy as np
from jax import lax
from jax.experimental import pallas as pl
from jax.experimental.pallas import tpu as pltpu

F32 = jnp.float32
BF16 = jnp.bfloat16
EPS = 1e-6
SQRT_HALF = math.sqrt(0.5)

LANES = 128
SUBLANES = 8
VMEM_LIMIT_BYTES = 56 * 1024 * 1024

CHUNK = 128
A_GROUPS = 8
SSD_HEAD_DIM = 64
SSD_GROUPS = 8
SSD_STATE = 128
X_HEAD_DIM = 128
N_BRANCH = 3
DT_REPLICAS = 3
SEQ_BATCH = 4

def _dot(a, b):
    return jnp.dot(a, b, preferred_element_type=F32)


def _dot_nt(a, b):
    return lax.dot_general(a, b, (((1,), (1,)), ((), ())), preferred_element_type=F32)


def _dot_tn(a, b):
    return lax.dot_general(a, b, (((0,), (0,)), ((), ())), preferred_element_type=F32)


MXU_COLS = 256


def _pack_w(w):
    return w.astype(BF16)


def _ncols(w):
    return sum(r.shape[1] for r in w) if isinstance(w, tuple) else w.shape[1]


def _dotw(a, w, cols=None):
    if not isinstance(w, tuple):
        return _dot(a, w[...] if cols is None else w[:, cols])
    width = w[0].shape[1]
    lo, hi = (0, width * len(w)) if cols is None else (cols.start, cols.stop)
    parts = []
    while lo < hi:
        k, off = divmod(lo, width)
        n = min(width - off, hi - lo)
        parts.append(_dot(a, w[k][:, off:off + n]))
        lo += n
    return parts[0] if len(parts) == 1 else jnp.concatenate(parts, axis=1)


def _dot_cols(a, w_ref, fn, width=MXU_COLS):
    n = w_ref.shape[1]
    width = min(width, n)
    blocks = [fn(_dotw(a, w_ref, slice(c, c + width))) for c in range(0, n, width)]
    return blocks[0] if len(blocks) == 1 else jnp.concatenate(blocks, axis=1)


def _rmsnorm(x, g):
    return x * lax.rsqrt(jnp.mean(x * x, axis=-1, keepdims=True) + EPS) * g


def _gelu(x):
    return 0.5 * x * (1.0 + lax.erf(x * SQRT_HALF))


def _sigmoid(x):
    return 0.5 * jnp.tanh(0.5 * x) + 0.5


def _silu(x):
    return x * _sigmoid(x)


def _bf16_parts(x):
    hi = x.astype(BF16)
    r1 = x - hi.astype(F32)
    mid = r1.astype(BF16)
    lo = (r1 - mid.astype(F32)).astype(BF16)
    return hi, mid, lo


def _split_by_replica(x, n_heads):
    hi, mid, lo = _bf16_parts(x)
    lane = lax.broadcasted_iota(jnp.int32, x.shape, 1)
    return jnp.where(lane < n_heads, hi, jnp.where(lane < 2 * n_heads, mid, lo))


def _cumsum_rows(tri, x):
    hi, mid, lo = _bf16_parts(x)
    return _dot(tri, hi) + _dot(tri, mid) + _dot(tri, lo)


def _memkv_kernel(x_ref, g_ref, w_ref, k_ref, v_ref, *, n_heads):
    tm = x_ref.shape[0]
    hd = k_ref.shape[-1]
    hn = _rmsnorm(x_ref[...], g_ref[...]).astype(BF16)
    kv = _dotw(hn, w_ref)
    for h in range(n_heads):
        k_ref[pl.ds(h, tm, stride=n_heads), :] = kv[:, h * hd:(h + 1) * hd]
        v_ref[pl.ds(h, tm, stride=n_heads), :] = kv[:, (n_heads + h) * hd:(n_heads + h + 1) * hd]


class _Cols(NamedTuple):
    mat: jax.Array
    start: int
    width: int

    @property
    def shape(self):
        return (self.mat.shape[0], self.width)


def _const_spec(op=None):
    if isinstance(op, _Cols):
        idx = op.start // op.width
        return pl.BlockSpec(op.shape, lambda *_: (0, idx), pipeline_mode=pl.Buffered(1))
    return pl.BlockSpec(memory_space=pltpu.VMEM)


def _flatten(ops):
    return [o for op in ops for o in (op if isinstance(op, list) else [op])]


def _const_specs(ops):
    return [_const_spec(op) for op in _flatten(ops)]


def _arrays(ops):
    return [op.mat if isinstance(op, _Cols) else op for op in _flatten(ops)]


def _regroup(kernel_fn, n_lead, ops):
    sizes = [len(op) if isinstance(op, list) else 0 for op in ops]

    def wrapped(*refs):
        refs = list(refs)
        args, pos = refs[:n_lead], n_lead
        for n in sizes:
            args.append(tuple(refs[pos:pos + n]) if n else refs[pos])
            pos += max(n, 1)
        return kernel_fn(*args, *refs[pos:])

    return wrapped


def _params(n_grid):
    return pltpu.CompilerParams(dimension_semantics=("arbitrary",) * n_grid, vmem_limit_bytes=VMEM_LIMIT_BYTES)


def _mem_kv(mem2d, g, w, tm):
    rows, d = mem2d.shape
    n_heads = w.shape[1] // 2 // X_HEAD_DIM
    out = jax.ShapeDtypeStruct((rows * n_heads, X_HEAD_DIM), F32)
    out_spec = pl.BlockSpec((tm * n_heads, X_HEAD_DIM), lambda i: (i, 0))
    return pl.pallas_call(
        functools.partial(_memkv_kernel, n_heads=n_heads),
        out_shape=(out, out),
        grid=(rows // tm,),
        in_specs=[pl.BlockSpec((tm, d), lambda i: (i, 0)), _const_spec(), _const_spec()],
        out_specs=(out_spec, out_spec),
        compiler_params=_params(1),
        name="mem_kv",
    )(mem2d, g, w)


def _x_kernel(x_ref, g_ref, wq_ref, wgx_ref, wmx_ref, k_ref, v_ref, wpx_ref, m_ref, *, nb, rows, head_dim):
    d = x_ref.shape[-1]
    x = x_ref[...].reshape(nb * rows, d)
    hn = _rmsnorm(x, g_ref[...]).astype(BF16)
    q = _dotw(hn, wq_ref)
    gate = _silu(_dotw(hn, wgx_ref))
    gm = _sigmoid(_dotw(hn, wmx_ref))
    n_heads = q.shape[1] // head_dim
    scale = head_dim ** -0.5
    outs = []
    n_mem = k_ref.shape[1] // n_heads
    for b in range(nb):
        qb = q[b * rows:(b + 1) * rows].astype(BF16)
        heads = range(n_heads)
        kb = [k_ref[b, pl.ds(h, n_mem, stride=n_heads), :].astype(BF16) for h in heads]
        vb = [v_ref[b, pl.ds(h, n_mem, stride=n_heads), :].astype(BF16) for h in heads]
        scores = [_dot_nt(qb[:, h * head_dim:(h + 1) * head_dim], kb[h]) * scale for h in heads]
        exps = [jnp.exp(s - jnp.max(s, axis=-1, keepdims=True)) for s in scores]
        probs = [(e / jnp.sum(e, axis=-1, keepdims=True)).astype(BF16) for e in exps]
        outs.append(jnp.concatenate([_dot(probs[h], vb[h]) for h in heads], axis=-1))
    o = outs[0] if nb == 1 else jnp.concatenate(outs, axis=0)
    hx = (o * gate).astype(BF16)
    m = gm * _dotw(hx, wpx_ref)
    m_ref[...] = m.reshape(nb, rows, d)


def _xs_kernel(x_ref, g_ref, wq_ref, wgx_ref, wmx_ref, k_ref, v_ref, wpx_ref, m_ref, *, nb, rows, head_dim):
    d = x_ref.shape[-1]
    x = x_ref[...].reshape(nb * rows, d)
    hn = _rmsnorm(x, g_ref[...]).astype(BF16)
    q = _dotw(hn, wq_ref)
    gate = _silu(_dotw(hn, wgx_ref))
    gm = _sigmoid(_dotw(hn, wmx_ref))
    n_heads = q.shape[1] // head_dim
    n_kv = k_ref.shape[1]
    scale = head_dim ** -0.5
    row_head = lax.broadcasted_iota(jnp.int32, (n_heads * rows, n_kv), 0) // rows
    col_head = lax.broadcasted_iota(jnp.int32, (n_heads * rows, n_kv), 1) % n_heads
    same_head = row_head == col_head
    seqs = range(nb)
    scores = []
    for b in seqs:
        qb = q[b * rows:(b + 1) * rows]
        q_heads = jnp.concatenate([qb[:, h * head_dim:(h + 1) * head_dim] for h in range(n_heads)], axis=0)
        scores.append(_dot_nt(q_heads.astype(BF16), k_ref[b].astype(BF16)))
    masked = [jnp.where(same_head, s * scale, -jnp.inf) for s in scores]
    exps = [jnp.exp(s - jnp.max(s, axis=-1, keepdims=True)) for s in masked]
    probs = [(e / jnp.sum(e, axis=-1, keepdims=True)).astype(BF16) for e in exps]
    o_heads = [_dot(probs[b], v_ref[b].astype(BF16)) for b in seqs]
    outs = [jnp.concatenate([oh[h * rows:(h + 1) * rows] for h in range(n_heads)], axis=1) for oh in o_heads]
    o = outs[0] if nb == 1 else jnp.concatenate(outs, axis=0)
    hx = (o * gate).astype(BF16)
    m = gm * _dotw(hx, wpx_ref)
    m_ref[...] = m.reshape(nb, rows, d)


def _branch_x(x3, g, wq, wgx, wmx, k3, v3, wpx, *, nb, rows, heads_in_rows):
    bsz, seq, d = x3.shape
    n_mem, dx = k3.shape[1], k3.shape[2]
    kern = functools.partial(_xs_kernel if heads_in_rows else _x_kernel, nb=nb, rows=rows, head_dim=X_HEAD_DIM)
    return pl.pallas_call(
        kern,
        out_shape=jax.ShapeDtypeStruct((bsz, seq, d), F32),
        grid=(bsz // nb, seq // rows),
        in_specs=[pl.BlockSpec((nb, rows, d), lambda i, j: (i, j, 0))] + _const_specs([g, wq, wgx, wmx]) + [
            pl.BlockSpec((nb, n_mem, dx), lambda i, j: (i, 0, 0)),
            pl.BlockSpec((nb, n_mem, dx), lambda i, j: (i, 0, 0)),
            _const_spec(wpx),
        ],
        out_specs=pl.BlockSpec((nb, rows, d), lambda i, j: (i, j, 0)),
        compiler_params=_params(2),
        name="branch_x",
    )(x3, *_arrays([g, wq, wgx, wmx]), k3, v3, wpx)


def _a_kernel(x_ref, g_ref, wu_ref, wv_ref, wga_ref, wma_ref, lng_ref, lnb_ref, ws_ref, bs_ref, wpa_ref, min_ref,
              mout_ref, *maybe_v_ref, chunk, groups, sub):
    tm, d = x_ref.shape
    gd = d // groups
    bias = bs_ref[...]
    for t in range(tm // sub):
        ts = slice(t * sub, (t + 1) * sub)
        hn = _rmsnorm(x_ref[ts, :], g_ref[...]).astype(BF16)
        v = _dot_cols(hn, wv_ref, _gelu)
        u = _dot_cols(hn, wu_ref, _gelu)
        vc = v - jnp.mean(v, axis=-1, keepdims=True)
        vn = vc * lax.rsqrt(jnp.mean(vc * vc, axis=-1, keepdims=True) + EPS) * lng_ref[...] + lnb_ref[...]
        if maybe_v_ref:
            maybe_v_ref[0][ts, :] = vn
        ga = _dot_cols(hn, wga_ref, _silu)
        vb = vn.astype(BF16)
        n_c = sub // chunk
        wide = [_dot(ws_ref[gi], jnp.concatenate([vb[c * chunk:(c + 1) * chunk, gi * gd:(gi + 1) * gd]
                                                  for c in range(n_c)], axis=1)) for gi in range(groups)]
        mixed = [jnp.concatenate([wg[:, c * gd:(c + 1) * gd] for wg in wide], axis=1) + bias for c in range(n_c)]
        s = mixed[0] if n_c == 1 else jnp.concatenate(mixed, axis=0)
        ha = (u * s * ga).astype(BF16)
        gm = _sigmoid(_dotw(hn, wma_ref))
        mout_ref[ts, :] = min_ref[ts, :] + gm * _dotw(ha, wpa_ref)


def _branch_a(x2, g, wu, wv, wga, wma, lng, lnb, ws, bs, wpa, m_in, *, tm, want_v):
    rows, d = x2.shape
    row_spec = pl.BlockSpec((tm, d), lambda i: (i, 0))
    out_shape = [jax.ShapeDtypeStruct((rows, d), F32)]
    out_specs = [row_spec]
    if want_v:
        out_shape.append(jax.ShapeDtypeStruct((rows, d), F32))
        out_specs.append(row_spec)
    kern = functools.partial(_a_kernel, chunk=CHUNK, groups=A_GROUPS, sub=min(tm, 256))
    consts = [g, wu, wv, wga, wma, lng, lnb, ws, bs, wpa]
    res = pl.pallas_call(
        kern,
        out_shape=tuple(out_shape),
        grid=(rows // tm,),
        in_specs=[row_spec] + _const_specs(consts) + [row_spec],
        out_specs=tuple(out_specs),
        compiler_params=_params(1),
        name="branch_a",
    )(x2, *_arrays(consts), m_in)
    return res if want_v else (res[0], None)


def _finish_b(x, y, g_ref, wz_ref, wmb_ref, ng_ref, wpb_ref, m_in, wout_ref, fg_ref, *, n_groups, final_norm):
    hn = _rmsnorm(x, g_ref[...]).astype(BF16)
    gm = _sigmoid(_dotw(hn, wmb_ref))
    yf = y * _silu(_dotw(hn, wz_ref))
    gw = yf.shape[1] // n_groups
    parts = []
    for gi in range(n_groups):
        yg = yf[:, gi * gw:(gi + 1) * gw]
        parts.append(yg * lax.rsqrt(jnp.mean(yg * yg, axis=-1, keepdims=True) + EPS))
    hb = (jnp.concatenate(parts, axis=-1) * ng_ref[...]).astype(BF16)
    m = m_in + gm * _dotw(hb, wpb_ref)
    out = x + _dotw(m.astype(BF16), wout_ref)
    return _rmsnorm(out, fg_ref[...]) if final_norm else out


HANDOFF = ("xs", "bm", "cm", "zs", "dt", "gm", "tail")


def _bp_stage1(x_ref, g_ref, wz_ref, wxbc_ref, wdt_ref, wmb_ref, cw_ref, cb_ref, dtb_ref, perm_ref, xs_scr, out,
               first_tile, *, tm, chunk, d_inner, bc_width, conv_w):
    pad = SUBLANES
    vpc = chunk // pad
    n_chunks = tm // chunk
    width = MXU_COLS
    hn = _rmsnorm(x_ref[0], g_ref[...]).astype(BF16)
    for c0 in range(0, out["gm"].shape[1], width):
        cols = slice(c0, c0 + width)
        out["gm"][:, cols] = _sigmoid(_dotw(hn, wmb_ref, cols))
        yield
    hn = _dot(perm_ref[0], hn).astype(BF16)
    out["dt"][...] = jax.nn.softplus(_dotw(hn, wdt_ref) + dtb_ref[...])
    yield
    for c0 in range(0, d_inner, width):
        cols = slice(c0, c0 + width)
        out["zs"][:, cols] = _silu(_dotw(hn, wz_ref, cols))
        yield
    first_sublane = lax.broadcasted_iota(jnp.int32, (pad, width), 0) == 0
    for c0 in range(0, _ncols(wxbc_ref), width):
        cols = slice(c0, c0 + width)
        raw_all = _dotw(hn, wxbc_ref, cols)
        prev_tail = jnp.where(first_tile, 0.0, xs_scr[:, cols])
        acts = []
        for c in range(n_chunks):
            raw = raw_all[c * chunk:(c + 1) * chunk]
            conv = cb_ref[:, cols] + raw * cw_ref[conv_w - 1:conv_w, cols]
            for back in range(1, conv_w):
                head = []
                for v in range(back):
                    w = vpc - back + v
                    cur = pltpu.roll(raw[w * pad:(w + 1) * pad], 1, axis=0)
                    prv = pltpu.roll(prev_tail[(w - vpc + conv_w - 1) * pad:(w - vpc + conv_w) * pad], 1, axis=0)
                    head.append(jnp.where(first_sublane, prv, cur))
                shifted = jnp.concatenate(head + [raw[:chunk - back * pad]], axis=0)
                conv = conv + shifted * cw_ref[conv_w - 1 - back:conv_w - back, cols]
            acts.append(_silu(conv))
            prev_tail = raw[chunk - (conv_w - 1) * pad:]
        xs_scr[:, cols] = prev_tail
        out["tail"][:, cols] = prev_tail
        act = acts[0] if n_chunks == 1 else jnp.concatenate(acts, axis=0)
        if c0 < d_inner:
            out["xs"][:, cols] = act
        elif c0 < d_inner + bc_width:
            out["bm"][:, c0 - d_inner:c0 - d_inner + width] = act.astype(BF16)
        else:
            out["cm"][:, c0 - d_inner - bc_width:c0 - d_inner - bc_width + width] = act.astype(BF16)
        yield


def _bp_stage2(x_ref, min_ref, alog_ref, dsk_ref, ng_ref, wpb_ref, wout_ref, fg_ref, tri_ref, e3_ref, perm_ref,
               y_ref, st_scr, inp, first_tile, *, tm, chunk, n_groups, n_heads, head_dim, n_state, final_norm):
    pad = SUBLANES
    vpc = chunk // pad
    hpg = n_heads // n_groups
    gw = hpg * head_dim
    width = MXU_COLS

    def token_of(q):
        return (q % pad) * vpc + q // pad

    li = token_of(lax.broadcasted_iota(jnp.int32, (chunk, chunk), 0))
    si = token_of(lax.broadcasted_iota(jnp.int32, (chunk, chunk), 1))
    causal = li >= si
    head_of_lane = lax.broadcasted_iota(jnp.int32, (chunk, gw), 1) // head_dim
    tri = tri_ref[...]
    e3 = e3_ref[...]
    a_row = -jnp.exp(alog_ref[...])

    hb_rows = []
    for c in range(tm // chunk):
        rs = slice(c * chunk, (c + 1) * chunk)
        dt = inp["dt"][rs, :]
        acs = _cumsum_rows(tri, dt * a_row)
        acs_t = acs.T
        acs_e = _dot(_split_by_replica(acs, n_heads).astype(BF16), e3)
        a_last = acs_e[chunk - 1:chunk, :]
        xs_c = inp["xs"][rs, :]
        xdt_c = xs_c * _dot(_split_by_replica(dt, n_heads).astype(BF16), e3)
        xdt_b = xdt_c.astype(BF16)
        xd_state = (xdt_c * jnp.exp(a_last - acs_e)).astype(BF16)
        decay_in = jnp.exp(acs_e)
        st_prev = st_scr[...]
        if c == 0:
            st_prev = jnp.where(first_tile, 0.0, st_prev)
        st_prev_b = st_prev.astype(BF16)
        yield
        groups = range(n_groups)
        n_sl = [slice(gi * n_state, (gi + 1) * n_state) for gi in groups]
        g_sl = [slice(gi * gw, (gi + 1) * gw) for gi in groups]
        scores = [_dot_nt(inp["cm"][rs, n_sl[gi]], inp["bm"][rs, n_sl[gi]]) for gi in groups]
        yield
        y_offs = [_dot(inp["cm"][rs, n_sl[gi]], st_prev_b[:, g_sl[gi]]) for gi in groups]
        st_parts = [_dot_tn(inp["bm"][rs, n_sl[gi]], xd_state[:, g_sl[gi]]) for gi in groups]
        st_scr[...] = st_prev * jnp.exp(a_last) + jnp.concatenate(st_parts, axis=1)
        yield
        y_diags = []
        for gi in groups:
            m_heads, x_blocks = [], []
            xg = xdt_b[:, g_sl[gi]]
            for r in range(hpg):
                h = gi * hpg + r
                diff = acs[:, h:h + 1] - acs_t[h:h + 1, :]
                decay = jnp.exp(jnp.where(causal, diff, -jnp.inf))
                m_heads.append((scores[gi] * decay).astype(BF16))
                x_blocks.append(jnp.where(head_of_lane == r, xg, jnp.zeros_like(xg)))
            y_diags.append(_dot(jnp.concatenate(m_heads, axis=1), jnp.concatenate(x_blocks, axis=0)))
            yield
        hb_parts = []
        for gi in groups:
            gs = g_sl[gi]
            y = y_diags[gi] + y_offs[gi] * decay_in[:, gs] + xs_c[:, gs] * dsk_ref[:, gs]
            yf = y * inp["zs"][rs, gs]
            yn = yf * lax.rsqrt(jnp.mean(yf * yf, axis=-1, keepdims=True) + EPS)
            hb_parts.append((yn * ng_ref[:, gs]).astype(BF16))
            if gi % 2:
                yield
        hb_rows.append(jnp.concatenate(hb_parts, axis=1))
    hb = hb_rows[0] if len(hb_rows) == 1 else jnp.concatenate(hb_rows, axis=0)
    hb = _dot(perm_ref[1], hb).astype(BF16)
    yield
    m_parts = []
    for c0 in range(0, wpb_ref.shape[1], width):
        cols = slice(c0, c0 + width)
        m_parts.append((min_ref[0, :, cols] + inp["gm"][:, cols] * _dotw(hb, wpb_ref, cols)).astype(BF16))
        yield
    m = jnp.concatenate(m_parts, axis=1)
    out = x_ref[0] + _dotw(m, wout_ref)
    y_ref[0] = _rmsnorm(out, fg_ref[...]) if final_norm else out


def _bp_kernel(x1_ref, x2_ref, min_ref, g_ref, wz_ref, wxbc_ref, wdt_ref, wmb_ref, cw_ref, cb_ref, dtb_ref, alog_ref,
               dsk_ref, ng_ref, wpb_ref, wout_ref, fg_ref, tri_ref, e3_ref, perm_ref,
               y_ref, ssm_ref, conv_ref, xs_scr, st_scr, *slot_refs,
               nj, n_tiles, tm, chunk, n_groups, n_heads, head_dim, n_state, conv_w, final_norm):
    t = pl.program_id(0)
    d_inner = n_heads * head_dim
    pad = SUBLANES
    tile1 = jnp.minimum(t, n_tiles - 1)
    tile2 = jnp.maximum(t - 1, 0)
    n_h = len(HANDOFF)
    slots = [dict(zip(HANDOFF, slot_refs[i * n_h:(i + 1) * n_h])) for i in range(2)]

    @pl.when(t == 0)
    def _init():
        for ref in slots[1].values():
            ref[...] = jnp.zeros_like(ref)
        st_scr[...] = jnp.zeros_like(st_scr)
        xs_scr[...] = jnp.zeros_like(xs_scr)

    def step(slot_w, slot_r):
        s1 = _bp_stage1(x1_ref, g_ref, wz_ref, wxbc_ref, wdt_ref, wmb_ref, cw_ref, cb_ref, dtb_ref, perm_ref, xs_scr,
                        slot_w, tile1 % nj == 0, tm=tm, chunk=chunk, d_inner=d_inner,
                        bc_width=n_groups * n_state, conv_w=conv_w)
        s2 = _bp_stage2(x2_ref, min_ref, alog_ref, dsk_ref, ng_ref, wpb_ref, wout_ref, fg_ref, tri_ref, e3_ref,
                        perm_ref, y_ref, st_scr, slot_r, tile2 % nj == 0, tm=tm, chunk=chunk, n_groups=n_groups,
                        n_heads=n_heads, head_dim=head_dim, n_state=n_state, final_norm=final_norm)
        live = [s2, s1]
        while live:
            for gen in list(live):
                if next(gen, StopIteration) is StopIteration:
                    live.remove(gen)

        @pl.when((t > 0) & (tile2 % nj == nj - 1))
        def _seq_end():
            tail = slot_r["tail"][...]
            conv_ref[0] = jnp.concatenate([tail[(v + 1) * pad - 1:(v + 1) * pad] for v in range(conv_w - 1)], axis=0)
            ssm_ref[0] = st_scr[...].T

    for parity in range(2):
        pl.when(t % 2 == parity)(functools.partial(step, slots[parity], slots[1 - parity]))


def _branch_b_prompt(x3, m3, w, *, tm, final_norm):
    bsz, seq, d = x3.shape
    conv_dim = w["cw"].shape[1]
    d_inner = w["ng"].shape[1]
    n_heads = d_inner // SSD_HEAD_DIM
    conv_w = w["cw"].shape[0]
    nj = seq // tm
    n_tiles = bsz * nj
    bc_width = SSD_GROUPS * SSD_STATE
    kern = functools.partial(_bp_kernel, nj=nj, n_tiles=n_tiles, tm=tm, chunk=CHUNK, n_groups=SSD_GROUPS,
                             n_heads=n_heads, head_dim=SSD_HEAD_DIM, n_state=SSD_STATE, conv_w=conv_w,
                             final_norm=final_norm)

    def tile1(t):
        return jnp.minimum(t, n_tiles - 1)

    def tile2(t):
        return jnp.maximum(t - 1, 0)

    spec1 = pl.BlockSpec((1, tm, d), lambda t: (tile1(t) // nj, tile1(t) % nj, 0))
    spec2 = pl.BlockSpec((1, tm, d), lambda t: (tile2(t) // nj, tile2(t) % nj, 0))
    consts = [w[k] for k in ("g", "wz", "wxbc", "wdt", "wmb", "cw", "cb", "dtb", "alog", "dsk", "ng", "wpb", "wout",
                             "fg", "tri", "e3")]
    tok = (np.arange(tm) // CHUNK) * CHUNK + np.tile(_chunk_tokens(), tm // CHUNK)
    p = (tok[:, None] == np.arange(tm)[None, :]).astype(np.float32)
    perm = jnp.asarray(np.stack([p, p.T]), dtype=BF16)
    handoff = {"xs": ((tm, d_inner), F32), "bm": ((tm, bc_width), BF16), "cm": ((tm, bc_width), BF16),
               "zs": ((tm, d_inner), F32), "dt": ((tm, LANES), F32), "gm": ((tm, d), F32),
               "tail": (((conv_w - 1) * SUBLANES, conv_dim), F32)}
    return pl.pallas_call(
        _regroup(kern, 3, consts),
        out_shape=(jax.ShapeDtypeStruct((bsz, seq, d), F32),
                   jax.ShapeDtypeStruct((bsz, d_inner, SSD_STATE), F32),
                   jax.ShapeDtypeStruct((bsz, conv_w - 1, conv_dim), F32)),
        grid=(n_tiles + 1,),
        in_specs=[spec1, spec2, spec2] + _const_specs(consts + [perm]),
        out_specs=(spec2,
                   pl.BlockSpec((1, d_inner, SSD_STATE), lambda t: (tile2(t) // nj, 0, 0)),
                   pl.BlockSpec((1, conv_w - 1, conv_dim), lambda t: (tile2(t) // nj, 0, 0))),
        scratch_shapes=([pltpu.VMEM(((conv_w - 1) * SUBLANES, conv_dim), F32), pltpu.VMEM((SSD_STATE, d_inner), F32)]
                        + [pltpu.VMEM(*handoff[k]) for _ in range(2) for k in HANDOFF]),
        compiler_params=_params(1),
        name="branch_b_prompt",
    )(x3, x3, m3, *_arrays(consts), perm)


def _bs_kernel(x_ref, g_ref, wxbc_ref, wdt_ref, cw_ref, cb_ref, dtb_ref, alog_ref, dsk_ref,
               tri_ref, e3_ref, ecol_ref, conv0_ref, h0_ref,
               y_ref, hnew_ref, convnew_ref, xs_scr, tr_scr,
               *, nb, rows, n_groups, n_heads, head_dim, n_state, conv_w):
    hpg = n_heads // n_groups
    gw = hpg * head_dim
    d_inner = n_heads * head_dim
    d = x_ref.shape[-1]
    r_all = nb * rows
    hs_w = n_heads * rows

    @pl.when(pl.program_id(0) == 0)
    def _init():
        tr_scr[...] = jnp.zeros_like(tr_scr)
        xs_scr[...] = jnp.zeros_like(xs_scr)

    x = x_ref[...].reshape(r_all, d)
    hn = _rmsnorm(x, g_ref[...]).astype(BF16)
    xbc_raw = _dotw(hn, wxbc_ref)
    dt = jax.nn.softplus(_dotw(hn, wdt_ref) + dtb_ref[...])

    raw = xbc_raw.reshape(nb, rows, xbc_raw.shape[1])
    xs_scr[:, rows - (conv_w - 1):, :] = conv0_ref[...]
    state = xs_scr[...]
    row_in_seq = lax.broadcasted_iota(jnp.int32, raw.shape, 1)
    conv = cb_ref[...] + raw * cw_ref[conv_w - 1:conv_w, :]
    for back in range(1, conv_w):
        shifted = jnp.where(row_in_seq < back, pltpu.roll(state, back, axis=1), pltpu.roll(raw, back, axis=1))
        conv = conv + shifted * cw_ref[conv_w - 1 - back:conv_w - back, :]
    convnew_ref[...] = raw[:, rows - (conv_w - 1):, :]
    xbc = _silu(conv.reshape(r_all, xbc_raw.shape[1]))
    xs = xbc[:, :d_inner]
    bm = xbc[:, d_inner:d_inner + n_groups * n_state]
    cm = xbc[:, d_inner + n_groups * n_state:]

    a_row = -jnp.exp(alog_ref[...])
    acs = _cumsum_rows(tri_ref[...], dt * a_row)
    e3 = e3_ref[...]
    acs_split = _split_by_replica(acs, n_heads).astype(BF16)
    acs_e = _dot(acs_split, e3)
    xdt = xs * _dot(_split_by_replica(dt, n_heads).astype(BF16), e3)
    decay_in = jnp.exp(acs_e)

    acs_col = _dot(acs_split, ecol_ref[...])
    l_idx = lax.broadcasted_iota(jnp.int32, (r_all, hs_w), 0) % rows
    s_idx = lax.broadcasted_iota(jnp.int32, (r_all, hs_w), 1) % rows
    on_diag = jnp.where(l_idx == s_idx, acs_col, 0.0).reshape(nb, rows, hs_w)
    acs_row = jnp.broadcast_to(jnp.sum(on_diag, axis=1, keepdims=True), (nb, rows, hs_w)).reshape(r_all, hs_w)
    decay = jnp.exp(jnp.where(l_idx >= s_idx, acs_col - acs_row, -jnp.inf))

    slot = LANES // nb
    ones_rows = lax.broadcasted_iota(jnp.int32, (slot, n_state), 0)
    ones_blk = jnp.where((ones_rows >= rows) & (ones_rows < rows + 3), 1.0, 0.0)

    def seq_rows(b):
        return slice(b * rows, (b + 1) * rows)

    lane_in_tile = lax.broadcasted_iota(jnp.int32, (rows, LANES), 1)

    def expand_rows(v, start_of, width):
        c = v.shape[1]
        blocks = []
        for h in range(n_heads):
            start = start_of(h)
            lo = start // LANES * LANES
            hi = max(lo + LANES, start + width)
            piece = v[:, lo:hi]
            if width < LANES:
                piece = jnp.where(lane_in_tile // width == (start - lo) // width, piece, 0.0)
            parts = [jnp.zeros((rows, lo), F32)] if lo else []
            parts.append(piece)
            if hi < c:
                parts.append(jnp.zeros((rows, c - hi), F32))
            blocks.append(jnp.concatenate(parts, axis=1) if len(parts) > 1 else piece)
        return jnp.concatenate(blocks, axis=0).astype(BF16)

    y_diags = []
    for b0 in range(0, nb, SEQ_BATCH):
        seqs = range(b0, min(b0 + SEQ_BATCH, nb))
        b_exps = [expand_rows(bm[seq_rows(b)], lambda h: h // hpg * n_state, n_state) for b in seqs]
        scores = [_dot_nt(cm[seq_rows(b)].astype(BF16), be) for b, be in zip(seqs, b_exps)]
        x_blks = [expand_rows(xdt[seq_rows(b)], lambda h: h * head_dim, head_dim) for b in seqs]
        m_alls = [(s * decay[seq_rows(b)]).astype(BF16) for b, s in zip(seqs, scores)]
        y_diags += [_dot(m, xb) for m, xb in zip(m_alls, x_blks)]
    for b in range(nb):
        a_last = acs_e[(b + 1) * rows - 1:(b + 1) * rows, :]
        d_hi, d_mid, d_lo = _bf16_parts(jnp.exp(a_last))
        tr_scr[b * slot:b * slot + rows, :] = xdt[seq_rows(b)] * jnp.exp(a_last - acs_e[seq_rows(b)])
        tr_scr[b * slot + rows:b * slot + rows + 1, :] = d_hi.astype(F32)
        tr_scr[b * slot + rows + 1:b * slot + rows + 2, :] = d_mid.astype(F32)
        tr_scr[b * slot + rows + 2:b * slot + rows + 3, :] = d_lo.astype(F32)
    tr_t = tr_scr[...].T.astype(BF16)

    ys = []
    for b in range(nb):
        rs = slice(b * rows, (b + 1) * rows)
        bm_b, cm_bb = bm[rs], cm[rs].astype(BF16)
        y_parts = []
        for gi in range(n_groups):
            ns = slice(gi * n_state, (gi + 1) * n_state)
            gs = slice(gi * gw, (gi + 1) * gw)
            h0_g = h0_ref[b, gs, :]
            y_parts.append(_dot_nt(cm_bb[:, ns], h0_g.astype(BF16)))
            w_seq = jnp.concatenate(
                [jnp.concatenate([bm_b[:, ns], jnp.zeros((slot - rows, n_state), F32)], axis=0), ones_blk], axis=1)
            pieces = ([jnp.zeros((b * slot, 2 * n_state), F32)] if b else []) + [w_seq]
            if b < nb - 1:
                pieces.append(jnp.zeros((LANES - (b + 1) * slot, 2 * n_state), F32))
            upd = _dot(tr_t[gs, :], jnp.concatenate(pieces, axis=0).astype(BF16))
            hnew_ref[b, gs, :] = h0_g * upd[:, n_state:] + upd[:, :n_state]
        ys.append(y_diags[b] + jnp.concatenate(y_parts, axis=1) * decay_in[rs])
    y = jnp.concatenate(ys, axis=0) + xs * dsk_ref[...]
    y_ref[...] = y.reshape(nb, rows, d_inner)


def _bfin_kernel(x_ref, y_ref, min_ref, g_ref, wz_ref, wmb_ref, ng_ref, wpb_ref, wout_ref, fg_ref, out_ref,
                 *, n_groups, final_norm):
    out_ref[...] = _finish_b(x_ref[...], y_ref[...], g_ref, wz_ref, wmb_ref, ng_ref, wpb_ref, min_ref[...],
                             wout_ref, fg_ref, n_groups=n_groups, final_norm=final_norm)


def _branch_b_sample(x3, m3, conv0, h0, w, *, layer, nb, tm, final_norm):
    bsz, rows, d = x3.shape
    conv_dim = w["cw"].shape[1]
    d_inner = w["ng"].shape[1]
    n_heads = d_inner // SSD_HEAD_DIM
    conv_w = w["cw"].shape[0]
    kern = functools.partial(_bs_kernel, nb=nb, rows=rows, n_groups=SSD_GROUPS, n_heads=n_heads,
                             head_dim=SSD_HEAD_DIM, n_state=SSD_STATE, conv_w=conv_w)

    def seq_spec(width):
        return pl.BlockSpec((nb, rows, width), lambda i: (i, 0, 0))

    conv_spec = pl.BlockSpec((nb, conv_w - 1, conv_dim), lambda i: (i, 0, 0))
    h_spec = pl.BlockSpec((nb, d_inner, SSD_STATE), lambda i: (i, 0, 0))
    consts = [w[k] for k in ("g", "wxbc", "wdt", "cw", "cb", "dtb", "alog", "dsk", "tri_s", "e3", "ecol")]
    y, h_new, conv_new = pl.pallas_call(
        _regroup(kern, 1, consts),
        out_shape=(jax.ShapeDtypeStruct((bsz, rows, d_inner), F32),
                   jax.ShapeDtypeStruct((bsz, d_inner, SSD_STATE), F32),
                   jax.ShapeDtypeStruct((bsz, conv_w - 1, conv_dim), F32)),
        grid=(bsz // nb,),
        in_specs=[seq_spec(d)] + _const_specs(consts) + [
            pl.BlockSpec((None, nb, conv_w - 1, conv_dim), lambda i: (layer, i, 0, 0)), h_spec],
        out_specs=(seq_spec(d_inner), h_spec, conv_spec),
        scratch_shapes=[pltpu.VMEM((nb, SUBLANES, conv_dim), F32), pltpu.VMEM((LANES, d_inner), F32)],
        compiler_params=_params(1),
        name="branch_b_sample",
    )(x3, *_arrays(consts), conv0, h0)

    n_rows = bsz * rows
    fin_consts = [w[k] for k in ("g", "wz", "wmb", "ng", "wpb", "wout", "fg")]

    def rows_spec(width):
        return pl.BlockSpec((tm, width), lambda i: (i, 0))

    out = pl.pallas_call(
        _regroup(functools.partial(_bfin_kernel, n_groups=SSD_GROUPS, final_norm=final_norm), 3, fin_consts),
        out_shape=jax.ShapeDtypeStruct((n_rows, d), F32),
        grid=(n_rows // tm,),
        in_specs=[rows_spec(d), rows_spec(d_inner), rows_spec(d)] + _const_specs(fin_consts),
        out_specs=rows_spec(d),
        compiler_params=_params(1),
        name="branch_b_sample_out",
    )(x3.reshape(n_rows, d), y.reshape(n_rows, d_inner), m3.reshape(n_rows, d), *_arrays(fin_consts))
    return out.reshape(bsz, rows, d), h_new, conv_new


def _tile(n, pref):
    return pref if n % pref == 0 else n


def _chunk_tokens():
    q = np.arange(CHUNK)
    return (q % SUBLANES) * (CHUNK // SUBLANES) + q // SUBLANES


def _head_expand(n_heads, width, n_rep):
    j = np.arange(LANES)[:, None]
    c = np.arange(n_heads * width)[None, :]
    return jnp.asarray(((j < n_rep * n_heads) & (j % n_heads == c // width)).astype(np.float32), dtype=BF16)


def _layer_weights(l, d, dec_rows, nb_s, norm_g, w_in, conv_w, conv_b, dt_bias, a_log, d_skip, ssd_norm_g, ln_v_g,
                   ln_v_b, w_spatial, b_spatial, w_proj_a, w_proj_b, w_proj_x, w_out, final_norm_g):
    d_a = d
    d_inner = w_proj_b.shape[1]
    conv_dim = conv_w.shape[2]
    n_heads = a_log.shape[1]
    d_x = w_proj_x.shape[1]
    sizes = (d_a, d_a, d_a, d_inner, conv_dim, n_heads, d_x, d_x, N_BRANCH * d)
    offs = np.concatenate([[0], np.cumsum(sizes)])
    wl = w_in[l]
    sec = lambda i: wl[:, offs[i]:offs[i + 1]]
    row = lambda v: v.reshape(1, -1).astype(F32)
    rep3 = lambda v, fill: jnp.concatenate(
        [v] * DT_REPLICAS + [jnp.full(v.shape[:-1] + (LANES - DT_REPLICAS * v.shape[-1],), fill, v.dtype)], axis=-1)

    tril = jnp.tril(jnp.ones((CHUNK, CHUNK), F32))
    ws_p = jnp.where(tril[None] > 0, w_spatial[l], 0.0)
    bs_p = jnp.repeat(b_spatial[l].T, d_a // A_GROUPS, axis=1)
    n_seq = CHUNK // dec_rows
    sel = jnp.asarray(np.tile(np.eye(dec_rows, dtype=np.float32), (n_seq, 1)))
    blocks = jnp.asarray(np.kron(np.eye(n_seq, dtype=np.float32), np.ones((dec_rows, dec_rows), np.float32)))
    ws_s = jnp.einsum("ia,gab,jb->gij", sel, ws_p[:, :dec_rows, :dec_rows], sel,
                      precision=lax.Precision.HIGHEST) * blocks
    bs_s = jnp.tile(bs_p[:dec_rows], (n_seq, 1))

    w = {
        "g": row(norm_g[l]),
        "lng": row(ln_v_g[l]), "lnb": row(ln_v_b[l]),
        "ws_p": ws_p.astype(BF16), "bs_p": bs_p, "ws_s": ws_s.astype(BF16), "bs_s": bs_s,
        "wpa": _pack_w(w_proj_a[l]),
        "cw": conv_w[l].astype(F32), "cb": row(conv_b[l]),
        "dtb": rep3(row(dt_bias[l]), 0.0), "alog": rep3(row(a_log[l]), 0.0),
        "dsk": jnp.repeat(row(d_skip[l]), SSD_HEAD_DIM, axis=1),
        "ng": row(ssd_norm_g[l]),
        "wpb": _pack_w(w_proj_b[l]),
        "wout": _pack_w(w_out[l]), "fg": row(final_norm_g),
        "tri": jnp.asarray(_chunk_tokens()[:, None] >= _chunk_tokens()[None, :], dtype=BF16),
        "tri_s": jnp.asarray(np.kron(np.eye(nb_s), np.tril(np.ones((dec_rows, dec_rows)))), dtype=BF16),
        "e3": _head_expand(n_heads, SSD_HEAD_DIM, DT_REPLICAS),
        "ecol": _head_expand(n_heads, dec_rows, DT_REPLICAS),
        "wpx": _pack_w(w_proj_x[l]),
    }
    w_bf = _pack_w(wl)
    tail = w_bf[:, offs[6]:]
    assert all(o % d == 0 for o in offs[:5]) and d_x % LANES == 0 and d % d_x == 0
    for name, i in (("wu", 0), ("wv", 1), ("wga", 2), ("wz", 3), ("wxbc", 4)):
        windows = [_Cols(w_bf, c, d) for c in range(offs[i], offs[i + 1], d)]
        w[name] = windows if len(windows) > 1 else windows[0]
    w["wdt"] = _pack_w(rep3(sec(5), 0.0))
    w["wq"], w["wgx"] = _Cols(tail, 0, d_x), _Cols(tail, d_x, d_x)
    for k, name in enumerate(("wma", "wmb", "wmx")):
        w[name] = _Cols(tail, 2 * d_x + k * d, d)
    return w


def _layer(x3, k3, v3, w, *, nb_x, rows_x, heads_in_rows, tm_a, ws, bs, want_v, b_fn):
    bsz, seq, d = x3.shape
    m = _branch_x(x3, w["g"], w["wq"], w["wgx"], w["wmx"], k3, v3, w["wpx"], nb=nb_x, rows=rows_x,
                  heads_in_rows=heads_in_rows)
    m, vn = _branch_a(x3.reshape(bsz * seq, d), w["g"], w["wu"], w["wv"], w["wga"], w["wma"], w["lng"], w["lnb"],
                      ws, bs, w["wpa"], m.reshape(bsz * seq, d), tm=tm_a, want_v=want_v)
    return b_fn(x3, m.reshape(bsz, seq, d)), vn


def kernel(x_prompt, x_sample, mem_prompt, cache_mem_k, cache_mem_v, state_ssm, state_conv, norm_g, w_in, conv_w,
           conv_b, dt_bias, a_log, d_skip, ssd_norm_g, ln_v_g, ln_v_b, w_spatial, b_spatial, mem_norm_g, w_mem_kv,
           w_proj_a, w_proj_b, w_proj_x, w_out, final_norm_g):
    depth = w_in.shape[0]
    bsz, seq, d = x_prompt.shape
    dec_b, dec_rows, _ = x_sample.shape
    n_mem = mem_prompt.shape[1]
    d_x = w_proj_x.shape[1]
    d_inner = w_proj_b.shape[1]
    n_heads = a_log.shape[1]
    assert seq % CHUNK == 0 and CHUNK % dec_rows == 0 and dec_rows == SUBLANES
    assert DT_REPLICAS * n_heads <= LANES and n_heads * dec_rows % LANES == 0

    nb_s = _tile(dec_b, 8)
    nb_xs = _tile(dec_b, 16)
    yp, ys = x_prompt, x_sample
    outs = {k: [] for k in ("mk", "mv", "hp", "cp", "hs", "cs", "vs")}
    for l in range(depth):
        w = _layer_weights(l, d, dec_rows, nb_s, norm_g, w_in, conv_w, conv_b, dt_bias, a_log, d_skip, ssd_norm_g,
                           ln_v_g, ln_v_b, w_spatial, b_spatial, w_proj_a, w_proj_b, w_proj_x, w_out, final_norm_g)
        final_norm = l == depth - 1
        mk, mv = _mem_kv(mem_prompt.reshape(bsz * n_mem, d), mem_norm_g[l].reshape(1, d).astype(F32),
                         _pack_w(w_mem_kv[l]), _tile(bsz * n_mem, 512))
        (yp, hp, cp), _ = _layer(
            yp, mk.reshape(bsz, n_mem * (d_x // X_HEAD_DIM), X_HEAD_DIM),
            mv.reshape(bsz, n_mem * (d_x // X_HEAD_DIM), X_HEAD_DIM), w,
            nb_x=1, rows_x=_tile(seq, 512), heads_in_rows=False, tm_a=_tile(bsz * seq, 512), ws=w["ws_p"], bs=w["bs_p"], want_v=False,
            b_fn=functools.partial(_branch_b_prompt, w=w, tm=_tile(seq, 256), final_norm=final_norm))
        (ys, hs, cs), vs = _layer(
            ys, cache_mem_k[l].reshape(dec_b, n_mem * (d_x // X_HEAD_DIM), X_HEAD_DIM),
            cache_mem_v[l].reshape(dec_b, n_mem * (d_x // X_HEAD_DIM), X_HEAD_DIM), w,
            nb_x=nb_xs, rows_x=dec_rows, heads_in_rows=True, tm_a=_tile(dec_b * dec_rows, 512), ws=w["ws_s"], bs=w["bs_s"], want_v=True,
            b_fn=functools.partial(_branch_b_sample, conv0=state_conv, layer=l,
                                   h0=state_ssm[l].reshape(dec_b, d_inner, SSD_STATE), w=w, nb=nb_s,
                                   tm=_tile(dec_b * dec_rows, 512), final_norm=final_norm))
        outs["mk"].append(mk.reshape(bsz, n_mem, d_x // X_HEAD_DIM, X_HEAD_DIM))
        outs["mv"].append(mv.reshape(bsz, n_mem, d_x // X_HEAD_DIM, X_HEAD_DIM))
        outs["hp"].append(hp.reshape(bsz, n_heads, SSD_HEAD_DIM, SSD_STATE))
        outs["cp"].append(cp)
        outs["hs"].append(hs.reshape(dec_b, n_heads, SSD_HEAD_DIM, SSD_STATE))
        outs["cs"].append(cs)
        outs["vs"].append(vs.reshape(dec_b, dec_rows, d))
    st = lambda k: jnp.stack(outs[k])
    return (yp, ys, st("mk"), st("mv"), st("hp"), st("cp"), st("hs"), st("cs"), st("vs"))
```

```python
import functools
import math
from typing import NamedTuple

import jax
import jax.numpy as jnp
import numpy as np
from jax import lax
from jax.experimental import pallas as pl
from jax.experimental.pallas import tpu as pltpu

F32 = jnp.float32
BF16 = jnp.bfloat16
EPS = 1e-6
SQRT_HALF = math.sqrt(0.5)

LANES = 128
SUBLANES = 8
VMEM_LIMIT_BYTES = 56 * 1024 * 1024

CHUNK = 128
A_GROUPS = 8
SSD_HEAD_DIM = 64
SSD_GROUPS = 8
SSD_STATE = 128
X_HEAD_DIM = 128
N_BRANCH = 3
DT_REPLICAS = 3
SEQ_BATCH = 4

def _dot(a, b):
    return jnp.dot(a, b, preferred_element_type=F32)


def _dot_nt(a, b):
    return lax.dot_general(a, b, (((1,), (1,)), ((), ())), preferred_element_type=F32)


def _dot_tn(a, b):
    return lax.dot_general(a, b, (((0,), (0,)), ((), ())), preferred_element_type=F32)


MXU_COLS = 256


def _pack_w(w):
    return w.astype(BF16)


def _ncols(w):
    return sum(r.shape[1] for r in w) if isinstance(w, tuple) else w.shape[1]


def _dotw(a, w, cols=None):
    if not isinstance(w, tuple):
        return _dot(a, w[...] if cols is None else w[:, cols])
    width = w[0].shape[1]
    lo, hi = (0, width * len(w)) if cols is None else (cols.start, cols.stop)
    parts = []
    while lo < hi:
        k, off = divmod(lo, width)
        n = min(width - off, hi - lo)
        parts.append(_dot(a, w[k][:, off:off + n]))
        lo += n
    return parts[0] if len(parts) == 1 else jnp.concatenate(parts, axis=1)


def _dot_cols(a, w_ref, fn, width=MXU_COLS):
    n = w_ref.shape[1]
    width = min(width, n)
    blocks = [fn(_dotw(a, w_ref, slice(c, c + width))) for c in range(0, n, width)]
    return blocks[0] if len(blocks) == 1 else jnp.concatenate(blocks, axis=1)


def _rmsnorm(x, g):
    return x * lax.rsqrt(jnp.mean(x * x, axis=-1, keepdims=True) + EPS) * g


def _gelu(x):
    return 0.5 * x * (1.0 + lax.erf(x * SQRT_HALF))


def _sigmoid(x):
    return 0.5 * jnp.tanh(0.5 * x) + 0.5


def _silu(x):
    return x * _sigmoid(x)


def _bf16_parts(x):
    hi = x.astype(BF16)
    r1 = x - hi.astype(F32)
    mid = r1.astype(BF16)
    lo = (r1 - mid.astype(F32)).astype(BF16)
    return hi, mid, lo


def _split_by_replica(x, n_heads):
    hi, mid, lo = _bf16_parts(x)
    lane = lax.broadcasted_iota(jnp.int32, x.shape, 1)
    return jnp.where(lane < n_heads, hi, jnp.where(lane < 2 * n_heads, mid, lo))


def _cumsum_rows(tri, x):
    hi, mid, lo = _bf16_parts(x)
    return _dot(tri, hi) + _dot(tri, mid) + _dot(tri, lo)


def _memkv_kernel(x_ref, g_ref, w_ref, k_ref, v_ref, *, n_heads):
    tm = x_ref.shape[0]
    hd = k_ref.shape[-1]
    hn = _rmsnorm(x_ref[...], g_ref[...]).astype(BF16)
    kv = _dotw(hn, w_ref)
    for h in range(n_heads):
        k_ref[pl.ds(h, tm, stride=n_heads), :] = kv[:, h * hd:(h + 1) * hd]
        v_ref[pl.ds(h, tm, stride=n_heads), :] = kv[:, (n_heads + h) * hd:(n_heads + h + 1) * hd]


class _Cols(NamedTuple):
    mat: jax.Array
    start: int
    width: int

    @property
    def shape(self):
        return (self.mat.shape[0], self.width)


def _const_spec(op=None):
    if isinstance(op, _Cols):
        idx = op.start // op.width
        return pl.BlockSpec(op.shape, lambda *_: (0, idx), pipeline_mode=pl.Buffered(1))
    return pl.BlockSpec(memory_space=pltpu.VMEM)


def _flatten(ops):
    return [o for op in ops for o in (op if isinstance(op, list) else [op])]


def _const_specs(ops):
    return [_const_spec(op) for op in _flatten(ops)]


def _arrays(ops):
    return [op.mat if isinstance(op, _Cols) else op for op in _flatten(ops)]


def _regroup(kernel_fn, n_lead, ops):
    sizes = [len(op) if isinstance(op, list) else 0 for op in ops]

    def wrapped(*refs):
        refs = list(refs)
        args, pos = refs[:n_lead], n_lead
        for n in sizes:
            args.append(tuple(refs[pos:pos + n]) if n else refs[pos])
            pos += max(n, 1)
        return kernel_fn(*args, *refs[pos:])

    return wrapped


def _params(n_grid):
    return pltpu.CompilerParams(dimension_semantics=("arbitrary",) * n_grid, vmem_limit_bytes=VMEM_LIMIT_BYTES)


def _mem_kv(mem2d, g, w, tm):
    rows, d = mem2d.shape
    n_heads = w.shape[1] // 2 // X_HEAD_DIM
    out = jax.ShapeDtypeStruct((rows * n_heads, X_HEAD_DIM), F32)
    out_spec = pl.BlockSpec((tm * n_heads, X_HEAD_DIM), lambda i: (i, 0))
    return pl.pallas_call(
        functools.partial(_memkv_kernel, n_heads=n_heads),
        out_shape=(out, out),
        grid=(rows // tm,),
        in_specs=[pl.BlockSpec((tm, d), lambda i: (i, 0)), _const_spec(), _const_spec()],
        out_specs=(out_spec, out_spec),
        compiler_params=_params(1),
        name="mem_kv",
    )(mem2d, g, w)


def _x_kernel(x_ref, g_ref, wq_ref, wgx_ref, wmx_ref, k_ref, v_ref, wpx_ref, m_ref, *, nb, rows, head_dim):
    d = x_ref.shape[-1]
    x = x_ref[...].reshape(nb * rows, d)
    hn = _rmsnorm(x, g_ref[...]).astype(BF16)
    q = _dotw(hn, wq_ref)
    gate = _silu(_dotw(hn, wgx_ref))
    gm = _sigmoid(_dotw(hn, wmx_ref))
    n_heads = q.shape[1] // head_dim
    scale = head_dim ** -0.5
    outs = []
    n_mem = k_ref.shape[1] // n_heads
    for b in range(nb):
        qb = q[b * rows:(b + 1) * rows].astype(BF16)
        heads = range(n_heads)
        kb = [k_ref[b, pl.ds(h, n_mem, stride=n_heads), :].astype(BF16) for h in heads]
        vb = [v_ref[b, pl.ds(h, n_mem, stride=n_heads), :].astype(BF16) for h in heads]
        scores = [_dot_nt(qb[:, h * head_dim:(h + 1) * head_dim], kb[h]) * scale for h in heads]
        exps = [jnp.exp(s - jnp.max(s, axis=-1, keepdims=True)) for s in scores]
        probs = [(e / jnp.sum(e, axis=-1, keepdims=True)).astype(BF16) for e in exps]
        outs.append(jnp.concatenate([_dot(probs[h], vb[h]) for h in heads], axis=-1))
    o = outs[0] if nb == 1 else jnp.concatenate(outs, axis=0)
    hx = (o * gate).astype(BF16)
    m = gm * _dotw(hx, wpx_ref)
    m_ref[...] = m.reshape(nb, rows, d)


def _xs_kernel(x_ref, g_ref, wq_ref, wgx_ref, wmx_ref, k_ref, v_ref, wpx_ref, m_ref, *, nb, rows, head_dim):
    d = x_ref.shape[-1]
    x = x_ref[...].reshape(nb * rows, d)
    hn = _rmsnorm(x, g_ref[...]).astype(BF16)
    q = _dotw(hn, wq_ref)
    gate = _silu(_dotw(hn, wgx_ref))
    gm = _sigmoid(_dotw(hn, wmx_ref))
    n_heads = q.shape[1] // head_dim
    n_kv = k_ref.shape[1]
    scale = head_dim ** -0.5
    row_head = lax.broadcasted_iota(jnp.int32, (n_heads * rows, n_kv), 0) // rows
    col_head = lax.broadcasted_iota(jnp.int32, (n_heads * rows, n_kv), 1) % n_heads
    same_head = row_head == col_head
    seqs = range(nb)
    scores = []
    for b in seqs:
        qb = q[b * rows:(b + 1) * rows]
        q_heads = jnp.concatenate([qb[:, h * head_dim:(h + 1) * head_dim] for h in range(n_heads)], axis=0)
        scores.append(_dot_nt(q_heads.astype(BF16), k_ref[b].astype(BF16)))
    masked = [jnp.where(same_head, s * scale, -jnp.inf) for s in scores]
    exps = [jnp.exp(s - jnp.max(s, axis=-1, keepdims=True)) for s in masked]
    probs = [(e / jnp.sum(e, axis=-1, keepdims=True)).astype(BF16) for e in exps]
    o_heads = [_dot(probs[b], v_ref[b].astype(BF16)) for b in seqs]
    outs = [jnp.concatenate([oh[h * rows:(h + 1) * rows] for h in range(n_heads)], axis=1) for oh in o_heads]
    o = outs[0] if nb == 1 else jnp.concatenate(outs, axis=0)
    hx = (o * gate).astype(BF16)
    m = gm * _dotw(hx, wpx_ref)
    m_ref[...] = m.reshape(nb, rows, d)


def _branch_x(x3, g, wq, wgx, wmx, k3, v3, wpx, *, nb, rows, heads_in_rows):
    bsz, seq, d = x3.shape
    n_mem, dx = k3.shape[1], k3.shape[2]
    kern = functools.partial(_xs_kernel if heads_in_rows else _x_kernel, nb=nb, rows=rows, head_dim=X_HEAD_DIM)
    return pl.pallas_call(
        kern,
        out_shape=jax.ShapeDtypeStruct((bsz, seq, d), F32),
        grid=(bsz // nb, seq // rows),
        in_specs=[pl.BlockSpec((nb, rows, d), lambda i, j: (i, j, 0))] + _const_specs([g, wq, wgx, wmx]) + [
            pl.BlockSpec((nb, n_mem, dx), lambda i, j: (i, 0, 0)),
            pl.BlockSpec((nb, n_mem, dx), lambda i, j: (i, 0, 0)),
            _const_spec(wpx),
        ],
        out_specs=pl.BlockSpec((nb, rows, d), lambda i, j: (i, j, 0)),
        compiler_params=_params(2),
        name="branch_x",
    )(x3, *_arrays([g, wq, wgx, wmx]), k3, v3, wpx)


def _a_kernel(x_ref, g_ref, wu_ref, wv_ref, wga_ref, wma_ref, lng_ref, lnb_ref, ws_ref, bs_ref, wpa_ref, min_ref,
              mout_ref, *maybe_v_ref, chunk, groups, sub):
    tm, d = x_ref.shape
    gd = d // groups
    bias = bs_ref[...]
    for t in range(tm // sub):
        ts = slice(t * sub, (t + 1) * sub)
        hn = _rmsnorm(x_ref[ts, :], g_ref[...]).astype(BF16)
        v = _dot_cols(hn, wv_ref, _gelu)
        u = _dot_cols(hn, wu_ref, _gelu)
        vc = v - jnp.mean(v, axis=-1, keepdims=True)
        vn = vc * lax.rsqrt(jnp.mean(vc * vc, axis=-1, keepdims=True) + EPS) * lng_ref[...] + lnb_ref[...]
        if maybe_v_ref:
            maybe_v_ref[0][ts, :] = vn
        ga = _dot_cols(hn, wga_ref, _silu)
        vb = vn.astype(BF16)
        n_c = sub // chunk
        wide = [_dot(ws_ref[gi], jnp.concatenate([vb[c * chunk:(c + 1) * chunk, gi * gd:(gi + 1) * gd]
                                                  for c in range(n_c)], axis=1)) for gi in range(groups)]
        mixed = [jnp.concatenate([wg[:, c * gd:(c + 1) * gd] for wg in wide], axis=1) + bias for c in range(n_c)]
        s = mixed[0] if n_c == 1 else jnp.concatenate(mixed, axis=0)
        ha = (u * s * ga).astype(BF16)
        gm = _sigmoid(_dotw(hn, wma_ref))
        mout_ref[ts, :] = min_ref[ts, :] + gm * _dotw(ha, wpa_ref)


def _branch_a(x2, g, wu, wv, wga, wma, lng, lnb, ws, bs, wpa, m_in, *, tm, want_v):
    rows, d = x2.shape
    row_spec = pl.BlockSpec((tm, d), lambda i: (i, 0))
    out_shape = [jax.ShapeDtypeStruct((rows, d), F32)]
    out_specs = [row_spec]
    if want_v:
        out_shape.append(jax.ShapeDtypeStruct((rows, d), F32))
        out_specs.append(row_spec)
    kern = functools.partial(_a_kernel, chunk=CHUNK, groups=A_GROUPS, sub=min(tm, 256))
    consts = [g, wu, wv, wga, wma, lng, lnb, ws, bs, wpa]
    res = pl.pallas_call(
        kern,
        out_shape=tuple(out_shape),
        grid=(rows // tm,),
        in_specs=[row_spec] + _const_specs(consts) + [row_spec],
        out_specs=tuple(out_specs),
        compiler_params=_params(1),
        name="branch_a",
    )(x2, *_arrays(consts), m_in)
    return res if want_v else (res[0], None)


def _finish_b(x, y, g_ref, wz_ref, wmb_ref, ng_ref, wpb_ref, m_in, wout_ref, fg_ref, *, n_groups, final_norm):
    hn = _rmsnorm(x, g_ref[...]).astype(BF16)
    gm = _sigmoid(_dotw(hn, wmb_ref))
    yf = y * _silu(_dotw(hn, wz_ref))
    gw = yf.shape[1] // n_groups
    parts = []
    for gi in range(n_groups):
        yg = yf[:, gi * gw:(gi + 1) * gw]
        parts.append(yg * lax.rsqrt(jnp.mean(yg * yg, axis=-1, keepdims=True) + EPS))
    hb = (jnp.concatenate(parts, axis=-1) * ng_ref[...]).astype(BF16)
    m = m_in + gm * _dotw(hb, wpb_ref)
    out = x + _dotw(m.astype(BF16), wout_ref)
    return _rmsnorm(out, fg_ref[...]) if final_norm else out


HANDOFF = ("xs", "bm", "cm", "zs", "dt", "gm", "tail")


def _bp_stage1(x_ref, g_ref, wz_ref, wxbc_ref, wdt_ref, wmb_ref, cw_ref, cb_ref, dtb_ref, perm_ref, xs_scr, out,
               first_tile, *, tm, chunk, d_inner, bc_width, conv_w):
    pad = SUBLANES
    vpc = chunk // pad
    n_chunks = tm // chunk
    width = MXU_COLS
    hn = _rmsnorm(x_ref[0], g_ref[...]).astype(BF16)
    for c0 in range(0, out["gm"].shape[1], width):
        cols = slice(c0, c0 + width)
        out["gm"][:, cols] = _sigmoid(_dotw(hn, wmb_ref, cols))
        yield
    hn = _dot(perm_ref[0], hn).astype(BF16)
    out["dt"][...] = jax.nn.softplus(_dotw(hn, wdt_ref) + dtb_ref[...])
    yield
    for c0 in range(0, d_inner, width):
        cols = slice(c0, c0 + width)
        out["zs"][:, cols] = _silu(_dotw(hn, wz_ref, cols))
        yield
    first_sublane = lax.broadcasted_iota(jnp.int32, (pad, width), 0) == 0
    for c0 in range(0, _ncols(wxbc_ref), width):
        cols = slice(c0, c0 + width)
        raw_all = _dotw(hn, wxbc_ref, cols)
        prev_tail = jnp.where(first_tile, 0.0, xs_scr[:, cols])
        acts = []
        for c in range(n_chunks):
            raw = raw_all[c * chunk:(c + 1) * chunk]
            conv = cb_ref[:, cols] + raw * cw_ref[conv_w - 1:conv_w, cols]
            for back in range(1, conv_w):
                head = []
                for v in range(back):
                    w = vpc - back + v
                    cur = pltpu.roll(raw[w * pad:(w + 1) * pad], 1, axis=0)
                    prv = pltpu.roll(prev_tail[(w - vpc + conv_w - 1) * pad:(w - vpc + conv_w) * pad], 1, axis=0)
                    head.append(jnp.where(first_sublane, prv, cur))
                shifted = jnp.concatenate(head + [raw[:chunk - back * pad]], axis=0)
                conv = conv + shifted * cw_ref[conv_w - 1 - back:conv_w - back, cols]
            acts.append(_silu(conv))
            prev_tail = raw[chunk - (conv_w - 1) * pad:]
        xs_scr[:, cols] = prev_tail
        out["tail"][:, cols] = prev_tail
        act = acts[0] if n_chunks == 1 else jnp.concatenate(acts, axis=0)
        if c0 < d_inner:
            out["xs"][:, cols] = act
        elif c0 < d_inner + bc_width:
            out["bm"][:, c0 - d_inner:c0 - d_inner + width] = act.astype(BF16)
        else:
            out["cm"][:, c0 - d_inner - bc_width:c0 - d_inner - bc_width + width] = act.astype(BF16)
        yield


def _bp_stage2(x_ref, min_ref, alog_ref, dsk_ref, ng_ref, wpb_ref, wout_ref, fg_ref, tri_ref, e3_ref, perm_ref,
               y_ref, st_scr, inp, first_tile, *, tm, chunk, n_groups, n_heads, head_dim, n_state, final_norm):
    pad = SUBLANES
    vpc = chunk // pad
    hpg = n_heads // n_groups
    gw = hpg * head_dim
    width = MXU_COLS

    def token_of(q):
        return (q % pad) * vpc + q // pad

    li = token_of(lax.broadcasted_iota(jnp.int32, (chunk, chunk), 0))
    si = token_of(lax.broadcasted_iota(jnp.int32, (chunk, chunk), 1))
    causal = li >= si
    head_of_lane = lax.broadcasted_iota(jnp.int32, (chunk, gw), 1) // head_dim
    tri = tri_ref[...]
    e3 = e3_ref[...]
    a_row = -jnp.exp(alog_ref[...])

    hb_rows = []
    for c in range(tm // chunk):
        rs = slice(c * chunk, (c + 1) * chunk)
        dt = inp["dt"][rs, :]
        acs = _cumsum_rows(tri, dt * a_row)
        acs_t = acs.T
        acs_e = _dot(_split_by_replica(acs, n_heads).astype(BF16), e3)
        a_last = acs_e[chunk - 1:chunk, :]
        xs_c = inp["xs"][rs, :]
        xdt_c = xs_c * _dot(_split_by_replica(dt, n_heads).astype(BF16), e3)
        xdt_b = xdt_c.astype(BF16)
        xd_state = (xdt_c * jnp.exp(a_last - acs_e)).astype(BF16)
        decay_in = jnp.exp(acs_e)
        st_prev = st_scr[...]
        if c == 0:
            st_prev = jnp.where(first_tile, 0.0, st_prev)
        st_prev_b = st_prev.astype(BF16)
        yield
        groups = range(n_groups)
        n_sl = [slice(gi * n_state, (gi + 1) * n_state) for gi in groups]
        g_sl = [slice(gi * gw, (gi + 1) * gw) for gi in groups]
        scores = [_dot_nt(inp["cm"][rs, n_sl[gi]], inp["bm"][rs, n_sl[gi]]) for gi in groups]
        yield
        y_offs = [_dot(inp["cm"][rs, n_sl[gi]], st_prev_b[:, g_sl[gi]]) for gi in groups]
        st_parts = [_dot_tn(inp["bm"][rs, n_sl[gi]], xd_state[:, g_sl[gi]]) for gi in groups]
        st_scr[...] = st_prev * jnp.exp(a_last) + jnp.concatenate(st_parts, axis=1)
        yield
        y_diags = []
        for gi in groups:
            m_heads, x_blocks = [], []
            xg = xdt_b[:, g_sl[gi]]
            for r in range(hpg):
                h = gi * hpg + r
                diff = acs[:, h:h + 1] - acs_t[h:h + 1, :]
                decay = jnp.exp(jnp.where(causal, diff, -jnp.inf))
                m_heads.append((scores[gi] * decay).astype(BF16))
                x_blocks.append(jnp.where(head_of_lane == r, xg, jnp.zeros_like(xg)))
            y_diags.append(_dot(jnp.concatenate(m_heads, axis=1), jnp.concatenate(x_blocks, axis=0)))
            yield
        hb_parts = []
        for gi in groups:
            gs = g_sl[gi]
            y = y_diags[gi] + y_offs[gi] * decay_in[:, gs] + xs_c[:, gs] * dsk_ref[:, gs]
            yf = y * inp["zs"][rs, gs]
            yn = yf * lax.rsqrt(jnp.mean(yf * yf, axis=-1, keepdims=True) + EPS)
            hb_parts.append((yn * ng_ref[:, gs]).astype(BF16))
            if gi % 2:
                yield
        hb_rows.append(jnp.concatenate(hb_parts, axis=1))
    hb = hb_rows[0] if len(hb_rows) == 1 else jnp.concatenate(hb_rows, axis=0)
    hb = _dot(perm_ref[1], hb).astype(BF16)
    yield
    m_parts = []
    for c0 in range(0, wpb_ref.shape[1], width):
        cols = slice(c0, c0 + width)
        m_parts.append((min_ref[0, :, cols] + inp["gm"][:, cols] * _dotw(hb, wpb_ref, cols)).astype(BF16))
        yield
    m = jnp.concatenate(m_parts, axis=1)
    out = x_ref[0] + _dotw(m, wout_ref)
    y_ref[0] = _rmsnorm(out, fg_ref[...]) if final_norm else out


BP_WEIGHTS = ("wz", "wxbc", "wdt", "wmb", "wpb", "wout")


def _bp_kernel(x1_ref, x2_ref, min_ref, g_ref, cw_ref, cb_ref, dtb_ref, alog_ref, dsk_ref, ng_ref, fg_ref, tri_ref,
               e3_ref, perm_ref, *rest,
               n_hbm, copies, nj, n_tiles, tm, chunk, n_groups, n_heads, head_dim, n_state, conv_w, final_norm):
    hbm_refs, (y_ref, ssm_ref, conv_ref, xs_scr, st_scr), rest = rest[:n_hbm], rest[n_hbm:n_hbm + 5], rest[n_hbm + 5:]
    w_scr, slot_refs = rest[:len(BP_WEIGHTS)], rest[len(BP_WEIGHTS):]
    wz_ref, wxbc_ref, wdt_ref, wmb_ref, wpb_ref, wout_ref = w_scr
    t = pl.program_id(0)
    d_inner = n_heads * head_dim
    pad = SUBLANES
    tile1 = jnp.minimum(t, n_tiles - 1)
    tile2 = jnp.maximum(t - 1, 0)
    n_h = len(HANDOFF)
    slots = [dict(zip(HANDOFF, slot_refs[i * n_h:(i + 1) * n_h])) for i in range(2)]

    @pl.when(t == 0)
    def _init():
        for src, start, width, dst, dst_start in copies:
            pltpu.sync_copy(hbm_refs[src].at[:, pl.ds(start, width)], w_scr[dst].at[:, pl.ds(dst_start, width)])
        for ref in slots[1].values():
            ref[...] = jnp.zeros_like(ref)
        st_scr[...] = jnp.zeros_like(st_scr)
        xs_scr[...] = jnp.zeros_like(xs_scr)

    def step(slot_w, slot_r):
        s1 = _bp_stage1(x1_ref, g_ref, wz_ref, wxbc_ref, wdt_ref, wmb_ref, cw_ref, cb_ref, dtb_ref, perm_ref, xs_scr,
                        slot_w, tile1 % nj == 0, tm=tm, chunk=chunk, d_inner=d_inner,
                        bc_width=n_groups * n_state, conv_w=conv_w)
        s2 = _bp_stage2(x2_ref, min_ref, alog_ref, dsk_ref, ng_ref, wpb_ref, wout_ref, fg_ref, tri_ref, e3_ref,
                        perm_ref, y_ref, st_scr, slot_r, tile2 % nj == 0, tm=tm, chunk=chunk, n_groups=n_groups,
                        n_heads=n_heads, head_dim=head_dim, n_state=n_state, final_norm=final_norm)
        live = [s2, s1]
        while live:
            for gen in list(live):
                if next(gen, StopIteration) is StopIteration:
                    live.remove(gen)

        @pl.when((t > 0) & (tile2 % nj == nj - 1))
        def _seq_end():
            tail = slot_r["tail"][...]
            conv_ref[0] = jnp.concatenate([tail[(v + 1) * pad - 1:(v + 1) * pad] for v in range(conv_w - 1)], axis=0)
            ssm_ref[0] = st_scr[...].T

    for parity in range(2):
        pl.when(t % 2 == parity)(functools.partial(step, slots[parity], slots[1 - parity]))


def _branch_b_prompt(x3, m3, w, *, tm, final_norm):
    bsz, seq, d = x3.shape
    conv_dim = w["cw"].shape[1]
    d_inner = w["ng"].shape[1]
    n_heads = d_inner // SSD_HEAD_DIM
    conv_w = w["cw"].shape[0]
    nj = seq // tm
    n_tiles = bsz * nj
    bc_width = SSD_GROUPS * SSD_STATE
    hbm_ops, copies, w_shapes = [], [], []
    for wi, name in enumerate(BP_WEIGHTS):
        parts = w[name] if isinstance(w[name], list) else [w[name]]
        dst = 0
        for part in parts:
            mat, start, width = (part.mat, part.start, part.width) if isinstance(part, _Cols) else (part, 0, part.shape[1])
            src = next((i for i, op in enumerate(hbm_ops) if op is mat), len(hbm_ops))
            if src == len(hbm_ops):
                hbm_ops.append(mat)
            copies.append((src, start, width, wi, dst))
            dst += width
        w_shapes.append((parts[0].shape[0], dst))
    kern = functools.partial(_bp_kernel, n_hbm=len(hbm_ops), copies=tuple(copies), nj=nj, n_tiles=n_tiles, tm=tm,
                             chunk=CHUNK, n_groups=SSD_GROUPS, n_heads=n_heads, head_dim=SSD_HEAD_DIM,
                             n_state=SSD_STATE, conv_w=conv_w, final_norm=final_norm)

    def tile1(t):
        return jnp.minimum(t, n_tiles - 1)

    def tile2(t):
        return jnp.maximum(t - 1, 0)

    spec1 = pl.BlockSpec((1, tm, d), lambda t: (tile1(t) // nj, tile1(t) % nj, 0))
    spec2 = pl.BlockSpec((1, tm, d), lambda t: (tile2(t) // nj, tile2(t) % nj, 0))
    consts = [w[k] for k in ("g", "cw", "cb", "dtb", "alog", "dsk", "ng", "fg", "tri", "e3")]
    tok = (np.arange(tm) // CHUNK) * CHUNK + np.tile(_chunk_tokens(), tm // CHUNK)
    p = (tok[:, None] == np.arange(tm)[None, :]).astype(np.float32)
    perm = jnp.asarray(np.stack([p, p.T]), dtype=BF16)
    handoff = {"xs": ((tm, d_inner), F32), "bm": ((tm, bc_width), BF16), "cm": ((tm, bc_width), BF16),
               "zs": ((tm, d_inner), F32), "dt": ((tm, LANES), F32), "gm": ((tm, d), F32),
               "tail": (((conv_w - 1) * SUBLANES, conv_dim), F32)}
    return pl.pallas_call(
        kern,
        out_shape=(jax.ShapeDtypeStruct((bsz, seq, d), F32),
                   jax.ShapeDtypeStruct((bsz, d_inner, SSD_STATE), F32),
                   jax.ShapeDtypeStruct((bsz, conv_w - 1, conv_dim), F32)),
        grid=(n_tiles + 1,),
        in_specs=([spec1, spec2, spec2] + _const_specs(consts + [perm])
                  + [pl.BlockSpec(memory_space=pl.ANY)] * len(hbm_ops)),
        out_specs=(spec2,
                   pl.BlockSpec((1, d_inner, SSD_STATE), lambda t: (tile2(t) // nj, 0, 0)),
                   pl.BlockSpec((1, conv_w - 1, conv_dim), lambda t: (tile2(t) // nj, 0, 0))),
        scratch_shapes=([pltpu.VMEM(((conv_w - 1) * SUBLANES, conv_dim), F32), pltpu.VMEM((SSD_STATE, d_inner), F32)]
                        + [pltpu.VMEM(shape, BF16) for shape in w_shapes]
                        + [pltpu.VMEM(*handoff[k]) for _ in range(2) for k in HANDOFF]),
        compiler_params=_params(1),
        name="branch_b_prompt",
    )(x3, x3, m3, *_arrays(consts), perm, *hbm_ops)


def _bs_kernel(x_ref, g_ref, wxbc_ref, wdt_ref, cw_ref, cb_ref, dtb_ref, alog_ref, dsk_ref,
               tri_ref, e3_ref, ecol_ref, conv0_ref, h0_ref,
               y_ref, hnew_ref, convnew_ref, xs_scr, tr_scr,
               *, nb, rows, n_groups, n_heads, head_dim, n_state, conv_w):
    hpg = n_heads // n_groups
    gw = hpg * head_dim
    d_inner = n_heads * head_dim
    d = x_ref.shape[-1]
    r_all = nb * rows
    hs_w = n_heads * rows

    @pl.when(pl.program_id(0) == 0)
    def _init():
        tr_scr[...] = jnp.zeros_like(tr_scr)
        xs_scr[...] = jnp.zeros_like(xs_scr)

    x = x_ref[...].reshape(r_all, d)
    hn = _rmsnorm(x, g_ref[...]).astype(BF16)
    xbc_raw = _dotw(hn, wxbc_ref)
    dt = jax.nn.softplus(_dotw(hn, wdt_ref) + dtb_ref[...])

    raw = xbc_raw.reshape(nb, rows, xbc_raw.shape[1])
    xs_scr[:, rows - (conv_w - 1):, :] = conv0_ref[...]
    state = xs_scr[...]
    row_in_seq = lax.broadcasted_iota(jnp.int32, raw.shape, 1)
    conv = cb_ref[...] + raw * cw_ref[conv_w - 1:conv_w, :]
    for back in range(1, conv_w):
        shifted = jnp.where(row_in_seq < back, pltpu.roll(state, back, axis=1), pltpu.roll(raw, back, axis=1))
        conv = conv + shifted * cw_ref[conv_w - 1 - back:conv_w - back, :]
    convnew_ref[...] = raw[:, rows - (conv_w - 1):, :]
    xbc = _silu(conv.reshape(r_all, xbc_raw.shape[1]))
    xs = xbc[:, :d_inner]
    bm = xbc[:, d_inner:d_inner + n_groups * n_state]
    cm = xbc[:, d_inner + n_groups * n_state:]

    a_row = -jnp.exp(alog_ref[...])
    acs = _cumsum_rows(tri_ref[...], dt * a_row)
    e3 = e3_ref[...]
    acs_split = _split_by_replica(acs, n_heads).astype(BF16)
    acs_e = _dot(acs_split, e3)
    xdt = xs * _dot(_split_by_replica(dt, n_heads).astype(BF16), e3)
    decay_in = jnp.exp(acs_e)

    acs_col = _dot(acs_split, ecol_ref[...])
    l_idx = lax.broadcasted_iota(jnp.int32, (r_all, hs_w), 0) % rows
    s_idx = lax.broadcasted_iota(jnp.int32, (r_all, hs_w), 1) % rows
    on_diag = jnp.where(l_idx == s_idx, acs_col, 0.0).reshape(nb, rows, hs_w)
    acs_row = jnp.broadcast_to(jnp.sum(on_diag, axis=1, keepdims=True), (nb, rows, hs_w)).reshape(r_all, hs_w)
    decay = jnp.exp(jnp.where(l_idx >= s_idx, acs_col - acs_row, -jnp.inf))

    slot = LANES // nb
    ones_rows = lax.broadcasted_iota(jnp.int32, (slot, n_state), 0)
    ones_blk = jnp.where((ones_rows >= rows) & (ones_rows < rows + 3), 1.0, 0.0)

    def seq_rows(b):
        return slice(b * rows, (b + 1) * rows)

    lane_in_tile = lax.broadcasted_iota(jnp.int32, (rows, LANES), 1)

    def expand_rows(v, start_of, width):
        c = v.shape[1]
        blocks = []
        for h in range(n_heads):
            start = start_of(h)
            lo = start // LANES * LANES
            hi = max(lo + LANES, start + width)
            piece = v[:, lo:hi]
            if width < LANES:
                piece = jnp.where(lane_in_tile // width == (start - lo) // width, piece, 0.0)
            parts = [jnp.zeros((rows, lo), F32)] if lo else []
            parts.append(piece)
            if hi < c:
                parts.append(jnp.zeros((rows, c - hi), F32))
            blocks.append(jnp.concatenate(parts, axis=1) if len(parts) > 1 else piece)
        return jnp.concatenate(blocks, axis=0).astype(BF16)

    y_diags = []
    for b0 in range(0, nb, SEQ_BATCH):
        seqs = range(b0, min(b0 + SEQ_BATCH, nb))
        b_exps = [expand_rows(bm[seq_rows(b)], lambda h: h // hpg * n_state, n_state) for b in seqs]
        scores = [_dot_nt(cm[seq_rows(b)].astype(BF16), be) for b, be in zip(seqs, b_exps)]
        x_blks = [expand_rows(xdt[seq_rows(b)], lambda h: h * head_dim, head_dim) for b in seqs]
        m_alls = [(s * decay[seq_rows(b)]).astype(BF16) for b, s in zip(seqs, scores)]
        y_diags += [_dot(m, xb) for m, xb in zip(m_alls, x_blks)]
    for b in range(nb):
        a_last = acs_e[(b + 1) * rows - 1:(b + 1) * rows, :]
        d_hi, d_mid, d_lo = _bf16_parts(jnp.exp(a_last))
        tr_scr[b * slot:b * slot + rows, :] = xdt[seq_rows(b)] * jnp.exp(a_last - acs_e[seq_rows(b)])
        tr_scr[b * slot + rows:b * slot + rows + 1, :] = d_hi.astype(F32)
        tr_scr[b * slot + rows + 1:b * slot + rows + 2, :] = d_mid.astype(F32)
        tr_scr[b * slot + rows + 2:b * slot + rows + 3, :] = d_lo.astype(F32)
    tr_t = tr_scr[...].T.astype(BF16)

    ys = []
    for b in range(nb):
        rs = slice(b * rows, (b + 1) * rows)
        bm_b, cm_bb = bm[rs], cm[rs].astype(BF16)
        y_parts = []
        for gi in range(n_groups):
            ns = slice(gi * n_state, (gi + 1) * n_state)
            gs = slice(gi * gw, (gi + 1) * gw)
            h0_g = h0_ref[b, gs, :]
            y_parts.append(_dot_nt(cm_bb[:, ns], h0_g.astype(BF16)))
            w_seq = jnp.concatenate(
                [jnp.concatenate([bm_b[:, ns], jnp.zeros((slot - rows, n_state), F32)], axis=0), ones_blk], axis=1)
            pieces = ([jnp.zeros((b * slot, 2 * n_state), F32)] if b else []) + [w_seq]
            if b < nb - 1:
                pieces.append(jnp.zeros((LANES - (b + 1) * slot, 2 * n_state), F32))
            upd = _dot(tr_t[gs, :], jnp.concatenate(pieces, axis=0).astype(BF16))
            hnew_ref[b, gs, :] = h0_g * upd[:, n_state:] + upd[:, :n_state]
        ys.append(y_diags[b] + jnp.concatenate(y_parts, axis=1) * decay_in[rs])
    y = jnp.concatenate(ys, axis=0) + xs * dsk_ref[...]
    y_ref[...] = y.reshape(nb, rows, d_inner)


def _bfin_kernel(x_ref, y_ref, min_ref, g_ref, wz_ref, wmb_ref, ng_ref, wpb_ref, wout_ref, fg_ref, out_ref,
                 *, n_groups, final_norm):
    out_ref[...] = _finish_b(x_ref[...], y_ref[...], g_ref, wz_ref, wmb_ref, ng_ref, wpb_ref, min_ref[...],
                             wout_ref, fg_ref, n_groups=n_groups, final_norm=final_norm)


def _branch_b_sample(x3, m3, conv0, h0, w, *, layer, nb, tm, final_norm):
    bsz, rows, d = x3.shape
    conv_dim = w["cw"].shape[1]
    d_inner = w["ng"].shape[1]
    n_heads = d_inner // SSD_HEAD_DIM
    conv_w = w["cw"].shape[0]
    kern = functools.partial(_bs_kernel, nb=nb, rows=rows, n_groups=SSD_GROUPS, n_heads=n_heads,
                             head_dim=SSD_HEAD_DIM, n_state=SSD_STATE, conv_w=conv_w)

    def seq_spec(width):
        return pl.BlockSpec((nb, rows, width), lambda i: (i, 0, 0))

    conv_spec = pl.BlockSpec((nb, conv_w - 1, conv_dim), lambda i: (i, 0, 0))
    h_spec = pl.BlockSpec((nb, d_inner, SSD_STATE), lambda i: (i, 0, 0))
    consts = [w[k] for k in ("g", "wxbc", "wdt", "cw", "cb", "dtb", "alog", "dsk", "tri_s", "e3", "ecol")]
    y, h_new, conv_new = pl.pallas_call(
        _regroup(kern, 1, consts),
        out_shape=(jax.ShapeDtypeStruct((bsz, rows, d_inner), F32),
                   jax.ShapeDtypeStruct((bsz, d_inner, SSD_STATE), F32),
                   jax.ShapeDtypeStruct((bsz, conv_w - 1, conv_dim), F32)),
        grid=(bsz // nb,),
        in_specs=[seq_spec(d)] + _const_specs(consts) + [
            pl.BlockSpec((None, nb, conv_w - 1, conv_dim), lambda i: (layer, i, 0, 0)), h_spec],
        out_specs=(seq_spec(d_inner), h_spec, conv_spec),
        scratch_shapes=[pltpu.VMEM((nb, SUBLANES, conv_dim), F32), pltpu.VMEM((LANES, d_inner), F32)],
        compiler_params=_params(1),
        name="branch_b_sample",
    )(x3, *_arrays(consts), conv0, h0)

    n_rows = bsz * rows
    fin_consts = [w[k] for k in ("g", "wz", "wmb", "ng", "wpb", "wout", "fg")]

    def rows_spec(width):
        return pl.BlockSpec((tm, width), lambda i: (i, 0))

    out = pl.pallas_call(
        _regroup(functools.partial(_bfin_kernel, n_groups=SSD_GROUPS, final_norm=final_norm), 3, fin_consts),
        out_shape=jax.ShapeDtypeStruct((n_rows, d), F32),
        grid=(n_rows // tm,),
        in_specs=[rows_spec(d), rows_spec(d_inner), rows_spec(d)] + _const_specs(fin_consts),
        out_specs=rows_spec(d),
        compiler_params=_params(1),
        name="branch_b_sample_out",
    )(x3.reshape(n_rows, d), y.reshape(n_rows, d_inner), m3.reshape(n_rows, d), *_arrays(fin_consts))
    return out.reshape(bsz, rows, d), h_new, conv_new


def _tile(n, pref):
    return pref if n % pref == 0 else n


def _chunk_tokens():
    q = np.arange(CHUNK)
    return (q % SUBLANES) * (CHUNK // SUBLANES) + q // SUBLANES


def _head_expand(n_heads, width, n_rep):
    j = np.arange(LANES)[:, None]
    c = np.arange(n_heads * width)[None, :]
    return jnp.asarray(((j < n_rep * n_heads) & (j % n_heads == c // width)).astype(np.float32), dtype=BF16)


def _layer_weights(l, d, dec_rows, nb_s, norm_g, w_in, conv_w, conv_b, dt_bias, a_log, d_skip, ssd_norm_g, ln_v_g,
                   ln_v_b, w_spatial, b_spatial, w_proj_a, w_proj_b, w_proj_x, w_out, final_norm_g):
    d_a = d
    d_inner = w_proj_b.shape[1]
    conv_dim = conv_w.shape[2]
    n_heads = a_log.shape[1]
    d_x = w_proj_x.shape[1]
    sizes = (d_a, d_a, d_a, d_inner, conv_dim, n_heads, d_x, d_x, N_BRANCH * d)
    offs = np.concatenate([[0], np.cumsum(sizes)])
    wl = w_in[l]
    sec = lambda i: wl[:, offs[i]:offs[i + 1]]
    row = lambda v: v.reshape(1, -1).astype(F32)
    rep3 = lambda v, fill: jnp.concatenate(
        [v] * DT_REPLICAS + [jnp.full(v.shape[:-1] + (LANES - DT_REPLICAS * v.shape[-1],), fill, v.dtype)], axis=-1)

    tril = jnp.tril(jnp.ones((CHUNK, CHUNK), F32))
    ws_p = jnp.where(tril[None] > 0, w_spatial[l], 0.0)
    bs_p = jnp.repeat(b_spatial[l].T, d_a // A_GROUPS, axis=1)
    n_seq = CHUNK // dec_rows
    sel = jnp.asarray(np.tile(np.eye(dec_rows, dtype=np.float32), (n_seq, 1)))
    blocks = jnp.asarray(np.kron(np.eye(n_seq, dtype=np.float32), np.ones((dec_rows, dec_rows), np.float32)))
    ws_s = jnp.einsum("ia,gab,jb->gij", sel, ws_p[:, :dec_rows, :dec_rows], sel,
                      precision=lax.Precision.HIGHEST) * blocks
    bs_s = jnp.tile(bs_p[:dec_rows], (n_seq, 1))

    w = {
        "g": row(norm_g[l]),
        "lng": row(ln_v_g[l]), "lnb": row(ln_v_b[l]),
        "ws_p": ws_p.astype(BF16), "bs_p": bs_p, "ws_s": ws_s.astype(BF16), "bs_s": bs_s,
        "wpa": _pack_w(w_proj_a[l]),
        "cw": conv_w[l].astype(F32), "cb": row(conv_b[l]),
        "dtb": rep3(row(dt_bias[l]), 0.0), "alog": rep3(row(a_log[l]), 0.0),
        "dsk": jnp.repeat(row(d_skip[l]), SSD_HEAD_DIM, axis=1),
        "ng": row(ssd_norm_g[l]),
        "wpb": _pack_w(w_proj_b[l]),
        "wout": _pack_w(w_out[l]), "fg": row(final_norm_g),
        "tri": jnp.asarray(_chunk_tokens()[:, None] >= _chunk_tokens()[None, :], dtype=BF16),
        "tri_s": jnp.asarray(np.kron(np.eye(nb_s), np.tril(np.ones((dec_rows, dec_rows)))), dtype=BF16),
        "e3": _head_expand(n_heads, SSD_HEAD_DIM, DT_REPLICAS),
        "ecol": _head_expand(n_heads, dec_rows, DT_REPLICAS),
        "wpx": _pack_w(w_proj_x[l]),
    }
    w_bf = _pack_w(wl)
    tail = w_bf[:, offs[6]:]
    assert all(o % d == 0 for o in offs[:5]) and d_x % LANES == 0 and d % d_x == 0
    for name, i in (("wu", 0), ("wv", 1), ("wga", 2), ("wz", 3), ("wxbc", 4)):
        windows = [_Cols(w_bf, c, d) for c in range(offs[i], offs[i + 1], d)]
        w[name] = windows if len(windows) > 1 else windows[0]
    w["wdt"] = _pack_w(rep3(sec(5), 0.0))
    w["wq"], w["wgx"] = _Cols(tail, 0, d_x), _Cols(tail, d_x, d_x)
    for k, name in enumerate(("wma", "wmb", "wmx")):
        w[name] = _Cols(tail, 2 * d_x + k * d, d)
    return w


def _layer(x3, k3, v3, w, *, nb_x, rows_x, heads_in_rows, tm_a, ws, bs, want_v, b_fn):
    bsz, seq, d = x3.shape
    m = _branch_x(x3, w["g"], w["wq"], w["wgx"], w["wmx"], k3, v3, w["wpx"], nb=nb_x, rows=rows_x,
                  heads_in_rows=heads_in_rows)
    m, vn = _branch_a(x3.reshape(bsz * seq, d), w["g"], w["wu"], w["wv"], w["wga"], w["wma"], w["lng"], w["lnb"],
                      ws, bs, w["wpa"], m.reshape(bsz * seq, d), tm=tm_a, want_v=want_v)
    return b_fn(x3, m.reshape(bsz, seq, d)), vn


def kernel(x_prompt, x_sample, mem_prompt, cache_mem_k, cache_mem_v, state_ssm, state_conv, norm_g, w_in, conv_w,
           conv_b, dt_bias, a_log, d_skip, ssd_norm_g, ln_v_g, ln_v_b, w_spatial, b_spatial, mem_norm_g, w_mem_kv,
           w_proj_a, w_proj_b, w_proj_x, w_out, final_norm_g):
    depth = w_in.shape[0]
    bsz, seq, d = x_prompt.shape
    dec_b, dec_rows, _ = x_sample.shape
    n_mem = mem_prompt.shape[1]
    d_x = w_proj_x.shape[1]
    d_inner = w_proj_b.shape[1]
    n_heads = a_log.shape[1]
    assert seq % CHUNK == 0 and CHUNK % dec_rows == 0 and dec_rows == SUBLANES
    assert DT_REPLICAS * n_heads <= LANES and n_heads * dec_rows % LANES == 0

    nb_s = _tile(dec_b, 8)
    nb_xs = _tile(dec_b, 16)
    yp, ys = x_prompt, x_sample
    outs = {k: [] for k in ("mk", "mv", "hp", "cp", "hs", "cs", "vs")}
    for l in range(depth):
        w = _layer_weights(l, d, dec_rows, nb_s, norm_g, w_in, conv_w, conv_b, dt_bias, a_log, d_skip, ssd_norm_g,
                           ln_v_g, ln_v_b, w_spatial, b_spatial, w_proj_a, w_proj_b, w_proj_x, w_out, final_norm_g)
        final_norm = l == depth - 1
        mk, mv = _mem_kv(mem_prompt.reshape(bsz * n_mem, d), mem_norm_g[l].reshape(1, d).astype(F32),
                         _pack_w(w_mem_kv[l]), _tile(bsz * n_mem, 512))
        (yp, hp, cp), _ = _layer(
            yp, mk.reshape(bsz, n_mem * (d_x // X_HEAD_DIM), X_HEAD_DIM),
            mv.reshape(bsz, n_mem * (d_x // X_HEAD_DIM), X_HEAD_DIM), w,
            nb_x=1, rows_x=_tile(seq, 512), heads_in_rows=False, tm_a=_tile(bsz * seq, 512), ws=w["ws_p"], bs=w["bs_p"], want_v=False,
            b_fn=functools.partial(_branch_b_prompt, w=w, tm=_tile(seq, 256), final_norm=final_norm))
        (ys, hs, cs), vs = _layer(
            ys, cache_mem_k[l].reshape(dec_b, n_mem * (d_x // X_HEAD_DIM), X_HEAD_DIM),
            cache_mem_v[l].reshape(dec_b, n_mem * (d_x // X_HEAD_DIM), X_HEAD_DIM), w,
            nb_x=nb_xs, rows_x=dec_rows, heads_in_rows=True, tm_a=_tile(dec_b * dec_rows, 512), ws=w["ws_s"], bs=w["bs_s"], want_v=True,
            b_fn=functools.partial(_branch_b_sample, conv0=state_conv, layer=l,
                                   h0=state_ssm[l].reshape(dec_b, d_inner, SSD_STATE), w=w, nb=nb_s,
                                   tm=_tile(dec_b * dec_rows, 512), final_norm=final_norm))
        outs["mk"].append(mk.reshape(bsz, n_mem, d_x // X_HEAD_DIM, X_HEAD_DIM))
        outs["mv"].append(mv.reshape(bsz, n_mem, d_x // X_HEAD_DIM, X_HEAD_DIM))
        outs["hp"].append(hp.reshape(bsz, n_heads, SSD_HEAD_DIM, SSD_STATE))
        outs["cp"].append(cp)
        outs["hs"].append(hs.reshape(dec_b, n_heads, SSD_HEAD_DIM, SSD_STATE))
        outs["cs"].append(cs)
        outs["vs"].append(vs.reshape(dec_b, dec_rows, d))
    st = lambda k: jnp.stack(outs[k])
    return (yp, ys, st("mk"), st("mv"), st("hp"), st("cp"), st("hs"), st("cs"), st("vs"))
```

```python
import functools
import math
from typing import NamedTuple

import jax
import jax.numpy as jnp
import numpy as np
from jax import lax
from jax.experimental import pallas as pl
from jax.experimental.pallas import tpu as pltpu

F32 = jnp.float32
BF16 = jnp.bfloat16
EPS = 1e-6
SQRT_HALF = math.sqrt(0.5)

LANES = 128
SUBLANES = 8
VMEM_LIMIT_BYTES = 56 * 1024 * 1024

CHUNK = 128
A_GROUPS = 8
SSD_HEAD_DIM = 64
SSD_GROUPS = 8
SSD_STATE = 128
X_HEAD_DIM = 128
N_BRANCH = 3
DT_REPLICAS = 3
SEQ_BATCH = 4

def _dot(a, b):
    return jnp.dot(a, b, preferred_element_type=F32)


def _dot_nt(a, b):
    return lax.dot_general(a, b, (((1,), (1,)), ((), ())), preferred_element_type=F32)


def _dot_tn(a, b):
    return lax.dot_general(a, b, (((0,), (0,)), ((), ())), preferred_element_type=F32)


MXU_COLS = 256


def _pack_w(w):
    return w.astype(BF16)


def _ncols(w):
    return sum(r.shape[1] for r in w) if isinstance(w, tuple) else w.shape[1]


def _dotw(a, w, cols=None):
    if not isinstance(w, tuple):
        return _dot(a, w[...] if cols is None else w[:, cols])
    width = w[0].shape[1]
    lo, hi = (0, width * len(w)) if cols is None else (cols.start, cols.stop)
    parts = []
    while lo < hi:
        k, off = divmod(lo, width)
        n = min(width - off, hi - lo)
        parts.append(_dot(a, w[k][:, off:off + n]))
        lo += n
    return parts[0] if len(parts) == 1 else jnp.concatenate(parts, axis=1)


def _dot_cols(a, w_ref, fn, width=MXU_COLS):
    n = w_ref.shape[1]
    width = min(width, n)
    blocks = [fn(_dotw(a, w_ref, slice(c, c + width))) for c in range(0, n, width)]
    return blocks[0] if len(blocks) == 1 else jnp.concatenate(blocks, axis=1)


def _rmsnorm(x, g):
    return x * lax.rsqrt(jnp.mean(x * x, axis=-1, keepdims=True) + EPS) * g


def _gelu(x):
    return 0.5 * x * (1.0 + lax.erf(x * SQRT_HALF))


def _sigmoid(x):
    return 0.5 * jnp.tanh(0.5 * x) + 0.5


def _silu(x):
    return x * _sigmoid(x)


def _bf16_parts(x):
    hi = x.astype(BF16)
    r1 = x - hi.astype(F32)
    mid = r1.astype(BF16)
    lo = (r1 - mid.astype(F32)).astype(BF16)
    return hi, mid, lo


def _split_by_replica(x, n_heads):
    hi, mid, lo = _bf16_parts(x)
    lane = lax.broadcasted_iota(jnp.int32, x.shape, 1)
    return jnp.where(lane < n_heads, hi, jnp.where(lane < 2 * n_heads, mid, lo))


def _cumsum_rows(tri, x):
    hi, mid, lo = _bf16_parts(x)
    return _dot(tri, hi) + _dot(tri, mid) + _dot(tri, lo)


def _memkv_kernel(x_ref, g_ref, w_ref, k_ref, v_ref, *, n_heads):
    tm = x_ref.shape[0]
    hd = k_ref.shape[-1]
    hn = _rmsnorm(x_ref[...], g_ref[...]).astype(BF16)
    kv = _dotw(hn, w_ref)
    for h in range(n_heads):
        k_ref[pl.ds(h, tm, stride=n_heads), :] = kv[:, h * hd:(h + 1) * hd]
        v_ref[pl.ds(h, tm, stride=n_heads), :] = kv[:, (n_heads + h) * hd:(n_heads + h + 1) * hd]


class _Cols(NamedTuple):
    mat: jax.Array
    start: int
    width: int

    @property
    def shape(self):
        return (self.mat.shape[0], self.width)


def _const_spec(op=None):
    if isinstance(op, _Cols):
        idx = op.start // op.width
        return pl.BlockSpec(op.shape, lambda *_: (0, idx), pipeline_mode=pl.Buffered(1))
    return pl.BlockSpec(memory_space=pltpu.VMEM)


def _flatten(ops):
    return [o for op in ops for o in (op if isinstance(op, list) else [op])]


def _const_specs(ops):
    return [_const_spec(op) for op in _flatten(ops)]


def _arrays(ops):
    return [op.mat if isinstance(op, _Cols) else op for op in _flatten(ops)]


def _regroup(kernel_fn, n_lead, ops):
    sizes = [len(op) if isinstance(op, list) else 0 for op in ops]

    def wrapped(*refs):
        refs = list(refs)
        args, pos = refs[:n_lead], n_lead
        for n in sizes:
            args.append(tuple(refs[pos:pos + n]) if n else refs[pos])
            pos += max(n, 1)
        return kernel_fn(*args, *refs[pos:])

    return wrapped


def _params(n_grid):
    return pltpu.CompilerParams(dimension_semantics=("arbitrary",) * n_grid, vmem_limit_bytes=VMEM_LIMIT_BYTES)


def _mem_kv(mem2d, g, w, tm):
    rows, d = mem2d.shape
    n_heads = w.shape[1] // 2 // X_HEAD_DIM
    out = jax.ShapeDtypeStruct((rows * n_heads, X_HEAD_DIM), F32)
    out_spec = pl.BlockSpec((tm * n_heads, X_HEAD_DIM), lambda i: (i, 0))
    return pl.pallas_call(
        functools.partial(_memkv_kernel, n_heads=n_heads),
        out_shape=(out, out),
        grid=(rows // tm,),
        in_specs=[pl.BlockSpec((tm, d), lambda i: (i, 0)), _const_spec(), _const_spec()],
        out_specs=(out_spec, out_spec),
        compiler_params=_params(1),
        name="mem_kv",
    )(mem2d, g, w)


def _x_kernel(x_ref, g_ref, wq_ref, wgx_ref, wmx_ref, k_ref, v_ref, wpx_ref, m_ref, *, nb, rows, head_dim):
    d = x_ref.shape[-1]
    x = x_ref[...].reshape(nb * rows, d)
    hn = _rmsnorm(x, g_ref[...]).astype(BF16)
    q = _dotw(hn, wq_ref)
    gate = _silu(_dotw(hn, wgx_ref))
    gm = _sigmoid(_dotw(hn, wmx_ref))
    n_heads = q.shape[1] // head_dim
    scale = head_dim ** -0.5
    outs = []
    n_mem = k_ref.shape[1] // n_heads
    for b in range(nb):
        qb = q[b * rows:(b + 1) * rows].astype(BF16)
        heads = range(n_heads)
        kb = [k_ref[b, pl.ds(h, n_mem, stride=n_heads), :].astype(BF16) for h in heads]
        vb = [v_ref[b, pl.ds(h, n_mem, stride=n_heads), :].astype(BF16) for h in heads]
        scores = [_dot_nt(qb[:, h * head_dim:(h + 1) * head_dim], kb[h]) * scale for h in heads]
        exps = [jnp.exp(s - jnp.max(s, axis=-1, keepdims=True)) for s in scores]
        probs = [(e / jnp.sum(e, axis=-1, keepdims=True)).astype(BF16) for e in exps]
        outs.append(jnp.concatenate([_dot(probs[h], vb[h]) for h in heads], axis=-1))
    o = outs[0] if nb == 1 else jnp.concatenate(outs, axis=0)
    hx = (o * gate).astype(BF16)
    m = gm * _dotw(hx, wpx_ref)
    m_ref[...] = m.reshape(nb, rows, d)


def _xs_kernel(x_ref, g_ref, wq_ref, wgx_ref, wmx_ref, k_ref, v_ref, wpx_ref, m_ref, *, nb, rows, head_dim):
    d = x_ref.shape[-1]
    x = x_ref[...].reshape(nb * rows, d)
    hn = _rmsnorm(x, g_ref[...]).astype(BF16)
    q = _dotw(hn, wq_ref)
    gate = _silu(_dotw(hn, wgx_ref))
    gm = _sigmoid(_dotw(hn, wmx_ref))
    n_heads = q.shape[1] // head_dim
    n_kv = k_ref.shape[1]
    scale = head_dim ** -0.5
    row_head = lax.broadcasted_iota(jnp.int32, (n_heads * rows, n_kv), 0) // rows
    col_head = lax.broadcasted_iota(jnp.int32, (n_heads * rows, n_kv), 1) % n_heads
    same_head = row_head == col_head
    seqs = range(nb)
    scores = []
    for b in seqs:
        qb = q[b * rows:(b + 1) * rows]
        q_heads = jnp.concatenate([qb[:, h * head_dim:(h + 1) * head_dim] for h in range(n_heads)], axis=0)
        scores.append(_dot_nt(q_heads.astype(BF16), k_ref[b].astype(BF16)))
    masked = [jnp.where(same_head, s * scale, -jnp.inf) for s in scores]
    exps = [jnp.exp(s - jnp.max(s, axis=-1, keepdims=True)) for s in masked]
    probs = [(e / jnp.sum(e, axis=-1, keepdims=True)).astype(BF16) for e in exps]
    o_heads = [_dot(probs[b], v_ref[b].astype(BF16)) for b in seqs]
    outs = [jnp.concatenate([oh[h * rows:(h + 1) * rows] for h in range(n_heads)], axis=1) for oh in o_heads]
    o = outs[0] if nb == 1 else jnp.concatenate(outs, axis=0)
    hx = (o * gate).astype(BF16)
    m = gm * _dotw(hx, wpx_ref)
    m_ref[...] = m.reshape(nb, rows, d)


def _branch_x(x3, g, wq, wgx, wmx, k3, v3, wpx, *, nb, rows, heads_in_rows):
    bsz, seq, d = x3.shape
    n_mem, dx = k3.shape[1], k3.shape[2]
    kern = functools.partial(_xs_kernel if heads_in_rows else _x_kernel, nb=nb, rows=rows, head_dim=X_HEAD_DIM)
    return pl.pallas_call(
        kern,
        out_shape=jax.ShapeDtypeStruct((bsz, seq, d), F32),
        grid=(bsz // nb, seq // rows),
        in_specs=[pl.BlockSpec((nb, rows, d), lambda i, j: (i, j, 0))] + _const_specs([g, wq, wgx, wmx]) + [
            pl.BlockSpec((nb, n_mem, dx), lambda i, j: (i, 0, 0)),
            pl.BlockSpec((nb, n_mem, dx), lambda i, j: (i, 0, 0)),
            _const_spec(wpx),
        ],
        out_specs=pl.BlockSpec((nb, rows, d), lambda i, j: (i, j, 0)),
        compiler_params=_params(2),
        name="branch_x",
    )(x3, *_arrays([g, wq, wgx, wmx]), k3, v3, wpx)


def _a_kernel(x_ref, g_ref, wu_ref, wv_ref, wga_ref, wma_ref, lng_ref, lnb_ref, ws_ref, bs_ref, wpa_ref, min_ref,
              mout_ref, *maybe_v_ref, chunk, groups, sub):
    tm, d = x_ref.shape
    gd = d // groups
    bias = bs_ref[...]
    for t in range(tm // sub):
        ts = slice(t * sub, (t + 1) * sub)
        hn = _rmsnorm(x_ref[ts, :], g_ref[...]).astype(BF16)
        v = _dot_cols(hn, wv_ref, _gelu)
        u = _dot_cols(hn, wu_ref, _gelu)
        vc = v - jnp.mean(v, axis=-1, keepdims=True)
        vn = vc * lax.rsqrt(jnp.mean(vc * vc, axis=-1, keepdims=True) + EPS) * lng_ref[...] + lnb_ref[...]
        if maybe_v_ref:
            maybe_v_ref[0][ts, :] = vn
        ga = _dot_cols(hn, wga_ref, _silu)
        vb = vn.astype(BF16)
        n_c = sub // chunk
        wide = [_dot(ws_ref[gi], jnp.concatenate([vb[c * chunk:(c + 1) * chunk, gi * gd:(gi + 1) * gd]
                                                  for c in range(n_c)], axis=1)) for gi in range(groups)]
        mixed = [jnp.concatenate([wg[:, c * gd:(c + 1) * gd] for wg in wide], axis=1) + bias for c in range(n_c)]
        s = mixed[0] if n_c == 1 else jnp.concatenate(mixed, axis=0)
        ha = (u * s * ga).astype(BF16)
        gm = _sigmoid(_dotw(hn, wma_ref))
        mout_ref[ts, :] = min_ref[ts, :] + gm * _dotw(ha, wpa_ref)


def _branch_a(x2, g, wu, wv, wga, wma, lng, lnb, ws, bs, wpa, m_in, *, tm, want_v):
    rows, d = x2.shape
    row_spec = pl.BlockSpec((tm, d), lambda i: (i, 0))
    out_shape = [jax.ShapeDtypeStruct((rows, d), F32)]
    out_specs = [row_spec]
    if want_v:
        out_shape.append(jax.ShapeDtypeStruct((rows, d), F32))
        out_specs.append(row_spec)
    kern = functools.partial(_a_kernel, chunk=CHUNK, groups=A_GROUPS, sub=min(tm, 256))
    consts = [g, wu, wv, wga, wma, lng, lnb, ws, bs, wpa]
    res = pl.pallas_call(
        kern,
        out_shape=tuple(out_shape),
        grid=(rows // tm,),
        in_specs=[row_spec] + _const_specs(consts) + [row_spec],
        out_specs=tuple(out_specs),
        compiler_params=_params(1),
        name="branch_a",
    )(x2, *_arrays(consts), m_in)
    return res if want_v else (res[0], None)


def _finish_b(x, y, g_ref, wz_ref, wmb_ref, ng_ref, wpb_ref, m_in, wout_ref, fg_ref, *, n_groups, final_norm):
    hn = _rmsnorm(x, g_ref[...]).astype(BF16)
    gm = _sigmoid(_dotw(hn, wmb_ref))
    yf = y * _silu(_dotw(hn, wz_ref))
    gw = yf.shape[1] // n_groups
    parts = []
    for gi in range(n_groups):
        yg = yf[:, gi * gw:(gi + 1) * gw]
        parts.append(yg * lax.rsqrt(jnp.mean(yg * yg, axis=-1, keepdims=True) + EPS))
    hb = (jnp.concatenate(parts, axis=-1) * ng_ref[...]).astype(BF16)
    m = m_in + gm * _dotw(hb, wpb_ref)
    out = x + _dotw(m.astype(BF16), wout_ref)
    return _rmsnorm(out, fg_ref[...]) if final_norm else out


HANDOFF = ("xs", "bm", "cm", "zs", "dt", "gm", "tail")


def _bp_stage1(x_ref, g_ref, wz_ref, wxbc_ref, wdt_ref, wmb_ref, cw_ref, cb_ref, dtb_ref, perm_ref, xs_scr, out,
               first_tile, *, tm, chunk, d_inner, bc_width, conv_w):
    pad = SUBLANES
    vpc = chunk // pad
    n_chunks = tm // chunk
    width = MXU_COLS
    hn = _rmsnorm(x_ref[0], g_ref[...]).astype(BF16)
    for c0 in range(0, out["gm"].shape[1], width):
        cols = slice(c0, c0 + width)
        out["gm"][:, cols] = _sigmoid(_dotw(hn, wmb_ref, cols))
        yield
    hn = _dot(perm_ref[0], hn).astype(BF16)
    out["dt"][...] = jax.nn.softplus(_dotw(hn, wdt_ref) + dtb_ref[...])
    yield
    for c0 in range(0, d_inner, width):
        cols = slice(c0, c0 + width)
        out["zs"][:, cols] = _silu(_dotw(hn, wz_ref, cols))
        yield
    first_sublane = lax.broadcasted_iota(jnp.int32, (pad, width), 0) == 0
    for c0 in range(0, _ncols(wxbc_ref), width):
        cols = slice(c0, c0 + width)
        raw_all = _dotw(hn, wxbc_ref, cols)
        prev_tail = jnp.where(first_tile, 0.0, xs_scr[:, cols])
        acts = []
        for c in range(n_chunks):
            raw = raw_all[c * chunk:(c + 1) * chunk]
            conv = cb_ref[:, cols] + raw * cw_ref[conv_w - 1:conv_w, cols]
            for back in range(1, conv_w):
                head = []
                for v in range(back):
                    w = vpc - back + v
                    cur = pltpu.roll(raw[w * pad:(w + 1) * pad], 1, axis=0)
                    prv = pltpu.roll(prev_tail[(w - vpc + conv_w - 1) * pad:(w - vpc + conv_w) * pad], 1, axis=0)
                    head.append(jnp.where(first_sublane, prv, cur))
                shifted = jnp.concatenate(head + [raw[:chunk - back * pad]], axis=0)
                conv = conv + shifted * cw_ref[conv_w - 1 - back:conv_w - back, cols]
            acts.append(_silu(conv))
            prev_tail = raw[chunk - (conv_w - 1) * pad:]
        xs_scr[:, cols] = prev_tail
        out["tail"][:, cols] = prev_tail
        act = acts[0] if n_chunks == 1 else jnp.concatenate(acts, axis=0)
        if c0 < d_inner:
            out["xs"][:, cols] = act
        elif c0 < d_inner + bc_width:
            out["bm"][:, c0 - d_inner:c0 - d_inner + width] = act.astype(BF16)
        else:
            out["cm"][:, c0 - d_inner - bc_width:c0 - d_inner - bc_width + width] = act.astype(BF16)
        yield


def _bp_stage2(x_ref, min_ref, alog_ref, dsk_ref, ng_ref, wpb_ref, wout_ref, fg_ref, tri_ref, e3_ref, perm_ref,
               y_ref, st_scr, inp, first_tile, *, tm, chunk, n_groups, n_heads, head_dim, n_state, final_norm):
    pad = SUBLANES
    vpc = chunk // pad
    hpg = n_heads // n_groups
    gw = hpg * head_dim
    width = MXU_COLS

    def token_of(q):
        return (q % pad) * vpc + q // pad

    li = token_of(lax.broadcasted_iota(jnp.int32, (chunk, chunk), 0))
    si = token_of(lax.broadcasted_iota(jnp.int32, (chunk, chunk), 1))
    causal = li >= si
    head_of_lane = lax.broadcasted_iota(jnp.int32, (chunk, gw), 1) // head_dim
    tri = tri_ref[...]
    e3 = e3_ref[...]
    a_row = -jnp.exp(alog_ref[...])

    hb_rows = []
    for c in range(tm // chunk):
        rs = slice(c * chunk, (c + 1) * chunk)
        dt = inp["dt"][rs, :]
        acs = _cumsum_rows(tri, dt * a_row)
        acs_t = acs.T
        acs_e = _dot(_split_by_replica(acs, n_heads).astype(BF16), e3)
        a_last = acs_e[chunk - 1:chunk, :]
        xs_c = inp["xs"][rs, :]
        xdt_c = xs_c * _dot(_split_by_replica(dt, n_heads).astype(BF16), e3)
        xdt_b = xdt_c.astype(BF16)
        xd_state = (xdt_c * jnp.exp(a_last - acs_e)).astype(BF16)
        decay_in = jnp.exp(acs_e)
        st_prev = st_scr[...]
        if c == 0:
            st_prev = jnp.where(first_tile, 0.0, st_prev)
        st_prev_b = st_prev.astype(BF16)
        yield
        groups = range(n_groups)
        n_sl = [slice(gi * n_state, (gi + 1) * n_state) for gi in groups]
        g_sl = [slice(gi * gw, (gi + 1) * gw) for gi in groups]
        scores = [_dot_nt(inp["cm"][rs, n_sl[gi]], inp["bm"][rs, n_sl[gi]]) for gi in groups]
        yield
        y_offs = [_dot(inp["cm"][rs, n_sl[gi]], st_prev_b[:, g_sl[gi]]) for gi in groups]
        st_parts = [_dot_tn(inp["bm"][rs, n_sl[gi]], xd_state[:, g_sl[gi]]) for gi in groups]
        st_scr[...] = st_prev * jnp.exp(a_last) + jnp.concatenate(st_parts, axis=1)
        yield
        y_diags = []
        for gi in groups:
            m_heads, x_blocks = [], []
            xg = xdt_b[:, g_sl[gi]]
            for r in range(hpg):
                h = gi * hpg + r
                diff = acs[:, h:h + 1] - acs_t[h:h + 1, :]
                decay = jnp.exp(jnp.where(causal, diff, -jnp.inf))
                m_heads.append((scores[gi] * decay).astype(BF16))
                x_blocks.append(jnp.where(head_of_lane == r, xg, jnp.zeros_like(xg)))
            y_diags.append(_dot(jnp.concatenate(m_heads, axis=1), jnp.concatenate(x_blocks, axis=0)))
            yield
        hb_parts = []
        for gi in groups:
            gs = g_sl[gi]
            y = y_diags[gi] + y_offs[gi] * decay_in[:, gs] + xs_c[:, gs] * dsk_ref[:, gs]
            yf = y * inp["zs"][rs, gs]
            yn = yf * lax.rsqrt(jnp.mean(yf * yf, axis=-1, keepdims=True) + EPS)
            hb_parts.append((yn * ng_ref[:, gs]).astype(BF16))
            if gi % 2:
                yield
        hb_rows.append(jnp.concatenate(hb_parts, axis=1))
    hb = hb_rows[0] if len(hb_rows) == 1 else jnp.concatenate(hb_rows, axis=0)
    hb = _dot(perm_ref[1], hb).astype(BF16)
    yield
    m_parts = []
    for c0 in range(0, wpb_ref.shape[1], width):
        cols = slice(c0, c0 + width)
        m_parts.append((min_ref[0, :, cols] + inp["gm"][:, cols] * _dotw(hb, wpb_ref, cols)).astype(BF16))
        yield
    m = jnp.concatenate(m_parts, axis=1)
    out = x_ref[0] + _dotw(m, wout_ref)
    y_ref[0] = _rmsnorm(out, fg_ref[...]) if final_norm else out


BP_WEIGHTS = ("wz", "wxbc", "wdt", "wmb", "wpb", "wout")


def _bp_kernel(x1_ref, x2_ref, min_ref, g_ref, cw_ref, cb_ref, dtb_ref, alog_ref, dsk_ref, ng_ref, fg_ref, tri_ref,
               e3_ref, perm_ref, *rest,
               n_hbm, copies, nj, n_tiles, tm, chunk, n_groups, n_heads, head_dim, n_state, conv_w, final_norm):
    hbm_refs, (y_ref, ssm_ref, conv_ref, xs_scr, st_scr), rest = rest[:n_hbm], rest[n_hbm:n_hbm + 5], rest[n_hbm + 5:]
    w_scr, copy_sem, slot_refs = rest[:len(BP_WEIGHTS)], rest[len(BP_WEIGHTS)], rest[len(BP_WEIGHTS) + 1:]
    wz_ref, wxbc_ref, wdt_ref, wmb_ref, wpb_ref, wout_ref = w_scr
    t = pl.program_id(0)
    d_inner = n_heads * head_dim
    pad = SUBLANES
    tile1 = jnp.minimum(t, n_tiles - 1)
    tile2 = jnp.maximum(t - 1, 0)
    n_h = len(HANDOFF)
    slots = [dict(zip(HANDOFF, slot_refs[i * n_h:(i + 1) * n_h])) for i in range(2)]

    @pl.when(t == 0)
    def _init():
        dmas = [pltpu.make_async_copy(hbm_refs[src].at[:, pl.ds(start, width)],
                                      w_scr[dst].at[:, pl.ds(dst_start, width)], copy_sem.at[i])
                for i, (src, start, width, dst, dst_start) in enumerate(copies)]
        for dma in dmas:
            dma.start()
        for ref in slots[1].values():
            ref[...] = jnp.zeros_like(ref)
        st_scr[...] = jnp.zeros_like(st_scr)
        xs_scr[...] = jnp.zeros_like(xs_scr)
        for dma in dmas:
            dma.wait()

    def step(slot_w, slot_r):
        s1 = _bp_stage1(x1_ref, g_ref, wz_ref, wxbc_ref, wdt_ref, wmb_ref, cw_ref, cb_ref, dtb_ref, perm_ref, xs_scr,
                        slot_w, tile1 % nj == 0, tm=tm, chunk=chunk, d_inner=d_inner,
                        bc_width=n_groups * n_state, conv_w=conv_w)
        s2 = _bp_stage2(x2_ref, min_ref, alog_ref, dsk_ref, ng_ref, wpb_ref, wout_ref, fg_ref, tri_ref, e3_ref,
                        perm_ref, y_ref, st_scr, slot_r, tile2 % nj == 0, tm=tm, chunk=chunk, n_groups=n_groups,
                        n_heads=n_heads, head_dim=head_dim, n_state=n_state, final_norm=final_norm)
        live = [s2, s1]
        while live:
            for gen in list(live):
                if next(gen, StopIteration) is StopIteration:
                    live.remove(gen)

        @pl.when((t > 0) & (tile2 % nj == nj - 1))
        def _seq_end():
            tail = slot_r["tail"][...]
            conv_ref[0] = jnp.concatenate([tail[(v + 1) * pad - 1:(v + 1) * pad] for v in range(conv_w - 1)], axis=0)
            ssm_ref[0] = st_scr[...].T

    for parity in range(2):
        pl.when(t % 2 == parity)(functools.partial(step, slots[parity], slots[1 - parity]))


def _branch_b_prompt(x3, m3, w, *, tm, final_norm):
    bsz, seq, d = x3.shape
    conv_dim = w["cw"].shape[1]
    d_inner = w["ng"].shape[1]
    n_heads = d_inner // SSD_HEAD_DIM
    conv_w = w["cw"].shape[0]
    nj = seq // tm
    n_tiles = bsz * nj
    bc_width = SSD_GROUPS * SSD_STATE
    hbm_ops, copies, w_shapes = [], [], []
    for wi, name in enumerate(BP_WEIGHTS):
        parts = w[name] if isinstance(w[name], list) else [w[name]]
        dst = 0
        for part in parts:
            mat, start = (part.mat, part.start) if isinstance(part, _Cols) else (part, 0)
            src = next((i for i, op in enumerate(hbm_ops) if op is mat), len(hbm_ops))
            if src == len(hbm_ops):
                hbm_ops.append(mat)
            copies.append((src, start, part.shape[1], wi, dst))
            dst += part.shape[1]
        w_shapes.append((parts[0].shape[0], dst))
    kern = functools.partial(_bp_kernel, n_hbm=len(hbm_ops), copies=tuple(copies), nj=nj, n_tiles=n_tiles, tm=tm,
                             chunk=CHUNK, n_groups=SSD_GROUPS, n_heads=n_heads, head_dim=SSD_HEAD_DIM,
                             n_state=SSD_STATE, conv_w=conv_w, final_norm=final_norm)

    def tile1(t):
        return jnp.minimum(t, n_tiles - 1)

    def tile2(t):
        return jnp.maximum(t - 1, 0)

    spec1 = pl.BlockSpec((1, tm, d), lambda t: (tile1(t) // nj, tile1(t) % nj, 0))
    spec2 = pl.BlockSpec((1, tm, d), lambda t: (tile2(t) // nj, tile2(t) % nj, 0))
    consts = [w[k] for k in ("g", "cw", "cb", "dtb", "alog", "dsk", "ng", "fg", "tri", "e3")]
    tok = (np.arange(tm) // CHUNK) * CHUNK + np.tile(_chunk_tokens(), tm // CHUNK)
    p = (tok[:, None] == np.arange(tm)[None, :]).astype(np.float32)
    perm = jnp.asarray(np.stack([p, p.T]), dtype=BF16)
    handoff = {"xs": ((tm, d_inner), F32), "bm": ((tm, bc_width), BF16), "cm": ((tm, bc_width), BF16),
               "zs": ((tm, d_inner), F32), "dt": ((tm, LANES), F32), "gm": ((tm, d), F32),
               "tail": (((conv_w - 1) * SUBLANES, conv_dim), F32)}
    return pl.pallas_call(
        kern,
        out_shape=(jax.ShapeDtypeStruct((bsz, seq, d), F32),
                   jax.ShapeDtypeStruct((bsz, d_inner, SSD_STATE), F32),
                   jax.ShapeDtypeStruct((bsz, conv_w - 1, conv_dim), F32)),
        grid=(n_tiles + 1,),
        in_specs=([spec1, spec2, spec2] + _const_specs(consts + [perm])
                  + [pl.BlockSpec(memory_space=pl.ANY)] * len(hbm_ops)),
        out_specs=(spec2,
                   pl.BlockSpec((1, d_inner, SSD_STATE), lambda t: (tile2(t) // nj, 0, 0)),
                   pl.BlockSpec((1, conv_w - 1, conv_dim), lambda t: (tile2(t) // nj, 0, 0))),
        scratch_shapes=([pltpu.VMEM(((conv_w - 1) * SUBLANES, conv_dim), F32), pltpu.VMEM((SSD_STATE, d_inner), F32)]
                        + [pltpu.VMEM(shape, BF16) for shape in w_shapes]
                        + [pltpu.SemaphoreType.DMA((len(copies),))]
                        + [pltpu.VMEM(*handoff[k]) for _ in range(2) for k in HANDOFF]),
        compiler_params=_params(1),
        name="branch_b_prompt",
    )(x3, x3, m3, *_arrays(consts), perm, *hbm_ops)


def _bs_kernel(x_ref, g_ref, wxbc_ref, wdt_ref, cw_ref, cb_ref, dtb_ref, alog_ref, dsk_ref,
               tri_ref, e3_ref, ecol_ref, conv0_ref, h0_ref,
               y_ref, hnew_ref, convnew_ref, xs_scr, tr_scr,
               *, nb, rows, n_groups, n_heads, head_dim, n_state, conv_w):
    hpg = n_heads // n_groups
    gw = hpg * head_dim
    d_inner = n_heads * head_dim
    d = x_ref.shape[-1]
    r_all = nb * rows
    hs_w = n_heads * rows

    @pl.when(pl.program_id(0) == 0)
    def _init():
        tr_scr[...] = jnp.zeros_like(tr_scr)
        xs_scr[...] = jnp.zeros_like(xs_scr)

    x = x_ref[...].reshape(r_all, d)
    hn = _rmsnorm(x, g_ref[...]).astype(BF16)
    xbc_raw = _dotw(hn, wxbc_ref)
    dt = jax.nn.softplus(_dotw(hn, wdt_ref) + dtb_ref[...])

    raw = xbc_raw.reshape(nb, rows, xbc_raw.shape[1])
    xs_scr[:, rows - (conv_w - 1):, :] = conv0_ref[...]
    state = xs_scr[...]
    row_in_seq = lax.broadcasted_iota(jnp.int32, raw.shape, 1)
    conv = cb_ref[...] + raw * cw_ref[conv_w - 1:conv_w, :]
    for back in range(1, conv_w):
        shifted = jnp.where(row_in_seq < back, pltpu.roll(state, back, axis=1), pltpu.roll(raw, back, axis=1))
        conv = conv + shifted * cw_ref[conv_w - 1 - back:conv_w - back, :]
    convnew_ref[...] = raw[:, rows - (conv_w - 1):, :]
    xbc = _silu(conv.reshape(r_all, xbc_raw.shape[1]))
    xs = xbc[:, :d_inner]
    bm = xbc[:, d_inner:d_inner + n_groups * n_state]
    cm = xbc[:, d_inner + n_groups * n_state:]

    a_row = -jnp.exp(alog_ref[...])
    acs = _cumsum_rows(tri_ref[...], dt * a_row)
    e3 = e3_ref[...]
    acs_split = _split_by_replica(acs, n_heads).astype(BF16)
    acs_e = _dot(acs_split, e3)
    xdt = xs * _dot(_split_by_replica(dt, n_heads).astype(BF16), e3)
    decay_in = jnp.exp(acs_e)

    acs_col = _dot(acs_split, ecol_ref[...])
    l_idx = lax.broadcasted_iota(jnp.int32, (r_all, hs_w), 0) % rows
    s_idx = lax.broadcasted_iota(jnp.int32, (r_all, hs_w), 1) % rows
    on_diag = jnp.where(l_idx == s_idx, acs_col, 0.0).reshape(nb, rows, hs_w)
    acs_row = jnp.broadcast_to(jnp.sum(on_diag, axis=1, keepdims=True), (nb, rows, hs_w)).reshape(r_all, hs_w)
    decay = jnp.exp(jnp.where(l_idx >= s_idx, acs_col - acs_row, -jnp.inf))

    slot = LANES // nb
    ones_rows = lax.broadcasted_iota(jnp.int32, (slot, n_state), 0)
    ones_blk = jnp.where((ones_rows >= rows) & (ones_rows < rows + 3), 1.0, 0.0)

    def seq_rows(b):
        return slice(b * rows, (b + 1) * rows)

    lane_in_tile = lax.broadcasted_iota(jnp.int32, (rows, LANES), 1)

    def expand_rows(v, start_of, width):
        c = v.shape[1]
        blocks = []
        for h in range(n_heads):
            start = start_of(h)
            lo = start // LANES * LANES
            hi = max(lo + LANES, start + width)
            piece = v[:, lo:hi]
            if width < LANES:
                piece = jnp.where(lane_in_tile // width == (start - lo) // width, piece, 0.0)
            parts = [jnp.zeros((rows, lo), F32)] if lo else []
            parts.append(piece)
            if hi < c:
                parts.append(jnp.zeros((rows, c - hi), F32))
            blocks.append(jnp.concatenate(parts, axis=1) if len(parts) > 1 else piece)
        return jnp.concatenate(blocks, axis=0).astype(BF16)

    y_diags = []
    for b0 in range(0, nb, SEQ_BATCH):
        seqs = range(b0, min(b0 + SEQ_BATCH, nb))
        b_exps = [expand_rows(bm[seq_rows(b)], lambda h: h // hpg * n_state, n_state) for b in seqs]
        scores = [_dot_nt(cm[seq_rows(b)].astype(BF16), be) for b, be in zip(seqs, b_exps)]
        x_blks = [expand_rows(xdt[seq_rows(b)], lambda h: h * head_dim, head_dim) for b in seqs]
        m_alls = [(s * decay[seq_rows(b)]).astype(BF16) for b, s in zip(seqs, scores)]
        y_diags += [_dot(m, xb) for m, xb in zip(m_alls, x_blks)]
    for b in range(nb):
        a_last = acs_e[(b + 1) * rows - 1:(b + 1) * rows, :]
        d_hi, d_mid, d_lo = _bf16_parts(jnp.exp(a_last))
        tr_scr[b * slot:b * slot + rows, :] = xdt[seq_rows(b)] * jnp.exp(a_last - acs_e[seq_rows(b)])
        tr_scr[b * slot + rows:b * slot + rows + 1, :] = d_hi.astype(F32)
        tr_scr[b * slot + rows + 1:b * slot + rows + 2, :] = d_mid.astype(F32)
        tr_scr[b * slot + rows + 2:b * slot + rows + 3, :] = d_lo.astype(F32)
    tr_t = tr_scr[...].T.astype(BF16)

    ys = []
    for b in range(nb):
        rs = slice(b * rows, (b + 1) * rows)
        bm_b, cm_bb = bm[rs], cm[rs].astype(BF16)
        y_parts = []
        for gi in range(n_groups):
            ns = slice(gi * n_state, (gi + 1) * n_state)
            gs = slice(gi * gw, (gi + 1) * gw)
            h0_g = h0_ref[b, gs, :]
            y_parts.append(_dot_nt(cm_bb[:, ns], h0_g.astype(BF16)))
            w_seq = jnp.concatenate(
                [jnp.concatenate([bm_b[:, ns], jnp.zeros((slot - rows, n_state), F32)], axis=0), ones_blk], axis=1)
            pieces = ([jnp.zeros((b * slot, 2 * n_state), F32)] if b else []) + [w_seq]
            if b < nb - 1:
                pieces.append(jnp.zeros((LANES - (b + 1) * slot, 2 * n_state), F32))
            upd = _dot(tr_t[gs, :], jnp.concatenate(pieces, axis=0).astype(BF16))
            hnew_ref[b, gs, :] = h0_g * upd[:, n_state:] + upd[:, :n_state]
        ys.append(y_diags[b] + jnp.concatenate(y_parts, axis=1) * decay_in[rs])
    y = jnp.concatenate(ys, axis=0) + xs * dsk_ref[...]
    y_ref[...] = y.reshape(nb, rows, d_inner)


def _bfin_kernel(x_ref, y_ref, min_ref, g_ref, wz_ref, wmb_ref, ng_ref, wpb_ref, wout_ref, fg_ref, out_ref,
                 *, n_groups, final_norm):
    out_ref[...] = _finish_b(x_ref[...], y_ref[...], g_ref, wz_ref, wmb_ref, ng_ref, wpb_ref, min_ref[...],
                             wout_ref, fg_ref, n_groups=n_groups, final_norm=final_norm)


def _branch_b_sample(x3, m3, conv0, h0, w, *, layer, nb, tm, final_norm):
    bsz, rows, d = x3.shape
    conv_dim = w["cw"].shape[1]
    d_inner = w["ng"].shape[1]
    n_heads = d_inner // SSD_HEAD_DIM
    conv_w = w["cw"].shape[0]
    kern = functools.partial(_bs_kernel, nb=nb, rows=rows, n_groups=SSD_GROUPS, n_heads=n_heads,
                             head_dim=SSD_HEAD_DIM, n_state=SSD_STATE, conv_w=conv_w)

    def seq_spec(width):
        return pl.BlockSpec((nb, rows, width), lambda i: (i, 0, 0))

    conv_spec = pl.BlockSpec((nb, conv_w - 1, conv_dim), lambda i: (i, 0, 0))
    h_spec = pl.BlockSpec((nb, d_inner, SSD_STATE), lambda i: (i, 0, 0))
    consts = [w[k] for k in ("g", "wxbc", "wdt", "cw", "cb", "dtb", "alog", "dsk", "tri_s", "e3", "ecol")]
    y, h_new, conv_new = pl.pallas_call(
        _regroup(kern, 1, consts),
        out_shape=(jax.ShapeDtypeStruct((bsz, rows, d_inner), F32),
                   jax.ShapeDtypeStruct((bsz, d_inner, SSD_STATE), F32),
                   jax.ShapeDtypeStruct((bsz, conv_w - 1, conv_dim), F32)),
        grid=(bsz // nb,),
        in_specs=[seq_spec(d)] + _const_specs(consts) + [
            pl.BlockSpec((None, nb, conv_w - 1, conv_dim), lambda i: (layer, i, 0, 0)), h_spec],
        out_specs=(seq_spec(d_inner), h_spec, conv_spec),
        scratch_shapes=[pltpu.VMEM((nb, SUBLANES, conv_dim), F32), pltpu.VMEM((LANES, d_inner), F32)],
        compiler_params=_params(1),
        name="branch_b_sample",
    )(x3, *_arrays(consts), conv0, h0)

    n_rows = bsz * rows
    fin_consts = [w[k] for k in ("g", "wz", "wmb", "ng", "wpb", "wout", "fg")]

    def rows_spec(width):
        return pl.BlockSpec((tm, width), lambda i: (i, 0))

    out = pl.pallas_call(
        _regroup(functools.partial(_bfin_kernel, n_groups=SSD_GROUPS, final_norm=final_norm), 3, fin_consts),
        out_shape=jax.ShapeDtypeStruct((n_rows, d), F32),
        grid=(n_rows // tm,),
        in_specs=[rows_spec(d), rows_spec(d_inner), rows_spec(d)] + _const_specs(fin_consts),
        out_specs=rows_spec(d),
        compiler_params=_params(1),
        name="branch_b_sample_out",
    )(x3.reshape(n_rows, d), y.reshape(n_rows, d_inner), m3.reshape(n_rows, d), *_arrays(fin_consts))
    return out.reshape(bsz, rows, d), h_new, conv_new


def _tile(n, pref):
    return pref if n % pref == 0 else n


def _chunk_tokens():
    q = np.arange(CHUNK)
    return (q % SUBLANES) * (CHUNK // SUBLANES) + q // SUBLANES


def _head_expand(n_heads, width, n_rep):
    j = np.arange(LANES)[:, None]
    c = np.arange(n_heads * width)[None, :]
    return jnp.asarray(((j < n_rep * n_heads) & (j % n_heads == c // width)).astype(np.float32), dtype=BF16)


def _layer_weights(l, d, dec_rows, nb_s, norm_g, w_in, conv_w, conv_b, dt_bias, a_log, d_skip, ssd_norm_g, ln_v_g,
                   ln_v_b, w_spatial, b_spatial, w_proj_a, w_proj_b, w_proj_x, w_out, final_norm_g):
    d_a = d
    d_inner = w_proj_b.shape[1]
    conv_dim = conv_w.shape[2]
    n_heads = a_log.shape[1]
    d_x = w_proj_x.shape[1]
    sizes = (d_a, d_a, d_a, d_inner, conv_dim, n_heads, d_x, d_x, N_BRANCH * d)
    offs = np.concatenate([[0], np.cumsum(sizes)])
    wl = w_in[l]
    sec = lambda i: wl[:, offs[i]:offs[i + 1]]
    row = lambda v: v.reshape(1, -1).astype(F32)
    rep3 = lambda v, fill: jnp.concatenate(
        [v] * DT_REPLICAS + [jnp.full(v.shape[:-1] + (LANES - DT_REPLICAS * v.shape[-1],), fill, v.dtype)], axis=-1)

    tril = jnp.tril(jnp.ones((CHUNK, CHUNK), F32))
    ws_p = jnp.where(tril[None] > 0, w_spatial[l], 0.0)
    bs_p = jnp.repeat(b_spatial[l].T, d_a // A_GROUPS, axis=1)
    n_seq = CHUNK // dec_rows
    sel = jnp.asarray(np.tile(np.eye(dec_rows, dtype=np.float32), (n_seq, 1)))
    blocks = jnp.asarray(np.kron(np.eye(n_seq, dtype=np.float32), np.ones((dec_rows, dec_rows), np.float32)))
    ws_s = jnp.einsum("ia,gab,jb->gij", sel, ws_p[:, :dec_rows, :dec_rows], sel,
                      precision=lax.Precision.HIGHEST) * blocks
    bs_s = jnp.tile(bs_p[:dec_rows], (n_seq, 1))

    w = {
        "g": row(norm_g[l]),
        "lng": row(ln_v_g[l]), "lnb": row(ln_v_b[l]),
        "ws_p": ws_p.astype(BF16), "bs_p": bs_p, "ws_s": ws_s.astype(BF16), "bs_s": bs_s,
        "wpa": _pack_w(w_proj_a[l]),
        "cw": conv_w[l].astype(F32), "cb": row(conv_b[l]),
        "dtb": rep3(row(dt_bias[l]), 0.0), "alog": rep3(row(a_log[l]), 0.0),
        "dsk": jnp.repeat(row(d_skip[l]), SSD_HEAD_DIM, axis=1),
        "ng": row(ssd_norm_g[l]),
        "wpb": _pack_w(w_proj_b[l]),
        "wout": _pack_w(w_out[l]), "fg": row(final_norm_g),
        "tri": jnp.asarray(_chunk_tokens()[:, None] >= _chunk_tokens()[None, :], dtype=BF16),
        "tri_s": jnp.asarray(np.kron(np.eye(nb_s), np.tril(np.ones((dec_rows, dec_rows)))), dtype=BF16),
        "e3": _head_expand(n_heads, SSD_HEAD_DIM, DT_REPLICAS),
        "ecol": _head_expand(n_heads, dec_rows, DT_REPLICAS),
        "wpx": _pack_w(w_proj_x[l]),
    }
    w_bf = _pack_w(wl)
    tail = w_bf[:, offs[6]:]
    assert all(o % d == 0 for o in offs[:5]) and d_x % LANES == 0 and d % d_x == 0
    for name, i in (("wu", 0), ("wv", 1), ("wga", 2), ("wz", 3), ("wxbc", 4)):
        windows = [_Cols(w_bf, c, d) for c in range(offs[i], offs[i + 1], d)]
        w[name] = windows if len(windows) > 1 else windows[0]
    w["wdt"] = _pack_w(rep3(sec(5), 0.0))
    w["wq"], w["wgx"] = _Cols(tail, 0, d_x), _Cols(tail, d_x, d_x)
    for k, name in enumerate(("wma", "wmb", "wmx")):
        w[name] = _Cols(tail, 2 * d_x + k * d, d)
    return w


def _layer(x3, k3, v3, w, *, nb_x, rows_x, heads_in_rows, tm_a, ws, bs, want_v, b_fn):
    bsz, seq, d = x3.shape
    m = _branch_x(x3, w["g"], w["wq"], w["wgx"], w["wmx"], k3, v3, w["wpx"], nb=nb_x, rows=rows_x,
                  heads_in_rows=heads_in_rows)
    m, vn = _branch_a(x3.reshape(bsz * seq, d), w["g"], w["wu"], w["wv"], w["wga"], w["wma"], w["lng"], w["lnb"],
                      ws, bs, w["wpa"], m.reshape(bsz * seq, d), tm=tm_a, want_v=want_v)
    return b_fn(x3, m.reshape(bsz, seq, d)), vn


def kernel(x_prompt, x_sample, mem_prompt, cache_mem_k, cache_mem_v, state_ssm, state_conv, norm_g, w_in, conv_w,
           conv_b, dt_bias, a_log, d_skip, ssd_norm_g, ln_v_g, ln_v_b, w_spatial, b_spatial, mem_norm_g, w_mem_kv,
           w_proj_a, w_proj_b, w_proj_x, w_out, final_norm_g):
    depth = w_in.shape[0]
    bsz, seq, d = x_prompt.shape
    dec_b, dec_rows, _ = x_sample.shape
    n_mem = mem_prompt.shape[1]
    d_x = w_proj_x.shape[1]
    d_inner = w_proj_b.shape[1]
    n_heads = a_log.shape[1]
    assert seq % CHUNK == 0 and CHUNK % dec_rows == 0 and dec_rows == SUBLANES
    assert DT_REPLICAS * n_heads <= LANES and n_heads * dec_rows % LANES == 0

    nb_s = _tile(dec_b, 8)
    nb_xs = _tile(dec_b, 16)
    yp, ys = x_prompt, x_sample
    outs = {k: [] for k in ("mk", "mv", "hp", "cp", "hs", "cs", "vs")}
    for l in range(depth):
        w = _layer_weights(l, d, dec_rows, nb_s, norm_g, w_in, conv_w, conv_b, dt_bias, a_log, d_skip, ssd_norm_g,
                           ln_v_g, ln_v_b, w_spatial, b_spatial, w_proj_a, w_proj_b, w_proj_x, w_out, final_norm_g)
        final_norm = l == depth - 1
        mk, mv = _mem_kv(mem_prompt.reshape(bsz * n_mem, d), mem_norm_g[l].reshape(1, d).astype(F32),
                         _pack_w(w_mem_kv[l]), _tile(bsz * n_mem, 512))
        (yp, hp, cp), _ = _layer(
            yp, mk.reshape(bsz, n_mem * (d_x // X_HEAD_DIM), X_HEAD_DIM),
            mv.reshape(bsz, n_mem * (d_x // X_HEAD_DIM), X_HEAD_DIM), w,
            nb_x=1, rows_x=_tile(seq, 512), heads_in_rows=False, tm_a=_tile(bsz * seq, 512), ws=w["ws_p"], bs=w["bs_p"], want_v=False,
            b_fn=functools.partial(_branch_b_prompt, w=w, tm=_tile(seq, 256), final_norm=final_norm))
        (ys, hs, cs), vs = _layer(
            ys, cache_mem_k[l].reshape(dec_b, n_mem * (d_x // X_HEAD_DIM), X_HEAD_DIM),
            cache_mem_v[l].reshape(dec_b, n_mem * (d_x // X_HEAD_DIM), X_HEAD_DIM), w,
            nb_x=nb_xs, rows_x=dec_rows, heads_in_rows=True, tm_a=_tile(dec_b * dec_rows, 512), ws=w["ws_s"], bs=w["bs_s"], want_v=True,
            b_fn=functools.partial(_branch_b_sample, conv0=state_conv, layer=l,
                                   h0=state_ssm[l].reshape(dec_b, d_inner, SSD_STATE), w=w, nb=nb_s,
                                   tm=_tile(dec_b * dec_rows, 512), final_norm=final_norm))
        outs["mk"].append(mk.reshape(bsz, n_mem, d_x // X_HEAD_DIM, X_HEAD_DIM))
        outs["mv"].append(mv.reshape(bsz, n_mem, d_x // X_HEAD_DIM, X_HEAD_DIM))
        outs["hp"].append(hp.reshape(bsz, n_heads, SSD_HEAD_DIM, SSD_STATE))
        outs["cp"].append(cp)
        outs["hs"].append(hs.reshape(dec_b, n_heads, SSD_HEAD_DIM, SSD_STATE))
        outs["cs"].append(cs)
        outs["vs"].append(vs.reshape(dec_b, dec_rows, d))
    st = lambda k: jnp.stack(outs[k])
    return (yp, ys, st("mk"), st("mv"), st("hp"), st("cp"), st("hs"), st("cs"), st("vs"))
```

```python
import functools
import math
from typing import NamedTuple

import jax
import jax.numpy as jnp
import numpy as np
from jax import lax
from jax.experimental import pallas as pl
from jax.experimental.pallas import tpu as pltpu

F32 = jnp.float32
BF16 = jnp.bfloat16
EPS = 1e-6
SQRT_HALF = math.sqrt(0.5)

LANES = 128
SUBLANES = 8
VMEM_LIMIT_BYTES = 56 * 1024 * 1024

CHUNK = 128
A_GROUPS = 8
SSD_HEAD_DIM = 64
SSD_GROUPS = 8
SSD_STATE = 128
X_HEAD_DIM = 128
N_BRANCH = 3
DT_REPLICAS = 3
SEQ_BATCH = 4

def _dot(a, b):
    return jnp.dot(a, b, preferred_element_type=F32)


def _dot_nt(a, b):
    return lax.dot_general(a, b, (((1,), (1,)), ((), ())), preferred_element_type=F32)


def _dot_tn(a, b):
    return lax.dot_general(a, b, (((0,), (0,)), ((), ())), preferred_element_type=F32)


MXU_COLS = 256


def _pack_w(w):
    return w.astype(BF16)


def _ncols(w):
    return sum(r.shape[1] for r in w) if isinstance(w, tuple) else w.shape[1]


def _dotw(a, w, cols=None):
    if not isinstance(w, tuple):
        return _dot(a, w[...] if cols is None else w[:, cols])
    width = w[0].shape[1]
    lo, hi = (0, width * len(w)) if cols is None else (cols.start, cols.stop)
    parts = []
    while lo < hi:
        k, off = divmod(lo, width)
        n = min(width - off, hi - lo)
        parts.append(_dot(a, w[k][:, off:off + n]))
        lo += n
    return parts[0] if len(parts) == 1 else jnp.concatenate(parts, axis=1)


def _dot_cols(a, w_ref, fn, width=MXU_COLS):
    n = w_ref.shape[1]
    width = min(width, n)
    blocks = [fn(_dotw(a, w_ref, slice(c, c + width))) for c in range(0, n, width)]
    return blocks[0] if len(blocks) == 1 else jnp.concatenate(blocks, axis=1)


def _rmsnorm(x, g):
    return x * lax.rsqrt(jnp.mean(x * x, axis=-1, keepdims=True) + EPS) * g


def _gelu(x):
    return 0.5 * x * (1.0 + lax.erf(x * SQRT_HALF))


def _sigmoid(x):
    return 0.5 * jnp.tanh(0.5 * x) + 0.5


def _silu(x):
    return x * _sigmoid(x)


def _bf16_parts(x):
    hi = x.astype(BF16)
    r1 = x - hi.astype(F32)
    mid = r1.astype(BF16)
    lo = (r1 - mid.astype(F32)).astype(BF16)
    return hi, mid, lo


def _split_by_replica(x, n_heads):
    hi, mid, lo = _bf16_parts(x)
    lane = lax.broadcasted_iota(jnp.int32, x.shape, 1)
    return jnp.where(lane < n_heads, hi, jnp.where(lane < 2 * n_heads, mid, lo))


def _cumsum_rows(tri, x):
    hi, mid, lo = _bf16_parts(x)
    return _dot(tri, hi) + _dot(tri, mid) + _dot(tri, lo)


def _memkv_kernel(x_ref, g_ref, w_ref, k_ref, v_ref, *, n_heads):
    tm = x_ref.shape[0]
    hd = k_ref.shape[-1]
    hn = _rmsnorm(x_ref[...], g_ref[...]).astype(BF16)
    kv = _dotw(hn, w_ref)
    for h in range(n_heads):
        k_ref[pl.ds(h, tm, stride=n_heads), :] = kv[:, h * hd:(h + 1) * hd]
        v_ref[pl.ds(h, tm, stride=n_heads), :] = kv[:, (n_heads + h) * hd:(n_heads + h + 1) * hd]


class _Cols(NamedTuple):
    mat: jax.Array
    start: int
    width: int

    @property
    def shape(self):
        return (self.mat.shape[0], self.width)


def _const_spec(op=None):
    if isinstance(op, _Cols):
        idx = op.start // op.width
        return pl.BlockSpec(op.shape, lambda *_: (0, idx), pipeline_mode=pl.Buffered(1))
    return pl.BlockSpec(memory_space=pltpu.VMEM)


def _flatten(ops):
    return [o for op in ops for o in (op if isinstance(op, list) else [op])]


def _const_specs(ops):
    return [_const_spec(op) for op in _flatten(ops)]


def _arrays(ops):
    return [op.mat if isinstance(op, _Cols) else op for op in _flatten(ops)]


def _regroup(kernel_fn, n_lead, ops):
    sizes = [len(op) if isinstance(op, list) else 0 for op in ops]

    def wrapped(*refs):
        refs = list(refs)
        args, pos = refs[:n_lead], n_lead
        for n in sizes:
            args.append(tuple(refs[pos:pos + n]) if n else refs[pos])
            pos += max(n, 1)
        return kernel_fn(*args, *refs[pos:])

    return wrapped


def _params(n_grid):
    return pltpu.CompilerParams(dimension_semantics=("arbitrary",) * n_grid, vmem_limit_bytes=VMEM_LIMIT_BYTES)


def _mem_kv(mem2d, g, w, tm):
    rows, d = mem2d.shape
    n_heads = w.shape[1] // 2 // X_HEAD_DIM
    out = jax.ShapeDtypeStruct((rows * n_heads, X_HEAD_DIM), F32)
    out_spec = pl.BlockSpec((tm * n_heads, X_HEAD_DIM), lambda i: (i, 0))
    return pl.pallas_call(
        functools.partial(_memkv_kernel, n_heads=n_heads),
        out_shape=(out, out),
        grid=(rows // tm,),
        in_specs=[pl.BlockSpec((tm, d), lambda i: (i, 0)), _const_spec(), _const_spec()],
        out_specs=(out_spec, out_spec),
        compiler_params=_params(1),
        name="mem_kv",
    )(mem2d, g, w)


def _x_kernel(x_ref, g_ref, wq_ref, wgx_ref, wmx_ref, k_ref, v_ref, wpx_ref, m_ref, *, nb, rows, head_dim):
    d = x_ref.shape[-1]
    x = x_ref[...].reshape(nb * rows, d)
    hn = _rmsnorm(x, g_ref[...]).astype(BF16)
    q = _dotw(hn, wq_ref)
    gate = _silu(_dotw(hn, wgx_ref))
    gm = _sigmoid(_dotw(hn, wmx_ref))
    n_heads = q.shape[1] // head_dim
    scale = head_dim ** -0.5
    outs = []
    n_mem = k_ref.shape[1] // n_heads
    for b in range(nb):
        qb = q[b * rows:(b + 1) * rows].astype(BF16)
        heads = range(n_heads)
        kb = [k_ref[b, pl.ds(h, n_mem, stride=n_heads), :].astype(BF16) for h in heads]
        vb = [v_ref[b, pl.ds(h, n_mem, stride=n_heads), :].astype(BF16) for h in heads]
        scores = [_dot_nt(qb[:, h * head_dim:(h + 1) * head_dim], kb[h]) * scale for h in heads]
        exps = [jnp.exp(s - jnp.max(s, axis=-1, keepdims=True)) for s in scores]
        probs = [(e / jnp.sum(e, axis=-1, keepdims=True)).astype(BF16) for e in exps]
        outs.append(jnp.concatenate([_dot(probs[h], vb[h]) for h in heads], axis=-1))
    o = outs[0] if nb == 1 else jnp.concatenate(outs, axis=0)
    hx = (o * gate).astype(BF16)
    m = gm * _dotw(hx, wpx_ref)
    m_ref[...] = m.reshape(nb, rows, d)


def _xs_kernel(x_ref, g_ref, wq_ref, wgx_ref, wmx_ref, k_ref, v_ref, wpx_ref, m_ref, *, nb, rows, head_dim):
    d = x_ref.shape[-1]
    x = x_ref[...].reshape(nb * rows, d)
    hn = _rmsnorm(x, g_ref[...]).astype(BF16)
    q = _dotw(hn, wq_ref)
    gate = _silu(_dotw(hn, wgx_ref))
    gm = _sigmoid(_dotw(hn, wmx_ref))
    n_heads = q.shape[1] // head_dim
    n_kv = k_ref.shape[1]
    scale = head_dim ** -0.5
    row_head = lax.broadcasted_iota(jnp.int32, (n_heads * rows, n_kv), 0) // rows
    col_head = lax.broadcasted_iota(jnp.int32, (n_heads * rows, n_kv), 1) % n_heads
    same_head = row_head == col_head
    seqs = range(nb)
    scores = []
    for b in seqs:
        qb = q[b * rows:(b + 1) * rows]
        q_heads = jnp.concatenate([qb[:, h * head_dim:(h + 1) * head_dim] for h in range(n_heads)], axis=0)
        scores.append(_dot_nt(q_heads.astype(BF16), k_ref[b].astype(BF16)))
    masked = [jnp.where(same_head, s * scale, -jnp.inf) for s in scores]
    exps = [jnp.exp(s - jnp.max(s, axis=-1, keepdims=True)) for s in masked]
    probs = [(e / jnp.sum(e, axis=-1, keepdims=True)).astype(BF16) for e in exps]
    o_heads = [_dot(probs[b], v_ref[b].astype(BF16)) for b in seqs]
    outs = [jnp.concatenate([oh[h * rows:(h + 1) * rows] for h in range(n_heads)], axis=1) for oh in o_heads]
    o = outs[0] if nb == 1 else jnp.concatenate(outs, axis=0)
    hx = (o * gate).astype(BF16)
    m = gm * _dotw(hx, wpx_ref)
    m_ref[...] = m.reshape(nb, rows, d)


def _branch_x(x3, g, wq, wgx, wmx, k3, v3, wpx, *, nb, rows, heads_in_rows):
    bsz, seq, d = x3.shape
    n_mem, dx = k3.shape[1], k3.shape[2]
    kern = functools.partial(_xs_kernel if heads_in_rows else _x_kernel, nb=nb, rows=rows, head_dim=X_HEAD_DIM)
    return pl.pallas_call(
        kern,
        out_shape=jax.ShapeDtypeStruct((bsz, seq, d), F32),
        grid=(bsz // nb, seq // rows),
        in_specs=[pl.BlockSpec((nb, rows, d), lambda i, j: (i, j, 0))] + _const_specs([g, wq, wgx, wmx]) + [
            pl.BlockSpec((nb, n_mem, dx), lambda i, j: (i, 0, 0)),
            pl.BlockSpec((nb, n_mem, dx), lambda i, j: (i, 0, 0)),
            _const_spec(wpx),
        ],
        out_specs=pl.BlockSpec((nb, rows, d), lambda i, j: (i, j, 0)),
        compiler_params=_params(2),
        name="branch_x",
    )(x3, *_arrays([g, wq, wgx, wmx]), k3, v3, wpx)


def _a_kernel(x_ref, g_ref, wu_ref, wv_ref, wga_ref, wma_ref, lng_ref, lnb_ref, ws_ref, bs_ref, wpa_ref, min_ref,
              mout_ref, *maybe_v_ref, chunk, groups, sub):
    tm, d = x_ref.shape
    gd = d // groups
    bias = bs_ref[...]
    for t in range(tm // sub):
        ts = slice(t * sub, (t + 1) * sub)
        hn = _rmsnorm(x_ref[ts, :], g_ref[...]).astype(BF16)
        v = _dot_cols(hn, wv_ref, _gelu)
        u = _dot_cols(hn, wu_ref, _gelu)
        vc = v - jnp.mean(v, axis=-1, keepdims=True)
        vn = vc * lax.rsqrt(jnp.mean(vc * vc, axis=-1, keepdims=True) + EPS) * lng_ref[...] + lnb_ref[...]
        if maybe_v_ref:
            maybe_v_ref[0][ts, :] = vn
        ga = _dot_cols(hn, wga_ref, _silu)
        vb = vn.astype(BF16)
        n_c = sub // chunk
        wide = [_dot(ws_ref[gi], jnp.concatenate([vb[c * chunk:(c + 1) * chunk, gi * gd:(gi + 1) * gd]
                                                  for c in range(n_c)], axis=1)) for gi in range(groups)]
        mixed = [jnp.concatenate([wg[:, c * gd:(c + 1) * gd] for wg in wide], axis=1) + bias for c in range(n_c)]
        s = mixed[0] if n_c == 1 else jnp.concatenate(mixed, axis=0)
        ha = (u * s * ga).astype(BF16)
        gm = _sigmoid(_dotw(hn, wma_ref))
        mout_ref[ts, :] = min_ref[ts, :] + gm * _dotw(ha, wpa_ref)


def _branch_a(x2, g, wu, wv, wga, wma, lng, lnb, ws, bs, wpa, m_in, *, tm, want_v):
    rows, d = x2.shape
    row_spec = pl.BlockSpec((tm, d), lambda i: (i, 0))
    out_shape = [jax.ShapeDtypeStruct((rows, d), F32)]
    out_specs = [row_spec]
    if want_v:
        out_shape.append(jax.ShapeDtypeStruct((rows, d), F32))
        out_specs.append(row_spec)
    kern = functools.partial(_a_kernel, chunk=CHUNK, groups=A_GROUPS, sub=min(tm, 256))
    consts = [g, wu, wv, wga, wma, lng, lnb, ws, bs, wpa]
    res = pl.pallas_call(
        kern,
        out_shape=tuple(out_shape),
        grid=(rows // tm,),
        in_specs=[row_spec] + _const_specs(consts) + [row_spec],
        out_specs=tuple(out_specs),
        compiler_params=_params(1),
        name="branch_a",
    )(x2, *_arrays(consts), m_in)
    return res if want_v else (res[0], None)


def _finish_b(x, y, g_ref, wz_ref, wmb_ref, ng_ref, wpb_ref, m_in, wout_ref, fg_ref, *, n_groups, final_norm):
    hn = _rmsnorm(x, g_ref[...]).astype(BF16)
    gm = _sigmoid(_dotw(hn, wmb_ref))
    yf = y * _silu(_dotw(hn, wz_ref))
    gw = yf.shape[1] // n_groups
    parts = []
    for gi in range(n_groups):
        yg = yf[:, gi * gw:(gi + 1) * gw]
        parts.append(yg * lax.rsqrt(jnp.mean(yg * yg, axis=-1, keepdims=True) + EPS))
    hb = (jnp.concatenate(parts, axis=-1) * ng_ref[...]).astype(BF16)
    m = m_in + gm * _dotw(hb, wpb_ref)
    out = x + _dotw(m.astype(BF16), wout_ref)
    return _rmsnorm(out, fg_ref[...]) if final_norm else out


HANDOFF = ("xs", "bm", "cm", "zs", "dt", "gm", "tail")


def _bp_stage1(x_ref, g_ref, wz_ref, wxbc_ref, wdt_ref, wmb_ref, cw_ref, cb_ref, dtb_ref, perm_ref, xs_scr, out,
               first_tile, *, tm, chunk, d_inner, bc_width, conv_w):
    pad = SUBLANES
    vpc = chunk // pad
    n_chunks = tm // chunk
    width = MXU_COLS
    hn = _rmsnorm(x_ref[0], g_ref[...]).astype(BF16)
    for c0 in range(0, out["gm"].shape[1], width):
        cols = slice(c0, c0 + width)
        out["gm"][:, cols] = _sigmoid(_dotw(hn, wmb_ref, cols))
        yield
    hn = _dot(perm_ref[0], hn).astype(BF16)
    out["dt"][...] = jax.nn.softplus(_dotw(hn, wdt_ref) + dtb_ref[...])
    yield
    for c0 in range(0, d_inner, width):
        cols = slice(c0, c0 + width)
        out["zs"][:, cols] = _silu(_dotw(hn, wz_ref, cols))
        yield
    first_sublane = lax.broadcasted_iota(jnp.int32, (pad, width), 0) == 0
    for c0 in range(0, _ncols(wxbc_ref), width):
        cols = slice(c0, c0 + width)
        raw_all = _dotw(hn, wxbc_ref, cols)
        prev_tail = jnp.where(first_tile, 0.0, xs_scr[:, cols])
        acts = []
        for c in range(n_chunks):
            raw = raw_all[c * chunk:(c + 1) * chunk]
            conv = cb_ref[:, cols] + raw * cw_ref[conv_w - 1:conv_w, cols]
            for back in range(1, conv_w):
                head = []
                for v in range(back):
                    w = vpc - back + v
                    cur = pltpu.roll(raw[w * pad:(w + 1) * pad], 1, axis=0)
                    prv = pltpu.roll(prev_tail[(w - vpc + conv_w - 1) * pad:(w - vpc + conv_w) * pad], 1, axis=0)
                    head.append(jnp.where(first_sublane, prv, cur))
                shifted = jnp.concatenate(head + [raw[:chunk - back * pad]], axis=0)
                conv = conv + shifted * cw_ref[conv_w - 1 - back:conv_w - back, cols]
            acts.append(_silu(conv))
            prev_tail = raw[chunk - (conv_w - 1) * pad:]
        xs_scr[:, cols] = prev_tail
        out["tail"][:, cols] = prev_tail
        act = acts[0] if n_chunks == 1 else jnp.concatenate(acts, axis=0)
        if c0 < d_inner:
            out["xs"][:, cols] = act
        elif c0 < d_inner + bc_width:
            out["bm"][:, c0 - d_inner:c0 - d_inner + width] = act.astype(BF16)
        else:
            out["cm"][:, c0 - d_inner - bc_width:c0 - d_inner - bc_width + width] = act.astype(BF16)
        yield


def _bp_stage2(x_ref, min_ref, alog_ref, dsk_ref, ng_ref, wpb_ref, wout_ref, fg_ref, tri_ref, e3_ref, perm_ref,
               y_ref, st_scr, inp, first_tile, *, tm, chunk, n_groups, n_heads, head_dim, n_state, final_norm):
    pad = SUBLANES
    vpc = chunk // pad
    hpg = n_heads // n_groups
    gw = hpg * head_dim
    width = MXU_COLS

    def token_of(q):
        return (q % pad) * vpc + q // pad

    li = token_of(lax.broadcasted_iota(jnp.int32, (chunk, chunk), 0))
    si = token_of(lax.broadcasted_iota(jnp.int32, (chunk, chunk), 1))
    causal = li >= si
    head_of_lane = lax.broadcasted_iota(jnp.int32, (chunk, gw), 1) // head_dim
    tri = tri_ref[...]
    e3 = e3_ref[...]
    a_row = -jnp.exp(alog_ref[...])

    hb_rows = []
    for c in range(tm // chunk):
        rs = slice(c * chunk, (c + 1) * chunk)
        dt = inp["dt"][rs, :]
        acs = _cumsum_rows(tri, dt * a_row)
        acs_t = acs.T
        acs_e = _dot(_split_by_replica(acs, n_heads).astype(BF16), e3)
        a_last = acs_e[chunk - 1:chunk, :]
        xs_c = inp["xs"][rs, :]
        xdt_c = xs_c * _dot(_split_by_replica(dt, n_heads).astype(BF16), e3)
        xdt_b = xdt_c.astype(BF16)
        xd_state = (xdt_c * jnp.exp(a_last - acs_e)).astype(BF16)
        decay_in = jnp.exp(acs_e)
        st_prev = st_scr[...]
        if c == 0:
            st_prev = jnp.where(first_tile, 0.0, st_prev)
        st_prev_b = st_prev.astype(BF16)
        yield
        groups = range(n_groups)
        n_sl = [slice(gi * n_state, (gi + 1) * n_state) for gi in groups]
        g_sl = [slice(gi * gw, (gi + 1) * gw) for gi in groups]
        scores = [_dot_nt(inp["cm"][rs, n_sl[gi]], inp["bm"][rs, n_sl[gi]]) for gi in groups]
        yield
        y_offs = [_dot(inp["cm"][rs, n_sl[gi]], st_prev_b[:, g_sl[gi]]) for gi in groups]
        st_parts = [_dot_tn(inp["bm"][rs, n_sl[gi]], xd_state[:, g_sl[gi]]) for gi in groups]
        st_scr[...] = st_prev * jnp.exp(a_last) + jnp.concatenate(st_parts, axis=1)
        yield
        y_diags = []
        for gi in groups:
            m_heads, x_blocks = [], []
            xg = xdt_b[:, g_sl[gi]]
            for r in range(hpg):
                h = gi * hpg + r
                diff = acs[:, h:h + 1] - acs_t[h:h + 1, :]
                decay = jnp.exp(jnp.where(causal, diff, -jnp.inf))
                m_heads.append((scores[gi] * decay).astype(BF16))
                x_blocks.append(jnp.where(head_of_lane == r, xg, jnp.zeros_like(xg)))
            y_diags.append(_dot(jnp.concatenate(m_heads, axis=1), jnp.concatenate(x_blocks, axis=0)))
            yield
        hb_parts = []
        for gi in groups:
            gs = g_sl[gi]
            y = y_diags[gi] + y_offs[gi] * decay_in[:, gs] + xs_c[:, gs] * dsk_ref[:, gs]
            yf = y * inp["zs"][rs, gs]
            yn = yf * lax.rsqrt(jnp.mean(yf * yf, axis=-1, keepdims=True) + EPS)
            hb_parts.append((yn * ng_ref[:, gs]).astype(BF16))
            if gi % 2:
                yield
        hb_rows.append(jnp.concatenate(hb_parts, axis=1))
    hb = hb_rows[0] if len(hb_rows) == 1 else jnp.concatenate(hb_rows, axis=0)
    hb = _dot(perm_ref[1], hb).astype(BF16)
    yield
    m_parts = []
    for c0 in range(0, wpb_ref.shape[1], width):
        cols = slice(c0, c0 + width)
        m_parts.append((min_ref[0, :, cols] + inp["gm"][:, cols] * _dotw(hb, wpb_ref, cols)).astype(BF16))
        yield
    m = jnp.concatenate(m_parts, axis=1)
    out = x_ref[0] + _dotw(m, wout_ref)
    y_ref[0] = _rmsnorm(out, fg_ref[...]) if final_norm else out


BP_WEIGHTS = ("wz", "wxbc", "wdt", "wmb", "wpb", "wout")


def _bp_kernel(x1_ref, x2_ref, min_ref, g_ref, cw_ref, cb_ref, dtb_ref, alog_ref, dsk_ref, ng_ref, fg_ref, tri_ref,
               e3_ref, perm_ref, *rest,
               n_hbm, copies, nj, n_tiles, tm, chunk, n_groups, n_heads, head_dim, n_state, conv_w, final_norm):
    hbm_refs, (y_ref, ssm_ref, conv_ref, xs_scr, st_scr), rest = rest[:n_hbm], rest[n_hbm:n_hbm + 5], rest[n_hbm + 5:]
    w_scr, copy_sem, slot_refs = rest[:len(BP_WEIGHTS)], rest[len(BP_WEIGHTS)], rest[len(BP_WEIGHTS) + 1:]
    wz_ref, wxbc_ref, wdt_ref, wmb_ref, wpb_ref, wout_ref = w_scr
    t = pl.program_id(0)
    d_inner = n_heads * head_dim
    pad = SUBLANES
    tile1 = jnp.minimum(t, n_tiles - 1)
    tile2 = jnp.maximum(t - 1, 0)
    both_slots = dict(zip(HANDOFF, slot_refs))
    slots = [{k: ref.at[i] for k, ref in both_slots.items()} for i in (t % 2, 1 - t % 2, 1)]

    @pl.when(t == 0)
    def _init():
        dmas = [pltpu.make_async_copy(hbm_refs[src].at[:, pl.ds(start, width)],
                                      w_scr[dst].at[:, pl.ds(dst_start, width)], copy_sem.at[i])
                for i, (src, start, width, dst, dst_start) in enumerate(copies)]
        for dma in dmas:
            dma.start()
        for ref in slots[2].values():
            ref[...] = jnp.zeros_like(ref)
        st_scr[...] = jnp.zeros_like(st_scr)
        xs_scr[...] = jnp.zeros_like(xs_scr)
        for dma in dmas:
            dma.wait()

    def step(slot_w, slot_r):
        s1 = _bp_stage1(x1_ref, g_ref, wz_ref, wxbc_ref, wdt_ref, wmb_ref, cw_ref, cb_ref, dtb_ref, perm_ref, xs_scr,
                        slot_w, tile1 % nj == 0, tm=tm, chunk=chunk, d_inner=d_inner,
                        bc_width=n_groups * n_state, conv_w=conv_w)
        s2 = _bp_stage2(x2_ref, min_ref, alog_ref, dsk_ref, ng_ref, wpb_ref, wout_ref, fg_ref, tri_ref, e3_ref,
                        perm_ref, y_ref, st_scr, slot_r, tile2 % nj == 0, tm=tm, chunk=chunk, n_groups=n_groups,
                        n_heads=n_heads, head_dim=head_dim, n_state=n_state, final_norm=final_norm)
        live = [s2, s1]
        while live:
            for gen in list(live):
                if next(gen, StopIteration) is StopIteration:
                    live.remove(gen)

        @pl.when((t > 0) & (tile2 % nj == nj - 1))
        def _seq_end():
            tail = slot_r["tail"][...]
            conv_ref[0] = jnp.concatenate([tail[(v + 1) * pad - 1:(v + 1) * pad] for v in range(conv_w - 1)], axis=0)
            ssm_ref[0] = st_scr[...].T

    step(slots[0], slots[1])


def _branch_b_prompt(x3, m3, w, *, tm, final_norm):
    bsz, seq, d = x3.shape
    conv_dim = w["cw"].shape[1]
    d_inner = w["ng"].shape[1]
    n_heads = d_inner // SSD_HEAD_DIM
    conv_w = w["cw"].shape[0]
    nj = seq // tm
    n_tiles = bsz * nj
    bc_width = SSD_GROUPS * SSD_STATE
    hbm_ops, copies, w_shapes = [], [], []
    for wi, name in enumerate(BP_WEIGHTS):
        parts = w[name] if isinstance(w[name], list) else [w[name]]
        dst = 0
        for part in parts:
            mat, start = (part.mat, part.start) if isinstance(part, _Cols) else (part, 0)
            src = next((i for i, op in enumerate(hbm_ops) if op is mat), len(hbm_ops))
            if src == len(hbm_ops):
                hbm_ops.append(mat)
            copies.append((src, start, part.shape[1], wi, dst))
            dst += part.shape[1]
        w_shapes.append((parts[0].shape[0], dst))
    kern = functools.partial(_bp_kernel, n_hbm=len(hbm_ops), copies=tuple(copies), nj=nj, n_tiles=n_tiles, tm=tm,
                             chunk=CHUNK, n_groups=SSD_GROUPS, n_heads=n_heads, head_dim=SSD_HEAD_DIM,
                             n_state=SSD_STATE, conv_w=conv_w, final_norm=final_norm)

    def tile1(t):
        return jnp.minimum(t, n_tiles - 1)

    def tile2(t):
        return jnp.maximum(t - 1, 0)

    spec1 = pl.BlockSpec((1, tm, d), lambda t: (tile1(t) // nj, tile1(t) % nj, 0))
    spec2 = pl.BlockSpec((1, tm, d), lambda t: (tile2(t) // nj, tile2(t) % nj, 0))
    consts = [w[k] for k in ("g", "cw", "cb", "dtb", "alog", "dsk", "ng", "fg", "tri", "e3")]
    tok = (np.arange(tm) // CHUNK) * CHUNK + np.tile(_chunk_tokens(), tm // CHUNK)
    p = (tok[:, None] == np.arange(tm)[None, :]).astype(np.float32)
    perm = jnp.asarray(np.stack([p, p.T]), dtype=BF16)
    handoff = {"xs": ((tm, d_inner), F32), "bm": ((tm, bc_width), BF16), "cm": ((tm, bc_width), BF16),
               "zs": ((tm, d_inner), F32), "dt": ((tm, LANES), F32), "gm": ((tm, d), F32),
               "tail": (((conv_w - 1) * SUBLANES, conv_dim), F32)}
    return pl.pallas_call(
        kern,
        out_shape=(jax.ShapeDtypeStruct((bsz, seq, d), F32),
                   jax.ShapeDtypeStruct((bsz, d_inner, SSD_STATE), F32),
                   jax.ShapeDtypeStruct((bsz, conv_w - 1, conv_dim), F32)),
        grid=(n_tiles + 1,),
        in_specs=([spec1, spec2, spec2] + _const_specs(consts + [perm])
                  + [pl.BlockSpec(memory_space=pl.ANY)] * len(hbm_ops)),
        out_specs=(spec2,
                   pl.BlockSpec((1, d_inner, SSD_STATE), lambda t: (tile2(t) // nj, 0, 0)),
                   pl.BlockSpec((1, conv_w - 1, conv_dim), lambda t: (tile2(t) // nj, 0, 0))),
        scratch_shapes=([pltpu.VMEM(((conv_w - 1) * SUBLANES, conv_dim), F32), pltpu.VMEM((SSD_STATE, d_inner), F32)]
                        + [pltpu.VMEM(shape, BF16) for shape in w_shapes]
                        + [pltpu.SemaphoreType.DMA((len(copies),))]
                        + [pltpu.VMEM((2,) + handoff[k][0], handoff[k][1]) for k in HANDOFF]),
        compiler_params=_params(1),
        name="branch_b_prompt",
    )(x3, x3, m3, *_arrays(consts), perm, *hbm_ops)


def _bs_kernel(x_ref, g_ref, wxbc_ref, wdt_ref, cw_ref, cb_ref, dtb_ref, alog_ref, dsk_ref,
               tri_ref, e3_ref, ecol_ref, conv0_ref, h0_ref,
               y_ref, hnew_ref, convnew_ref, xs_scr, tr_scr,
               *, nb, rows, n_groups, n_heads, head_dim, n_state, conv_w):
    hpg = n_heads // n_groups
    gw = hpg * head_dim
    d_inner = n_heads * head_dim
    d = x_ref.shape[-1]
    r_all = nb * rows
    hs_w = n_heads * rows

    @pl.when(pl.program_id(0) == 0)
    def _init():
        tr_scr[...] = jnp.zeros_like(tr_scr)
        xs_scr[...] = jnp.zeros_like(xs_scr)

    x = x_ref[...].reshape(r_all, d)
    hn = _rmsnorm(x, g_ref[...]).astype(BF16)
    xbc_raw = _dotw(hn, wxbc_ref)
    dt = jax.nn.softplus(_dotw(hn, wdt_ref) + dtb_ref[...])

    raw = xbc_raw.reshape(nb, rows, xbc_raw.shape[1])
    xs_scr[:, rows - (conv_w - 1):, :] = conv0_ref[...]
    state = xs_scr[...]
    row_in_seq = lax.broadcasted_iota(jnp.int32, raw.shape, 1)
    conv = cb_ref[...] + raw * cw_ref[conv_w - 1:conv_w, :]
    for back in range(1, conv_w):
        shifted = jnp.where(row_in_seq < back, pltpu.roll(state, back, axis=1), pltpu.roll(raw, back, axis=1))
        conv = conv + shifted * cw_ref[conv_w - 1 - back:conv_w - back, :]
    convnew_ref[...] = raw[:, rows - (conv_w - 1):, :]
    xbc = _silu(conv.reshape(r_all, xbc_raw.shape[1]))
    xs = xbc[:, :d_inner]
    bm = xbc[:, d_inner:d_inner + n_groups * n_state]
    cm = xbc[:, d_inner + n_groups * n_state:]

    a_row = -jnp.exp(alog_ref[...])
    acs = _cumsum_rows(tri_ref[...], dt * a_row)
    e3 = e3_ref[...]
    acs_split = _split_by_replica(acs, n_heads).astype(BF16)
    acs_e = _dot(acs_split, e3)
    xdt = xs * _dot(_split_by_replica(dt, n_heads).astype(BF16), e3)
    decay_in = jnp.exp(acs_e)

    acs_col = _dot(acs_split, ecol_ref[...])
    l_idx = lax.broadcasted_iota(jnp.int32, (r_all, hs_w), 0) % rows
    s_idx = lax.broadcasted_iota(jnp.int32, (r_all, hs_w), 1) % rows
    on_diag = jnp.where(l_idx == s_idx, acs_col, 0.0).reshape(nb, rows, hs_w)
    acs_row = jnp.broadcast_to(jnp.sum(on_diag, axis=1, keepdims=True), (nb, rows, hs_w)).reshape(r_all, hs_w)
    decay = jnp.exp(jnp.where(l_idx >= s_idx, acs_col - acs_row, -jnp.inf))

    slot = LANES // nb
    ones_rows = lax.broadcasted_iota(jnp.int32, (slot, n_state), 0)
    ones_blk = jnp.where((ones_rows >= rows) & (ones_rows < rows + 3), 1.0, 0.0)

    def seq_rows(b):
        return slice(b * rows, (b + 1) * rows)

    lane_in_tile = lax.broadcasted_iota(jnp.int32, (rows, LANES), 1)

    def expand_rows(v, start_of, width):
        c = v.shape[1]
        blocks = []
        for h in range(n_heads):
            start = start_of(h)
            lo = start // LANES * LANES
            hi = max(lo + LANES, start + width)
            piece = v[:, lo:hi]
            if width < LANES:
                piece = jnp.where(lane_in_tile // width == (start - lo) // width, piece, 0.0)
            parts = [jnp.zeros((rows, lo), F32)] if lo else []
            parts.append(piece)
            if hi < c:
                parts.append(jnp.zeros((rows, c - hi), F32))
            blocks.append(jnp.concatenate(parts, axis=1) if len(parts) > 1 else piece)
        return jnp.concatenate(blocks, axis=0).astype(BF16)

    y_diags = []
    for b0 in range(0, nb, SEQ_BATCH):
        seqs = range(b0, min(b0 + SEQ_BATCH, nb))
        b_exps = [expand_rows(bm[seq_rows(b)], lambda h: h // hpg * n_state, n_state) for b in seqs]
        scores = [_dot_nt(cm[seq_rows(b)].astype(BF16), be) for b, be in zip(seqs, b_exps)]
        x_blks = [expand_rows(xdt[seq_rows(b)], lambda h: h * head_dim, head_dim) for b in seqs]
        m_alls = [(s * decay[seq_rows(b)]).astype(BF16) for b, s in zip(seqs, scores)]
        y_diags += [_dot(m, xb) for m, xb in zip(m_alls, x_blks)]
    for b in range(nb):
        a_last = acs_e[(b + 1) * rows - 1:(b + 1) * rows, :]
        d_hi, d_mid, d_lo = _bf16_parts(jnp.exp(a_last))
        tr_scr[b * slot:b * slot + rows, :] = xdt[seq_rows(b)] * jnp.exp(a_last - acs_e[seq_rows(b)])
        tr_scr[b * slot + rows:b * slot + rows + 1, :] = d_hi.astype(F32)
        tr_scr[b * slot + rows + 1:b * slot + rows + 2, :] = d_mid.astype(F32)
        tr_scr[b * slot + rows + 2:b * slot + rows + 3, :] = d_lo.astype(F32)
    tr_t = tr_scr[...].T.astype(BF16)

    ys = []
    for b in range(nb):
        rs = slice(b * rows, (b + 1) * rows)
        bm_b, cm_bb = bm[rs], cm[rs].astype(BF16)
        y_parts = []
        for gi in range(n_groups):
            ns = slice(gi * n_state, (gi + 1) * n_state)
            gs = slice(gi * gw, (gi + 1) * gw)
            h0_g = h0_ref[b, gs, :]
            y_parts.append(_dot_nt(cm_bb[:, ns], h0_g.astype(BF16)))
            w_seq = jnp.concatenate(
                [jnp.concatenate([bm_b[:, ns], jnp.zeros((slot - rows, n_state), F32)], axis=0), ones_blk], axis=1)
            pieces = ([jnp.zeros((b * slot, 2 * n_state), F32)] if b else []) + [w_seq]
            if b < nb - 1:
                pieces.append(jnp.zeros((LANES - (b + 1) * slot, 2 * n_state), F32))
            upd = _dot(tr_t[gs, :], jnp.concatenate(pieces, axis=0).astype(BF16))
            hnew_ref[b, gs, :] = h0_g * upd[:, n_state:] + upd[:, :n_state]
        ys.append(y_diags[b] + jnp.concatenate(y_parts, axis=1) * decay_in[rs])
    y = jnp.concatenate(ys, axis=0) + xs * dsk_ref[...]
    y_ref[...] = y.reshape(nb, rows, d_inner)


def _bfin_kernel(x_ref, y_ref, min_ref, g_ref, wz_ref, wmb_ref, ng_ref, wpb_ref, wout_ref, fg_ref, out_ref,
                 *, n_groups, final_norm):
    out_ref[...] = _finish_b(x_ref[...], y_ref[...], g_ref, wz_ref, wmb_ref, ng_ref, wpb_ref, min_ref[...],
                             wout_ref, fg_ref, n_groups=n_groups, final_norm=final_norm)


def _branch_b_sample(x3, m3, conv0, h0, w, *, layer, nb, tm, final_norm):
    bsz, rows, d = x3.shape
    conv_dim = w["cw"].shape[1]
    d_inner = w["ng"].shape[1]
    n_heads = d_inner // SSD_HEAD_DIM
    conv_w = w["cw"].shape[0]
    kern = functools.partial(_bs_kernel, nb=nb, rows=rows, n_groups=SSD_GROUPS, n_heads=n_heads,
                             head_dim=SSD_HEAD_DIM, n_state=SSD_STATE, conv_w=conv_w)

    def seq_spec(width):
        return pl.BlockSpec((nb, rows, width), lambda i: (i, 0, 0))

    conv_spec = pl.BlockSpec((nb, conv_w - 1, conv_dim), lambda i: (i, 0, 0))
    h_spec = pl.BlockSpec((nb, d_inner, SSD_STATE), lambda i: (i, 0, 0))
    consts = [w[k] for k in ("g", "wxbc", "wdt", "cw", "cb", "dtb", "alog", "dsk", "tri_s", "e3", "ecol")]
    y, h_new, conv_new = pl.pallas_call(
        _regroup(kern, 1, consts),
        out_shape=(jax.ShapeDtypeStruct((bsz, rows, d_inner), F32),
                   jax.ShapeDtypeStruct((bsz, d_inner, SSD_STATE), F32),
                   jax.ShapeDtypeStruct((bsz, conv_w - 1, conv_dim), F32)),
        grid=(bsz // nb,),
        in_specs=[seq_spec(d)] + _const_specs(consts) + [
            pl.BlockSpec((None, nb, conv_w - 1, conv_dim), lambda i: (layer, i, 0, 0)), h_spec],
        out_specs=(seq_spec(d_inner), h_spec, conv_spec),
        scratch_shapes=[pltpu.VMEM((nb, SUBLANES, conv_dim), F32), pltpu.VMEM((LANES, d_inner), F32)],
        compiler_params=_params(1),
        name="branch_b_sample",
    )(x3, *_arrays(consts), conv0, h0)

    n_rows = bsz * rows
    fin_consts = [w[k] for k in ("g", "wz", "wmb", "ng", "wpb", "wout", "fg")]

    def rows_spec(width):
        return pl.BlockSpec((tm, width), lambda i: (i, 0))

    out = pl.pallas_call(
        _regroup(functools.partial(_bfin_kernel, n_groups=SSD_GROUPS, final_norm=final_norm), 3, fin_consts),
        out_shape=jax.ShapeDtypeStruct((n_rows, d), F32),
        grid=(n_rows // tm,),
        in_specs=[rows_spec(d), rows_spec(d_inner), rows_spec(d)] + _const_specs(fin_consts),
        out_specs=rows_spec(d),
        compiler_params=_params(1),
        name="branch_b_sample_out",
    )(x3.reshape(n_rows, d), y.reshape(n_rows, d_inner), m3.reshape(n_rows, d), *_arrays(fin_consts))
    return out.reshape(bsz, rows, d), h_new, conv_new


def _tile(n, pref):
    return pref if n % pref == 0 else n


def _chunk_tokens():
    q = np.arange(CHUNK)
    return (q % SUBLANES) * (CHUNK // SUBLANES) + q // SUBLANES


def _head_expand(n_heads, width, n_rep):
    j = np.arange(LANES)[:, None]
    c = np.arange(n_heads * width)[None, :]
    return jnp.asarray(((j < n_rep * n_heads) & (j % n_heads == c // width)).astype(np.float32), dtype=BF16)


def _layer_weights(l, d, dec_rows, nb_s, norm_g, w_in, conv_w, conv_b, dt_bias, a_log, d_skip, ssd_norm_g, ln_v_g,
                   ln_v_b, w_spatial, b_spatial, w_proj_a, w_proj_b, w_proj_x, w_out, final_norm_g):
    d_a = d
    d_inner = w_proj_b.shape[1]
    conv_dim = conv_w.shape[2]
    n_heads = a_log.shape[1]
    d_x = w_proj_x.shape[1]
    sizes = (d_a, d_a, d_a, d_inner, conv_dim, n_heads, d_x, d_x, N_BRANCH * d)
    offs = np.concatenate([[0], np.cumsum(sizes)])
    wl = w_in[l]
    sec = lambda i: wl[:, offs[i]:offs[i + 1]]
    row = lambda v: v.reshape(1, -1).astype(F32)
    rep3 = lambda v, fill: jnp.concatenate(
        [v] * DT_REPLICAS + [jnp.full(v.shape[:-1] + (LANES - DT_REPLICAS * v.shape[-1],), fill, v.dtype)], axis=-1)

    tril = jnp.tril(jnp.ones((CHUNK, CHUNK), F32))
    ws_p = jnp.where(tril[None] > 0, w_spatial[l], 0.0)
    bs_p = jnp.repeat(b_spatial[l].T, d_a // A_GROUPS, axis=1)
    n_seq = CHUNK // dec_rows
    sel = jnp.asarray(np.tile(np.eye(dec_rows, dtype=np.float32), (n_seq, 1)))
    blocks = jnp.asarray(np.kron(np.eye(n_seq, dtype=np.float32), np.ones((dec_rows, dec_rows), np.float32)))
    ws_s = jnp.einsum("ia,gab,jb->gij", sel, ws_p[:, :dec_rows, :dec_rows], sel,
                      precision=lax.Precision.HIGHEST) * blocks
    bs_s = jnp.tile(bs_p[:dec_rows], (n_seq, 1))

    w = {
        "g": row(norm_g[l]),
        "lng": row(ln_v_g[l]), "lnb": row(ln_v_b[l]),
        "ws_p": ws_p.astype(BF16), "bs_p": bs_p, "ws_s": ws_s.astype(BF16), "bs_s": bs_s,
        "wpa": _pack_w(w_proj_a[l]),
        "cw": conv_w[l].astype(F32), "cb": row(conv_b[l]),
        "dtb": rep3(row(dt_bias[l]), 0.0), "alog": rep3(row(a_log[l]), 0.0),
        "dsk": jnp.repeat(row(d_skip[l]), SSD_HEAD_DIM, axis=1),
        "ng": row(ssd_norm_g[l]),
        "wpb": _pack_w(w_proj_b[l]),
        "wout": _pack_w(w_out[l]), "fg": row(final_norm_g),
        "tri": jnp.asarray(_chunk_tokens()[:, None] >= _chunk_tokens()[None, :], dtype=BF16),
        "tri_s": jnp.asarray(np.kron(np.eye(nb_s), np.tril(np.ones((dec_rows, dec_rows)))), dtype=BF16),
        "e3": _head_expand(n_heads, SSD_HEAD_DIM, DT_REPLICAS),
        "ecol": _head_expand(n_heads, dec_rows, DT_REPLICAS),
        "wpx": _pack_w(w_proj_x[l]),
    }
    w_bf = _pack_w(wl)
    tail = w_bf[:, offs[6]:]
    assert all(o % d == 0 for o in offs[:5]) and d_x % LANES == 0 and d % d_x == 0
    for name, i in (("wu", 0), ("wv", 1), ("wga", 2), ("wz", 3), ("wxbc", 4)):
        windows = [_Cols(w_bf, c, d) for c in range(offs[i], offs[i + 1], d)]
        w[name] = windows if len(windows) > 1 else windows[0]
    w["wdt"] = _pack_w(rep3(sec(5), 0.0))
    w["wq"], w["wgx"] = _Cols(tail, 0, d_x), _Cols(tail, d_x, d_x)
    for k, name in enumerate(("wma", "wmb", "wmx")):
        w[name] = _Cols(tail, 2 * d_x + k * d, d)
    return w


def _layer(x3, k3, v3, w, *, nb_x, rows_x, heads_in_rows, tm_a, ws, bs, want_v, b_fn):
    bsz, seq, d = x3.shape
    m = _branch_x(x3, w["g"], w["wq"], w["wgx"], w["wmx"], k3, v3, w["wpx"], nb=nb_x, rows=rows_x,
                  heads_in_rows=heads_in_rows)
    m, vn = _branch_a(x3.reshape(bsz * seq, d), w["g"], w["wu"], w["wv"], w["wga"], w["wma"], w["lng"], w["lnb"],
                      ws, bs, w["wpa"], m.reshape(bsz * seq, d), tm=tm_a, want_v=want_v)
    return b_fn(x3, m.reshape(bsz, seq, d)), vn


def kernel(x_prompt, x_sample, mem_prompt, cache_mem_k, cache_mem_v, state_ssm, state_conv, norm_g, w_in, conv_w,
           conv_b, dt_bias, a_log, d_skip, ssd_norm_g, ln_v_g, ln_v_b, w_spatial, b_spatial, mem_norm_g, w_mem_kv,
           w_proj_a, w_proj_b, w_proj_x, w_out, final_norm_g):
    depth = w_in.shape[0]
    bsz, seq, d = x_prompt.shape
    dec_b, dec_rows, _ = x_sample.shape
    n_mem = mem_prompt.shape[1]
    d_x = w_proj_x.shape[1]
    d_inner = w_proj_b.shape[1]
    n_heads = a_log.shape[1]
    assert seq % CHUNK == 0 and CHUNK % dec_rows == 0 and dec_rows == SUBLANES
    assert DT_REPLICAS * n_heads <= LANES and n_heads * dec_rows % LANES == 0

    nb_s = _tile(dec_b, 8)
    nb_xs = _tile(dec_b, 16)
    yp, ys = x_prompt, x_sample
    outs = {k: [] for k in ("mk", "mv", "hp", "cp", "hs", "cs", "vs")}
    for l in range(depth):
        w = _layer_weights(l, d, dec_rows, nb_s, norm_g, w_in, conv_w, conv_b, dt_bias, a_log, d_skip, ssd_norm_g,
                           ln_v_g, ln_v_b, w_spatial, b_spatial, w_proj_a, w_proj_b, w_proj_x, w_out, final_norm_g)
        final_norm = l == depth - 1
        mk, mv = _mem_kv(mem_prompt.reshape(bsz * n_mem, d), mem_norm_g[l].reshape(1, d).astype(F32),
                         _pack_w(w_mem_kv[l]), _tile(bsz * n_mem, 512))
        (yp, hp, cp), _ = _layer(
            yp, mk.reshape(bsz, n_mem * (d_x // X_HEAD_DIM), X_HEAD_DIM),
            mv.reshape(bsz, n_mem * (d_x // X_HEAD_DIM), X_HEAD_DIM), w,
            nb_x=1, rows_x=_tile(seq, 512), heads_in_rows=False, tm_a=_tile(bsz * seq, 512), ws=w["ws_p"], bs=w["bs_p"], want_v=False,
            b_fn=functools.partial(_branch_b_prompt, w=w, tm=_tile(seq, 256), final_norm=final_norm))
        (ys, hs, cs), vs = _layer(
            ys, cache_mem_k[l].reshape(dec_b, n_mem * (d_x // X_HEAD_DIM), X_HEAD_DIM),
            cache_mem_v[l].reshape(dec_b, n_mem * (d_x // X_HEAD_DIM), X_HEAD_DIM), w,
            nb_x=nb_xs, rows_x=dec_rows, heads_in_rows=True, tm_a=_tile(dec_b * dec_rows, 512), ws=w["ws_s"], bs=w["bs_s"], want_v=True,
            b_fn=functools.partial(_branch_b_sample, conv0=state_conv, layer=l,
                                   h0=state_ssm[l].reshape(dec_b, d_inner, SSD_STATE), w=w, nb=nb_s,
                                   tm=_tile(dec_b * dec_rows, 512), final_norm=final_norm))
        outs["mk"].append(mk.reshape(bsz, n_mem, d_x // X_HEAD_DIM, X_HEAD_DIM))
        outs["mv"].append(mv.reshape(bsz, n_mem, d_x // X_HEAD_DIM, X_HEAD_DIM))
        outs["hp"].append(hp.reshape(bsz, n_heads, SSD_HEAD_DIM, SSD_STATE))
        outs["cp"].append(cp)
        outs["hs"].append(hs.reshape(dec_b, n_heads, SSD_HEAD_DIM, SSD_STATE))
        outs["cs"].append(cs)
        outs["vs"].append(vs.reshape(dec_b, dec_rows, d))
    st = lambda k: jnp.stack(outs[k])
    return (yp, ys, st("mk"), st("mv"), st("hp"), st("cp"), st("hs"), st("cs"), st("vs"))
```

```python
import functools
import math
from typing import NamedTuple

import jax
import jax.numpy as jnp
import numpy as np
from jax import lax
from jax.experimental import pallas as pl
from jax.experimental.pallas import tpu as pltpu

F32 = jnp.float32
BF16 = jnp.bfloat16
EPS = 1e-6
SQRT_HALF = math.sqrt(0.5)

LANES = 128
SUBLANES = 8
VMEM_LIMIT_BYTES = 56 * 1024 * 1024

CHUNK = 128
A_GROUPS = 8
SSD_HEAD_DIM = 64
SSD_GROUPS = 8
SSD_STATE = 128
X_HEAD_DIM = 128
N_BRANCH = 3
DT_REPLICAS = 3
SEQ_BATCH = 4

def _dot(a, b):
    return jnp.dot(a, b, preferred_element_type=F32)


def _dot_nt(a, b):
    return lax.dot_general(a, b, (((1,), (1,)), ((), ())), preferred_element_type=F32)


def _dot_tn(a, b):
    return lax.dot_general(a, b, (((0,), (0,)), ((), ())), preferred_element_type=F32)


MXU_COLS = 256


def _pack_w(w):
    return w.astype(BF16)


def _ncols(w):
    return sum(r.shape[1] for r in w) if isinstance(w, tuple) else w.shape[1]


def _dotw(a, w, cols=None):
    if not isinstance(w, tuple):
        return _dot(a, w[...] if cols is None else w[:, cols])
    width = w[0].shape[1]
    lo, hi = (0, width * len(w)) if cols is None else (cols.start, cols.stop)
    parts = []
    while lo < hi:
        k, off = divmod(lo, width)
        n = min(width - off, hi - lo)
        parts.append(_dot(a, w[k][:, off:off + n]))
        lo += n
    return parts[0] if len(parts) == 1 else jnp.concatenate(parts, axis=1)


def _dot_cols(a, w_ref, fn, width=MXU_COLS):
    n = w_ref.shape[1]
    width = min(width, n)
    blocks = [fn(_dotw(a, w_ref, slice(c, c + width))) for c in range(0, n, width)]
    return blocks[0] if len(blocks) == 1 else jnp.concatenate(blocks, axis=1)


def _rmsnorm(x, g):
    return x * lax.rsqrt(jnp.mean(x * x, axis=-1, keepdims=True) + EPS) * g


def _gelu(x):
    return 0.5 * x * (1.0 + lax.erf(x * SQRT_HALF))


def _sigmoid(x):
    return 0.5 * jnp.tanh(0.5 * x) + 0.5


def _silu(x):
    return x * _sigmoid(x)


def _bf16_parts(x):
    hi = x.astype(BF16)
    r1 = x - hi.astype(F32)
    mid = r1.astype(BF16)
    lo = (r1 - mid.astype(F32)).astype(BF16)
    return hi, mid, lo


def _split_by_replica(x, n_heads):
    hi, mid, lo = _bf16_parts(x)
    lane = lax.broadcasted_iota(jnp.int32, x.shape, 1)
    return jnp.where(lane < n_heads, hi, jnp.where(lane < 2 * n_heads, mid, lo))


def _cumsum_rows(tri, x):
    hi, mid, lo = _bf16_parts(x)
    return _dot(tri, hi) + _dot(tri, mid) + _dot(tri, lo)


def _memkv_kernel(x_ref, g_ref, w_ref, k_ref, v_ref, *, n_heads):
    tm = x_ref.shape[0]
    hd = k_ref.shape[-1]
    hn = _rmsnorm(x_ref[...], g_ref[...]).astype(BF16)
    kv = _dotw(hn, w_ref)
    for h in range(n_heads):
        k_ref[pl.ds(h, tm, stride=n_heads), :] = kv[:, h * hd:(h + 1) * hd]
        v_ref[pl.ds(h, tm, stride=n_heads), :] = kv[:, (n_heads + h) * hd:(n_heads + h + 1) * hd]


class _Cols(NamedTuple):
    mat: jax.Array
    start: int
    width: int

    @property
    def shape(self):
        return (self.mat.shape[0], self.width)


def _const_spec(op=None):
    if isinstance(op, _Cols):
        idx = op.start // op.width
        return pl.BlockSpec(op.shape, lambda *_: (0, idx), pipeline_mode=pl.Buffered(1))
    return pl.BlockSpec(memory_space=pltpu.VMEM)


def _flatten(ops):
    return [o for op in ops for o in (op if isinstance(op, list) else [op])]


def _const_specs(ops):
    return [_const_spec(op) for op in _flatten(ops)]


def _arrays(ops):
    return [op.mat if isinstance(op, _Cols) else op for op in _flatten(ops)]


def _regroup(kernel_fn, n_lead, ops):
    sizes = [len(op) if isinstance(op, list) else 0 for op in ops]

    def wrapped(*refs):
        refs = list(refs)
        args, pos = refs[:n_lead], n_lead
        for n in sizes:
            args.append(tuple(refs[pos:pos + n]) if n else refs[pos])
            pos += max(n, 1)
        return kernel_fn(*args, *refs[pos:])

    return wrapped


def _params(n_grid):
    return pltpu.CompilerParams(dimension_semantics=("arbitrary",) * n_grid, vmem_limit_bytes=VMEM_LIMIT_BYTES)


def _mem_kv(mem2d, g, w, tm):
    rows, d = mem2d.shape
    n_heads = w.shape[1] // 2 // X_HEAD_DIM
    out = jax.ShapeDtypeStruct((rows * n_heads, X_HEAD_DIM), F32)
    out_spec = pl.BlockSpec((tm * n_heads, X_HEAD_DIM), lambda i: (i, 0))
    return pl.pallas_call(
        functools.partial(_memkv_kernel, n_heads=n_heads),
        out_shape=(out, out),
        grid=(rows // tm,),
        in_specs=[pl.BlockSpec((tm, d), lambda i: (i, 0)), _const_spec(), _const_spec()],
        out_specs=(out_spec, out_spec),
        compiler_params=_params(1),
        name="mem_kv",
    )(mem2d, g, w)


def _x_kernel(x_ref, g_ref, wq_ref, wgx_ref, wmx_ref, k_ref, v_ref, wpx_ref, m_ref, *, nb, rows, head_dim):
    d = x_ref.shape[-1]
    x = x_ref[...].reshape(nb * rows, d)
    hn = _rmsnorm(x, g_ref[...]).astype(BF16)
    q = _dotw(hn, wq_ref)
    gate = _silu(_dotw(hn, wgx_ref))
    gm = _sigmoid(_dotw(hn, wmx_ref))
    n_heads = q.shape[1] // head_dim
    scale = head_dim ** -0.5
    outs = []
    n_mem = k_ref.shape[1] // n_heads
    for b in range(nb):
        qb = q[b * rows:(b + 1) * rows].astype(BF16)
        heads = range(n_heads)
        kb = [k_ref[b, pl.ds(h, n_mem, stride=n_heads), :].astype(BF16) for h in heads]
        vb = [v_ref[b, pl.ds(h, n_mem, stride=n_heads), :].astype(BF16) for h in heads]
        scores = [_dot_nt(qb[:, h * head_dim:(h + 1) * head_dim], kb[h]) * scale for h in heads]
        exps = [jnp.exp(s - jnp.max(s, axis=-1, keepdims=True)) for s in scores]
        probs = [(e / jnp.sum(e, axis=-1, keepdims=True)).astype(BF16) for e in exps]
        outs.append(jnp.concatenate([_dot(probs[h], vb[h]) for h in heads], axis=-1))
    o = outs[0] if nb == 1 else jnp.concatenate(outs, axis=0)
    hx = (o * gate).astype(BF16)
    m = gm * _dotw(hx, wpx_ref)
    m_ref[...] = m.reshape(nb, rows, d)


def _xs_kernel(x_ref, g_ref, wq_ref, wgx_ref, wmx_ref, k_ref, v_ref, wpx_ref, m_ref, *, nb, rows, head_dim):
    d = x_ref.shape[-1]
    x = x_ref[...].reshape(nb * rows, d)
    hn = _rmsnorm(x, g_ref[...]).astype(BF16)
    q = _dotw(hn, wq_ref)
    gate = _silu(_dotw(hn, wgx_ref))
    gm = _sigmoid(_dotw(hn, wmx_ref))
    n_heads = q.shape[1] // head_dim
    n_kv = k_ref.shape[1]
    scale = head_dim ** -0.5
    row_head = lax.broadcasted_iota(jnp.int32, (n_heads * rows, n_kv), 0) // rows
    col_head = lax.broadcasted_iota(jnp.int32, (n_heads * rows, n_kv), 1) % n_heads
    same_head = row_head == col_head
    seqs = range(nb)
    scores = []
    for b in seqs:
        qb = q[b * rows:(b + 1) * rows]
        q_heads = jnp.concatenate([qb[:, h * head_dim:(h + 1) * head_dim] for h in range(n_heads)], axis=0)
        scores.append(_dot_nt(q_heads.astype(BF16), k_ref[b].astype(BF16)))
    masked = [jnp.where(same_head, s * scale, -jnp.inf) for s in scores]
    exps = [jnp.exp(s - jnp.max(s, axis=-1, keepdims=True)) for s in masked]
    probs = [(e / jnp.sum(e, axis=-1, keepdims=True)).astype(BF16) for e in exps]
    o_heads = [_dot(probs[b], v_ref[b].astype(BF16)) for b in seqs]
    outs = [jnp.concatenate([oh[h * rows:(h + 1) * rows] for h in range(n_heads)], axis=1) for oh in o_heads]
    o = outs[0] if nb == 1 else jnp.concatenate(outs, axis=0)
    hx = (o * gate).astype(BF16)
    m = gm * _dotw(hx, wpx_ref)
    m_ref[...] = m.reshape(nb, rows, d)


def _branch_x(x3, g, wq, wgx, wmx, k3, v3, wpx, *, nb, rows, heads_in_rows):
    bsz, seq, d = x3.shape
    n_mem, dx = k3.shape[1], k3.shape[2]
    kern = functools.partial(_xs_kernel if heads_in_rows else _x_kernel, nb=nb, rows=rows, head_dim=X_HEAD_DIM)
    return pl.pallas_call(
        kern,
        out_shape=jax.ShapeDtypeStruct((bsz, seq, d), F32),
        grid=(bsz // nb, seq // rows),
        in_specs=[pl.BlockSpec((nb, rows, d), lambda i, j: (i, j, 0))] + _const_specs([g, wq, wgx, wmx]) + [
            pl.BlockSpec((nb, n_mem, dx), lambda i, j: (i, 0, 0)),
            pl.BlockSpec((nb, n_mem, dx), lambda i, j: (i, 0, 0)),
            _const_spec(wpx),
        ],
        out_specs=pl.BlockSpec((nb, rows, d), lambda i, j: (i, j, 0)),
        compiler_params=_params(2),
        name="branch_x",
    )(x3, *_arrays([g, wq, wgx, wmx]), k3, v3, wpx)


def _a_kernel(x_ref, g_ref, wu_ref, wv_ref, wga_ref, wma_ref, lng_ref, lnb_ref, ws_ref, bs_ref, wpa_ref, min_ref,
              mout_ref, *maybe_v_ref, chunk, groups, sub):
    tm, d = x_ref.shape
    gd = d // groups
    bias = bs_ref[...]
    for t in range(tm // sub):
        ts = slice(t * sub, (t + 1) * sub)
        hn = _rmsnorm(x_ref[ts, :], g_ref[...]).astype(BF16)
        v = _dot_cols(hn, wv_ref, _gelu)
        u = _dot_cols(hn, wu_ref, _gelu)
        vc = v - jnp.mean(v, axis=-1, keepdims=True)
        vn = vc * lax.rsqrt(jnp.mean(vc * vc, axis=-1, keepdims=True) + EPS) * lng_ref[...] + lnb_ref[...]
        if maybe_v_ref:
            maybe_v_ref[0][ts, :] = vn
        ga = _dot_cols(hn, wga_ref, _silu)
        vb = vn.astype(BF16)
        n_c = sub // chunk
        wide = [_dot(ws_ref[gi], jnp.concatenate([vb[c * chunk:(c + 1) * chunk, gi * gd:(gi + 1) * gd]
                                                  for c in range(n_c)], axis=1)) for gi in range(groups)]
        mixed = [jnp.concatenate([wg[:, c * gd:(c + 1) * gd] for wg in wide], axis=1) + bias for c in range(n_c)]
        s = mixed[0] if n_c == 1 else jnp.concatenate(mixed, axis=0)
        ha = (u * s * ga).astype(BF16)
        gm = _sigmoid(_dotw(hn, wma_ref))
        mout_ref[ts, :] = min_ref[ts, :] + gm * _dotw(ha, wpa_ref)


def _branch_a(x2, g, wu, wv, wga, wma, lng, lnb, ws, bs, wpa, m_in, *, tm, want_v):
    rows, d = x2.shape
    row_spec = pl.BlockSpec((tm, d), lambda i: (i, 0))
    out_shape = [jax.ShapeDtypeStruct((rows, d), F32)]
    out_specs = [row_spec]
    if want_v:
        out_shape.append(jax.ShapeDtypeStruct((rows, d), F32))
        out_specs.append(row_spec)
    kern = functools.partial(_a_kernel, chunk=CHUNK, groups=A_GROUPS, sub=min(tm, 256))
    consts = [g, wu, wv, wga, wma, lng, lnb, ws, bs, wpa]
    res = pl.pallas_call(
        kern,
        out_shape=tuple(out_shape),
        grid=(rows // tm,),
        in_specs=[row_spec] + _const_specs(consts) + [row_spec],
        out_specs=tuple(out_specs),
        compiler_params=_params(1),
        name="branch_a",
    )(x2, *_arrays(consts), m_in)
    return res if want_v else (res[0], None)


def _finish_b(x, y, g_ref, wz_ref, wmb_ref, ng_ref, wpb_ref, m_in, wout_ref, fg_ref, *, n_groups, final_norm):
    hn = _rmsnorm(x, g_ref[...]).astype(BF16)
    gm = _sigmoid(_dotw(hn, wmb_ref))
    yf = y * _silu(_dotw(hn, wz_ref))
    gw = yf.shape[1] // n_groups
    parts = []
    for gi in range(n_groups):
        yg = yf[:, gi * gw:(gi + 1) * gw]
        parts.append(yg * lax.rsqrt(jnp.mean(yg * yg, axis=-1, keepdims=True) + EPS))
    hb = (jnp.concatenate(parts, axis=-1) * ng_ref[...]).astype(BF16)
    m = m_in + gm * _dotw(hb, wpb_ref)
    out = x + _dotw(m.astype(BF16), wout_ref)
    return _rmsnorm(out, fg_ref[...]) if final_norm else out


HANDOFF = ("xs", "bm", "cm", "zs", "dt", "gm", "tail")


def _bp_stage1(x_ref, g_ref, wz_ref, wxbc_ref, wdt_ref, wmb_ref, cw_ref, cb_ref, dtb_ref, perm_ref, xs_scr, out,
               first_tile, *, tm, chunk, d_inner, bc_width, conv_w):
    pad = SUBLANES
    vpc = chunk // pad
    n_chunks = tm // chunk
    width = MXU_COLS
    hn = _rmsnorm(x_ref[0], g_ref[...]).astype(BF16)
    for c0 in range(0, out["gm"].shape[1], width):
        cols = slice(c0, c0 + width)
        out["gm"][:, cols] = _sigmoid(_dotw(hn, wmb_ref, cols))
        yield
    hn = _dot(perm_ref[0], hn).astype(BF16)
    out["dt"][...] = jax.nn.softplus(_dotw(hn, wdt_ref) + dtb_ref[...])
    yield
    for c0 in range(0, d_inner, width):
        cols = slice(c0, c0 + width)
        out["zs"][:, cols] = _silu(_dotw(hn, wz_ref, cols))
        yield
    first_sublane = lax.broadcasted_iota(jnp.int32, (pad, width), 0) == 0
    for c0 in range(0, _ncols(wxbc_ref), width):
        cols = slice(c0, c0 + width)
        raw_all = _dotw(hn, wxbc_ref, cols)
        prev_tail = jnp.where(first_tile, 0.0, xs_scr[:, cols])
        acts = []
        for c in range(n_chunks):
            raw = raw_all[c * chunk:(c + 1) * chunk]
            conv = cb_ref[:, cols] + raw * cw_ref[conv_w - 1:conv_w, cols]
            for back in range(1, conv_w):
                head = []
                for v in range(back):
                    w = vpc - back + v
                    cur = pltpu.roll(raw[w * pad:(w + 1) * pad], 1, axis=0)
                    prv = pltpu.roll(prev_tail[(w - vpc + conv_w - 1) * pad:(w - vpc + conv_w) * pad], 1, axis=0)
                    head.append(jnp.where(first_sublane, prv, cur))
                shifted = jnp.concatenate(head + [raw[:chunk - back * pad]], axis=0)
                conv = conv + shifted * cw_ref[conv_w - 1 - back:conv_w - back, cols]
            acts.append(_silu(conv))
            prev_tail = raw[chunk - (conv_w - 1) * pad:]
        xs_scr[:, cols] = prev_tail
        out["tail"][:, cols] = prev_tail
        act = acts[0] if n_chunks == 1 else jnp.concatenate(acts, axis=0)
        if c0 < d_inner:
            out["xs"][:, cols] = act
        elif c0 < d_inner + bc_width:
            out["bm"][:, c0 - d_inner:c0 - d_inner + width] = act.astype(BF16)
        else:
            out["cm"][:, c0 - d_inner - bc_width:c0 - d_inner - bc_width + width] = act.astype(BF16)
        yield


def _bp_stage2(x_ref, min_ref, alog_ref, dsk_ref, ng_ref, wpb_ref, wout_ref, fg_ref, tri_ref, e3_ref, perm_ref,
               y_ref, st_scr, inp, first_tile, *, tm, chunk, n_groups, n_heads, head_dim, n_state, final_norm):
    pad = SUBLANES
    vpc = chunk // pad
    hpg = n_heads // n_groups
    gw = hpg * head_dim
    width = MXU_COLS

    def token_of(q):
        return (q % pad) * vpc + q // pad

    li = token_of(lax.broadcasted_iota(jnp.int32, (chunk, chunk), 0))
    si = token_of(lax.broadcasted_iota(jnp.int32, (chunk, chunk), 1))
    causal = li >= si
    head_of_lane = lax.broadcasted_iota(jnp.int32, (chunk, gw), 1) // head_dim
    tri = tri_ref[...]
    e3 = e3_ref[...]
    a_row = -jnp.exp(alog_ref[...])

    hb_rows = []
    for c in range(tm // chunk):
        rs = slice(c * chunk, (c + 1) * chunk)
        dt = inp["dt"][rs, :]
        acs = _cumsum_rows(tri, dt * a_row)
        acs_t = acs.T
        acs_e = _dot(_split_by_replica(acs, n_heads).astype(BF16), e3)
        a_last = acs_e[chunk - 1:chunk, :]
        xs_c = inp["xs"][rs, :]
        xdt_c = xs_c * _dot(_split_by_replica(dt, n_heads).astype(BF16), e3)
        xdt_b = xdt_c.astype(BF16)
        xd_state = (xdt_c * jnp.exp(a_last - acs_e)).astype(BF16)
        decay_in = jnp.exp(acs_e)
        st_prev = st_scr[...]
        if c == 0:
            st_prev = jnp.where(first_tile, 0.0, st_prev)
        st_prev_b = st_prev.astype(BF16)
        yield
        groups = range(n_groups)
        n_sl = [slice(gi * n_state, (gi + 1) * n_state) for gi in groups]
        g_sl = [slice(gi * gw, (gi + 1) * gw) for gi in groups]
        scores = [_dot_nt(inp["cm"][rs, n_sl[gi]], inp["bm"][rs, n_sl[gi]]) for gi in groups]
        yield
        y_offs = [_dot(inp["cm"][rs, n_sl[gi]], st_prev_b[:, g_sl[gi]]) for gi in groups]
        st_parts = [_dot_tn(inp["bm"][rs, n_sl[gi]], xd_state[:, g_sl[gi]]) for gi in groups]
        st_scr[...] = st_prev * jnp.exp(a_last) + jnp.concatenate(st_parts, axis=1)
        yield
        y_diags = []
        for gi in groups:
            m_heads, x_blocks = [], []
            xg = xdt_b[:, g_sl[gi]]
            for r in range(hpg):
                h = gi * hpg + r
                diff = acs[:, h:h + 1] - acs_t[h:h + 1, :]
                decay = jnp.exp(jnp.where(causal, diff, -jnp.inf))
                m_heads.append((scores[gi] * decay).astype(BF16))
                x_blocks.append(jnp.where(head_of_lane == r, xg, jnp.zeros_like(xg)))
            y_diags.append(_dot(jnp.concatenate(m_heads, axis=1), jnp.concatenate(x_blocks, axis=0)))
            yield
        hb_parts = []
        for gi in groups:
            gs = g_sl[gi]
            y = y_diags[gi] + y_offs[gi] * decay_in[:, gs] + xs_c[:, gs] * dsk_ref[:, gs]
            yf = y * inp["zs"][rs, gs]
            yn = yf * lax.rsqrt(jnp.mean(yf * yf, axis=-1, keepdims=True) + EPS)
            hb_parts.append((yn * ng_ref[:, gs]).astype(BF16))
            if gi % 2:
                yield
        hb_rows.append(jnp.concatenate(hb_parts, axis=1))
    hb = hb_rows[0] if len(hb_rows) == 1 else jnp.concatenate(hb_rows, axis=0)
    hb = _dot(perm_ref[1], hb).astype(BF16)
    yield
    m_parts = []
    for c0 in range(0, wpb_ref.shape[1], width):
        cols = slice(c0, c0 + width)
        m_parts.append((min_ref[0, :, cols] + inp["gm"][:, cols] * _dotw(hb, wpb_ref, cols)).astype(BF16))
        yield
    m = jnp.concatenate(m_parts, axis=1)
    out = x_ref[0] + _dotw(m, wout_ref)
    y_ref[0] = _rmsnorm(out, fg_ref[...]) if final_norm else out


BP_WEIGHTS = ("wz", "wxbc", "wdt", "wmb", "wpb", "wout")


def _bp_kernel(x1_ref, x2_ref, min_ref, g_ref, cw_ref, cb_ref, dtb_ref, alog_ref, dsk_ref, ng_ref, fg_ref, tri_ref,
               e3_ref, perm_ref, *rest,
               n_hbm, copies, nj, n_tiles, tm, chunk, n_groups, n_heads, head_dim, n_state, conv_w, final_norm):
    hbm_refs, (y_ref, ssm_ref, conv_ref, xs_scr, st_scr), rest = rest[:n_hbm], rest[n_hbm:n_hbm + 5], rest[n_hbm + 5:]
    w_scr, copy_sem, slot_refs = rest[:len(BP_WEIGHTS)], rest[len(BP_WEIGHTS)], rest[len(BP_WEIGHTS) + 1:]
    wz_ref, wxbc_ref, wdt_ref, wmb_ref, wpb_ref, wout_ref = w_scr
    t = pl.program_id(0)
    d_inner = n_heads * head_dim
    pad = SUBLANES
    tile1 = jnp.minimum(t, n_tiles - 1)
    tile2 = jnp.maximum(t - 1, 0)
    both_slots = dict(zip(HANDOFF, slot_refs))
    slot_w, slot_r = ({k: ref.at[i] for k, ref in both_slots.items()} for i in (t % 2, 1 - t % 2))

    @pl.when(t == 0)
    def _init():
        dmas = [pltpu.make_async_copy(hbm_refs[src].at[:, pl.ds(start, width)],
                                      w_scr[dst].at[:, pl.ds(dst_start, width)], copy_sem.at[i])
                for i, (src, start, width, dst, dst_start) in enumerate(copies)]
        for dma in dmas:
            dma.start()
        st_scr[...] = jnp.zeros_like(st_scr)
        xs_scr[...] = jnp.zeros_like(xs_scr)
        for dma in dmas:
            dma.wait()

    def step(run_stage1, run_stage2):
        live = []
        if run_stage2:
            live.append(_bp_stage2(x2_ref, min_ref, alog_ref, dsk_ref, ng_ref, wpb_ref, wout_ref, fg_ref, tri_ref,
                                   e3_ref, perm_ref, y_ref, st_scr, slot_r, tile2 % nj == 0, tm=tm, chunk=chunk,
                                   n_groups=n_groups, n_heads=n_heads, head_dim=head_dim, n_state=n_state,
                                   final_norm=final_norm))
        if run_stage1:
            live.append(_bp_stage1(x1_ref, g_ref, wz_ref, wxbc_ref, wdt_ref, wmb_ref, cw_ref, cb_ref, dtb_ref, perm_ref,
                                   xs_scr, slot_w, tile1 % nj == 0, tm=tm, chunk=chunk, d_inner=d_inner,
                                   bc_width=n_groups * n_state, conv_w=conv_w))
        while live:
            for gen in list(live):
                if next(gen, StopIteration) is StopIteration:
                    live.remove(gen)
        if run_stage2:
            @pl.when(tile2 % nj == nj - 1)
            def _seq_end():
                tail = slot_r["tail"][...]
                conv_ref[0] = jnp.concatenate([tail[(v + 1) * pad - 1:(v + 1) * pad] for v in range(conv_w - 1)],
                                              axis=0)
                ssm_ref[0] = st_scr[...].T

    pl.when(t == 0)(functools.partial(step, True, False))
    pl.when((t > 0) & (t < n_tiles))(functools.partial(step, True, True))
    pl.when(t == n_tiles)(functools.partial(step, False, True))


def _branch_b_prompt(x3, m3, w, *, tm, final_norm):
    bsz, seq, d = x3.shape
    conv_dim = w["cw"].shape[1]
    d_inner = w["ng"].shape[1]
    n_heads = d_inner // SSD_HEAD_DIM
    conv_w = w["cw"].shape[0]
    nj = seq // tm
    n_tiles = bsz * nj
    bc_width = SSD_GROUPS * SSD_STATE
    hbm_ops, copies, w_shapes = [], [], []
    for wi, name in enumerate(BP_WEIGHTS):
        parts = w[name] if isinstance(w[name], list) else [w[name]]
        dst = 0
        for part in parts:
            mat, start = (part.mat, part.start) if isinstance(part, _Cols) else (part, 0)
            src = next((i for i, op in enumerate(hbm_ops) if op is mat), len(hbm_ops))
            if src == len(hbm_ops):
                hbm_ops.append(mat)
            copies.append((src, start, part.shape[1], wi, dst))
            dst += part.shape[1]
        w_shapes.append((parts[0].shape[0], dst))
    kern = functools.partial(_bp_kernel, n_hbm=len(hbm_ops), copies=tuple(copies), nj=nj, n_tiles=n_tiles, tm=tm,
                             chunk=CHUNK, n_groups=SSD_GROUPS, n_heads=n_heads, head_dim=SSD_HEAD_DIM,
                             n_state=SSD_STATE, conv_w=conv_w, final_norm=final_norm)

    def tile1(t):
        return jnp.minimum(t, n_tiles - 1)

    def tile2(t):
        return jnp.maximum(t - 1, 0)

    spec1 = pl.BlockSpec((1, tm, d), lambda t: (tile1(t) // nj, tile1(t) % nj, 0))
    spec2 = pl.BlockSpec((1, tm, d), lambda t: (tile2(t) // nj, tile2(t) % nj, 0))
    consts = [w[k] for k in ("g", "cw", "cb", "dtb", "alog", "dsk", "ng", "fg", "tri", "e3")]
    tok = (np.arange(tm) // CHUNK) * CHUNK + np.tile(_chunk_tokens(), tm // CHUNK)
    p = (tok[:, None] == np.arange(tm)[None, :]).astype(np.float32)
    perm = jnp.asarray(np.stack([p, p.T]), dtype=BF16)
    handoff = {"xs": ((tm, d_inner), F32), "bm": ((tm, bc_width), BF16), "cm": ((tm, bc_width), BF16),
               "zs": ((tm, d_inner), F32), "dt": ((tm, LANES), F32), "gm": ((tm, d), F32),
               "tail": (((conv_w - 1) * SUBLANES, conv_dim), F32)}
    return pl.pallas_call(
        kern,
        out_shape=(jax.ShapeDtypeStruct((bsz, seq, d), F32),
                   jax.ShapeDtypeStruct((bsz, d_inner, SSD_STATE), F32),
                   jax.ShapeDtypeStruct((bsz, conv_w - 1, conv_dim), F32)),
        grid=(n_tiles + 1,),
        in_specs=([spec1, spec2, spec2] + _const_specs(consts + [perm])
                  + [pl.BlockSpec(memory_space=pl.ANY)] * len(hbm_ops)),
        out_specs=(spec2,
                   pl.BlockSpec((1, d_inner, SSD_STATE), lambda t: (tile2(t) // nj, 0, 0)),
                   pl.BlockSpec((1, conv_w - 1, conv_dim), lambda t: (tile2(t) // nj, 0, 0))),
        scratch_shapes=([pltpu.VMEM(((conv_w - 1) * SUBLANES, conv_dim), F32), pltpu.VMEM((SSD_STATE, d_inner), F32)]
                        + [pltpu.VMEM(shape, BF16) for shape in w_shapes]
                        + [pltpu.SemaphoreType.DMA((len(copies),))]
                        + [pltpu.VMEM((2,) + handoff[k][0], handoff[k][1]) for k in HANDOFF]),
        compiler_params=_params(1),
        name="branch_b_prompt",
    )(x3, x3, m3, *_arrays(consts), perm, *hbm_ops)


def _bs_kernel(x_ref, g_ref, wxbc_ref, wdt_ref, cw_ref, cb_ref, dtb_ref, alog_ref, dsk_ref,
               tri_ref, e3_ref, ecol_ref, conv0_ref, h0_ref,
               y_ref, hnew_ref, convnew_ref, xs_scr, tr_scr,
               *, nb, rows, n_groups, n_heads, head_dim, n_state, conv_w):
    hpg = n_heads // n_groups
    gw = hpg * head_dim
    d_inner = n_heads * head_dim
    d = x_ref.shape[-1]
    r_all = nb * rows
    hs_w = n_heads * rows

    @pl.when(pl.program_id(0) == 0)
    def _init():
        tr_scr[...] = jnp.zeros_like(tr_scr)
        xs_scr[...] = jnp.zeros_like(xs_scr)

    x = x_ref[...].reshape(r_all, d)
    hn = _rmsnorm(x, g_ref[...]).astype(BF16)
    xbc_raw = _dotw(hn, wxbc_ref)
    dt = jax.nn.softplus(_dotw(hn, wdt_ref) + dtb_ref[...])

    raw = xbc_raw.reshape(nb, rows, xbc_raw.shape[1])
    xs_scr[:, rows - (conv_w - 1):, :] = conv0_ref[...]
    state = xs_scr[...]
    row_in_seq = lax.broadcasted_iota(jnp.int32, raw.shape, 1)
    conv = cb_ref[...] + raw * cw_ref[conv_w - 1:conv_w, :]
    for back in range(1, conv_w):
        shifted = jnp.where(row_in_seq < back, pltpu.roll(state, back, axis=1), pltpu.roll(raw, back, axis=1))
        conv = conv + shifted * cw_ref[conv_w - 1 - back:conv_w - back, :]
    convnew_ref[...] = raw[:, rows - (conv_w - 1):, :]
    xbc = _silu(conv.reshape(r_all, xbc_raw.shape[1]))
    xs = xbc[:, :d_inner]
    bm = xbc[:, d_inner:d_inner + n_groups * n_state]
    cm = xbc[:, d_inner + n_groups * n_state:]

    a_row = -jnp.exp(alog_ref[...])
    acs = _cumsum_rows(tri_ref[...], dt * a_row)
    e3 = e3_ref[...]
    acs_split = _split_by_replica(acs, n_heads).astype(BF16)
    acs_e = _dot(acs_split, e3)
    xdt = xs * _dot(_split_by_replica(dt, n_heads).astype(BF16), e3)
    decay_in = jnp.exp(acs_e)

    acs_col = _dot(acs_split, ecol_ref[...])
    l_idx = lax.broadcasted_iota(jnp.int32, (r_all, hs_w), 0) % rows
    s_idx = lax.broadcasted_iota(jnp.int32, (r_all, hs_w), 1) % rows
    on_diag = jnp.where(l_idx == s_idx, acs_col, 0.0).reshape(nb, rows, hs_w)
    acs_row = jnp.broadcast_to(jnp.sum(on_diag, axis=1, keepdims=True), (nb, rows, hs_w)).reshape(r_all, hs_w)
    decay = jnp.exp(jnp.where(l_idx >= s_idx, acs_col - acs_row, -jnp.inf))

    slot = LANES // nb
    ones_rows = lax.broadcasted_iota(jnp.int32, (slot, n_state), 0)
    ones_blk = jnp.where((ones_rows >= rows) & (ones_rows < rows + 3), 1.0, 0.0)

    def seq_rows(b):
        return slice(b * rows, (b + 1) * rows)

    lane_in_tile = lax.broadcasted_iota(jnp.int32, (rows, LANES), 1)

    def expand_rows(v, start_of, width):
        c = v.shape[1]
        blocks = []
        for h in range(n_heads):
            start = start_of(h)
            lo = start // LANES * LANES
            hi = max(lo + LANES, start + width)
            piece = v[:, lo:hi]
            if width < LANES:
                piece = jnp.where(lane_in_tile // width == (start - lo) // width, piece, 0.0)
            parts = [jnp.zeros((rows, lo), F32)] if lo else []
            parts.append(piece)
            if hi < c:
                parts.append(jnp.zeros((rows, c - hi), F32))
            blocks.append(jnp.concatenate(parts, axis=1) if len(parts) > 1 else piece)
        return jnp.concatenate(blocks, axis=0).astype(BF16)

    y_diags = []
    for b0 in range(0, nb, SEQ_BATCH):
        seqs = range(b0, min(b0 + SEQ_BATCH, nb))
        b_exps = [expand_rows(bm[seq_rows(b)], lambda h: h // hpg * n_state, n_state) for b in seqs]
        scores = [_dot_nt(cm[seq_rows(b)].astype(BF16), be) for b, be in zip(seqs, b_exps)]
        x_blks = [expand_rows(xdt[seq_rows(b)], lambda h: h * head_dim, head_dim) for b in seqs]
        m_alls = [(s * decay[seq_rows(b)]).astype(BF16) for b, s in zip(seqs, scores)]
        y_diags += [_dot(m, xb) for m, xb in zip(m_alls, x_blks)]
    for b in range(nb):
        a_last = acs_e[(b + 1) * rows - 1:(b + 1) * rows, :]
        d_hi, d_mid, d_lo = _bf16_parts(jnp.exp(a_last))
        tr_scr[b * slot:b * slot + rows, :] = xdt[seq_rows(b)] * jnp.exp(a_last - acs_e[seq_rows(b)])
        tr_scr[b * slot + rows:b * slot + rows + 1, :] = d_hi.astype(F32)
        tr_scr[b * slot + rows + 1:b * slot + rows + 2, :] = d_mid.astype(F32)
        tr_scr[b * slot + rows + 2:b * slot + rows + 3, :] = d_lo.astype(F32)
    tr_t = tr_scr[...].T.astype(BF16)

    ys = []
    for b in range(nb):
        rs = slice(b * rows, (b + 1) * rows)
        bm_b, cm_bb = bm[rs], cm[rs].astype(BF16)
        y_parts = []
        for gi in range(n_groups):
            ns = slice(gi * n_state, (gi + 1) * n_state)
            gs = slice(gi * gw, (gi + 1) * gw)
            h0_g = h0_ref[b, gs, :]
            y_parts.append(_dot_nt(cm_bb[:, ns], h0_g.astype(BF16)))
            w_seq = jnp.concatenate(
                [jnp.concatenate([bm_b[:, ns], jnp.zeros((slot - rows, n_state), F32)], axis=0), ones_blk], axis=1)
            pieces = ([jnp.zeros((b * slot, 2 * n_state), F32)] if b else []) + [w_seq]
            if b < nb - 1:
                pieces.append(jnp.zeros((LANES - (b + 1) * slot, 2 * n_state), F32))
            upd = _dot(tr_t[gs, :], jnp.concatenate(pieces, axis=0).astype(BF16))
            hnew_ref[b, gs, :] = h0_g * upd[:, n_state:] + upd[:, :n_state]
        ys.append(y_diags[b] + jnp.concatenate(y_parts, axis=1) * decay_in[rs])
    y = jnp.concatenate(ys, axis=0) + xs * dsk_ref[...]
    y_ref[...] = y.reshape(nb, rows, d_inner)


def _bfin_kernel(x_ref, y_ref, min_ref, g_ref, wz_ref, wmb_ref, ng_ref, wpb_ref, wout_ref, fg_ref, out_ref,
                 *, n_groups, final_norm):
    out_ref[...] = _finish_b(x_ref[...], y_ref[...], g_ref, wz_ref, wmb_ref, ng_ref, wpb_ref, min_ref[...],
                             wout_ref, fg_ref, n_groups=n_groups, final_norm=final_norm)


def _branch_b_sample(x3, m3, conv0, h0, w, *, layer, nb, tm, final_norm):
    bsz, rows, d = x3.shape
    conv_dim = w["cw"].shape[1]
    d_inner = w["ng"].shape[1]
    n_heads = d_inner // SSD_HEAD_DIM
    conv_w = w["cw"].shape[0]
    kern = functools.partial(_bs_kernel, nb=nb, rows=rows, n_groups=SSD_GROUPS, n_heads=n_heads,
                             head_dim=SSD_HEAD_DIM, n_state=SSD_STATE, conv_w=conv_w)

    def seq_spec(width):
        return pl.BlockSpec((nb, rows, width), lambda i: (i, 0, 0))

    conv_spec = pl.BlockSpec((nb, conv_w - 1, conv_dim), lambda i: (i, 0, 0))
    h_spec = pl.BlockSpec((nb, d_inner, SSD_STATE), lambda i: (i, 0, 0))
    consts = [w[k] for k in ("g", "wxbc", "wdt", "cw", "cb", "dtb", "alog", "dsk", "tri_s", "e3", "ecol")]
    y, h_new, conv_new = pl.pallas_call(
        _regroup(kern, 1, consts),
        out_shape=(jax.ShapeDtypeStruct((bsz, rows, d_inner), F32),
                   jax.ShapeDtypeStruct((bsz, d_inner, SSD_STATE), F32),
                   jax.ShapeDtypeStruct((bsz, conv_w - 1, conv_dim), F32)),
        grid=(bsz // nb,),
        in_specs=[seq_spec(d)] + _const_specs(consts) + [
            pl.BlockSpec((None, nb, conv_w - 1, conv_dim), lambda i: (layer, i, 0, 0)), h_spec],
        out_specs=(seq_spec(d_inner), h_spec, conv_spec),
        scratch_shapes=[pltpu.VMEM((nb, SUBLANES, conv_dim), F32), pltpu.VMEM((LANES, d_inner), F32)],
        compiler_params=_params(1),
        name="branch_b_sample",
    )(x3, *_arrays(consts), conv0, h0)

    n_rows = bsz * rows
    fin_consts = [w[k] for k in ("g", "wz", "wmb", "ng", "wpb", "wout", "fg")]

    def rows_spec(width):
        return pl.BlockSpec((tm, width), lambda i: (i, 0))

    out = pl.pallas_call(
        _regroup(functools.partial(_bfin_kernel, n_groups=SSD_GROUPS, final_norm=final_norm), 3, fin_consts),
        out_shape=jax.ShapeDtypeStruct((n_rows, d), F32),
        grid=(n_rows // tm,),
        in_specs=[rows_spec(d), rows_spec(d_inner), rows_spec(d)] + _const_specs(fin_consts),
        out_specs=rows_spec(d),
        compiler_params=_params(1),
        name="branch_b_sample_out",
    )(x3.reshape(n_rows, d), y.reshape(n_rows, d_inner), m3.reshape(n_rows, d), *_arrays(fin_consts))
    return out.reshape(bsz, rows, d), h_new, conv_new


def _tile(n, pref):
    return pref if n % pref == 0 else n


def _chunk_tokens():
    q = np.arange(CHUNK)
    return (q % SUBLANES) * (CHUNK // SUBLANES) + q // SUBLANES


def _head_expand(n_heads, width, n_rep):
    j = np.arange(LANES)[:, None]
    c = np.arange(n_heads * width)[None, :]
    return jnp.asarray(((j < n_rep * n_heads) & (j % n_heads == c // width)).astype(np.float32), dtype=BF16)


def _layer_weights(l, d, dec_rows, nb_s, norm_g, w_in, conv_w, conv_b, dt_bias, a_log, d_skip, ssd_norm_g, ln_v_g,
                   ln_v_b, w_spatial, b_spatial, w_proj_a, w_proj_b, w_proj_x, w_out, final_norm_g):
    d_a = d
    d_inner = w_proj_b.shape[1]
    conv_dim = conv_w.shape[2]
    n_heads = a_log.shape[1]
    d_x = w_proj_x.shape[1]
    sizes = (d_a, d_a, d_a, d_inner, conv_dim, n_heads, d_x, d_x, N_BRANCH * d)
    offs = np.concatenate([[0], np.cumsum(sizes)])
    wl = w_in[l]
    sec = lambda i: wl[:, offs[i]:offs[i + 1]]
    row = lambda v: v.reshape(1, -1).astype(F32)
    rep3 = lambda v, fill: jnp.concatenate(
        [v] * DT_REPLICAS + [jnp.full(v.shape[:-1] + (LANES - DT_REPLICAS * v.shape[-1],), fill, v.dtype)], axis=-1)

    tril = jnp.tril(jnp.ones((CHUNK, CHUNK), F32))
    ws_p = jnp.where(tril[None] > 0, w_spatial[l], 0.0)
    bs_p = jnp.repeat(b_spatial[l].T, d_a // A_GROUPS, axis=1)
    n_seq = CHUNK // dec_rows
    sel = jnp.asarray(np.tile(np.eye(dec_rows, dtype=np.float32), (n_seq, 1)))
    blocks = jnp.asarray(np.kron(np.eye(n_seq, dtype=np.float32), np.ones((dec_rows, dec_rows), np.float32)))
    ws_s = jnp.einsum("ia,gab,jb->gij", sel, ws_p[:, :dec_rows, :dec_rows], sel,
                      precision=lax.Precision.HIGHEST) * blocks
    bs_s = jnp.tile(bs_p[:dec_rows], (n_seq, 1))

    w = {
        "g": row(norm_g[l]),
        "lng": row(ln_v_g[l]), "lnb": row(ln_v_b[l]),
        "ws_p": ws_p.astype(BF16), "bs_p": bs_p, "ws_s": ws_s.astype(BF16), "bs_s": bs_s,
        "wpa": _pack_w(w_proj_a[l]),
        "cw": conv_w[l].astype(F32), "cb": row(conv_b[l]),
        "dtb": rep3(row(dt_bias[l]), 0.0), "alog": rep3(row(a_log[l]), 0.0),
        "dsk": jnp.repeat(row(d_skip[l]), SSD_HEAD_DIM, axis=1),
        "ng": row(ssd_norm_g[l]),
        "wpb": _pack_w(w_proj_b[l]),
        "wout": _pack_w(w_out[l]), "fg": row(final_norm_g),
        "tri": jnp.asarray(_chunk_tokens()[:, None] >= _chunk_tokens()[None, :], dtype=BF16),
        "tri_s": jnp.asarray(np.kron(np.eye(nb_s), np.tril(np.ones((dec_rows, dec_rows)))), dtype=BF16),
        "e3": _head_expand(n_heads, SSD_HEAD_DIM, DT_REPLICAS),
        "ecol": _head_expand(n_heads, dec_rows, DT_REPLICAS),
        "wpx": _pack_w(w_proj_x[l]),
    }
    w_bf = _pack_w(wl)
    tail = w_bf[:, offs[6]:]
    assert all(o % d == 0 for o in offs[:5]) and d_x % LANES == 0 and d % d_x == 0
    for name, i in (("wu", 0), ("wv", 1), ("wga", 2), ("wz", 3), ("wxbc", 4)):
        windows = [_Cols(w_bf, c, d) for c in range(offs[i], offs[i + 1], d)]
        w[name] = windows if len(windows) > 1 else windows[0]
    w["wdt"] = _pack_w(rep3(sec(5), 0.0))
    w["wq"], w["wgx"] = _Cols(tail, 0, d_x), _Cols(tail, d_x, d_x)
    for k, name in enumerate(("wma", "wmb", "wmx")):
        w[name] = _Cols(tail, 2 * d_x + k * d, d)
    return w


def _layer(x3, k3, v3, w, *, nb_x, rows_x, heads_in_rows, tm_a, ws, bs, want_v, b_fn):
    bsz, seq, d = x3.shape
    m = _branch_x(x3, w["g"], w["wq"], w["wgx"], w["wmx"], k3, v3, w["wpx"], nb=nb_x, rows=rows_x,
                  heads_in_rows=heads_in_rows)
    m, vn = _branch_a(x3.reshape(bsz * seq, d), w["g"], w["wu"], w["wv"], w["wga"], w["wma"], w["lng"], w["lnb"],
                      ws, bs, w["wpa"], m.reshape(bsz * seq, d), tm=tm_a, want_v=want_v)
    return b_fn(x3, m.reshape(bsz, seq, d)), vn


def kernel(x_prompt, x_sample, mem_prompt, cache_mem_k, cache_mem_v, state_ssm, state_conv, norm_g, w_in, conv_w,
           conv_b, dt_bias, a_log, d_skip, ssd_norm_g, ln_v_g, ln_v_b, w_spatial, b_spatial, mem_norm_g, w_mem_kv,
           w_proj_a, w_proj_b, w_proj_x, w_out, final_norm_g):
    depth = w_in.shape[0]
    bsz, seq, d = x_prompt.shape
    dec_b, dec_rows, _ = x_sample.shape
    n_mem = mem_prompt.shape[1]
    d_x = w_proj_x.shape[1]
    d_inner = w_proj_b.shape[1]
    n_heads = a_log.shape[1]
    assert seq % CHUNK == 0 and CHUNK % dec_rows == 0 and dec_rows == SUBLANES
    assert DT_REPLICAS * n_heads <= LANES and n_heads * dec_rows % LANES == 0

    nb_s = _tile(dec_b, 8)
    nb_xs = _tile(dec_b, 16)
    yp, ys = x_prompt, x_sample
    outs = {k: [] for k in ("mk", "mv", "hp", "cp", "hs", "cs", "vs")}
    for l in range(depth):
        w = _layer_weights(l, d, dec_rows, nb_s, norm_g, w_in, conv_w, conv_b, dt_bias, a_log, d_skip, ssd_norm_g,
                           ln_v_g, ln_v_b, w_spatial, b_spatial, w_proj_a, w_proj_b, w_proj_x, w_out, final_norm_g)
        final_norm = l == depth - 1
        mk, mv = _mem_kv(mem_prompt.reshape(bsz * n_mem, d), mem_norm_g[l].reshape(1, d).astype(F32),
                         _pack_w(w_mem_kv[l]), _tile(bsz * n_mem, 512))
        (yp, hp, cp), _ = _layer(
            yp, mk.reshape(bsz, n_mem * (d_x // X_HEAD_DIM), X_HEAD_DIM),
            mv.reshape(bsz, n_mem * (d_x // X_HEAD_DIM), X_HEAD_DIM), w,
            nb_x=1, rows_x=_tile(seq, 512), heads_in_rows=False, tm_a=_tile(bsz * seq, 512), ws=w["ws_p"], bs=w["bs_p"], want_v=False,
            b_fn=functools.partial(_branch_b_prompt, w=w, tm=_tile(seq, 256), final_norm=final_norm))
        (ys, hs, cs), vs = _layer(
            ys, cache_mem_k[l].reshape(dec_b, n_mem * (d_x // X_HEAD_DIM), X_HEAD_DIM),
            cache_mem_v[l].reshape(dec_b, n_mem * (d_x // X_HEAD_DIM), X_HEAD_DIM), w,
            nb_x=nb_xs, rows_x=dec_rows, heads_in_rows=True, tm_a=_tile(dec_b * dec_rows, 512), ws=w["ws_s"], bs=w["bs_s"], want_v=True,
            b_fn=functools.partial(_branch_b_sample, conv0=state_conv, layer=l,
                                   h0=state_ssm[l].reshape(dec_b, d_inner, SSD_STATE), w=w, nb=nb_s,
                                   tm=_tile(dec_b * dec_rows, 512), final_norm=final_norm))
        outs["mk"].append(mk.reshape(bsz, n_mem, d_x // X_HEAD_DIM, X_HEAD_DIM))
        outs["mv"].append(mv.reshape(bsz, n_mem, d_x // X_HEAD_DIM, X_HEAD_DIM))
        outs["hp"].append(hp.reshape(bsz, n_heads, SSD_HEAD_DIM, SSD_STATE))
        outs["cp"].append(cp)
        outs["hs"].append(hs.reshape(dec_b, n_heads, SSD_HEAD_DIM, SSD_STATE))
        outs["cs"].append(cs)
        outs["vs"].append(vs.reshape(dec_b, dec_rows, d))
    st = lambda k: jnp.stack(outs[k])
    return (yp, ys, st("mk"), st("mv"), st("hp"), st("cp"), st("hs"), st("cs"), st("vs"))
```
